```python
import math
import jax
import jax.numpy as jnp
from jax import lax
import numpy as np

D_MODEL = 2048
BATCH = 4
SEQ = 4096
DEPTH = 4

GRID_W = 64
CTX_LEN = 256
N_MIXERS = 3
N_HYENA = (DEPTH + 2) // 3
N_HGRN = (DEPTH + 1) // 3
N_GLA = DEPTH // 3
N_MOD = 6
FFN_HIDDEN = 256 * (-(-8 * D_MODEL // (3 * 256)))
RMS_EPS = 1e-6
CHUNK = 64

HYENA_ORDER = 2
FILTER_BANDS = 16
FILTER_EMB = 1 + 2 * FILTER_BANDS
FILTER_HIDDEN = 64
HYENA_DECAY_TARGET = 1e-2
HYENA_SHORT_DECAY_PCT = 0.3
HYENA_LONG_DECAY_PCT = 1.5
HYENA_DECAY_MIN = math.log(HYENA_DECAY_TARGET) / HYENA_LONG_DECAY_PCT
HYENA_DECAY_MAX = math.log(HYENA_DECAY_TARGET) / HYENA_SHORT_DECAY_PCT

HGRN_EXPAND = 128
HGRN_HEADS = D_MODEL // HGRN_EXPAND
HGRN_IN_DIM = 5 * D_MODEL

GLA_HEADS = 4
GLA_KEY_DIM = D_MODEL // 2
GLA_VAL_DIM = D_MODEL
GLA_DK = GLA_KEY_DIM // GLA_HEADS
GLA_GATE_RANK = 16
GLA_GATE_NORM = 16.0
GLA_IN_DIM = 2 * GLA_KEY_DIM + 2 * GLA_VAL_DIM + 2 * GLA_GATE_RANK

kernel_name = "hybrid_hyena_hgrn2_gla_prefix_dit"


def rms_norm(x, g):
    xf = x.astype(jnp.float32)
    y = xf * lax.rsqrt(jnp.mean(xf * xf, axis=-1, keepdims=True) + RMS_EPS)
    return (y * g.astype(jnp.float32)).astype(x.dtype)


def head_rms_norm(o, g, n_heads):
    b, l, d = o.shape
    y = rms_norm(o.reshape(b, l, n_heads, d // n_heads), g.reshape(n_heads, d // n_heads))
    return y.reshape(b, l, d)


def modulate(h, shift, scale):
    return h * (1.0 + scale) + shift


def to_heads(a, n_heads):
    b, l, d = a.shape
    return a.reshape(b, l, n_heads, d // n_heads).transpose(0, 2, 1, 3)


def from_heads(a):
    b, h, l, d = a.shape
    return a.transpose(0, 2, 1, 3).reshape(b, l, h * d)


def swiglu(h, w_in, w_out):
    gate, up = jnp.split(h @ w_in, 2, axis=-1)
    return (jax.nn.silu(gate) * up) @ w_out


def short_conv3(u, w, n_rows):
    b, l, ch = u.shape
    r = u.reshape(b, n_rows, l // n_rows, ch)
    p = jnp.pad(r, ((0, 0), (0, 0), (1, 1), (0, 0)))
    y = w[0] * p[:, :, :-2] + w[1] * r + w[2] * p[:, :, 2:]
    return y.reshape(b, l, ch)


def hyena_filters(seq_len, fw1, fb1, ffreq, fw2, fb2, fwout):
    f32 = jnp.float32
    d = fwout.shape[-1] // (2 * HYENA_ORDER)
    pos = jnp.arange(seq_len, dtype=f32)
    t = pos / max(seq_len - 1, 1)
    bands = jnp.arange(1, FILTER_BANDS + 1, dtype=f32)
    ang = (2.0 * math.pi / seq_len) * pos[:, None] * bands[None, :]
    z = jnp.concatenate([t[:, None], jnp.cos(ang), -jnp.sin(ang)], axis=-1)
    hid = jnp.sin(ffreq[0].astype(f32) * (z @ fw1.astype(f32) + fb1.astype(f32)))
    hid = jnp.sin(ffreq[1].astype(f32) * (hid @ fw2.astype(f32) + fb2.astype(f32)))
    h = (hid @ fwout.astype(f32)).reshape(seq_len, 2, HYENA_ORDER, d)
    deltas = jnp.abs(jnp.linspace(HYENA_DECAY_MIN, HYENA_DECAY_MAX, d, dtype=f32))
    h = h * jnp.exp(-t[:, None] * deltas[None, :])[:, None, None, :]
    full = jnp.concatenate(
        [h[:, 0], jnp.zeros((1, HYENA_ORDER, d), f32), h[: seq_len - 1, 1][::-1]], axis=0)
    full = full / jnp.sum(jnp.abs(full), axis=0, keepdims=True)
    return jnp.fft.rfft(full, axis=0)


def fft_long_conv(u, h_freq, skip):
    l = u.shape[1]
    uf = u.astype(jnp.float32)
    y = jnp.fft.irfft(jnp.fft.rfft(uf, n=2 * l, axis=1) * h_freq[None], n=2 * l, axis=1)[:, :l]
    return (y + uf * skip.astype(jnp.float32)).astype(u.dtype)


def hyena_mix(h, n_rows, w_in, conv_w, fw1, fb1, ffreq, fw2, fb2, fwout, fskip, w_out):
    l = h.shape[1]
    v, x1, x2 = jnp.split(short_conv3(h @ w_in, conv_w, n_rows), 3, axis=-1)
    h_freq = hyena_filters(l, fw1, fb1, ffreq, fw2, fb2, fwout)
    z = x1 * fft_long_conv(v, h_freq[:, 0], fskip[0])
    z = x2 * fft_long_conv(z, h_freq[:, 1], fskip[1])
    return z @ w_out


def chunk_gla(q, k, v, g, s0):
    out_dtype = v.dtype
    q, k, v, g = (a.astype(jnp.float32) for a in (q, k, v, g))
    b, nh, l, _ = q.shape
    dv = v.shape[-1]
    n = l // CHUNK

    def to_chunks(a):
        return jnp.moveaxis(a.reshape(b, nh, n, CHUNK, a.shape[-1]), 2, 0)

    mask = jnp.tril(jnp.ones((CHUNK, CHUNK), bool))[:, :, None]

    def step(s, inp):
        qi, ki, vi, gi = inp
        bcum = jnp.cumsum(gi, axis=2)
        diff = bcum[:, :, :, None, :] - bcum[:, :, None, :, :]
        decay = jnp.exp(jnp.where(mask, diff, -jnp.inf))
        att = jnp.einsum('bhtd,bhsd,bhtsd->bhts', qi, ki, decay)
        o = (jnp.einsum('bhts,bhsv->bhtv', att, vi)
             + jnp.einsum('bhtd,bhdv->bhtv', qi * jnp.exp(bcum), s))
        btot = bcum[:, :, -1:, :]
        s_new = (jnp.exp(btot[:, :, 0, :])[..., None] * s
                 + jnp.einsum('bhsd,bhsv->bhdv', ki * jnp.exp(btot - bcum), vi))
        return s_new, o

    s_fin, oc = lax.scan(step, s0, tuple(to_chunks(a) for a in (q, k, v, g)))
    o = jnp.moveaxis(oc, 0, 2).reshape(b, nh, l, dv)
    return o.astype(out_dtype), s_fin


def prefix_scan(ctx_in, lat_in, reverse):
    if reverse:
        ctx_in = tuple(jnp.flip(a, axis=2) for a in ctx_in)
        lat_in = tuple(jnp.flip(a, axis=2) for a in lat_in)
    b, nh, _, dk = ctx_in[0].shape
    s0 = jnp.zeros((b, nh, dk, ctx_in[2].shape[-1]), jnp.float32)
    o_ctx, s_ctx = chunk_gla(*ctx_in, s0)
    o_lat, _ = chunk_gla(*lat_in, s_ctx)
    if reverse:
        o_ctx, o_lat = jnp.flip(o_ctx, axis=2), jnp.flip(o_lat, axis=2)
    return o_ctx, o_lat


def bidir_recurrence(ctx_dirs, lat_dirs):
    oc_f, ol_f = prefix_scan(ctx_dirs[0], lat_dirs[0], False)
    oc_b, ol_b = prefix_scan(ctx_dirs[1], lat_dirs[1], True)
    return from_heads(oc_f + oc_b), from_heads(ol_f + ol_b)


def hgrn2_inputs(h, w_in, lb):
    q, i_in, gate, f_fwd, f_bwd = jnp.split(h @ w_in, 5, axis=-1)
    q = to_heads(jax.nn.silu(q), HGRN_HEADS)
    v = to_heads(i_in, HGRN_HEADS)
    dirs = []
    for f_raw, lb_d in ((f_fwd, lb[0]), (f_bwd, lb[1])):
        log_f = jnp.logaddexp(jnp.log(lb_d),
                              jnp.log1p(-lb_d) + jax.nn.log_sigmoid(f_raw.astype(jnp.float32)))
        dirs.append((q, to_heads(-jnp.expm1(log_f), HGRN_HEADS), v, to_heads(log_f, HGRN_HEADS)))
    return dirs, gate


def hgrn2_mix(h_ctx, h_lat, layer_idx, w_in, lb_logits, onorm_g, w_out):
    lb_cum = jnp.cumsum(jax.nn.softmax(lb_logits.astype(jnp.float32), axis=1), axis=1)
    lb = lb_cum[:, layer_idx] - lb_cum[:, 0]
    ctx_dirs, g_ctx = hgrn2_inputs(h_ctx, w_in, lb)
    lat_dirs, g_lat = hgrn2_inputs(h_lat, w_in, lb)
    o_ctx, o_lat = bidir_recurrence(ctx_dirs, lat_dirs)
    y_ctx = (head_rms_norm(o_ctx, onorm_g, HGRN_HEADS) * jax.nn.silu(g_ctx)) @ w_out
    y_lat = (head_rms_norm(o_lat, onorm_g, HGRN_HEADS) * jax.nn.silu(g_lat)) @ w_out
    return y_ctx, y_lat


def gla_inputs(h, w_in, w_up, b_up):
    k0 = GLA_KEY_DIM
    v0 = 2 * GLA_KEY_DIM
    g0 = v0 + GLA_VAL_DIM
    a0 = g0 + GLA_VAL_DIM
    q, k, v, gate, a_f, a_b = jnp.split(h @ w_in, [k0, v0, g0, a0, a0 + GLA_GATE_RANK], axis=-1)
    q = to_heads(q * GLA_DK ** -0.5, GLA_HEADS)
    k = to_heads(k, GLA_HEADS)
    v = to_heads(v, GLA_HEADS)
    dirs = []
    for dr, a in enumerate((a_f, a_b)):
        log_a = jax.nn.log_sigmoid((a @ w_up[dr] + b_up[dr]).astype(jnp.float32)) / GLA_GATE_NORM
        dirs.append((q, k, v, to_heads(log_a, GLA_HEADS)))
    return dirs, gate


def gla_mix(h_ctx, h_lat, w_in, w_up, b_up, onorm_g, w_out):
    ctx_dirs, g_ctx = gla_inputs(h_ctx, w_in, w_up, b_up)
    lat_dirs, g_lat = gla_inputs(h_lat, w_in, w_up, b_up)
    o_ctx, o_lat = bidir_recurrence(ctx_dirs, lat_dirs)
    y_ctx = (head_rms_norm(o_ctx, onorm_g, GLA_HEADS) * jax.nn.silu(g_ctx)) @ w_out
    y_lat = (head_rms_norm(o_lat, onorm_g, GLA_HEADS) * jax.nn.silu(g_lat)) @ w_out
    return y_ctx, y_lat


def setup_inputs(seed: int = 0) -> dict:
    key = jax.random.key(seed)
    keys = iter(jax.random.split(key, 40))

    def nrm(shape, scale):
        return jax.random.normal(next(keys), shape, jnp.float32) * scale

    d = D_MODEL
    return {
        'x': nrm((BATCH, SEQ, d), 1.0),
        'c': nrm((BATCH, d), 1.0),
        'ctx': nrm((BATCH, CTX_LEN, d), 1.0),
        'c_ctx': nrm((d,), 1.0),
        'w_mod': nrm((DEPTH, d, N_MOD * d), 0.5 * d ** -0.5),
        'b_mod': nrm((DEPTH, N_MOD * d), 0.01),
        'norm1_g': 1.0 + nrm((DEPTH, d), 0.01),
        'norm2_g': 1.0 + nrm((DEPTH, d), 0.01),
        'w_ffn_in': nrm((DEPTH, d, 2 * FFN_HIDDEN), d ** -0.5),
        'w_ffn_out': nrm((DEPTH, FFN_HIDDEN, d), FFN_HIDDEN ** -0.5),
        'final_g': 1.0 + nrm((d,), 0.01),
        'hy_w_in': nrm((N_HYENA, d, 3 * d), d ** -0.5),
        'hy_conv_w': nrm((N_HYENA, 3, 3 * d), 3 ** -0.5),
        'hy_fw1': nrm((N_HYENA, FILTER_EMB, FILTER_HIDDEN), FILTER_EMB ** -0.5),
        'hy_fb1': nrm((N_HYENA, FILTER_HIDDEN), 0.1),
        'hy_ffreq': 1.0 + nrm((N_HYENA, 2, FILTER_HIDDEN), 0.1),
        'hy_fw2': nrm((N_HYENA, FILTER_HIDDEN, FILTER_HIDDEN), FILTER_HIDDEN ** -0.5),
        'hy_fb2': nrm((N_HYENA, FILTER_HIDDEN), 0.1),
        'hy_fwout': nrm((N_HYENA, FILTER_HIDDEN, 2 * HYENA_ORDER * d), FILTER_HIDDEN ** -0.5),
        'hy_fskip': nrm((N_HYENA, HYENA_ORDER, d), 0.1),
        'hy_w_out': nrm((N_HYENA, d, d), d ** -0.5),
        'hg_w_in': nrm((N_HGRN, d, HGRN_IN_DIM), d ** -0.5),
        'hg_lb_logits': nrm((2, DEPTH, d), 0.1),
        'hg_onorm_g': 1.0 + nrm((N_HGRN, d), 0.01),
        'hg_w_out': nrm((N_HGRN, d, d), d ** -0.5),
        'gla_w_in': nrm((N_GLA, d, GLA_IN_DIM), d ** -0.5),
        'gla_w_up': nrm((N_GLA, 2, GLA_GATE_RANK, GLA_KEY_DIM), GLA_GATE_RANK ** -0.5),
        'gla_b_up': nrm((N_GLA, 2, GLA_KEY_DIM), 0.1),
        'gla_onorm_g': 1.0 + nrm((N_GLA, GLA_VAL_DIM), 0.01),
        'gla_w_out': nrm((N_GLA, GLA_VAL_DIM, d), GLA_VAL_DIM ** -0.5),
    }


def reference(x, c, ctx, c_ctx, w_mod, b_mod, norm1_g, norm2_g, w_ffn_in, w_ffn_out, final_g,
              hy_w_in, hy_conv_w, hy_fw1, hy_fb1, hy_ffreq, hy_fw2, hy_fb2, hy_fwout, hy_fskip,
              hy_w_out, hg_w_in, hg_lb_logits, hg_onorm_g, hg_w_out,
              gla_w_in, gla_w_up, gla_b_up, gla_onorm_g, gla_w_out):
    rows = x.shape[1] // GRID_W
    sc = jax.nn.silu(c)[:, None, :]
    sc_ctx = jax.nn.silu(c_ctx)[None, None, :]
    for i in range(DEPTH):
        last = i == DEPTH - 1
        kind, j = i % N_MIXERS, i // N_MIXERS
        mod_lat = jnp.split(sc @ w_mod[i] + b_mod[i], N_MOD, axis=-1)
        mod_ctx = jnp.split(sc_ctx @ w_mod[i] + b_mod[i], N_MOD, axis=-1)
        h_lat = modulate(rms_norm(x, norm1_g[i]), mod_lat[0], mod_lat[1])
        need_ctx = (not last) or kind != 0
        h_ctx = modulate(rms_norm(ctx, norm1_g[i]), mod_ctx[0], mod_ctx[1]) if need_ctx else None
        if kind == 0:
            hy = (hy_w_in[j], hy_conv_w[j], hy_fw1[j], hy_fb1[j], hy_ffreq[j], hy_fw2[j],
                  hy_fb2[j], hy_fwout[j], hy_fskip[j], hy_w_out[j])
            y_lat = hyena_mix(h_lat, rows, *hy)
            y_ctx = hyena_mix(h_ctx, 1, *hy) if need_ctx else None
        elif kind == 1:
            y_ctx, y_lat = hgrn2_mix(h_ctx, h_lat, i, hg_w_in[j], hg_lb_logits,
                                     hg_onorm_g[j], hg_w_out[j])
        else:
            y_ctx, y_lat = gla_mix(h_ctx, h_lat, gla_w_in[j], gla_w_up[j], gla_b_up[j],
                                   gla_onorm_g[j], gla_w_out[j])
        x = x + mod_lat[2] * y_lat
        x = x + mod_lat[5] * swiglu(modulate(rms_norm(x, norm2_g[i]), mod_lat[3], mod_lat[4]),
                                    w_ffn_in[i], w_ffn_out[i])
        if not last:
            ctx = ctx + mod_ctx[2] * y_ctx
            ctx = ctx + mod_ctx[5] * swiglu(
                modulate(rms_norm(ctx, norm2_g[i]), mod_ctx[3], mod_ctx[4]),
                w_ffn_in[i], w_ffn_out[i])
    return rms_norm(x, final_g)
```

```python
import functools
import math

import ml_dtypes
import numpy as np
import jax
import jax.numpy as jnp
from jax import lax
from jax.experimental import pallas as pl
from jax.experimental.pallas import tpu as pltpu

F32 = jnp.float32
BF16 = jnp.bfloat16

N_MOD = 6
N_MIXERS = 3
RMS_EPS = 1e-6
GRID_W = 64
HYENA_ORDER = 2
FILTER_BANDS = 16
FILTER_EMB = 1 + 2 * FILTER_BANDS
HYENA_DECAY_MIN = math.log(1e-2) / 1.5
HYENA_DECAY_MAX = math.log(1e-2) / 0.3
HGRN_EXPAND = 128
GLA_HEADS = 4
GLA_GATE_RANK = 16
GLA_GATE_NORM = 16.0

LANES = 128
V7X_VMEM_LIMIT = 56 * 1024 * 1024

FFT_P = 128
REC_CHUNK = 64
MOD_ROWS = 8


def _cparams(*sem):
    return pltpu.CompilerParams(dimension_semantics=sem, vmem_limit_bytes=V7X_VMEM_LIMIT)


def _dot(a, b):
    return jnp.dot(a, b, preferred_element_type=F32)


def _dot_nt(a, b):
    return lax.dot_general(a, b, (((1,), (1,)), ((), ())), preferred_element_type=F32)


def _dot_tn(a, b):
    return lax.dot_general(a, b, (((0,), (0,)), ((), ())), preferred_element_type=F32)


def _split(x):
    hi = x.astype(BF16)
    lo = (x - hi.astype(F32)).astype(BF16)
    return hi, lo


def _stack3(x, pad_rows=0):
    hi, lo = _split(x)
    parts = [hi, lo, hi]
    if pad_rows:
        parts.append(jnp.zeros((pad_rows, x.shape[1]), BF16))
    return jnp.concatenate(parts, axis=0)


def _const3(c, pad_cols=0):
    hi = c.astype(ml_dtypes.bfloat16)
    lo = (c - hi.astype(np.float64)).astype(ml_dtypes.bfloat16)
    parts = [hi, hi, lo]
    if pad_cols:
        parts.append(np.zeros(c.shape[:-1] + (pad_cols,), ml_dtypes.bfloat16))
    return jnp.asarray(np.concatenate(parts, axis=-1))


def _dot3(a, bh, bl):
    ah, al = _split(a)
    return _dot(ah, bh) + _dot(ah, bl) + _dot(al, bh)


def _sigmoid(x):
    return jax.nn.sigmoid(x)


def _normmod(x, g, sh, sc):
    ms = jnp.mean(x * x, axis=-1, keepdims=True)
    y = x * lax.rsqrt(ms + RMS_EPS) * g
    return y * (1.0 + sc) + sh


def _mod_kernel(c_ref, w_ref, b_ref, o_ref):
    c = c_ref[...]
    s = (c * _sigmoid(c)).astype(BF16)
    o_ref[...] = _dot(s, w_ref[...].astype(BF16)) + b_ref[...]


def _modulation(c8, w_mod, b_mod):
    depth, d, n = w_mod.shape
    tn = 1024
    return pl.pallas_call(
        _mod_kernel,
        grid=(depth, n // tn),
        in_specs=[pl.BlockSpec((MOD_ROWS, d), lambda l, j: (0, 0)),
                  pl.BlockSpec((None, d, tn), lambda l, j: (l, 0, j)),
                  pl.BlockSpec((None, 1, tn), lambda l, j: (l, 0, j))],
        out_specs=pl.BlockSpec((None, MOD_ROWS, tn), lambda l, j: (l, 0, j)),
        out_shape=jax.ShapeDtypeStruct((depth, MOD_ROWS, n), F32),
        compiler_params=_cparams("parallel", "parallel"),
        name="modulation",
    )(c8, w_mod, b_mod.reshape(depth, 1, n))


def _mod_spec(d, bmap):
    return pl.BlockSpec((None, 1, d), lambda *idx: (bmap(idx[0]), 0, 0))


def _normmod_kernel(x_ref, g_ref, sh_ref, sc_ref, o_ref):
    o_ref[...] = _normmod(x_ref[...], g_ref[...], sh_ref[...], sc_ref[...]).astype(BF16)


def _norm_modulate(x, g, sh, sc, bmap, tm):
    m, d = x.shape
    return pl.pallas_call(
        _normmod_kernel,
        grid=(m // tm,),
        in_specs=[pl.BlockSpec((tm, d), lambda i: (i, 0)),
                  pl.BlockSpec((1, d), lambda i: (0, 0)),
                  _mod_spec(d, bmap), _mod_spec(d, bmap)],
        out_specs=pl.BlockSpec((tm, d), lambda i: (i, 0)),
        out_shape=jax.ShapeDtypeStruct((m, d), BF16),
        compiler_params=_cparams("parallel"),
        name="norm_modulate",
    )(x, g, sh, sc)


def _final_norm_kernel(x_ref, g_ref, o_ref):
    x = x_ref[...]
    ms = jnp.mean(x * x, axis=-1, keepdims=True)
    o_ref[...] = x * lax.rsqrt(ms + RMS_EPS) * g_ref[...]


def _final_norm(x, g, tm):
    m, d = x.shape
    return pl.pallas_call(
        _final_norm_kernel,
        grid=(m // tm,),
        in_specs=[pl.BlockSpec((tm, d), lambda i: (i, 0)), pl.BlockSpec((1, d), lambda i: (0, 0))],
        out_specs=pl.BlockSpec((tm, d), lambda i: (i, 0)),
        out_shape=jax.ShapeDtypeStruct((m, d), F32),
        compiler_params=_cparams("parallel"),
        name="final_norm",
    )(x, g)


def _matmul_kernel(a_ref, w_ref, o_ref):
    o_ref[...] = _dot(a_ref[...], w_ref[...])


def _matmul(a, w, tm, tn):
    m, k = a.shape
    n = w.shape[1]
    return pl.pallas_call(
        _matmul_kernel,
        grid=(m // tm, n // tn),
        in_specs=[pl.BlockSpec((tm, k), lambda i, j: (i, 0)),
                  pl.BlockSpec((k, tn), lambda i, j: (0, j))],
        out_specs=pl.BlockSpec((tm, tn), lambda i, j: (i, j)),
        out_shape=jax.ShapeDtypeStruct((m, n), F32),
        compiler_params=_cparams("parallel", "parallel"),
        name="projection",
    )(a, w)


def _out_res_kernel(a_ref, w_ref, x_ref, mg_ref, o_ref):
    o_ref[...] = x_ref[...] + mg_ref[...] * _dot(a_ref[...], w_ref[...])


def _ffn_kernel(x_ref, g_ref, sh_ref, sc_ref, mg_ref, wg_ref, wu_ref, wo_ref, o_ref, h_s, acc_s):
    j = pl.program_id(1)

    @pl.when(j == 0)
    def _():
        h_s[...] = _normmod(x_ref[...], g_ref[...], sh_ref[...], sc_ref[...]).astype(BF16)
        acc_s[...] = jnp.zeros_like(acc_s)

    h = h_s[...]
    a = _dot(h, wg_ref[...])
    u = _dot(h, wu_ref[...])
    act = (a * _sigmoid(a) * u).astype(BF16)
    acc_s[...] += _dot(act, wo_ref[...])

    @pl.when(j == pl.num_programs(1) - 1)
    def _():
        o_ref[...] = x_ref[...] + mg_ref[...] * acc_s[...]


def _ffn(x, g, sh, sc, mg, w_in, w_out, bmap, tm, tf):
    m, d = x.shape
    f = w_out.shape[0]
    nf = f // tf
    return pl.pallas_call(
        _ffn_kernel,
        grid=(m // tm, nf),
        in_specs=[pl.BlockSpec((tm, d), lambda i, j: (i, 0)),
                  pl.BlockSpec((1, d), lambda i, j: (0, 0)),
                  _mod_spec(d, bmap), _mod_spec(d, bmap), _mod_spec(d, bmap),
                  pl.BlockSpec((d, tf), lambda i, j: (0, j)),
                  pl.BlockSpec((d, tf), lambda i, j: (0, nf + j)),
                  pl.BlockSpec((tf, d), lambda i, j: (j, 0))],
        out_specs=pl.BlockSpec((tm, d), lambda i, j: (i, 0)),
        out_shape=jax.ShapeDtypeStruct((m, d), F32),
        scratch_shapes=[pltpu.VMEM((tm, d), BF16), pltpu.VMEM((tm, d), F32)],
        compiler_params=_cparams("parallel", "arbitrary"),
        name="ffn",
    )(x, g, sh, sc, mg, w_in, w_in, w_out)


def _hy_in_kernel(a_ref, w_ref, cw_ref, o_ref, *, row_len):
    acc = _dot(a_ref[...], w_ref[...])
    tm = acc.shape[0]
    rid = lax.broadcasted_iota(jnp.int32, (tm, 1), 0) % row_len
    up = jnp.where(rid == 0, 0.0, pltpu.roll(acc, 1, 0))
    dn = jnp.where(rid == row_len - 1, 0.0, pltpu.roll(acc, tm - 1, 0))
    cw = cw_ref[...]
    o_ref[...] = cw[0:1] * up + cw[1:2] * acc + cw[2:3] * dn


def _hyena_in_lat(h, w, cw, b, qh):
    m, d = h.shape
    p = FFT_P
    return pl.pallas_call(
        functools.partial(_hy_in_kernel, row_len=GRID_W),
        grid=(3, b * qh),
        in_specs=[pl.BlockSpec((p, d), lambda j, i: (i, 0)),
                  pl.BlockSpec((d, d), lambda j, i: (0, j)),
                  pl.BlockSpec((3, d), lambda j, i: (0, j))],
        out_specs=pl.BlockSpec((None, None, p, d), lambda j, i: (j, i // qh, 0, i % qh)),
        out_shape=jax.ShapeDtypeStruct((3, b, p, qh * d), F32),
        compiler_params=_cparams("parallel", "parallel"),
        name="hyena_in_lat",
    )(h, w, cw)


def _hyena_in_ctx(h, w, cw, b, lc):
    m, d = h.shape
    return pl.pallas_call(
        functools.partial(_hy_in_kernel, row_len=lc),
        grid=(3, b),
        in_specs=[pl.BlockSpec((lc, d), lambda j, i: (i, 0)),
                  pl.BlockSpec((d, d), lambda j, i: (0, j)),
                  pl.BlockSpec((3, d), lambda j, i: (0, j))],
        out_specs=pl.BlockSpec((None, lc, d), lambda j, i: (j, i, 0)),
        out_shape=jax.ShapeDtypeStruct((3, m, d), F32),
        compiler_params=_cparams("parallel", "parallel"),
        name="hyena_in_ctx",
    )(h, w, cw)


def _hyena_out_lat(z, w, x, mg, b, qh):
    m, d = x.shape
    p = FFT_P
    return pl.pallas_call(
        _out_res_kernel,
        grid=(b * qh,),
        in_specs=[pl.BlockSpec((None, p, d), lambda i: (i // qh, 0, i % qh)),
                  pl.BlockSpec((d, d), lambda i: (0, 0)),
                  pl.BlockSpec((p, d), lambda i: (i, 0)),
                  _mod_spec(d, lambda i: i // qh)],
        out_specs=pl.BlockSpec((p, d), lambda i: (i, 0)),
        out_shape=jax.ShapeDtypeStruct((m, d), F32),
        compiler_params=_cparams("parallel"),
        name="hyena_out_lat",
    )(z, w, x, mg)


def _out_res(a, w, x, mg, bmap, tm):
    m, d = x.shape
    return pl.pallas_call(
        _out_res_kernel,
        grid=(m // tm,),
        in_specs=[pl.BlockSpec((tm, d), lambda i: (i, 0)),
                  pl.BlockSpec((d, d), lambda i: (0, 0)),
                  pl.BlockSpec((tm, d), lambda i: (i, 0)),
                  _mod_spec(d, bmap)],
        out_specs=pl.BlockSpec((tm, d), lambda i: (i, 0)),
        out_shape=jax.ShapeDtypeStruct((m, d), F32),
        compiler_params=_cparams("parallel"),
        name="out_residual",
    )(a, w, x, mg)


def _filter_kernel(w1h_ref, w1l_ref, b1_ref, fq_ref, w2h_ref, w2l_ref, b2_ref,
                   wfh_ref, wfl_ref, wbh_ref, wbl_ref, dl_ref, h_ref, nrm_ref, *, seq, rows, perm_q):
    r = pl.program_id(1)
    ridx = r * rows + lax.broadcasted_iota(jnp.int32, (rows, 1), 0)
    if perm_q:
        n = FFT_P * (ridx % perm_q) + ridx // perm_q
    else:
        n = ridx
    pos = jnp.where(n < seq, n, 2 * seq - 1 - n)
    posf = pos.astype(F32)
    t = posf / float(max(seq - 1, 1))
    lane = lax.broadcasted_iota(jnp.int32, (rows, LANES), 1)
    band = jnp.where(lane <= FILTER_BANDS, lane, lane - FILTER_BANDS).astype(F32)
    ang = ((2.0 * math.pi / seq) * posf) * band
    z = jnp.where(lane == 0, t,
                  jnp.where(lane <= FILTER_BANDS, jnp.cos(ang),
                            jnp.where(lane <= 2 * FILTER_BANDS, -jnp.sin(ang), 0.0)))
    a1 = _dot3(z, w1h_ref[...], w1l_ref[...]) + b1_ref[...]
    hid = jnp.sin(fq_ref[0:1, :] * a1)
    a2 = _dot3(hid, w2h_ref[...], w2l_ref[...]) + b2_ref[...]
    hid = jnp.sin(fq_ref[1:2, :] * a2)
    hf = _dot3(hid, wfh_ref[...], wfl_ref[...])
    hb = _dot3(hid, wbh_ref[...], wbl_ref[...])
    sel = jnp.where(n < seq, hf, hb)
    h = jnp.where(n == seq, 0.0, sel * jnp.exp(-t * dl_ref[...]))
    h_ref[...] = h

    @pl.when(r == 0)
    def _():
        nrm_ref[...] = jnp.zeros_like(nrm_ref)

    nrm_ref[...] += jnp.sum(jnp.abs(h), axis=0, keepdims=True)


def _pad2(a, rows, cols):
    return jnp.pad(a, ((0, rows - a.shape[0]), (0, cols - a.shape[1])))


def _hyena_filter_time(fw1, fb1, ffreq, fw2, fb2, fwout, seq, perm_q):
    od = fwout.shape[1] // 2
    d = od // HYENA_ORDER
    n2 = 2 * seq
    rows = min(512, n2)
    ct = min(1024, od)
    hidden = LANES
    w1h, w1l = _split(_pad2(fw1, LANES, hidden))
    w2h, w2l = _split(_pad2(fw2, hidden, hidden))
    woh, wol = _split(_pad2(fwout, hidden, fwout.shape[1]))
    b1 = _pad2(fb1[None, :], 1, hidden)
    b2 = _pad2(fb2[None, :], 1, hidden)
    fq = _pad2(ffreq, 2, hidden)
    deltas = np.abs(np.linspace(HYENA_DECAY_MIN, HYENA_DECAY_MAX, d, dtype=np.float32))
    dl = jnp.asarray(np.tile(deltas, HYENA_ORDER)[None, :])
    nct = od // ct
    small = lambda shape: pl.BlockSpec(shape, lambda c, r: (0, 0))
    return pl.pallas_call(
        functools.partial(_filter_kernel, seq=seq, rows=rows, perm_q=perm_q),
        grid=(nct, n2 // rows),
        in_specs=[small((LANES, hidden)), small((LANES, hidden)), small((1, hidden)), small((2, hidden)),
                  small((hidden, hidden)), small((hidden, hidden)), small((1, hidden)),
                  pl.BlockSpec((hidden, ct), lambda c, r: (0, c)),
                  pl.BlockSpec((hidden, ct), lambda c, r: (0, c)),
                  pl.BlockSpec((hidden, ct), lambda c, r: (0, nct + c)),
                  pl.BlockSpec((hidden, ct), lambda c, r: (0, nct + c)),
                  pl.BlockSpec((1, ct), lambda c, r: (0, c))],
        out_specs=[pl.BlockSpec((rows, ct), lambda c, r: (r, c)),
                   pl.BlockSpec((1, ct), lambda c, r: (0, c))],
        out_shape=[jax.ShapeDtypeStruct((n2, od), F32), jax.ShapeDtypeStruct((1, od), F32)],
        compiler_params=_cparams("parallel", "arbitrary"),
        name="hyena_filter",
    )(w1h, w1l, b1, fq, w2h, w2l, b2, woh, wol, woh, wol, dl)


def _fft_consts(seq, k_in):
    n = 2 * seq
    p = FFT_P
    q = n // p
    qh = q // 2
    ka_used = qh + 1
    ka_pad = -(-ka_used // 8) * 8
    ka = np.arange(ka_used)
    tl = np.arange(p)
    th = np.arange(k_in)
    theta = 2 * np.pi * (ka[None, :, None] * th[None, None, :] / q + ka[None, :, None] * tl[:, None, None] / n)
    f1 = np.zeros((p, 2 * ka_pad, k_in))
    f1[:, :ka_used] = np.cos(theta)
    f1[:, ka_pad:ka_pad + ka_used] = -np.sin(theta)
    k3 = 3 * k_in
    f1c = _const3(f1, pad_cols=-(-k3 // LANES) * LANES - k3)
    ang = 2 * np.pi * np.outer(np.arange(p), np.arange(p)) / p
    cc, sc = np.cos(ang), np.sin(ang)
    m2f = _const3(np.block([[cc, sc], [-sc, cc]]))
    m2i = _const3(np.block([[cc, -sc], [sc, cc]]))
    tho = np.arange(qh)
    phi = 2 * np.pi * (tho[None, :, None] * ka[None, None, :] / q + ka[None, None, :] * tl[:, None, None] / n)
    wgt = np.where((ka == 0) | (ka == qh), 1.0, 2.0) / n
    g = np.zeros((p, qh, 2 * ka_pad))
    g[:, :, :ka_used] = wgt * np.cos(phi)
    g[:, :, ka_pad:ka_pad + ka_used] = -wgt * np.sin(phi)
    g3 = 6 * ka_pad
    gc = _const3(g, pad_cols=-(-g3 // LANES) * LANES - g3)
    return dict(q=q, qh=qh, ka_used=ka_used, ka_pad=ka_pad, f1=f1c, m2f=m2f, m2i=m2i, g=gc)


def _s1_kernel(x_ref, f_ref, ar_ref, ai_ref, *, group, ka_pad, pad_rows):
    for j in range(group):
        xx = _stack3(x_ref[j], pad_rows)
        a = _dot(f_ref[j], xx)
        ar_ref[j] = a[:ka_pad]
        ai_ref[j] = a[ka_pad:]


def _fft_stage1(x4, f1c, ka_pad, group, dt):
    b, p, k_in, d = x4.shape
    kc = f1c.shape[-1]
    out = jax.ShapeDtypeStruct((b, p, ka_pad, d), F32)
    ospec = pl.BlockSpec((None, group, ka_pad, dt), lambda bi, g, c: (bi, g, 0, c))
    return pl.pallas_call(
        functools.partial(_s1_kernel, group=group, ka_pad=ka_pad, pad_rows=kc - 3 * k_in),
        grid=(b, p // group, d // dt),
        in_specs=[pl.BlockSpec((None, group, k_in, dt), lambda bi, g, c: (bi, g, 0, c)),
                  pl.BlockSpec((group, 2 * ka_pad, kc), lambda bi, g, c: (g, 0, 0))],
        out_specs=[ospec, ospec],
        out_shape=[out, out],
        compiler_params=_cparams("parallel", "parallel", "parallel"),
        name="fft_stage1",
    )(x4, f1c)


def _s2_filter_kernel(ar_ref, ai_ref, m_ref, inv_ref, hr_ref, hi_ref, *, p):
    x = jnp.concatenate([ar_ref[...], ai_ref[...]], axis=0)
    y = _dot(m_ref[...], _stack3(x)) * inv_ref[...]
    hr_ref[...] = y[:p]
    hi_ref[...] = y[p:]


def _fft_stage2_filter(ar, ai, m2f, inv_nrm, dt):
    ka, p, d = ar.shape
    spec = pl.BlockSpec((None, p, dt), lambda k, c: (k, 0, c))
    out = jax.ShapeDtypeStruct((ka, p, d), F32)
    return pl.pallas_call(
        functools.partial(_s2_filter_kernel, p=p),
        grid=(ka, d // dt),
        in_specs=[spec, spec,
                  pl.BlockSpec((2 * p, 6 * p), lambda k, c: (0, 0)),
                  pl.BlockSpec((1, dt), lambda k, c: (0, c))],
        out_specs=[spec, spec],
        out_shape=[out, out],
        compiler_params=_cparams("parallel", "parallel"),
        name="fft_stage2_filter",
    )(ar, ai, m2f, inv_nrm)


def _s2_kernel(ar_ref, ai_ref, hr_ref, hi_ref, mf_ref, mi_ref, cr_ref, ci_ref, *, p, ka_used):
    ka = pl.program_id(1)

    @pl.when(ka < ka_used)
    def _():
        x = jnp.concatenate([ar_ref[...], ai_ref[...]], axis=0)
        xf = _dot(mf_ref[...], _stack3(x))
        xr, xi = xf[:p], xf[p:]
        hr, hi = hr_ref[...], hi_ref[...]
        y = jnp.concatenate([xr * hr - xi * hi, xr * hi + xi * hr], axis=0)
        c = _dot(mi_ref[...], _stack3(y))
        cr_ref[...] = c[:p]
        ci_ref[...] = c[p:]

    @pl.when(ka >= ka_used)
    def _():
        cr_ref[...] = jnp.zeros_like(cr_ref)
        ci_ref[...] = jnp.zeros_like(ci_ref)


def _fft_stage2(ar, ai, hr, hi, m2f, m2i, ka_used, h_col0, dt):
    b, ka, p, d = ar.shape
    hc = h_col0 // dt
    aspec = pl.BlockSpec((None, None, p, dt), lambda bi, k, c: (bi, k, 0, c))
    hspec = pl.BlockSpec((None, p, dt), lambda bi, k, c: (k, 0, hc + c))
    mspec = pl.BlockSpec((2 * p, 6 * p), lambda bi, k, c: (0, 0))
    out = jax.ShapeDtypeStruct((b, ka, p, d), F32)
    return pl.pallas_call(
        functools.partial(_s2_kernel, p=p, ka_used=ka_used),
        grid=(b, ka, d // dt),
        in_specs=[aspec, aspec, hspec, hspec, mspec, mspec],
        out_specs=[aspec, aspec],
        out_shape=[out, out],
        compiler_params=_cparams("parallel", "parallel", "parallel"),
        name="fft_stage2",
    )(ar, ai, hr, hi, m2f, m2i)


def _s3_kernel(cr_ref, ci_ref, g_ref, v_ref, x_ref, sk_ref, z_ref, *, group, pad_rows):
    sk = sk_ref[...]
    for j in range(group):
        c = jnp.concatenate([cr_ref[j], ci_ref[j]], axis=0)
        y = _dot(g_ref[j], _stack3(c, pad_rows))
        v = v_ref[j]
        z_ref[j] = ((y + v * sk) * x_ref[j]).astype(z_ref.dtype)


def _fft_stage3(cr, ci, gc, v4, x4, skip, group, dt, out_dtype):
    b, p, ka, d = cr.shape
    qh = v4.shape[2]
    kc = gc.shape[-1]
    cspec = pl.BlockSpec((None, group, ka, dt), lambda bi, g, c: (bi, g, 0, c))
    vspec = pl.BlockSpec((None, group, qh, dt), lambda bi, g, c: (bi, g, 0, c))
    return pl.pallas_call(
        functools.partial(_s3_kernel, group=group, pad_rows=kc - 6 * ka),
        grid=(b, p // group, d // dt),
        in_specs=[cspec, cspec,
                  pl.BlockSpec((group, qh, kc), lambda bi, g, c: (g, 0, 0)),
                  vspec, vspec,
                  pl.BlockSpec((1, dt), lambda bi, g, c: (0, c))],
        out_specs=vspec,
        out_shape=jax.ShapeDtypeStruct((b, p, qh, d), out_dtype),
        compiler_params=_cparams("parallel", "parallel", "parallel"),
        name="fft_stage3",
    )(cr, ci, gc, v4, x4, skip)


def _long_conv_lat(v4, x4, hr, hi, skip, plan, order, out_dtype):
    d = v4.shape[-1]
    dt1 = min(1024, d)
    dt2 = min(512, d)
    ar, ai = _fft_stage1(v4, plan["f1"], plan["ka_pad"], 16, dt1)
    ar, ai = (jnp.transpose(a, (0, 2, 1, 3)) for a in (ar, ai))
    cr, ci = _fft_stage2(ar, ai, hr, hi, plan["m2f"], plan["m2i"], plan["ka_used"], order * d, dt2)
    cr, ci = (jnp.transpose(a, (0, 2, 1, 3)) for a in (cr, ci))
    return _fft_stage3(cr, ci, plan["g"], v4, x4, skip, 16, dt1, out_dtype)


def _filter_spectrum_lat(hy, seq, plan_f):
    fw1, fb1, ffreq, fw2, fb2, fwout = hy
    q = plan_f["q"]
    h_time, nrm = _hyena_filter_time(fw1, fb1, ffreq, fw2, fb2, fwout, seq, q)
    od = h_time.shape[1]
    h4 = h_time.reshape(1, FFT_P, q, od)
    ar, ai = _fft_stage1(h4, plan_f["f1"], plan_f["ka_pad"], 8, min(1024, od))
    ar, ai = (jnp.transpose(a[0], (1, 0, 2)) for a in (ar, ai))
    return _fft_stage2_filter(ar, ai, plan_f["m2f"], 1.0 / nrm, min(512, od))


def _dense_consts(seq):
    n = 2 * seq
    kf = seq + 1
    kf_pad = -(-kf // LANES) * LANES
    k = np.arange(kf)
    fwd_full = np.zeros((2 * kf_pad, n))
    ang = 2 * np.pi * np.outer(k, np.arange(n)) / n
    fwd_full[:kf] = np.cos(ang)
    fwd_full[kf_pad:kf_pad + kf] = -np.sin(ang)
    wgt = np.where((k == 0) | (k == seq), 1.0, 2.0) / n
    inv = np.zeros((seq, 2 * kf_pad))
    angi = 2 * np.pi * np.outer(np.arange(seq), k) / n
    inv[:, :kf] = wgt * np.cos(angi)
    inv[:, kf_pad:kf_pad + kf] = -wgt * np.sin(angi)
    return dict(kf_pad=kf_pad, fwd_full=_const3(fwd_full), fwd=_const3(fwd_full[:, :seq]), inv=_const3(inv))


def _dense_spec_kernel(h_ref, f_ref, inv_ref, hr_ref, hi_ref, *, kf_pad):
    y = _dot(f_ref[...], _stack3(h_ref[...])) * inv_ref[...]
    hr_ref[...] = y[:kf_pad]
    hi_ref[...] = y[kf_pad:]


def _filter_spectrum_ctx(hy, seq, cons):
    fw1, fb1, ffreq, fw2, fb2, fwout = hy
    h_time, nrm = _hyena_filter_time(fw1, fb1, ffreq, fw2, fb2, fwout, seq, 0)
    n2, od = h_time.shape
    kf_pad = cons["kf_pad"]
    ct = min(512, od)
    out = jax.ShapeDtypeStruct((kf_pad, od), F32)
    ospec = pl.BlockSpec((kf_pad, ct), lambda c: (0, c))
    return pl.pallas_call(
        functools.partial(_dense_spec_kernel, kf_pad=kf_pad),
        grid=(od // ct,),
        in_specs=[pl.BlockSpec((n2, ct), lambda c: (0, c)),
                  pl.BlockSpec((2 * kf_pad, 3 * n2), lambda c: (0, 0)),
                  pl.BlockSpec((1, ct), lambda c: (0, c))],
        out_specs=[ospec, ospec],
        out_shape=[out, out],
        compiler_params=_cparams("parallel"),
        name="dense_filter_spectrum",
    )(h_time, cons["fwd_full"], 1.0 / nrm)


def _dense_conv_kernel(v_ref, x1_ref, x2_ref, h1r_ref, h1i_ref, h2r_ref, h2i_ref, sk_ref,
                       f_ref, g_ref, z_ref, *, kf_pad):
    def conv(u, hr, hi):
        s = _dot(f_ref[...], _stack3(u))
        sr, si = s[:kf_pad], s[kf_pad:]
        y = jnp.concatenate([sr * hr - si * hi, sr * hi + si * hr], axis=0)
        return _dot(g_ref[...], _stack3(y))

    v = v_ref[...]
    z1 = x1_ref[...] * (conv(v, h1r_ref[...], h1i_ref[...]) + v * sk_ref[0:1, :])
    z2 = x2_ref[...] * (conv(z1, h2r_ref[...], h2i_ref[...]) + z1 * sk_ref[1:2, :])
    z_ref[...] = z2.astype(z_ref.dtype)


def _hyena_core_ctx(u3, hr, hi, fskip, cons, b, seq):
    d = u3.shape[-1]
    dt = min(256, d)
    nd = d // dt
    kf_pad = cons["kf_pad"]
    uspec = lambda part: pl.BlockSpec((None, seq, dt), lambda bi, c: (part, bi, c))
    hspec = lambda order: pl.BlockSpec((kf_pad, dt), lambda bi, c: (0, order * nd + c))
    return pl.pallas_call(
        functools.partial(_dense_conv_kernel, kf_pad=kf_pad),
        grid=(b, nd),
        in_specs=[uspec(0), uspec(1), uspec(2), hspec(0), hspec(0), hspec(1), hspec(1),
                  pl.BlockSpec((HYENA_ORDER, dt), lambda bi, c: (0, c)),
                  pl.BlockSpec((2 * kf_pad, 3 * seq), lambda bi, c: (0, 0)),
                  pl.BlockSpec((seq, 6 * kf_pad), lambda bi, c: (0, 0))],
        out_specs=pl.BlockSpec((seq, dt), lambda bi, c: (bi, c)),
        out_shape=jax.ShapeDtypeStruct((b * seq, d), BF16),
        compiler_params=_cparams("parallel", "parallel"),
        name="hyena_core_ctx",
    )(u3, u3, u3, hr, hi, hr, hi, fskip, cons["fwd"], cons["inv"])


def _rec_consts(chunk):
    t = np.arange(chunk)
    coefs, masks = [], []
    for direction in (0, 1):
        if direction == 0:
            rows = [t[None, :] <= t[:, None], t[None, :] > t[:, None]]
        else:
            rows = [t[None, :] >= t[:, None], t[None, :] < t[:, None]]
        mk = [np.eye(chunk)]
        m = chunk // 2
        while m >= 1:
            blk = t // (2 * m)
            half = (t // m) % 2
            mid = blk * 2 * m + m
            e = np.zeros((chunk, chunk))
            for r in range(chunk):
                if direction == 0:
                    if half[r] == 1:
                        e[r, mid[r]:r + 1] = 1
                    else:
                        e[r, r + 1:mid[r]] = 1
                else:
                    if half[r] == 0:
                        e[r, r:mid[r]] = 1
                    else:
                        e[r, mid[r]:r] = 1
            same = blk[:, None] == blk[None, :]
            if direction == 0:
                mk.append(same & (half[:, None] == 1) & (half[None, :] == 0))
            else:
                mk.append(same & (half[:, None] == 0) & (half[None, :] == 1))
            rows.append(e)
            m //= 2
        rows.append(np.ones((16, chunk)))
        a = np.concatenate([np.asarray(r, np.float64) for r in rows], axis=0)
        a3 = np.concatenate([a, a, a], axis=1)
        pad = -(-a3.shape[1] // LANES) * LANES - a3.shape[1]
        a3 = np.pad(a3, ((0, 0), (0, pad)))
        coefs.append(a3)
        masks.append(np.stack([np.asarray(x, np.float32) for x in mk]))
    return (jnp.asarray(np.stack(coefs), dtype=BF16), jnp.asarray(np.stack(masks), dtype=F32))


def _rec_core(q_s, k_s, g, v_ref_val, coef_ref, mask_ref, s0_ref, o_ref, sfin_ref, st_s, ex_s,
              *, chunk, heads, dk, dv):
    c = pl.program_id(2)
    levels = int(math.log2(chunk))
    tot = (2 + levels) * chunk

    @pl.when(c == 0)
    def _():
        st_s[...] = s0_ref[...]

    g1 = g.astype(BF16)
    r1 = g - g1.astype(F32)
    g2 = r1.astype(BF16)
    g3 = (r1 - g2.astype(F32)).astype(BF16)
    pad = coef_ref.shape[-1] - 3 * chunk
    gs = jnp.concatenate([g1, g2, g3, jnp.zeros((pad, g.shape[1]), BF16)], axis=0)
    ex_s[...] = jnp.exp(_dot(coef_ref[...], gs))

    for h in range(heads):
        ks = slice(h * dk, (h + 1) * dk)
        vs = slice(h * dv, (h + 1) * dv)
        qh = q_s[:, ks]
        kh = k_s[:, ks]
        vh = v_ref_val[:, vs].astype(BF16)
        att = mask_ref[0] * _dot_nt(qh.astype(BF16), kh.astype(BF16))
        for lv in range(levels):
            e = ex_s[(2 + lv) * chunk:(3 + lv) * chunk, ks]
            att += mask_ref[1 + lv] * _dot_nt((qh * e).astype(BF16), (kh * e).astype(BF16))
        st = st_s[h]
        o = _dot(att.astype(BF16), vh)
        o += _dot_nt((qh * ex_s[0:chunk, ks]).astype(BF16), st.astype(BF16))
        o_ref[:, vs] = o
        kd = (kh * ex_s[chunk:2 * chunk, ks]).astype(BF16)
        st_s[h] = st * ex_s[tot:tot + 1, ks] + _dot_tn(vh, kd)

    @pl.when(c == pl.num_programs(2) - 1)
    def _():
        sfin_ref[...] = st_s[...]


def _hgrn_rec_kernel(q_ref, v_ref, f_ref, lb_ref, coef_ref, mask_ref, s0_ref, o_ref, sfin_ref,
                     q_s, k_s, st_s, ex_s, **kw):
    qr = q_ref[...]
    q_s[...] = qr * _sigmoid(qr)
    fr = f_ref[...]
    lb = lb_ref[...]
    g = jnp.log(lb + (1.0 - lb) * _sigmoid(fr))
    k_s[...] = (1.0 - lb) * _sigmoid(-fr)
    _rec_core(q_s, k_s, g, v_ref, coef_ref, mask_ref, s0_ref, o_ref, sfin_ref, st_s, ex_s, **kw)


def _gla_rec_kernel(q_ref, k_ref, v_ref, a_ref, wup_ref, bup_ref, coef_ref, mask_ref, s0_ref,
                    o_ref, sfin_ref, q_s, k_s, st_s, ex_s, *, qscale, **kw):
    q_s[...] = q_ref[...] * qscale
    k_s[...] = k_ref[...]
    xg = _dot(a_ref[...].astype(BF16), wup_ref[...]) + bup_ref[...]
    g = (jnp.minimum(xg, 0.0) - jnp.log(1.0 + jnp.exp(-jnp.abs(xg)))) * (1.0 / GLA_GATE_NORM)
    _rec_core(q_s, k_s, g, v_ref, coef_ref, mask_ref, s0_ref, o_ref, sfin_ref, st_s, ex_s, **kw)


def _rec_call(kind, proj, extra, s0, consts, b, seq, d, heads, dk, dv):
    chunk = REC_CHUNK
    nc = seq // chunk
    coef, masks = consts
    hk = heads * dk
    hv = heads * dv
    levels = int(math.log2(chunk))
    n_rows = (2 + levels) * chunk + 16

    def rmap(col):
        return lambda bi, di, c: (bi * nc + c + di * (nc - 1 - 2 * c), col)

    cspecs = [pl.BlockSpec((None, n_rows, coef.shape[-1]), lambda bi, di, c: (di, 0, 0)),
              pl.BlockSpec((None, levels + 1, chunk, chunk), lambda bi, di, c: (di, 0, 0, 0)),
              pl.BlockSpec((None, None, heads, dv, dk), lambda bi, di, c: (bi, di, 0, 0, 0))]
    if kind == "hgrn":
        lb = extra
        kernel = _hgrn_rec_kernel
        in_specs = [pl.BlockSpec((chunk, d), rmap(0)),
                    pl.BlockSpec((chunk, d), rmap(1)),
                    pl.BlockSpec((chunk, d), lambda bi, di, c: (bi * nc + c + di * (nc - 1 - 2 * c), 3 + di)),
                    pl.BlockSpec((None, 1, d), lambda bi, di, c: (di, 0, 0))] + cspecs
        args = (proj, proj, proj, lb, coef, masks, s0)
        kw = {}
    else:
        wup, bup = extra
        kernel = _gla_rec_kernel
        in_specs = [pl.BlockSpec((chunk, hk), rmap(0)),
                    pl.BlockSpec((chunk, hk), rmap(1)),
                    pl.BlockSpec((chunk, hv), rmap(1)),
                    pl.BlockSpec((chunk, LANES), rmap((2 * hk + 2 * hv) // LANES)),
                    pl.BlockSpec((None, LANES, hk), lambda bi, di, c: (di, 0, 0)),
                    pl.BlockSpec((None, 1, hk), lambda bi, di, c: (di, 0, 0))] + cspecs
        args = (proj, proj, proj, proj, wup, bup, coef, masks, s0)
        kw = dict(qscale=float(dk) ** -0.5)
    return pl.pallas_call(
        functools.partial(kernel, chunk=chunk, heads=heads, dk=dk, dv=dv, **kw),
        grid=(b, 2, nc),
        in_specs=in_specs,
        out_specs=[pl.BlockSpec((None, chunk, hv), lambda bi, di, c: (di, bi * nc + c + di * (nc - 1 - 2 * c), 0)),
                   pl.BlockSpec((None, None, heads, dv, dk), lambda bi, di, c: (bi, di, 0, 0, 0))],
        out_shape=[jax.ShapeDtypeStruct((2, b * seq, hv), F32),
                   jax.ShapeDtypeStruct((b, 2, heads, dv, dk), F32)],
        scratch_shapes=[pltpu.VMEM((chunk, hk), F32), pltpu.VMEM((chunk, hk), F32),
                        pltpu.VMEM((heads, dv, dk), F32), pltpu.VMEM((n_rows, hk), F32)],
        compiler_params=_cparams("parallel", "parallel", "arbitrary"),
        name=kind + "_recurrence",
    )(*args)


def _gated_out_kernel(o_ref, gate_ref, gn_ref, w_ref, x_ref, mg_ref, out_ref, *, heads):
    o = o_ref[0] + o_ref[1]
    d = o.shape[1]
    dh = d // heads
    parts = []
    for h in range(heads):
        seg = o[:, h * dh:(h + 1) * dh]
        ms = jnp.mean(seg * seg, axis=-1, keepdims=True)
        parts.append(seg * lax.rsqrt(ms + RMS_EPS))
    gate = gate_ref[...]
    y = jnp.concatenate(parts, axis=1) * gn_ref[...] * (gate * _sigmoid(gate))
    out_ref[...] = x_ref[...] + mg_ref[...] * _dot(y.astype(BF16), w_ref[...])


def _gated_out(o2, proj, gate_col, gn, w, x, mg, bmap, heads, tm):
    m, d = x.shape
    return pl.pallas_call(
        functools.partial(_gated_out_kernel, heads=heads),
        grid=(m // tm,),
        in_specs=[pl.BlockSpec((2, tm, d), lambda i: (0, i, 0)),
                  pl.BlockSpec((tm, d), lambda i: (i, gate_col)),
                  pl.BlockSpec((1, d), lambda i: (0, 0)),
                  pl.BlockSpec((d, d), lambda i: (0, 0)),
                  pl.BlockSpec((tm, d), lambda i: (i, 0)),
                  _mod_spec(d, bmap)],
        out_specs=pl.BlockSpec((tm, d), lambda i: (i, 0)),
        out_shape=jax.ShapeDtypeStruct((m, d), F32),
        compiler_params=_cparams("parallel"),
        name="gated_out",
    )(o2, proj, gn, w, x, mg)


def _recurrent_mix(kind, h_ctx, h_lat, w_in, extra, consts, b, lc, seq, d, heads, dk, dv, tn):
    proj_ctx = _matmul(h_ctx, w_in, min(1024, h_ctx.shape[0]), tn)
    proj_lat = _matmul(h_lat, w_in, min(1024, h_lat.shape[0]), tn)
    s0 = jnp.zeros((b, 2, heads, dv, dk), F32)
    o_ctx, s_ctx = _rec_call(kind, proj_ctx, extra, s0, consts, b, lc, d, heads, dk, dv)
    o_lat, _ = _rec_call(kind, proj_lat, extra, s_ctx, consts, b, seq, d, heads, dk, dv)
    return (proj_ctx, o_ctx), (proj_lat, o_lat)


def kernel(x, c, ctx, c_ctx, w_mod, b_mod, norm1_g, norm2_g, w_ffn_in, w_ffn_out, final_g, hy_w_in, hy_conv_w, hy_fw1, hy_fb1, hy_ffreq, hy_fw2, hy_fb2, hy_fwout, hy_fskip, hy_w_out, hg_w_in, hg_lb_logits, hg_onorm_g, hg_w_out, gla_w_in, gla_w_up, gla_b_up, gla_onorm_g, gla_w_out):
    b, seq, d = x.shape
    lc = ctx.shape[1]
    depth = w_mod.shape[0]
    assert b + 1 <= MOD_ROWS and seq % (FFT_P * 2) == 0 and FFT_P % GRID_W == 0
    m_lat, m_ctx = b * seq, b * lc
    ctx_row = b
    tm_lat, tm_ctx = 512, min(512, m_ctx)
    lat_map = lambda i, tm=tm_lat: (i * tm) // seq
    ctx_map = lambda i: ctx_row

    xl = x.reshape(m_lat, d)
    xc = ctx.reshape(m_ctx, d)
    c8 = jnp.zeros((MOD_ROWS, d), F32).at[:b].set(c).at[ctx_row].set(c_ctx)
    mod = _modulation(c8, w_mod, b_mod).reshape(depth, MOD_ROWS, N_MOD, 1, d)

    plan = _fft_consts(seq, seq // FFT_P)
    plan_f = _fft_consts(seq, 2 * seq // FFT_P)
    dense = _dense_consts(lc)
    rec_consts = _rec_consts(REC_CHUNK)
    qh = plan["qh"]

    for i in range(depth):
        last = i == depth - 1
        kind, j = i % N_MIXERS, i // N_MIXERS
        mv = [mod[i, :, k] for k in range(N_MOD)]
        g1 = norm1_g[i][None, :]
        need_ctx = (not last) or kind != 0
        h_lat = _norm_modulate(xl, g1, mv[0], mv[1], lat_map, tm_lat)
        h_ctx = _norm_modulate(xc, g1, mv[0], mv[1], ctx_map, tm_ctx) if need_ctx else None
        if kind == 0:
            w_in = hy_w_in[j].astype(BF16)
            w_out = hy_w_out[j].astype(BF16)
            hy = (hy_fw1[j], hy_fb1[j], hy_ffreq[j], hy_fw2[j], hy_fb2[j], hy_fwout[j])
            u = _hyena_in_lat(h_lat, w_in, hy_conv_w[j], b, qh).reshape(3, b, FFT_P, qh, d)
            hr, hi = _filter_spectrum_lat(hy, seq, plan_f)
            z1 = _long_conv_lat(u[0], u[1], hr, hi, hy_fskip[j, 0][None, :], plan, 0, F32)
            z2 = _long_conv_lat(z1, u[2], hr, hi, hy_fskip[j, 1][None, :], plan, 1, BF16)
            xl = _hyena_out_lat(z2.reshape(b, FFT_P, qh * d), w_out, xl, mv[2], b, qh)
            if need_ctx:
                uc = _hyena_in_ctx(h_ctx, w_in, hy_conv_w[j], b, lc)
                hcr, hci = _filter_spectrum_ctx(hy, lc, dense)
                zc = _hyena_core_ctx(uc, hcr, hci, hy_fskip[j], dense, b, lc)
                xc = _out_res(zc, w_out, xc, mv[2], ctx_map, tm_ctx)
        else:
            if kind == 1:
                heads = d // HGRN_EXPAND
                dk = dv = HGRN_EXPAND
                w_in = hg_w_in[j].astype(BF16)
                lb_cum = jnp.cumsum(jax.nn.softmax(hg_lb_logits.astype(F32), axis=1), axis=1)
                extra = (lb_cum[:, i] - lb_cum[:, 0])[:, None, :]
                gn, w_out, tn = hg_onorm_g[j], hg_w_out[j], 512
                rkind = "hgrn"
            else:
                heads = GLA_HEADS
                dk, dv = d // 2 // heads, d // heads
                n_in = gla_w_in.shape[-1]
                n_pad = -(-n_in // LANES) * LANES
                w_in = jnp.pad(gla_w_in[j], ((0, 0), (0, n_pad - n_in))).astype(BF16)
                r = GLA_GATE_RANK
                wup = jnp.zeros((2, LANES, heads * dk), F32)
                wup = wup.at[0, :r].set(gla_w_up[j, 0]).at[1, r:2 * r].set(gla_w_up[j, 1]).astype(BF16)
                extra = (wup, gla_b_up[j][:, None, :])
                gn, w_out = gla_onorm_g[j], gla_w_out[j]
                tn = n_pad // 7 if n_pad % (7 * LANES) == 0 else LANES
                rkind = "gla"
            (p_ctx, o_ctx), (p_lat, o_lat) = _recurrent_mix(
                rkind, h_ctx, h_lat, w_in, extra, rec_consts, b, lc, seq, d, heads, dk, dv, tn)
            gate_col = 2 if kind == 1 else (2 * heads * dk + heads * dv) // d
            w_out = w_out.astype(BF16)
            tm_g = 256
            xl = _gated_out(o_lat, p_lat, gate_col, gn[None, :], w_out, xl, mv[2],
                            lambda i: (i * tm_g) // seq, heads, tm_g)
            xc = _gated_out(o_ctx, p_ctx, gate_col, gn[None, :], w_out, xc, mv[2], ctx_map, heads, tm_g)
        g2 = norm2_g[i][None, :]
        wf_in = w_ffn_in[i].astype(BF16)
        wf_out = w_ffn_out[i].astype(BF16)
        xl = _ffn(xl, g2, mv[3], mv[4], mv[5], wf_in, wf_out, lat_map, tm_lat, 512)
        if not last:
            xc = _ffn(xc, g2, mv[3], mv[4], mv[5], wf_in, wf_out, ctx_map, tm_ctx, 512)
    return _final_norm(xl, final_g[None, :], tm_lat).reshape(b, seq, d)
```

```python
import functools
import math

import ml_dtypes
import numpy as np
import jax
import jax.numpy as jnp
from jax import lax
from jax.experimental import pallas as pl
from jax.experimental.pallas import tpu as pltpu

F32 = jnp.float32
BF16 = jnp.bfloat16

N_MOD = 6
N_MIXERS = 3
RMS_EPS = 1e-6
GRID_W = 64
HYENA_ORDER = 2
FILTER_BANDS = 16
FILTER_EMB = 1 + 2 * FILTER_BANDS
HYENA_DECAY_MIN = math.log(1e-2) / 1.5
HYENA_DECAY_MAX = math.log(1e-2) / 0.3
HGRN_EXPAND = 128
GLA_HEADS = 4
GLA_GATE_RANK = 16
GLA_GATE_NORM = 16.0

LANES = 128
SUBLANES = 8
V7X_VMEM_LIMIT = 56 * 1024 * 1024

FFT_P = 128
REC_CHUNK = 64
MOD_ROWS = 8


def _cparams(*sem):
    return pltpu.CompilerParams(dimension_semantics=sem, vmem_limit_bytes=V7X_VMEM_LIMIT)


def _dot(a, b):
    return jnp.dot(a, b, preferred_element_type=F32)


def _dot_nt(a, b):
    return lax.dot_general(a, b, (((1,), (1,)), ((), ())), preferred_element_type=F32)


def _dot_tn(a, b):
    return lax.dot_general(a, b, (((0,), (0,)), ((), ())), preferred_element_type=F32)


def _split(x):
    hi = x.astype(BF16)
    lo = (x - hi.astype(F32)).astype(BF16)
    return hi, lo


def _stack3(x, pad_rows=0):
    hi, lo = _split(x)
    parts = [hi, lo, hi]
    if pad_rows:
        parts.append(jnp.zeros((pad_rows, x.shape[1]), BF16))
    return jnp.concatenate(parts, axis=0)


def _const3(c, pad_cols=0):
    hi = c.astype(ml_dtypes.bfloat16)
    lo = (c - hi.astype(np.float64)).astype(ml_dtypes.bfloat16)
    parts = [hi, hi, lo]
    if pad_cols:
        parts.append(np.zeros(c.shape[:-1] + (pad_cols,), ml_dtypes.bfloat16))
    return jnp.asarray(np.concatenate(parts, axis=-1))


def _dot3(a, bh, bl):
    ah, al = _split(a)
    return _dot(ah, bh) + _dot(ah, bl) + _dot(al, bh)


def _sigmoid(x):
    return jax.nn.sigmoid(x)


def _normmod(x, g, sh, sc):
    ms = jnp.mean(x * x, axis=-1, keepdims=True)
    y = x * lax.rsqrt(ms + RMS_EPS) * g
    return y * (1.0 + sc) + sh


def _mod_kernel(c_ref, w_ref, b_ref, o_ref):
    c = c_ref[...]
    s = (c * _sigmoid(c)).astype(BF16)
    o_ref[...] = _dot(s, w_ref[...].astype(BF16)) + b_ref[...]


def _modulation(c8, w_mod, b_mod):
    depth, d, n = w_mod.shape
    tn = 1024
    return pl.pallas_call(
        _mod_kernel,
        grid=(depth, n // tn),
        in_specs=[pl.BlockSpec((MOD_ROWS, d), lambda l, j: (0, 0)),
                  pl.BlockSpec((None, d, tn), lambda l, j: (l, 0, j)),
                  pl.BlockSpec((None, 1, tn), lambda l, j: (l, 0, j))],
        out_specs=pl.BlockSpec((None, MOD_ROWS, tn), lambda l, j: (l, 0, j)),
        out_shape=jax.ShapeDtypeStruct((depth, MOD_ROWS, n), F32),
        compiler_params=_cparams("parallel", "parallel"),
        name="modulation",
    )(c8, w_mod, b_mod.reshape(depth, 1, n))


def _mod_spec(d, bmap):
    return pl.BlockSpec((None, 1, d), lambda *idx: (bmap(idx[0]), 0, 0))


def _normmod_kernel(x_ref, g_ref, sh_ref, sc_ref, o_ref):
    o_ref[...] = _normmod(x_ref[...], g_ref[...], sh_ref[...], sc_ref[...]).astype(BF16)


def _norm_modulate(x, g, sh, sc, bmap, tm):
    m, d = x.shape
    return pl.pallas_call(
        _normmod_kernel,
        grid=(m // tm,),
        in_specs=[pl.BlockSpec((tm, d), lambda i: (i, 0)),
                  pl.BlockSpec((1, d), lambda i: (0, 0)),
                  _mod_spec(d, bmap), _mod_spec(d, bmap)],
        out_specs=pl.BlockSpec((tm, d), lambda i: (i, 0)),
        out_shape=jax.ShapeDtypeStruct((m, d), BF16),
        compiler_params=_cparams("parallel"),
        name="norm_modulate",
    )(x, g, sh, sc)


def _final_norm_kernel(x_ref, g_ref, o_ref):
    x = x_ref[...]
    ms = jnp.mean(x * x, axis=-1, keepdims=True)
    o_ref[...] = x * lax.rsqrt(ms + RMS_EPS) * g_ref[...]


def _final_norm(x, g, tm):
    m, d = x.shape
    return pl.pallas_call(
        _final_norm_kernel,
        grid=(m // tm,),
        in_specs=[pl.BlockSpec((tm, d), lambda i: (i, 0)), pl.BlockSpec((1, d), lambda i: (0, 0))],
        out_specs=pl.BlockSpec((tm, d), lambda i: (i, 0)),
        out_shape=jax.ShapeDtypeStruct((m, d), F32),
        compiler_params=_cparams("parallel"),
        name="final_norm",
    )(x, g)


def _matmul_kernel(a_ref, w_ref, o_ref):
    o_ref[...] = _dot(a_ref[...], w_ref[...])


def _matmul(a, w, tm, tn):
    m, k = a.shape
    n = w.shape[1]
    return pl.pallas_call(
        _matmul_kernel,
        grid=(m // tm, n // tn),
        in_specs=[pl.BlockSpec((tm, k), lambda i, j: (i, 0)),
                  pl.BlockSpec((k, tn), lambda i, j: (0, j))],
        out_specs=pl.BlockSpec((tm, tn), lambda i, j: (i, j)),
        out_shape=jax.ShapeDtypeStruct((m, n), F32),
        compiler_params=_cparams("parallel", "parallel"),
        name="projection",
    )(a, w)


def _out_res_kernel(a_ref, w_ref, x_ref, mg_ref, o_ref):
    o_ref[...] = x_ref[...] + mg_ref[...] * _dot(a_ref[...], w_ref[...])


def _ffn_kernel(x_ref, g_ref, sh_ref, sc_ref, mg_ref, wg_ref, wu_ref, wo_ref, o_ref, h_s):
    j = pl.program_id(1)

    @pl.when(j == 0)
    def _():
        h_s[...] = _normmod(x_ref[...], g_ref[...], sh_ref[...], sc_ref[...]).astype(BF16)

    h = h_s[...]
    a = _dot(h, wg_ref[...])
    u = _dot(h, wu_ref[...])
    act = (a * _sigmoid(a) * u).astype(BF16)

    @pl.when(j == 0)
    def _():
        o_ref[...] = _dot(act, wo_ref[...])

    @pl.when(j > 0)
    def _():
        o_ref[...] += _dot(act, wo_ref[...])

    @pl.when(j == pl.num_programs(1) - 1)
    def _():
        o_ref[...] = x_ref[...] + mg_ref[...] * o_ref[...]


def _ffn(x, g, sh, sc, mg, w_in, w_out, bmap, tm, tf):
    m, d = x.shape
    f = w_out.shape[0]
    nf = f // tf
    return pl.pallas_call(
        _ffn_kernel,
        grid=(m // tm, nf),
        in_specs=[pl.BlockSpec((tm, d), lambda i, j: (i, 0), pipeline_mode=pl.Buffered(1)),
                  pl.BlockSpec((1, d), lambda i, j: (0, 0)),
                  _mod_spec(d, bmap), _mod_spec(d, bmap), _mod_spec(d, bmap),
                  pl.BlockSpec((d, tf), lambda i, j: (0, j)),
                  pl.BlockSpec((d, tf), lambda i, j: (0, nf + j)),
                  pl.BlockSpec((tf, d), lambda i, j: (j, 0))],
        out_specs=pl.BlockSpec((tm, d), lambda i, j: (i, 0)),
        out_shape=jax.ShapeDtypeStruct((m, d), F32),
        scratch_shapes=[pltpu.VMEM((tm, d), BF16)],
        compiler_params=_cparams("parallel", "arbitrary"),
        name="ffn",
    )(x, g, sh, sc, mg, w_in, w_in, w_out)


def _conv3_rows(acc, cw, row_len):
    tm = acc.shape[0]
    rid = lax.broadcasted_iota(jnp.int32, (tm, 1), 0) % row_len
    up = jnp.where(rid == 0, 0.0, pltpu.roll(acc, 1, 0))
    dn = jnp.where(rid == row_len - 1, 0.0, pltpu.roll(acc, tm - 1, 0))
    return cw[0:1] * up + cw[1:2] * acc + cw[2:3] * dn


def _hy_in_kernel(a_ref, w_ref, cw_ref, o_ref, *, row_len):
    o_ref[...] = _conv3_rows(_dot(a_ref[...], w_ref[...]), cw_ref[...], row_len)


def _hy_in_lat_kernel(a_ref, w_ref, cw_ref, o_ref, *, row_len, p):
    y = _conv3_rows(_dot(a_ref[...], w_ref[...]), cw_ref[...], row_len)
    tm, tn = y.shape
    o_ref[...] = jnp.swapaxes(y.reshape(tm // p, p, tn), 0, 1)


def _hyena_in_lat(h, w, cw, b, qh):
    m, d = h.shape
    p = FFT_P
    th = SUBLANES
    tn = min(512, d)
    nd = d // tn
    nq = qh // th
    return pl.pallas_call(
        functools.partial(_hy_in_lat_kernel, row_len=GRID_W, p=p),
        grid=(b * nq, 3 * nd),
        in_specs=[pl.BlockSpec((th * p, d), lambda i, j: (i, 0)),
                  pl.BlockSpec((d, tn), lambda i, j: (0, j)),
                  pl.BlockSpec((3, tn), lambda i, j: (0, j))],
        out_specs=pl.BlockSpec((None, None, p, th, tn), lambda i, j: (j // nd, i // nq, 0, i % nq, j % nd)),
        out_shape=jax.ShapeDtypeStruct((3, b, p, qh, d), F32),
        compiler_params=_cparams("parallel", "parallel"),
        name="hyena_in_lat",
    )(h, w, cw)


def _hyena_in_ctx(h, w, cw, b, lc):
    m, d = h.shape
    return pl.pallas_call(
        functools.partial(_hy_in_kernel, row_len=lc),
        grid=(3, b),
        in_specs=[pl.BlockSpec((lc, d), lambda j, i: (i, 0)),
                  pl.BlockSpec((d, d), lambda j, i: (0, j)),
                  pl.BlockSpec((3, d), lambda j, i: (0, j))],
        out_specs=pl.BlockSpec((None, lc, d), lambda j, i: (j, i, 0)),
        out_shape=jax.ShapeDtypeStruct((3, m, d), F32),
        compiler_params=_cparams("parallel", "parallel"),
        name="hyena_in_ctx",
    )(h, w, cw)


def _hy_out_lat_kernel(z_ref, w_ref, x_ref, mg_ref, o_ref, a_s):
    @pl.when(pl.program_id(1) == 0)
    def _():
        a_s[...] = jnp.swapaxes(z_ref[...], 0, 1).reshape(a_s.shape).astype(BF16)

    o_ref[...] = x_ref[...] + mg_ref[...] * _dot(a_s[...], w_ref[...])


def _hyena_out_lat(z, w, x, mg, b, qh):
    m, d = x.shape
    p = FFT_P
    th = SUBLANES
    tn = min(1024, d)
    nq = qh // th
    return pl.pallas_call(
        _hy_out_lat_kernel,
        grid=(b * nq, d // tn),
        in_specs=[pl.BlockSpec((None, p, th, d), lambda i, j: (i // nq, 0, i % nq, 0)),
                  pl.BlockSpec((d, tn), lambda i, j: (0, j)),
                  pl.BlockSpec((th * p, tn), lambda i, j: (i, j)),
                  pl.BlockSpec((None, 1, tn), lambda i, j: (i // nq, 0, j))],
        out_specs=pl.BlockSpec((th * p, tn), lambda i, j: (i, j)),
        out_shape=jax.ShapeDtypeStruct((m, d), F32),
        scratch_shapes=[pltpu.VMEM((th * p, d), BF16)],
        compiler_params=_cparams("parallel", "arbitrary"),
        name="hyena_out_lat",
    )(z, w, x, mg)


def _out_res(a, w, x, mg, bmap, tm):
    m, d = x.shape
    return pl.pallas_call(
        _out_res_kernel,
        grid=(m // tm,),
        in_specs=[pl.BlockSpec((tm, d), lambda i: (i, 0)),
                  pl.BlockSpec((d, d), lambda i: (0, 0)),
                  pl.BlockSpec((tm, d), lambda i: (i, 0)),
                  _mod_spec(d, bmap)],
        out_specs=pl.BlockSpec((tm, d), lambda i: (i, 0)),
        out_shape=jax.ShapeDtypeStruct((m, d), F32),
        compiler_params=_cparams("parallel"),
        name="out_residual",
    )(a, w, x, mg)


def _filter_kernel(w1h_ref, w1l_ref, b1_ref, fq_ref, w2h_ref, w2l_ref, b2_ref,
                   wfh_ref, wfl_ref, wbh_ref, wbl_ref, dl_ref, h_ref, nrm_ref, *, seq, rows, perm_q):
    r = pl.program_id(1)
    ridx = r * rows + lax.broadcasted_iota(jnp.int32, (rows, 1), 0)
    if perm_q:
        n = FFT_P * (ridx % perm_q) + ridx // perm_q
    else:
        n = ridx
    pos = jnp.where(n < seq, n, 2 * seq - 1 - n)
    posf = pos.astype(F32)
    t = posf / float(max(seq - 1, 1))
    lane = lax.broadcasted_iota(jnp.int32, (rows, LANES), 1)
    band = jnp.where(lane <= FILTER_BANDS, lane, lane - FILTER_BANDS).astype(F32)
    ang = ((2.0 * math.pi / seq) * posf) * band
    z = jnp.where(lane == 0, t,
                  jnp.where(lane <= FILTER_BANDS, jnp.cos(ang),
                            jnp.where(lane <= 2 * FILTER_BANDS, -jnp.sin(ang), 0.0)))
    a1 = _dot3(z, w1h_ref[...], w1l_ref[...]) + b1_ref[...]
    hid = jnp.sin(fq_ref[0:1, :] * a1)
    a2 = _dot3(hid, w2h_ref[...], w2l_ref[...]) + b2_ref[...]
    hid = jnp.sin(fq_ref[1:2, :] * a2)
    hf = _dot3(hid, wfh_ref[...], wfl_ref[...])
    hb = _dot3(hid, wbh_ref[...], wbl_ref[...])
    sel = jnp.where(n < seq, hf, hb)
    h = jnp.where(n == seq, 0.0, sel * jnp.exp(-t * dl_ref[...]))
    h_ref[...] = h

    @pl.when(r == 0)
    def _():
        nrm_ref[...] = jnp.zeros_like(nrm_ref)

    nrm_ref[...] += jnp.sum(jnp.abs(h), axis=0, keepdims=True)


def _pad2(a, rows, cols):
    return jnp.pad(a, ((0, rows - a.shape[0]), (0, cols - a.shape[1])))


def _hyena_filter_time(fw1, fb1, ffreq, fw2, fb2, fwout, seq, perm_q):
    od = fwout.shape[1] // 2
    d = od // HYENA_ORDER
    n2 = 2 * seq
    rows = min(512, n2)
    ct = min(1024, od)
    hidden = LANES
    w1h, w1l = _split(_pad2(fw1, LANES, hidden))
    w2h, w2l = _split(_pad2(fw2, hidden, hidden))
    woh, wol = _split(_pad2(fwout, hidden, fwout.shape[1]))
    b1 = _pad2(fb1[None, :], 1, hidden)
    b2 = _pad2(fb2[None, :], 1, hidden)
    fq = _pad2(ffreq, 2, hidden)
    deltas = np.abs(np.linspace(HYENA_DECAY_MIN, HYENA_DECAY_MAX, d, dtype=np.float32))
    dl = jnp.asarray(np.tile(deltas, HYENA_ORDER)[None, :])
    nct = od // ct
    small = lambda shape: pl.BlockSpec(shape, lambda c, r: (0, 0))
    return pl.pallas_call(
        functools.partial(_filter_kernel, seq=seq, rows=rows, perm_q=perm_q),
        grid=(nct, n2 // rows),
        in_specs=[small((LANES, hidden)), small((LANES, hidden)), small((1, hidden)), small((2, hidden)),
                  small((hidden, hidden)), small((hidden, hidden)), small((1, hidden)),
                  pl.BlockSpec((hidden, ct), lambda c, r: (0, c)),
                  pl.BlockSpec((hidden, ct), lambda c, r: (0, c)),
                  pl.BlockSpec((hidden, ct), lambda c, r: (0, nct + c)),
                  pl.BlockSpec((hidden, ct), lambda c, r: (0, nct + c)),
                  pl.BlockSpec((1, ct), lambda c, r: (0, c))],
        out_specs=[pl.BlockSpec((rows, ct), lambda c, r: (r, c)),
                   pl.BlockSpec((1, ct), lambda c, r: (0, c))],
        out_shape=[jax.ShapeDtypeStruct((n2, od), F32), jax.ShapeDtypeStruct((1, od), F32)],
        compiler_params=_cparams("parallel", "arbitrary"),
        name="hyena_filter",
    )(w1h, w1l, b1, fq, w2h, w2l, b2, woh, wol, woh, wol, dl)


def _fft_consts(seq, k_in):
    n = 2 * seq
    p = FFT_P
    q = n // p
    qh = q // 2
    ka_used = qh + 1
    ka_pad = -(-ka_used // 8) * 8
    ka = np.arange(ka_used)
    tl = np.arange(p)
    th = np.arange(k_in)
    theta = 2 * np.pi * (ka[None, :, None] * th[None, None, :] / q + ka[None, :, None] * tl[:, None, None] / n)
    f1 = np.zeros((p, 2 * ka_pad, k_in))
    f1[:, :ka_used] = np.cos(theta)
    f1[:, ka_pad:ka_pad + ka_used] = -np.sin(theta)
    k3 = 3 * k_in
    f1c = _const3(f1, pad_cols=-(-k3 // LANES) * LANES - k3)
    ang = 2 * np.pi * np.outer(np.arange(p), np.arange(p)) / p
    cc, sc = np.cos(ang), np.sin(ang)
    m2f_np, m2i_np = np.block([[cc, sc], [-sc, cc]]), np.block([[cc, -sc], [sc, cc]])
    m2f = _const3(m2f_np)
    m2f1, m2i1 = (jnp.asarray(m.astype(ml_dtypes.bfloat16)) for m in (m2f_np, m2i_np))
    tho = np.arange(qh)
    phi = 2 * np.pi * (tho[None, :, None] * ka[None, None, :] / q + ka[None, None, :] * tl[:, None, None] / n)
    wgt = np.where((ka == 0) | (ka == qh), 1.0, 2.0) / n
    g = np.zeros((p, qh, 2 * ka_pad))
    g[:, :, :ka_used] = wgt * np.cos(phi)
    g[:, :, ka_pad:ka_pad + ka_used] = -wgt * np.sin(phi)
    g3 = 6 * ka_pad
    gc = _const3(g, pad_cols=-(-g3 // LANES) * LANES - g3)
    return dict(q=q, qh=qh, ka_used=ka_used, ka_pad=ka_pad, f1=f1c, m2f=m2f, m2f1=m2f1, m2i1=m2i1, g=gc)


def _s1_kernel(x_ref, f_ref, ar_ref, ai_ref, a_s, *, group, ka_pad, pad_rows):
    for j in range(group):
        a_s[j] = _dot(f_ref[j], _stack3(x_ref[j], pad_rows))
    a = jnp.swapaxes(a_s[...], 0, 1)
    ar_ref[...] = a[:ka_pad]
    ai_ref[...] = a[ka_pad:]


def _fft_stage1(x5, part, f1c, ka_pad, group, dt):
    _, b, p, k_in, d = x5.shape
    kc = f1c.shape[-1]
    out = jax.ShapeDtypeStruct((b, ka_pad, p, d), F32)
    ospec = pl.BlockSpec((None, ka_pad, group, dt), lambda bi, g, c: (bi, 0, g, c))
    return pl.pallas_call(
        functools.partial(_s1_kernel, group=group, ka_pad=ka_pad, pad_rows=kc - 3 * k_in),
        grid=(b, p // group, d // dt),
        in_specs=[pl.BlockSpec((None, None, group, k_in, dt), lambda bi, g, c: (part, bi, g, 0, c)),
                  pl.BlockSpec((group, 2 * ka_pad, kc), lambda bi, g, c: (g, 0, 0))],
        out_specs=[ospec, ospec],
        out_shape=[out, out],
        scratch_shapes=[pltpu.VMEM((group, 2 * ka_pad, dt), F32)],
        compiler_params=_cparams("parallel", "parallel", "parallel"),
        name="fft_stage1",
    )(x5, f1c)


def _s2_filter_kernel(ar_ref, ai_ref, m_ref, inv_ref, hr_ref, hi_ref, *, p):
    x = jnp.concatenate([ar_ref[...], ai_ref[...]], axis=0)
    y = _dot(m_ref[...], _stack3(x)) * inv_ref[...]
    hr_ref[...] = y[:p]
    hi_ref[...] = y[p:]


def _fft_stage2_filter(ar, ai, m2f, inv_nrm, dt):
    ka, p, d = ar.shape
    spec = pl.BlockSpec((None, p, dt), lambda k, c: (k, 0, c))
    out = jax.ShapeDtypeStruct((ka, p, d), F32)
    return pl.pallas_call(
        functools.partial(_s2_filter_kernel, p=p),
        grid=(ka, d // dt),
        in_specs=[spec, spec,
                  pl.BlockSpec((2 * p, 6 * p), lambda k, c: (0, 0)),
                  pl.BlockSpec((1, dt), lambda k, c: (0, c))],
        out_specs=[spec, spec],
        out_shape=[out, out],
        compiler_params=_cparams("parallel", "parallel"),
        name="fft_stage2_filter",
    )(ar, ai, m2f, inv_nrm)


def _s2_kernel(ar_ref, ai_ref, hr_ref, hi_ref, mf_ref, mi_ref, cr_ref, ci_ref, cr_s, ci_s, *, p, ka_used):
    kg = pl.program_id(1)
    for i in range(SUBLANES):
        valid = kg * SUBLANES + i < ka_used

        @pl.when(valid)
        def _():
            x = jnp.concatenate([ar_ref[i], ai_ref[i]], axis=0).astype(BF16)
            xf = _dot(mf_ref[...], x)
            xr, xi = xf[:p], xf[p:]
            hr, hi = hr_ref[i], hi_ref[i]
            y = jnp.concatenate([xr * hr - xi * hi, xr * hi + xi * hr], axis=0).astype(BF16)
            c = _dot(mi_ref[...], y)
            cr_s[i] = c[:p]
            ci_s[i] = c[p:]

        @pl.when(jnp.logical_not(valid))
        def _():
            cr_s[i] = jnp.zeros(cr_s.shape[1:], F32)
            ci_s[i] = jnp.zeros(ci_s.shape[1:], F32)

    cr_ref[...] = jnp.swapaxes(cr_s[...], 0, 1)
    ci_ref[...] = jnp.swapaxes(ci_s[...], 0, 1)


def _fft_stage2(ar, ai, hr, hi, m2f, m2i, ka_used, h_col0, dt):
    b, ka, p, d = ar.shape
    hc = h_col0 // dt
    aspec = pl.BlockSpec((None, SUBLANES, p, dt), lambda bi, k, c: (bi, k, 0, c))
    hspec = pl.BlockSpec((SUBLANES, p, dt), lambda bi, k, c: (k, 0, hc + c))
    mspec = pl.BlockSpec((2 * p, 2 * p), lambda bi, k, c: (0, 0))
    ospec = pl.BlockSpec((None, p, SUBLANES, dt), lambda bi, k, c: (bi, 0, k, c))
    out = jax.ShapeDtypeStruct((b, p, ka, d), F32)
    return pl.pallas_call(
        functools.partial(_s2_kernel, p=p, ka_used=ka_used),
        grid=(b, ka // SUBLANES, d // dt),
        in_specs=[aspec, aspec, hspec, hspec, mspec, mspec],
        out_specs=[ospec, ospec],
        out_shape=[out, out],
        scratch_shapes=[pltpu.VMEM((SUBLANES, p, dt), F32), pltpu.VMEM((SUBLANES, p, dt), F32)],
        compiler_params=_cparams("parallel", "parallel", "parallel"),
        name="fft_stage2",
    )(ar, ai, hr, hi, m2f, m2i)


def _s3_kernel(cr_ref, ci_ref, g_ref, v_ref, x_ref, sk_ref, z_ref, *, group, pad_rows):
    sk = sk_ref[...]
    for j in range(group):
        c = jnp.concatenate([cr_ref[j], ci_ref[j]], axis=0)
        y = _dot(g_ref[j], _stack3(c, pad_rows))
        v = v_ref[j]
        z_ref[j] = ((y + v * sk) * x_ref[j]).astype(z_ref.dtype)


def _fft_stage3(cr, ci, gc, v5, vpart, x5, xpart, skip, group, dt, out_dtype):
    b, p, ka, d = cr.shape
    qh = v5.shape[3]
    kc = gc.shape[-1]
    cspec = pl.BlockSpec((None, group, ka, dt), lambda bi, g, c: (bi, g, 0, c))
    pspec = lambda part: pl.BlockSpec((None, None, group, qh, dt), lambda bi, g, c: (part, bi, g, 0, c))
    return pl.pallas_call(
        functools.partial(_s3_kernel, group=group, pad_rows=kc - 6 * ka),
        grid=(b, p // group, d // dt),
        in_specs=[cspec, cspec,
                  pl.BlockSpec((group, qh, kc), lambda bi, g, c: (g, 0, 0)),
                  pspec(vpart), pspec(xpart),
                  pl.BlockSpec((1, dt), lambda bi, g, c: (0, c))],
        out_specs=pspec(0),
        out_shape=jax.ShapeDtypeStruct((1, b, p, qh, d), out_dtype),
        compiler_params=_cparams("parallel", "parallel", "parallel"),
        name="fft_stage3",
    )(cr, ci, gc, v5, x5, skip)


def _long_conv_lat(v5, vpart, x5, xpart, hr, hi, skip, plan, order, out_dtype):
    d = v5.shape[-1]
    dt1 = min(1024, d)
    dt2 = min(512, d)
    ar, ai = _fft_stage1(v5, vpart, plan["f1"], plan["ka_pad"], 16, dt1)
    cr, ci = _fft_stage2(ar, ai, hr, hi, plan["m2f1"], plan["m2i1"], plan["ka_used"], order * d, dt2)
    return _fft_stage3(cr, ci, plan["g"], v5, vpart, x5, xpart, skip, 16, dt1, out_dtype)


def _filter_spectrum_lat(hy, seq, plan_f):
    fw1, fb1, ffreq, fw2, fb2, fwout = hy
    q = plan_f["q"]
    h_time, nrm = _hyena_filter_time(fw1, fb1, ffreq, fw2, fb2, fwout, seq, q)
    od = h_time.shape[1]
    h5 = h_time.reshape(1, 1, FFT_P, q, od)
    ar, ai = _fft_stage1(h5, 0, plan_f["f1"], plan_f["ka_pad"], 8, min(1024, od))
    return _fft_stage2_filter(ar[0], ai[0], plan_f["m2f"], 1.0 / nrm, min(512, od))


def _dense_consts(seq):
    n = 2 * seq
    kf = seq + 1
    kf_pad = -(-kf // LANES) * LANES
    k = np.arange(kf)
    fwd_full = np.zeros((2 * kf_pad, n))
    ang = 2 * np.pi * np.outer(k, np.arange(n)) / n
    fwd_full[:kf] = np.cos(ang)
    fwd_full[kf_pad:kf_pad + kf] = -np.sin(ang)
    wgt = np.where((k == 0) | (k == seq), 1.0, 2.0) / n
    inv = np.zeros((seq, 2 * kf_pad))
    angi = 2 * np.pi * np.outer(np.arange(seq), k) / n
    inv[:, :kf] = wgt * np.cos(angi)
    inv[:, kf_pad:kf_pad + kf] = -wgt * np.sin(angi)
    return dict(kf_pad=kf_pad, fwd_full=_const3(fwd_full), fwd=_const3(fwd_full[:, :seq]), inv=_const3(inv))


def _dense_spec_kernel(h_ref, f_ref, inv_ref, hr_ref, hi_ref, *, kf_pad):
    y = _dot(f_ref[...], _stack3(h_ref[...])) * inv_ref[...]
    hr_ref[...] = y[:kf_pad]
    hi_ref[...] = y[kf_pad:]


def _filter_spectrum_ctx(hy, seq, cons):
    fw1, fb1, ffreq, fw2, fb2, fwout = hy
    h_time, nrm = _hyena_filter_time(fw1, fb1, ffreq, fw2, fb2, fwout, seq, 0)
    n2, od = h_time.shape
    kf_pad = cons["kf_pad"]
    ct = min(512, od)
    out = jax.ShapeDtypeStruct((kf_pad, od), F32)
    ospec = pl.BlockSpec((kf_pad, ct), lambda c: (0, c))
    return pl.pallas_call(
        functools.partial(_dense_spec_kernel, kf_pad=kf_pad),
        grid=(od // ct,),
        in_specs=[pl.BlockSpec((n2, ct), lambda c: (0, c)),
                  pl.BlockSpec((2 * kf_pad, 3 * n2), lambda c: (0, 0)),
                  pl.BlockSpec((1, ct), lambda c: (0, c))],
        out_specs=[ospec, ospec],
        out_shape=[out, out],
        compiler_params=_cparams("parallel"),
        name="dense_filter_spectrum",
    )(h_time, cons["fwd_full"], 1.0 / nrm)


def _dense_conv_kernel(v_ref, x1_ref, x2_ref, h1r_ref, h1i_ref, h2r_ref, h2i_ref, sk_ref,
                       f_ref, g_ref, z_ref, *, kf_pad):
    def conv(u, hr, hi):
        s = _dot(f_ref[...], _stack3(u))
        sr, si = s[:kf_pad], s[kf_pad:]
        y = jnp.concatenate([sr * hr - si * hi, sr * hi + si * hr], axis=0)
        return _dot(g_ref[...], _stack3(y))

    v = v_ref[...]
    z1 = x1_ref[...] * (conv(v, h1r_ref[...], h1i_ref[...]) + v * sk_ref[0:1, :])
    z2 = x2_ref[...] * (conv(z1, h2r_ref[...], h2i_ref[...]) + z1 * sk_ref[1:2, :])
    z_ref[...] = z2.astype(z_ref.dtype)


def _hyena_core_ctx(u3, hr, hi, fskip, cons, b, seq):
    d = u3.shape[-1]
    dt = min(256, d)
    nd = d // dt
    kf_pad = cons["kf_pad"]
    uspec = lambda part: pl.BlockSpec((None, seq, dt), lambda bi, c: (part, bi, c))
    hspec = lambda order: pl.BlockSpec((kf_pad, dt), lambda bi, c: (0, order * nd + c))
    return pl.pallas_call(
        functools.partial(_dense_conv_kernel, kf_pad=kf_pad),
        grid=(b, nd),
        in_specs=[uspec(0), uspec(1), uspec(2), hspec(0), hspec(0), hspec(1), hspec(1),
                  pl.BlockSpec((HYENA_ORDER, dt), lambda bi, c: (0, c)),
                  pl.BlockSpec((2 * kf_pad, 3 * seq), lambda bi, c: (0, 0)),
                  pl.BlockSpec((seq, 6 * kf_pad), lambda bi, c: (0, 0))],
        out_specs=pl.BlockSpec((seq, dt), lambda bi, c: (bi, c)),
        out_shape=jax.ShapeDtypeStruct((b * seq, d), BF16),
        compiler_params=_cparams("parallel", "parallel"),
        name="hyena_core_ctx",
    )(u3, u3, u3, hr, hi, hr, hi, fskip, cons["fwd"], cons["inv"])


def _rec_consts(chunk):
    t = np.arange(chunk)
    coefs, masks = [], []
    for direction in (0, 1):
        if direction == 0:
            rows = [t[None, :] <= t[:, None], t[None, :] > t[:, None]]
        else:
            rows = [t[None, :] >= t[:, None], t[None, :] < t[:, None]]
        mk = [np.eye(chunk)]
        m = chunk // 2
        while m >= 1:
            blk = t // (2 * m)
            half = (t // m) % 2
            mid = blk * 2 * m + m
            e = np.zeros((chunk, chunk))
            for r in range(chunk):
                if direction == 0:
                    if half[r] == 1:
                        e[r, mid[r]:r + 1] = 1
                    else:
                        e[r, r + 1:mid[r]] = 1
                else:
                    if half[r] == 0:
                        e[r, r:mid[r]] = 1
                    else:
                        e[r, mid[r]:r] = 1
            same = blk[:, None] == blk[None, :]
            if direction == 0:
                mk.append(same & (half[:, None] == 1) & (half[None, :] == 0))
            else:
                mk.append(same & (half[:, None] == 0) & (half[None, :] == 1))
            rows.append(e)
            m //= 2
        rows.append(np.ones((16, chunk)))
        a = np.concatenate([np.asarray(r, np.float64) for r in rows], axis=0)
        a3 = np.concatenate([a, a, a], axis=1)
        pad = -(-a3.shape[1] // LANES) * LANES - a3.shape[1]
        a3 = np.pad(a3, ((0, 0), (0, pad)))
        coefs.append(a3)
        masks.append(np.stack([np.asarray(x, np.float32) for x in mk]))
    return (jnp.asarray(np.stack(coefs), dtype=BF16), jnp.asarray(np.stack(masks), dtype=F32))


def _rec_core(q_s, k_s, g, v_ref_val, coef_ref, mask_ref, s0_ref, o_ref, sfin_ref, st_s, ex_s,
              *, chunk, heads, dk, dv):
    c = pl.program_id(2)
    levels = int(math.log2(chunk))
    tot = (2 + levels) * chunk

    @pl.when(c == 0)
    def _():
        st_s[...] = s0_ref[...]

    g1 = g.astype(BF16)
    r1 = g - g1.astype(F32)
    g2 = r1.astype(BF16)
    g3 = (r1 - g2.astype(F32)).astype(BF16)
    pad = coef_ref.shape[-1] - 3 * chunk
    gs = jnp.concatenate([g1, g2, g3, jnp.zeros((pad, g.shape[1]), BF16)], axis=0)
    ex_s[...] = jnp.exp(_dot(coef_ref[...], gs))

    for h in range(heads):
        ks = slice(h * dk, (h + 1) * dk)
        vs = slice(h * dv, (h + 1) * dv)
        qh = q_s[:, ks]
        kh = k_s[:, ks]
        vh = v_ref_val[:, vs].astype(BF16)
        att = mask_ref[0] * _dot_nt(qh.astype(BF16), kh.astype(BF16))
        for lv in range(levels):
            e = ex_s[(2 + lv) * chunk:(3 + lv) * chunk, ks]
            att += mask_ref[1 + lv] * _dot_nt((qh * e).astype(BF16), (kh * e).astype(BF16))
        st = st_s[h]
        o = _dot(att.astype(BF16), vh)
        o += _dot_nt((qh * ex_s[0:chunk, ks]).astype(BF16), st.astype(BF16))
        o_ref[:, vs] = o
        kd = (kh * ex_s[chunk:2 * chunk, ks]).astype(BF16)
        st_s[h] = st * ex_s[tot:tot + 1, ks] + _dot_tn(vh, kd)

    @pl.when(c == pl.num_programs(2) - 1)
    def _():
        sfin_ref[...] = st_s[...]


def _hgrn_rec_kernel(q_ref, v_ref, f_ref, lb_ref, coef_ref, mask_ref, s0_ref, o_ref, sfin_ref,
                     q_s, k_s, st_s, ex_s, **kw):
    qr = q_ref[...]
    q_s[...] = qr * _sigmoid(qr)
    fr = f_ref[...]
    lb = lb_ref[...]
    g = jnp.log(lb + (1.0 - lb) * _sigmoid(fr))
    k_s[...] = (1.0 - lb) * _sigmoid(-fr)
    _rec_core(q_s, k_s, g, v_ref, coef_ref, mask_ref, s0_ref, o_ref, sfin_ref, st_s, ex_s, **kw)


def _gla_rec_kernel(q_ref, k_ref, v_ref, a_ref, wup_ref, bup_ref, coef_ref, mask_ref, s0_ref,
                    o_ref, sfin_ref, q_s, k_s, st_s, ex_s, *, qscale, **kw):
    q_s[...] = q_ref[...] * qscale
    k_s[...] = k_ref[...]
    xg = _dot(a_ref[...].astype(BF16), wup_ref[...]) + bup_ref[...]
    g = (jnp.minimum(xg, 0.0) - jnp.log(1.0 + jnp.exp(-jnp.abs(xg)))) * (1.0 / GLA_GATE_NORM)
    _rec_core(q_s, k_s, g, v_ref, coef_ref, mask_ref, s0_ref, o_ref, sfin_ref, st_s, ex_s, **kw)


def _rec_call(kind, proj, extra, s0, consts, b, seq, d, heads, dk, dv):
    chunk = REC_CHUNK
    nc = seq // chunk
    coef, masks = consts
    hk = heads * dk
    hv = heads * dv
    levels = int(math.log2(chunk))
    n_rows = (2 + levels) * chunk + 16

    def rmap(col):
        return lambda bi, di, c: (bi * nc + c + di * (nc - 1 - 2 * c), col)

    cspecs = [pl.BlockSpec((None, n_rows, coef.shape[-1]), lambda bi, di, c: (di, 0, 0)),
              pl.BlockSpec((None, levels + 1, chunk, chunk), lambda bi, di, c: (di, 0, 0, 0)),
              pl.BlockSpec((None, None, heads, dv, dk), lambda bi, di, c: (bi, di, 0, 0, 0))]
    if kind == "hgrn":
        lb = extra
        kernel = _hgrn_rec_kernel
        in_specs = [pl.BlockSpec((chunk, d), rmap(0)),
                    pl.BlockSpec((chunk, d), rmap(1)),
                    pl.BlockSpec((chunk, d), lambda bi, di, c: (bi * nc + c + di * (nc - 1 - 2 * c), 3 + di)),
                    pl.BlockSpec((None, 1, d), lambda bi, di, c: (di, 0, 0))] + cspecs
        args = (proj, proj, proj, lb, coef, masks, s0)
        kw = {}
    else:
        wup, bup = extra
        kernel = _gla_rec_kernel
        in_specs = [pl.BlockSpec((chunk, hk), rmap(0)),
                    pl.BlockSpec((chunk, hk), rmap(1)),
                    pl.BlockSpec((chunk, hv), rmap(1)),
                    pl.BlockSpec((chunk, LANES), rmap((2 * hk + 2 * hv) // LANES)),
                    pl.BlockSpec((None, LANES, hk), lambda bi, di, c: (di, 0, 0)),
                    pl.BlockSpec((None, 1, hk), lambda bi, di, c: (di, 0, 0))] + cspecs
        args = (proj, proj, proj, proj, wup, bup, coef, masks, s0)
        kw = dict(qscale=float(dk) ** -0.5)
    return pl.pallas_call(
        functools.partial(kernel, chunk=chunk, heads=heads, dk=dk, dv=dv, **kw),
        grid=(b, 2, nc),
        in_specs=in_specs,
        out_specs=[pl.BlockSpec((None, chunk, hv), lambda bi, di, c: (di, bi * nc + c + di * (nc - 1 - 2 * c), 0)),
                   pl.BlockSpec((None, None, heads, dv, dk), lambda bi, di, c: (bi, di, 0, 0, 0))],
        out_shape=[jax.ShapeDtypeStruct((2, b * seq, hv), F32),
                   jax.ShapeDtypeStruct((b, 2, heads, dv, dk), F32)],
        scratch_shapes=[pltpu.VMEM((chunk, hk), F32), pltpu.VMEM((chunk, hk), F32),
                        pltpu.VMEM((heads, dv, dk), F32), pltpu.VMEM((n_rows, hk), F32)],
        compiler_params=_cparams("parallel", "parallel", "arbitrary"),
        name=kind + "_recurrence",
    )(*args)


def _gated_out_kernel(o_ref, gate_ref, gn_ref, w_ref, x_ref, mg_ref, out_ref, *, heads):
    o = o_ref[0] + o_ref[1]
    d = o.shape[1]
    dh = d // heads
    parts = []
    for h in range(heads):
        seg = o[:, h * dh:(h + 1) * dh]
        ms = jnp.mean(seg * seg, axis=-1, keepdims=True)
        parts.append(seg * lax.rsqrt(ms + RMS_EPS))
    gate = gate_ref[...]
    y = jnp.concatenate(parts, axis=1) * gn_ref[...] * (gate * _sigmoid(gate))
    out_ref[...] = x_ref[...] + mg_ref[...] * _dot(y.astype(BF16), w_ref[...])


def _gated_out(o2, proj, gate_col, gn, w, x, mg, bmap, heads, tm):
    m, d = x.shape
    return pl.pallas_call(
        functools.partial(_gated_out_kernel, heads=heads),
        grid=(m // tm,),
        in_specs=[pl.BlockSpec((2, tm, d), lambda i: (0, i, 0)),
                  pl.BlockSpec((tm, d), lambda i: (i, gate_col)),
                  pl.BlockSpec((1, d), lambda i: (0, 0)),
                  pl.BlockSpec((d, d), lambda i: (0, 0)),
                  pl.BlockSpec((tm, d), lambda i: (i, 0)),
                  _mod_spec(d, bmap)],
        out_specs=pl.BlockSpec((tm, d), lambda i: (i, 0)),
        out_shape=jax.ShapeDtypeStruct((m, d), F32),
        compiler_params=_cparams("parallel"),
        name="gated_out",
    )(o2, proj, gn, w, x, mg)


def _recurrent_mix(kind, h_ctx, h_lat, w_in, extra, consts, b, lc, seq, d, heads, dk, dv, tn):
    proj_ctx = _matmul(h_ctx, w_in, min(1024, h_ctx.shape[0]), tn)
    proj_lat = _matmul(h_lat, w_in, min(1024, h_lat.shape[0]), tn)
    s0 = jnp.zeros((b, 2, heads, dv, dk), F32)
    o_ctx, s_ctx = _rec_call(kind, proj_ctx, extra, s0, consts, b, lc, d, heads, dk, dv)
    o_lat, _ = _rec_call(kind, proj_lat, extra, s_ctx, consts, b, seq, d, heads, dk, dv)
    return (proj_ctx, o_ctx), (proj_lat, o_lat)


def kernel(x, c, ctx, c_ctx, w_mod, b_mod, norm1_g, norm2_g, w_ffn_in, w_ffn_out, final_g, hy_w_in, hy_conv_w, hy_fw1, hy_fb1, hy_ffreq, hy_fw2, hy_fb2, hy_fwout, hy_fskip, hy_w_out, hg_w_in, hg_lb_logits, hg_onorm_g, hg_w_out, gla_w_in, gla_w_up, gla_b_up, gla_onorm_g, gla_w_out):
    b, seq, d = x.shape
    lc = ctx.shape[1]
    depth = w_mod.shape[0]
    assert b + 1 <= MOD_ROWS and seq % (FFT_P * 2) == 0 and FFT_P % GRID_W == 0
    m_lat, m_ctx = b * seq, b * lc
    ctx_row = b
    tm_lat, tm_ctx = 512, min(512, m_ctx)
    lat_map = lambda i, tm=tm_lat: (i * tm) // seq
    ctx_map = lambda i: ctx_row

    xl = x.reshape(m_lat, d)
    xc = ctx.reshape(m_ctx, d)
    c8 = jnp.zeros((MOD_ROWS, d), F32).at[:b].set(c).at[ctx_row].set(c_ctx)
    mod = _modulation(c8, w_mod, b_mod).reshape(depth, MOD_ROWS, N_MOD, 1, d)

    plan = _fft_consts(seq, seq // FFT_P)
    plan_f = _fft_consts(seq, 2 * seq // FFT_P)
    dense = _dense_consts(lc)
    rec_consts = _rec_consts(REC_CHUNK)
    qh = plan["qh"]

    for i in range(depth):
        last = i == depth - 1
        kind, j = i % N_MIXERS, i // N_MIXERS
        mv = [mod[i, :, k] for k in range(N_MOD)]
        g1 = norm1_g[i][None, :]
        need_ctx = (not last) or kind != 0
        h_lat = _norm_modulate(xl, g1, mv[0], mv[1], lat_map, tm_lat)
        h_ctx = _norm_modulate(xc, g1, mv[0], mv[1], ctx_map, tm_ctx) if need_ctx else None
        if kind == 0:
            w_in = hy_w_in[j].astype(BF16)
            w_out = hy_w_out[j].astype(BF16)
            hy = (hy_fw1[j], hy_fb1[j], hy_ffreq[j], hy_fw2[j], hy_fb2[j], hy_fwout[j])
            u = _hyena_in_lat(h_lat, w_in, hy_conv_w[j], b, qh)
            hr, hi = _filter_spectrum_lat(hy, seq, plan_f)
            z1 = _long_conv_lat(u, 0, u, 1, hr, hi, hy_fskip[j, 0][None, :], plan, 0, F32)
            z2 = _long_conv_lat(z1, 0, u, 2, hr, hi, hy_fskip[j, 1][None, :], plan, 1, F32)
            xl = _hyena_out_lat(z2[0], w_out, xl, mv[2], b, qh)
            if need_ctx:
                uc = _hyena_in_ctx(h_ctx, w_in, hy_conv_w[j], b, lc)
                hcr, hci = _filter_spectrum_ctx(hy, lc, dense)
                zc = _hyena_core_ctx(uc, hcr, hci, hy_fskip[j], dense, b, lc)
                xc = _out_res(zc, w_out, xc, mv[2], ctx_map, tm_ctx)
        else:
            if kind == 1:
                heads = d // HGRN_EXPAND
                dk = dv = HGRN_EXPAND
                w_in = hg_w_in[j].astype(BF16)
                lb_cum = jnp.cumsum(jax.nn.softmax(hg_lb_logits.astype(F32), axis=1), axis=1)
                extra = (lb_cum[:, i] - lb_cum[:, 0])[:, None, :]
                gn, w_out, tn = hg_onorm_g[j], hg_w_out[j], 512
                rkind = "hgrn"
            else:
                heads = GLA_HEADS
                dk, dv = d // 2 // heads, d // heads
                n_in = gla_w_in.shape[-1]
                n_pad = -(-n_in // LANES) * LANES
                w_in = jnp.pad(gla_w_in[j], ((0, 0), (0, n_pad - n_in))).astype(BF16)
                r = GLA_GATE_RANK
                wup = jnp.zeros((2, LANES, heads * dk), F32)
                wup = wup.at[0, :r].set(gla_w_up[j, 0]).at[1, r:2 * r].set(gla_w_up[j, 1]).astype(BF16)
                extra = (wup, gla_b_up[j][:, None, :])
                gn, w_out = gla_onorm_g[j], gla_w_out[j]
                tn = n_pad // 7 if n_pad % (7 * LANES) == 0 else LANES
                rkind = "gla"
            (p_ctx, o_ctx), (p_lat, o_lat) = _recurrent_mix(
                rkind, h_ctx, h_lat, w_in, extra, rec_consts, b, lc, seq, d, heads, dk, dv, tn)
            gate_col = 2 if kind == 1 else (2 * heads * dk + heads * dv) // d
            w_out = w_out.astype(BF16)
            tm_g = 256
            xl = _gated_out(o_lat, p_lat, gate_col, gn[None, :], w_out, xl, mv[2],
                            lambda i: (i * tm_g) // seq, heads, tm_g)
            xc = _gated_out(o_ctx, p_ctx, gate_col, gn[None, :], w_out, xc, mv[2], ctx_map, heads, tm_g)
        g2 = norm2_g[i][None, :]
        wf_in = w_ffn_in[i].astype(BF16)
        wf_out = w_ffn_out[i].astype(BF16)
        tm_f = 1024
        xl = _ffn(xl, g2, mv[3], mv[4], mv[5], wf_in, wf_out, lambda i: (i * tm_f) // seq, tm_f, 512)
        if not last:
            xc = _ffn(xc, g2, mv[3], mv[4], mv[5], wf_in, wf_out, ctx_map, tm_ctx, 512)
    return _final_norm(xl, final_g[None, :], tm_lat).reshape(b, seq, d)
```

```python
import functools
import math

import ml_dtypes
import numpy as np
import jax
import jax.numpy as jnp
from jax import lax
from jax.experimental import pallas as pl
from jax.experimental.pallas import tpu as pltpu

F32 = jnp.float32
BF16 = jnp.bfloat16

N_MOD = 6
N_MIXERS = 3
RMS_EPS = 1e-6
GRID_W = 64
HYENA_ORDER = 2
FILTER_BANDS = 16
FILTER_EMB = 1 + 2 * FILTER_BANDS
HYENA_DECAY_MIN = math.log(1e-2) / 1.5
HYENA_DECAY_MAX = math.log(1e-2) / 0.3
HGRN_EXPAND = 128
GLA_HEADS = 4
GLA_GATE_RANK = 16
GLA_GATE_NORM = 16.0

LANES = 128
SUBLANES = 8
V7X_VMEM_LIMIT = 56 * 1024 * 1024

FFT_P = 128
REC_CHUNK = 64
MOD_ROWS = 8


def _cparams(*sem):
    return pltpu.CompilerParams(dimension_semantics=sem, vmem_limit_bytes=V7X_VMEM_LIMIT)


def _dot(a, b):
    return jnp.dot(a, b, preferred_element_type=F32)


def _dot_nt(a, b):
    return lax.dot_general(a, b, (((1,), (1,)), ((), ())), preferred_element_type=F32)


def _dot_tn(a, b):
    return lax.dot_general(a, b, (((0,), (0,)), ((), ())), preferred_element_type=F32)


def _split(x):
    hi = x.astype(BF16)
    lo = (x - hi.astype(F32)).astype(BF16)
    return hi, lo


def _stack3(x, pad_rows=0):
    hi, lo = _split(x)
    parts = [hi, lo, hi]
    if pad_rows:
        parts.append(jnp.zeros((pad_rows, x.shape[1]), BF16))
    return jnp.concatenate(parts, axis=0)


def _const3(c, pad_cols=0):
    hi = c.astype(ml_dtypes.bfloat16)
    lo = (c - hi.astype(np.float64)).astype(ml_dtypes.bfloat16)
    parts = [hi, hi, lo]
    if pad_cols:
        parts.append(np.zeros(c.shape[:-1] + (pad_cols,), ml_dtypes.bfloat16))
    return jnp.asarray(np.concatenate(parts, axis=-1))


def _dot3(a, bh, bl):
    ah, al = _split(a)
    return _dot(ah, bh) + _dot(ah, bl) + _dot(al, bh)


def _sigmoid(x):
    return jax.nn.sigmoid(x)


def _normmod(x, g, sh, sc):
    ms = jnp.mean(x * x, axis=-1, keepdims=True)
    y = x * lax.rsqrt(ms + RMS_EPS) * g
    return y * (1.0 + sc) + sh


def _mod_kernel(c_ref, w_ref, b_ref, o_ref):
    c = c_ref[...]
    s = (c * _sigmoid(c)).astype(BF16)
    o_ref[...] = _dot(s, w_ref[...].astype(BF16)) + b_ref[...]


def _modulation(c8, w_mod, b_mod):
    depth, d, n = w_mod.shape
    tn = 1024
    return pl.pallas_call(
        _mod_kernel,
        grid=(depth, n // tn),
        in_specs=[pl.BlockSpec((MOD_ROWS, d), lambda l, j: (0, 0)),
                  pl.BlockSpec((None, d, tn), lambda l, j: (l, 0, j)),
                  pl.BlockSpec((None, 1, tn), lambda l, j: (l, 0, j))],
        out_specs=pl.BlockSpec((None, MOD_ROWS, tn), lambda l, j: (l, 0, j)),
        out_shape=jax.ShapeDtypeStruct((depth, MOD_ROWS, n), F32),
        compiler_params=_cparams("parallel", "parallel"),
        name="modulation",
    )(c8, w_mod, b_mod.reshape(depth, 1, n))


def _mod_spec(d, bmap):
    return pl.BlockSpec((None, 1, d), lambda *idx: (bmap(idx[0]), 0, 0))


def _normmod_kernel(x_ref, g_ref, sh_ref, sc_ref, o_ref):
    o_ref[...] = _normmod(x_ref[...], g_ref[...], sh_ref[...], sc_ref[...]).astype(BF16)


def _norm_modulate(x, g, sh, sc, bmap, tm):
    m, d = x.shape
    return pl.pallas_call(
        _normmod_kernel,
        grid=(m // tm,),
        in_specs=[pl.BlockSpec((tm, d), lambda i: (i, 0)),
                  pl.BlockSpec((1, d), lambda i: (0, 0)),
                  _mod_spec(d, bmap), _mod_spec(d, bmap)],
        out_specs=pl.BlockSpec((tm, d), lambda i: (i, 0)),
        out_shape=jax.ShapeDtypeStruct((m, d), BF16),
        compiler_params=_cparams("parallel"),
        name="norm_modulate",
    )(x, g, sh, sc)


def _final_norm_kernel(x_ref, g_ref, o_ref):
    x = x_ref[...]
    ms = jnp.mean(x * x, axis=-1, keepdims=True)
    o_ref[...] = x * lax.rsqrt(ms + RMS_EPS) * g_ref[...]


def _final_norm(x, g, tm):
    m, d = x.shape
    return pl.pallas_call(
        _final_norm_kernel,
        grid=(m // tm,),
        in_specs=[pl.BlockSpec((tm, d), lambda i: (i, 0)), pl.BlockSpec((1, d), lambda i: (0, 0))],
        out_specs=pl.BlockSpec((tm, d), lambda i: (i, 0)),
        out_shape=jax.ShapeDtypeStruct((m, d), F32),
        compiler_params=_cparams("parallel"),
        name="final_norm",
    )(x, g)


def _lane_tile(n, cap):
    best = n
    for t in range(LANES, min(n, cap) + 1, LANES):
        if n % t == 0:
            best = t
    return best


def _cast_kernel(x_ref, o_ref):
    o_ref[...] = x_ref[...].astype(o_ref.dtype)


def _to_bf16(w):
    shape = w.shape
    w2 = w.reshape(-1, shape[-1])
    r, c = w2.shape
    tr, tc = min(512, r), _lane_tile(c, 2048)
    out = pl.pallas_call(
        _cast_kernel,
        grid=(r // tr, c // tc),
        in_specs=[pl.BlockSpec((tr, tc), lambda i, j: (i, j))],
        out_specs=pl.BlockSpec((tr, tc), lambda i, j: (i, j)),
        out_shape=jax.ShapeDtypeStruct((r, c), BF16),
        compiler_params=_cparams("parallel", "parallel"),
        name="cast_bf16",
    )(w2)
    return out.reshape(shape)


def _proj_kernel(x_ref, g_ref, sh_ref, sc_ref, w_ref, o_ref, h_s):
    @pl.when(pl.program_id(1) == 0)
    def _():
        h_s[...] = _normmod(x_ref[...], g_ref[...], sh_ref[...], sc_ref[...]).astype(BF16)

    o_ref[...] = _dot(h_s[...], w_ref[...])


def _projection(x, g, sh, sc, w, bmap, tm, tn):
    m, d = x.shape
    n = w.shape[1]
    return pl.pallas_call(
        _proj_kernel,
        grid=(m // tm, n // tn),
        in_specs=[pl.BlockSpec((tm, d), lambda i, j: (i, 0)),
                  pl.BlockSpec((1, d), lambda i, j: (0, 0)),
                  _mod_spec(d, bmap), _mod_spec(d, bmap),
                  pl.BlockSpec((d, tn), lambda i, j: (0, j))],
        out_specs=pl.BlockSpec((tm, tn), lambda i, j: (i, j)),
        out_shape=jax.ShapeDtypeStruct((m, n), F32),
        scratch_shapes=[pltpu.VMEM((tm, d), BF16)],
        compiler_params=_cparams("parallel", "arbitrary"),
        name="projection",
    )(x, g, sh, sc, w)


def _out_res_kernel(a_ref, w_ref, x_ref, mg_ref, o_ref):
    o_ref[...] = x_ref[...] + mg_ref[...] * _dot(a_ref[...], w_ref[...])


def _ffn_kernel(x_ref, g_ref, sh_ref, sc_ref, mg_ref, wg_ref, wu_ref, wo_ref, o_ref, h_s, acc_s):
    j = pl.program_id(1)

    @pl.when(j == 0)
    def _():
        h_s[...] = _normmod(x_ref[...], g_ref[...], sh_ref[...], sc_ref[...]).astype(BF16)
        acc_s[...] = jnp.zeros_like(acc_s)

    h = h_s[...]
    a = _dot(h, wg_ref[...])
    u = _dot(h, wu_ref[...])
    act = (a * _sigmoid(a) * u).astype(BF16)
    acc_s[...] += _dot(act, wo_ref[...])

    @pl.when(j == pl.num_programs(1) - 1)
    def _():
        o_ref[...] = x_ref[...] + mg_ref[...] * acc_s[...]


def _ffn(x, g, sh, sc, mg, w_in, w_out, layer, bmap, tm, tf):
    m, d = x.shape
    f = w_out.shape[1]
    nf = f // tf
    return pl.pallas_call(
        _ffn_kernel,
        grid=(m // tm, nf),
        in_specs=[pl.BlockSpec((tm, d), lambda i, j: (i, 0)),
                  pl.BlockSpec((1, d), lambda i, j: (0, 0)),
                  _mod_spec(d, bmap), _mod_spec(d, bmap), _mod_spec(d, bmap),
                  pl.BlockSpec((None, d, tf), lambda i, j: (layer, 0, j)),
                  pl.BlockSpec((None, d, tf), lambda i, j: (layer, 0, nf + j)),
                  pl.BlockSpec((None, tf, d), lambda i, j: (layer, j, 0))],
        out_specs=pl.BlockSpec((tm, d), lambda i, j: (i, 0)),
        out_shape=jax.ShapeDtypeStruct((m, d), F32),
        scratch_shapes=[pltpu.VMEM((tm, d), BF16), pltpu.VMEM((tm, d), F32)],
        compiler_params=_cparams("parallel", "arbitrary"),
        name="ffn",
    )(x, g, sh, sc, mg, w_in, w_in, w_out)


def _conv3_rows(acc, cw, row_len):
    tm = acc.shape[0]
    rid = lax.broadcasted_iota(jnp.int32, (tm, 1), 0) % row_len
    up = jnp.where(rid == 0, 0.0, pltpu.roll(acc, 1, 0))
    dn = jnp.where(rid == row_len - 1, 0.0, pltpu.roll(acc, tm - 1, 0))
    return cw[0:1] * up + cw[1:2] * acc + cw[2:3] * dn


def _hy_in_kernel(a_ref, w_ref, cw_ref, o_ref, *, row_len):
    o_ref[...] = _conv3_rows(_dot(a_ref[...], w_ref[...]), cw_ref[...], row_len)


def _hy_in_lat_kernel(x_ref, g_ref, sh_ref, sc_ref, w_ref, cw_ref, o_ref, h_s, *, row_len, p):
    @pl.when(pl.program_id(1) == 0)
    def _():
        h_s[...] = _normmod(x_ref[...], g_ref[...], sh_ref[...], sc_ref[...]).astype(BF16)

    y = _conv3_rows(_dot(h_s[...], w_ref[...]), cw_ref[...], row_len)
    tm, tn = y.shape
    o_ref[...] = jnp.swapaxes(y.reshape(tm // p, p, tn), 0, 1)


def _hyena_in_lat(x, g, sh, sc, w, cw, b, qh):
    m, d = x.shape
    p = FFT_P
    th = SUBLANES
    tn = min(512, d)
    nd = d // tn
    nq = qh // th
    bmap = lambda i: i // nq
    return pl.pallas_call(
        functools.partial(_hy_in_lat_kernel, row_len=GRID_W, p=p),
        grid=(b * nq, 3 * nd),
        in_specs=[pl.BlockSpec((th * p, d), lambda i, j: (i, 0)),
                  pl.BlockSpec((1, d), lambda i, j: (0, 0)),
                  _mod_spec(d, bmap), _mod_spec(d, bmap),
                  pl.BlockSpec((d, tn), lambda i, j: (0, j)),
                  pl.BlockSpec((3, tn), lambda i, j: (0, j))],
        out_specs=pl.BlockSpec((None, None, p, th, tn), lambda i, j: (j // nd, i // nq, 0, i % nq, j % nd)),
        out_shape=jax.ShapeDtypeStruct((3, b, p, qh, d), F32),
        scratch_shapes=[pltpu.VMEM((th * p, d), BF16)],
        compiler_params=_cparams("parallel", "arbitrary"),
        name="hyena_in_lat",
    )(x, g, sh, sc, w, cw)


def _hyena_in_ctx(h, w, cw, b, lc):
    m, d = h.shape
    return pl.pallas_call(
        functools.partial(_hy_in_kernel, row_len=lc),
        grid=(3, b),
        in_specs=[pl.BlockSpec((lc, d), lambda j, i: (i, 0)),
                  pl.BlockSpec((d, d), lambda j, i: (0, j)),
                  pl.BlockSpec((3, d), lambda j, i: (0, j))],
        out_specs=pl.BlockSpec((None, lc, d), lambda j, i: (j, i, 0)),
        out_shape=jax.ShapeDtypeStruct((3, m, d), F32),
        compiler_params=_cparams("parallel", "parallel"),
        name="hyena_in_ctx",
    )(h, w, cw)


def _hy_out_lat_kernel(z_ref, w_ref, x_ref, mg_ref, o_ref, a_s):
    @pl.when(pl.program_id(1) == 0)
    def _():
        a_s[...] = jnp.swapaxes(z_ref[...], 0, 1).reshape(a_s.shape).astype(BF16)

    o_ref[...] = x_ref[...] + mg_ref[...] * _dot(a_s[...], w_ref[...])


def _hyena_out_lat(z, w, x, mg, b, qh):
    m, d = x.shape
    p = FFT_P
    th = SUBLANES
    tn = min(1024, d)
    nq = qh // th
    return pl.pallas_call(
        _hy_out_lat_kernel,
        grid=(b * nq, d // tn),
        in_specs=[pl.BlockSpec((None, p, th, d), lambda i, j: (i // nq, 0, i % nq, 0)),
                  pl.BlockSpec((d, tn), lambda i, j: (0, j)),
                  pl.BlockSpec((th * p, tn), lambda i, j: (i, j)),
                  pl.BlockSpec((None, 1, tn), lambda i, j: (i // nq, 0, j))],
        out_specs=pl.BlockSpec((th * p, tn), lambda i, j: (i, j)),
        out_shape=jax.ShapeDtypeStruct((m, d), F32),
        scratch_shapes=[pltpu.VMEM((th * p, d), BF16)],
        compiler_params=_cparams("parallel", "arbitrary"),
        name="hyena_out_lat",
    )(z, w, x, mg)


def _out_res(a, w, x, mg, bmap, tm):
    m, d = x.shape
    return pl.pallas_call(
        _out_res_kernel,
        grid=(m // tm,),
        in_specs=[pl.BlockSpec((tm, d), lambda i: (i, 0)),
                  pl.BlockSpec((d, d), lambda i: (0, 0)),
                  pl.BlockSpec((tm, d), lambda i: (i, 0)),
                  _mod_spec(d, bmap)],
        out_specs=pl.BlockSpec((tm, d), lambda i: (i, 0)),
        out_shape=jax.ShapeDtypeStruct((m, d), F32),
        compiler_params=_cparams("parallel"),
        name="out_residual",
    )(a, w, x, mg)


def _filter_kernel(z_ref, w1h_ref, w1l_ref, b1_ref, fq_ref, w2h_ref, w2l_ref, b2_ref,
                   woh_ref, wol_ref, dl_ref, h_ref, nrm_ref):
    z = z_ref[...]
    t = z[:, 0:1]
    valid = z[:, FILTER_EMB:FILTER_EMB + 1]
    a1 = _dot3(z, w1h_ref[...], w1l_ref[...]) + b1_ref[...]
    hid = jnp.sin(fq_ref[0:1, :] * a1)
    a2 = _dot3(hid, w2h_ref[...], w2l_ref[...]) + b2_ref[...]
    hid = jnp.sin(fq_ref[1:2, :] * a2)
    h = _dot3(hid, woh_ref[...], wol_ref[...]) * jnp.exp(-t * dl_ref[...]) * valid
    h_ref[...] = h.reshape(h_ref.shape)

    @pl.when((pl.program_id(1) == 0) & (pl.program_id(2) == 0))
    def _():
        nrm_ref[...] = jnp.zeros_like(nrm_ref)

    nrm_ref[...] += jnp.sum(jnp.abs(h), axis=0, keepdims=True)


def _pad2(a, rows, cols):
    return jnp.pad(a, ((0, rows - a.shape[0]), (0, cols - a.shape[1])))


def _filter_positions(seq, perm):
    if perm:
        r_hi = seq // FFT_P
        gt, ng = 2 * SUBLANES, FFT_P // (2 * SUBLANES)
        dirs, g, tl, th = np.meshgrid(np.arange(2), np.arange(ng), np.arange(gt), np.arange(r_hi), indexing="ij")
        n = FFT_P * (r_hi * dirs + th) + g * gt + tl
    else:
        r_hi, gt, ng = seq, 1, 1
        dirs, g, tl, th = np.meshgrid(np.arange(2), np.arange(1), np.arange(1), np.arange(seq), indexing="ij")
        n = seq * dirs + th
    n = n.reshape(-1)
    pos = np.where(n < seq, n, 2 * seq - 1 - n).astype(np.float32)
    z = np.zeros((n.size, LANES), np.float32)
    z[:, 0] = pos / np.float32(max(seq - 1, 1))
    bands = np.arange(1, FILTER_BANDS + 1, dtype=np.float32)
    ang = (np.float32(2.0 * math.pi / seq) * pos)[:, None] * bands[None, :]
    z[:, 1:1 + FILTER_BANDS] = np.cos(ang.astype(np.float64))
    z[:, 1 + FILTER_BANDS:FILTER_EMB] = -np.sin(ang.astype(np.float64))
    z[:, FILTER_EMB] = n != seq
    return jnp.asarray(z), gt, ng, r_hi


def _hyena_filter_time(fw1, fb1, ffreq, fw2, fb2, fwout, seq, perm):
    od = fwout.shape[1] // 2
    d = od // HYENA_ORDER
    z, gt, ng, r_hi = _filter_positions(seq, perm)
    rows = gt * r_hi
    ct = min(2048, od)
    hidden = LANES
    w1h, w1l = _split(_pad2(fw1, LANES, hidden))
    w2h, w2l = _split(_pad2(fw2, hidden, hidden))
    woh, wol = _split(_pad2(fwout, hidden, fwout.shape[1]))
    b1 = _pad2(fb1[None, :], 1, hidden)
    b2 = _pad2(fb2[None, :], 1, hidden)
    fq = _pad2(ffreq, 2, hidden)
    deltas = np.abs(np.linspace(HYENA_DECAY_MIN, HYENA_DECAY_MAX, d, dtype=np.float32))
    dl = jnp.asarray(np.tile(deltas, HYENA_ORDER)[None, :])
    nct = od // ct
    small = lambda shape: pl.BlockSpec(shape, lambda c, di, g: (0, 0))
    wspec = pl.BlockSpec((hidden, ct), lambda c, di, g: (0, di * nct + c))
    return pl.pallas_call(
        _filter_kernel,
        grid=(nct, 2, ng),
        in_specs=[pl.BlockSpec((rows, LANES), lambda c, di, g: (di * ng + g, 0)),
                  small((LANES, hidden)), small((LANES, hidden)), small((1, hidden)), small((2, hidden)),
                  small((hidden, hidden)), small((hidden, hidden)), small((1, hidden)),
                  wspec, wspec,
                  pl.BlockSpec((1, ct), lambda c, di, g: (0, c))],
        out_specs=[pl.BlockSpec((None, gt, None, r_hi, ct), lambda c, di, g: (g, 0, di, 0, c)),
                   pl.BlockSpec((1, ct), lambda c, di, g: (0, c))],
        out_shape=[jax.ShapeDtypeStruct((ng, gt, 2, r_hi, od), F32), jax.ShapeDtypeStruct((1, od), F32)],
        compiler_params=_cparams("parallel", "arbitrary", "arbitrary"),
        name="hyena_filter",
    )(z, w1h, w1l, b1, fq, w2h, w2l, b2, woh, wol, dl)


def _fft_consts(seq, k_in):
    n = 2 * seq
    p = FFT_P
    q = n // p
    qh = q // 2
    ka_used = qh + 1
    ka_pad = -(-ka_used // 8) * 8
    ka = np.arange(ka_used)
    tl = np.arange(p)
    th = np.arange(k_in)
    theta = 2 * np.pi * (ka[None, :, None] * th[None, None, :] / q + ka[None, :, None] * tl[:, None, None] / n)
    f1 = np.zeros((p, 2 * ka_pad, k_in))
    f1[:, :ka_used] = np.cos(theta)
    f1[:, ka_pad:ka_pad + ka_used] = -np.sin(theta)
    k3 = 3 * k_in
    f1c = _const3(f1, pad_cols=-(-k3 // LANES) * LANES - k3)
    ang = 2 * np.pi * np.outer(np.arange(p), np.arange(p)) / p
    cc, sc = np.cos(ang), np.sin(ang)
    m2f_np, m2i_np = np.block([[cc, sc], [-sc, cc]]), np.block([[cc, -sc], [sc, cc]])
    m2f = _const3(m2f_np)
    m2f1, m2i1 = (jnp.asarray(m.astype(ml_dtypes.bfloat16)) for m in (m2f_np, m2i_np))
    tho = np.arange(qh)
    phi = 2 * np.pi * (tho[None, :, None] * ka[None, None, :] / q + ka[None, None, :] * tl[:, None, None] / n)
    wgt = np.where((ka == 0) | (ka == qh), 1.0, 2.0) / n
    kr = 2 * ka_pad
    g = np.zeros((p, qh, -(-kr // LANES) * LANES))
    col = 2 * SUBLANES * (ka // SUBLANES) + ka % SUBLANES
    g[:, :, col] = wgt * np.cos(phi)
    g[:, :, col + SUBLANES] = -wgt * np.sin(phi)
    gc = jnp.asarray(g.astype(ml_dtypes.bfloat16))
    return dict(q=q, qh=qh, ka_used=ka_used, ka_pad=ka_pad, f1=f1c, m2f=m2f, m2f1=m2f1, m2i1=m2i1, g=gc)


def _s1_kernel(x_ref, f_ref, ar_ref, ai_ref, a_s, *, group, ka_pad, pad_rows):
    for j in range(group):
        a_s[j] = _dot(f_ref[j], _stack3(x_ref[j], pad_rows))
    a = jnp.swapaxes(a_s[...], 0, 1)
    ar_ref[...] = a[:ka_pad].astype(ar_ref.dtype)
    ai_ref[...] = a[ka_pad:].astype(ai_ref.dtype)


def _fft_stage1(x5, part, f1c, ka_pad, group, dt, out_dtype):
    _, b, p, k_in, d = x5.shape
    kc = f1c.shape[-1]
    out = jax.ShapeDtypeStruct((b, ka_pad, p, d), out_dtype)
    ospec = pl.BlockSpec((None, ka_pad, group, dt), lambda bi, g, c: (bi, 0, g, c))
    return pl.pallas_call(
        functools.partial(_s1_kernel, group=group, ka_pad=ka_pad, pad_rows=kc - 3 * k_in),
        grid=(b, p // group, d // dt),
        in_specs=[pl.BlockSpec((None, None, group, k_in, dt), lambda bi, g, c: (part, bi, g, 0, c)),
                  pl.BlockSpec((group, 2 * ka_pad, kc), lambda bi, g, c: (g, 0, 0))],
        out_specs=[ospec, ospec],
        out_shape=[out, out],
        scratch_shapes=[pltpu.VMEM((group, 2 * ka_pad, dt), F32)],
        compiler_params=_cparams("parallel", "parallel", "parallel"),
        name="fft_stage1",
    )(x5, f1c)


def _s2_filter_kernel(ar_ref, ai_ref, m_ref, inv_ref, hr_ref, hi_ref, *, p):
    x = jnp.concatenate([ar_ref[...], ai_ref[...]], axis=0)
    y = _dot(m_ref[...], _stack3(x)) * inv_ref[...]
    hr_ref[...] = y[:p]
    hi_ref[...] = y[p:]


def _fft_stage2_filter(ar, ai, m2f, inv_nrm, dt):
    ka, p, d = ar.shape
    spec = pl.BlockSpec((None, p, dt), lambda k, c: (k, 0, c))
    out = jax.ShapeDtypeStruct((ka, p, d), F32)
    return pl.pallas_call(
        functools.partial(_s2_filter_kernel, p=p),
        grid=(ka, d // dt),
        in_specs=[spec, spec,
                  pl.BlockSpec((2 * p, 6 * p), lambda k, c: (0, 0)),
                  pl.BlockSpec((1, dt), lambda k, c: (0, c))],
        out_specs=[spec, spec],
        out_shape=[out, out],
        compiler_params=_cparams("parallel", "parallel"),
        name="fft_stage2_filter",
    )(ar, ai, m2f, inv_nrm)


def _s2_kernel(ar_ref, ai_ref, hr_ref, hi_ref, mf_ref, mi_ref, c_ref, c_s, *, p, ka_used):
    kg = pl.program_id(0)
    for i in range(SUBLANES):
        valid = kg * SUBLANES + i < ka_used

        @pl.when(valid)
        def _():
            x = jnp.concatenate([ar_ref[i], ai_ref[i]], axis=0)
            xf = _dot(mf_ref[...], x)
            xr, xi = xf[:p], xf[p:]
            hr, hi = hr_ref[i], hi_ref[i]
            y = jnp.concatenate([xr * hr - xi * hi, xr * hi + xi * hr], axis=0).astype(BF16)
            c = _dot(mi_ref[...], y)
            c_s[i] = c[:p]
            c_s[SUBLANES + i] = c[p:]

        @pl.when(jnp.logical_not(valid))
        def _():
            c_s[i] = jnp.zeros(c_s.shape[1:], F32)
            c_s[SUBLANES + i] = jnp.zeros(c_s.shape[1:], F32)

    c_ref[...] = jnp.swapaxes(c_s[...], 0, 1).astype(c_ref.dtype)


def _fft_stage2(ar, ai, hr, hi, m2f, m2i, ka_used, h_col0, dt):
    b, ka, p, d = ar.shape
    hc = h_col0 // dt
    aspec = pl.BlockSpec((None, SUBLANES, p, dt), lambda k, c, bi: (bi, k, 0, c))
    hspec = pl.BlockSpec((SUBLANES, p, dt), lambda k, c, bi: (k, 0, hc + c))
    mspec = pl.BlockSpec((2 * p, 2 * p), lambda k, c, bi: (0, 0))
    return pl.pallas_call(
        functools.partial(_s2_kernel, p=p, ka_used=ka_used),
        grid=(ka // SUBLANES, d // dt, b),
        in_specs=[aspec, aspec, hspec, hspec, mspec, mspec],
        out_specs=pl.BlockSpec((None, p, 2 * SUBLANES, dt), lambda k, c, bi: (bi, 0, k, c)),
        out_shape=jax.ShapeDtypeStruct((b, p, 2 * ka, d), BF16),
        scratch_shapes=[pltpu.VMEM((2 * SUBLANES, p, dt), F32)],
        compiler_params=_cparams("parallel", "parallel", "parallel"),
        name="fft_stage2",
    )(ar, ai, hr, hi, m2f, m2i)


def _s3_kernel(c_ref, g_ref, v_ref, x_ref, sk_ref, z_ref, *, group, pad_rows):
    sk = sk_ref[...]
    zpad = jnp.zeros((pad_rows, c_ref.shape[-1]), BF16)
    for j in range(group):
        y = _dot(g_ref[j], jnp.concatenate([c_ref[j], zpad], axis=0))
        v = v_ref[j]
        z_ref[j] = ((y + v * sk) * x_ref[j]).astype(z_ref.dtype)


def _fft_stage3(c, gc, v5, vpart, x5, xpart, skip, group, dt, out_dtype):
    b, p, kr, d = c.shape
    qh = v5.shape[3]
    kc = gc.shape[-1]
    pspec = lambda part: pl.BlockSpec((None, None, group, qh, dt), lambda bi, g, cc: (part, bi, g, 0, cc))
    return pl.pallas_call(
        functools.partial(_s3_kernel, group=group, pad_rows=kc - kr),
        grid=(b, p // group, d // dt),
        in_specs=[pl.BlockSpec((None, group, kr, dt), lambda bi, g, cc: (bi, g, 0, cc)),
                  pl.BlockSpec((group, qh, kc), lambda bi, g, cc: (g, 0, 0)),
                  pspec(vpart), pspec(xpart),
                  pl.BlockSpec((1, dt), lambda bi, g, cc: (0, cc))],
        out_specs=pspec(0),
        out_shape=jax.ShapeDtypeStruct((1, b, p, qh, d), out_dtype),
        compiler_params=_cparams("parallel", "parallel", "parallel"),
        name="fft_stage3",
    )(c, gc, v5, x5, skip)


def _long_conv_lat(v5, vpart, x5, xpart, hr, hi, skip, plan, order, out_dtype):
    d = v5.shape[-1]
    dt1 = min(1024, d)
    dt2 = min(512, d)
    ar, ai = _fft_stage1(v5, vpart, plan["f1"], plan["ka_pad"], 16, dt1, BF16)
    c = _fft_stage2(ar, ai, hr, hi, plan["m2f1"], plan["m2i1"], plan["ka_used"], order * d, dt2)
    return _fft_stage3(c, plan["g"], v5, vpart, x5, xpart, skip, 16, dt1, out_dtype)


def _filter_spectrum_lat(hy, seq, plan_f):
    fw1, fb1, ffreq, fw2, fb2, fwout = hy
    q = plan_f["q"]
    h_time, nrm = _hyena_filter_time(fw1, fb1, ffreq, fw2, fb2, fwout, seq, True)
    od = h_time.shape[-1]
    h5 = h_time.reshape(1, 1, FFT_P, q, od)
    ar, ai = _fft_stage1(h5, 0, plan_f["f1"], plan_f["ka_pad"], 8, min(1024, od), F32)
    return _fft_stage2_filter(ar[0], ai[0], plan_f["m2f"], 1.0 / nrm, min(1024, od))


def _dense_consts(seq):
    n = 2 * seq
    kf = seq + 1
    kf_pad = -(-kf // LANES) * LANES
    k = np.arange(kf)
    fwd_full = np.zeros((2 * kf_pad, n))
    ang = 2 * np.pi * np.outer(k, np.arange(n)) / n
    fwd_full[:kf] = np.cos(ang)
    fwd_full[kf_pad:kf_pad + kf] = -np.sin(ang)
    wgt = np.where((k == 0) | (k == seq), 1.0, 2.0) / n
    inv = np.zeros((seq, 2 * kf_pad))
    angi = 2 * np.pi * np.outer(np.arange(seq), k) / n
    inv[:, :kf] = wgt * np.cos(angi)
    inv[:, kf_pad:kf_pad + kf] = -wgt * np.sin(angi)
    return dict(kf_pad=kf_pad, fwd_full=_const3(fwd_full), fwd=_const3(fwd_full[:, :seq]), inv=_const3(inv))


def _dense_spec_kernel(h_ref, f_ref, inv_ref, hr_ref, hi_ref, *, kf_pad):
    y = _dot(f_ref[...], _stack3(h_ref[...])) * inv_ref[...]
    hr_ref[...] = y[:kf_pad]
    hi_ref[...] = y[kf_pad:]


def _filter_spectrum_ctx(hy, seq, cons):
    fw1, fb1, ffreq, fw2, fb2, fwout = hy
    h_time, nrm = _hyena_filter_time(fw1, fb1, ffreq, fw2, fb2, fwout, seq, False)
    od = h_time.shape[-1]
    n2 = 2 * seq
    h_time = h_time.reshape(n2, od)
    kf_pad = cons["kf_pad"]
    ct = min(512, od)
    out = jax.ShapeDtypeStruct((kf_pad, od), F32)
    ospec = pl.BlockSpec((kf_pad, ct), lambda c: (0, c))
    return pl.pallas_call(
        functools.partial(_dense_spec_kernel, kf_pad=kf_pad),
        grid=(od // ct,),
        in_specs=[pl.BlockSpec((n2, ct), lambda c: (0, c)),
                  pl.BlockSpec((2 * kf_pad, 3 * n2), lambda c: (0, 0)),
                  pl.BlockSpec((1, ct), lambda c: (0, c))],
        out_specs=[ospec, ospec],
        out_shape=[out, out],
        compiler_params=_cparams("parallel"),
        name="dense_filter_spectrum",
    )(h_time, cons["fwd_full"], 1.0 / nrm)


def _dense_conv_kernel(v_ref, x1_ref, x2_ref, h1r_ref, h1i_ref, h2r_ref, h2i_ref, sk_ref,
                       f_ref, g_ref, z_ref, *, kf_pad):
    def conv(u, hr, hi):
        s = _dot(f_ref[...], _stack3(u))
        sr, si = s[:kf_pad], s[kf_pad:]
        y = jnp.concatenate([sr * hr - si * hi, sr * hi + si * hr], axis=0)
        return _dot(g_ref[...], _stack3(y))

    v = v_ref[...]
    z1 = x1_ref[...] * (conv(v, h1r_ref[...], h1i_ref[...]) + v * sk_ref[0:1, :])
    z2 = x2_ref[...] * (conv(z1, h2r_ref[...], h2i_ref[...]) + z1 * sk_ref[1:2, :])
    z_ref[...] = z2.astype(z_ref.dtype)


def _hyena_core_ctx(u3, hr, hi, fskip, cons, b, seq):
    d = u3.shape[-1]
    dt = min(256, d)
    nd = d // dt
    kf_pad = cons["kf_pad"]
    uspec = lambda part: pl.BlockSpec((None, seq, dt), lambda bi, c: (part, bi, c))
    hspec = lambda order: pl.BlockSpec((kf_pad, dt), lambda bi, c: (0, order * nd + c))
    return pl.pallas_call(
        functools.partial(_dense_conv_kernel, kf_pad=kf_pad),
        grid=(b, nd),
        in_specs=[uspec(0), uspec(1), uspec(2), hspec(0), hspec(0), hspec(1), hspec(1),
                  pl.BlockSpec((HYENA_ORDER, dt), lambda bi, c: (0, c)),
                  pl.BlockSpec((2 * kf_pad, 3 * seq), lambda bi, c: (0, 0)),
                  pl.BlockSpec((seq, 6 * kf_pad), lambda bi, c: (0, 0))],
        out_specs=pl.BlockSpec((seq, dt), lambda bi, c: (bi, c)),
        out_shape=jax.ShapeDtypeStruct((b * seq, d), BF16),
        compiler_params=_cparams("parallel", "parallel"),
        name="hyena_core_ctx",
    )(u3, u3, u3, hr, hi, hr, hi, fskip, cons["fwd"], cons["inv"])


def _rec_consts(chunk):
    t = np.arange(chunk)
    coefs, masks = [], []
    for direction in (0, 1):
        if direction == 0:
            rows = [t[None, :] <= t[:, None], t[None, :] > t[:, None]]
        else:
            rows = [t[None, :] >= t[:, None], t[None, :] < t[:, None]]
        mk = [np.eye(chunk)]
        m = chunk // 2
        while m >= 1:
            blk = t // (2 * m)
            half = (t // m) % 2
            mid = blk * 2 * m + m
            e = np.zeros((chunk, chunk))
            for r in range(chunk):
                if direction == 0:
                    if half[r] == 1:
                        e[r, mid[r]:r + 1] = 1
                    else:
                        e[r, r + 1:mid[r]] = 1
                else:
                    if half[r] == 0:
                        e[r, r:mid[r]] = 1
                    else:
                        e[r, mid[r]:r] = 1
            same = blk[:, None] == blk[None, :]
            if direction == 0:
                mk.append(same & (half[:, None] == 1) & (half[None, :] == 0))
            else:
                mk.append(same & (half[:, None] == 0) & (half[None, :] == 1))
            rows.append(e)
            m //= 2
        rows.append(np.ones((16, chunk)))
        a = np.concatenate([np.asarray(r, np.float64) for r in rows], axis=0)
        a3 = np.concatenate([a, a, a], axis=1)
        pad = -(-a3.shape[1] // LANES) * LANES - a3.shape[1]
        a3 = np.pad(a3, ((0, 0), (0, pad)))
        coefs.append(a3)
        masks.append(np.stack([np.asarray(x, np.float32) for x in mk]))
    return (jnp.asarray(np.stack(coefs), dtype=BF16), jnp.asarray(np.stack(masks), dtype=F32))


def _rec_core(q_s, k_s, g, v_ref_val, coef_ref, mask_ref, s0_ref, o_ref, sfin_ref, st_s, ex_s,
              *, chunk, heads, dk, dv):
    c = pl.program_id(2)
    levels = int(math.log2(chunk))
    tot = (2 + levels) * chunk

    @pl.when(c == 0)
    def _():
        st_s[...] = s0_ref[...]

    g1 = g.astype(BF16)
    r1 = g - g1.astype(F32)
    g2 = r1.astype(BF16)
    g3 = (r1 - g2.astype(F32)).astype(BF16)
    pad = coef_ref.shape[-1] - 3 * chunk
    gs = jnp.concatenate([g1, g2, g3, jnp.zeros((pad, g.shape[1]), BF16)], axis=0)
    ex_s[...] = jnp.exp(_dot(coef_ref[...], gs))

    for h in range(heads):
        ks = slice(h * dk, (h + 1) * dk)
        vs = slice(h * dv, (h + 1) * dv)
        qh = q_s[:, ks]
        kh = k_s[:, ks]
        vh = v_ref_val[:, vs].astype(BF16)
        att = mask_ref[0] * _dot_nt(qh.astype(BF16), kh.astype(BF16))
        for lv in range(levels):
            e = ex_s[(2 + lv) * chunk:(3 + lv) * chunk, ks]
            att += mask_ref[1 + lv] * _dot_nt((qh * e).astype(BF16), (kh * e).astype(BF16))
        st = st_s[h]
        o = _dot(att.astype(BF16), vh)
        o += _dot_nt((qh * ex_s[0:chunk, ks]).astype(BF16), st.astype(BF16))
        o_ref[:, vs] = o
        kd = (kh * ex_s[chunk:2 * chunk, ks]).astype(BF16)
        st_s[h] = st * ex_s[tot:tot + 1, ks] + _dot_tn(vh, kd)

    @pl.when(c == pl.num_programs(2) - 1)
    def _():
        sfin_ref[...] = st_s[...]


def _hgrn_rec_kernel(q_ref, v_ref, f_ref, lb_ref, coef_ref, mask_ref, s0_ref, o_ref, sfin_ref,
                     q_s, k_s, st_s, ex_s, **kw):
    qr = q_ref[...]
    q_s[...] = qr * _sigmoid(qr)
    fr = f_ref[...]
    lb = lb_ref[...]
    g = jnp.log(lb + (1.0 - lb) * _sigmoid(fr))
    k_s[...] = (1.0 - lb) * _sigmoid(-fr)
    _rec_core(q_s, k_s, g, v_ref, coef_ref, mask_ref, s0_ref, o_ref, sfin_ref, st_s, ex_s, **kw)


def _gla_rec_kernel(q_ref, k_ref, v_ref, a_ref, wup_ref, bup_ref, coef_ref, mask_ref, s0_ref,
                    o_ref, sfin_ref, q_s, k_s, st_s, ex_s, *, qscale, **kw):
    q_s[...] = q_ref[...] * qscale
    k_s[...] = k_ref[...]
    xg = _dot(a_ref[...].astype(BF16), wup_ref[...]) + bup_ref[...]
    g = (jnp.minimum(xg, 0.0) - jnp.log(1.0 + jnp.exp(-jnp.abs(xg)))) * (1.0 / GLA_GATE_NORM)
    _rec_core(q_s, k_s, g, v_ref, coef_ref, mask_ref, s0_ref, o_ref, sfin_ref, st_s, ex_s, **kw)


def _rec_call(kind, proj, extra, s0, consts, b, seq, d, heads, dk, dv):
    chunk = REC_CHUNK
    nc = seq // chunk
    coef, masks = consts
    hk = heads * dk
    hv = heads * dv
    levels = int(math.log2(chunk))
    n_rows = (2 + levels) * chunk + 16

    def rmap(col):
        return lambda bi, di, c: (bi * nc + c + di * (nc - 1 - 2 * c), col)

    cspecs = [pl.BlockSpec((None, n_rows, coef.shape[-1]), lambda bi, di, c: (di, 0, 0)),
              pl.BlockSpec((None, levels + 1, chunk, chunk), lambda bi, di, c: (di, 0, 0, 0)),
              pl.BlockSpec((None, None, heads, dv, dk), lambda bi, di, c: (bi, di, 0, 0, 0))]
    if kind == "hgrn":
        lb = extra
        kernel = _hgrn_rec_kernel
        in_specs = [pl.BlockSpec((chunk, d), rmap(0)),
                    pl.BlockSpec((chunk, d), rmap(1)),
                    pl.BlockSpec((chunk, d), lambda bi, di, c: (bi * nc + c + di * (nc - 1 - 2 * c), 3 + di)),
                    pl.BlockSpec((None, 1, d), lambda bi, di, c: (di, 0, 0))] + cspecs
        args = (proj, proj, proj, lb, coef, masks, s0)
        kw = {}
    else:
        wup, bup = extra
        kernel = _gla_rec_kernel
        in_specs = [pl.BlockSpec((chunk, hk), rmap(0)),
                    pl.BlockSpec((chunk, hk), rmap(1)),
                    pl.BlockSpec((chunk, hv), rmap(1)),
                    pl.BlockSpec((chunk, LANES), rmap((2 * hk + 2 * hv) // LANES)),
                    pl.BlockSpec((None, LANES, hk), lambda bi, di, c: (di, 0, 0)),
                    pl.BlockSpec((None, 1, hk), lambda bi, di, c: (di, 0, 0))] + cspecs
        args = (proj, proj, proj, proj, wup, bup, coef, masks, s0)
        kw = dict(qscale=float(dk) ** -0.5)
    return pl.pallas_call(
        functools.partial(kernel, chunk=chunk, heads=heads, dk=dk, dv=dv, **kw),
        grid=(b, 2, nc),
        in_specs=in_specs,
        out_specs=[pl.BlockSpec((None, chunk, hv), lambda bi, di, c: (di, bi * nc + c + di * (nc - 1 - 2 * c), 0)),
                   pl.BlockSpec((None, None, heads, dv, dk), lambda bi, di, c: (bi, di, 0, 0, 0))],
        out_shape=[jax.ShapeDtypeStruct((2, b * seq, hv), F32),
                   jax.ShapeDtypeStruct((b, 2, heads, dv, dk), F32)],
        scratch_shapes=[pltpu.VMEM((chunk, hk), F32), pltpu.VMEM((chunk, hk), F32),
                        pltpu.VMEM((heads, dv, dk), F32), pltpu.VMEM((n_rows, hk), F32)],
        compiler_params=_cparams("parallel", "parallel", "arbitrary"),
        name=kind + "_recurrence",
    )(*args)


def _gated_out_kernel(o_ref, gate_ref, gn_ref, w_ref, x_ref, mg_ref, out_ref, *, heads):
    o = o_ref[0] + o_ref[1]
    d = o.shape[1]
    dh = d // heads
    parts = []
    for h in range(heads):
        seg = o[:, h * dh:(h + 1) * dh]
        ms = jnp.mean(seg * seg, axis=-1, keepdims=True)
        parts.append(seg * lax.rsqrt(ms + RMS_EPS))
    gate = gate_ref[...]
    y = jnp.concatenate(parts, axis=1) * gn_ref[...] * (gate * _sigmoid(gate))
    out_ref[...] = x_ref[...] + mg_ref[...] * _dot(y.astype(BF16), w_ref[...])


def _gated_out(o2, proj, gate_col, gn, w, x, mg, bmap, heads, tm):
    m, d = x.shape
    return pl.pallas_call(
        functools.partial(_gated_out_kernel, heads=heads),
        grid=(m // tm,),
        in_specs=[pl.BlockSpec((2, tm, d), lambda i: (0, i, 0)),
                  pl.BlockSpec((tm, d), lambda i: (i, gate_col)),
                  pl.BlockSpec((1, d), lambda i: (0, 0)),
                  pl.BlockSpec((d, d), lambda i: (0, 0)),
                  pl.BlockSpec((tm, d), lambda i: (i, 0)),
                  _mod_spec(d, bmap)],
        out_specs=pl.BlockSpec((tm, d), lambda i: (i, 0)),
        out_shape=jax.ShapeDtypeStruct((m, d), F32),
        compiler_params=_cparams("parallel"),
        name="gated_out",
    )(o2, proj, gn, w, x, mg)


def _recurrent_mix(kind, x_ctx, x_lat, norm, ctx_map, w_in, extra, consts, b, lc, seq, d, heads, dk, dv, tn):
    g1, sh, sc = norm
    tm_c, tm_l = min(1024, x_ctx.shape[0]), min(1024, x_lat.shape[0])
    proj_ctx = _projection(x_ctx, g1, sh, sc, w_in, ctx_map, tm_c, tn)
    proj_lat = _projection(x_lat, g1, sh, sc, w_in, lambda i: (i * tm_l) // seq, tm_l, tn)
    s0 = jnp.zeros((b, 2, heads, dv, dk), F32)
    o_ctx, s_ctx = _rec_call(kind, proj_ctx, extra, s0, consts, b, lc, d, heads, dk, dv)
    o_lat, _ = _rec_call(kind, proj_lat, extra, s_ctx, consts, b, seq, d, heads, dk, dv)
    return (proj_ctx, o_ctx), (proj_lat, o_lat)


def kernel(x, c, ctx, c_ctx, w_mod, b_mod, norm1_g, norm2_g, w_ffn_in, w_ffn_out, final_g, hy_w_in, hy_conv_w, hy_fw1, hy_fb1, hy_ffreq, hy_fw2, hy_fb2, hy_fwout, hy_fskip, hy_w_out, hg_w_in, hg_lb_logits, hg_onorm_g, hg_w_out, gla_w_in, gla_w_up, gla_b_up, gla_onorm_g, gla_w_out):
    b, seq, d = x.shape
    lc = ctx.shape[1]
    depth = w_mod.shape[0]
    assert b + 1 <= MOD_ROWS and seq % (FFT_P * 2) == 0 and FFT_P % GRID_W == 0
    m_lat, m_ctx = b * seq, b * lc
    ctx_row = b
    tm_lat, tm_ctx = 512, min(512, m_ctx)
    lat_map = lambda i, tm=tm_lat: (i * tm) // seq
    ctx_map = lambda i: ctx_row

    xl = x.reshape(m_lat, d)
    xc = ctx.reshape(m_ctx, d)
    c8 = jnp.zeros((MOD_ROWS, d), F32).at[:b].set(c).at[ctx_row].set(c_ctx)
    mod = _modulation(c8, w_mod, b_mod).reshape(depth, MOD_ROWS, N_MOD, 1, d)

    plan = _fft_consts(seq, seq // FFT_P)
    plan_f = _fft_consts(seq, 2 * seq // FFT_P)
    dense = _dense_consts(lc)
    rec_consts = _rec_consts(REC_CHUNK)
    qh = plan["qh"]

    wb_ffn_in, wb_ffn_out = _to_bf16(w_ffn_in), _to_bf16(w_ffn_out)
    wb_hy_in, wb_hy_out = _to_bf16(hy_w_in), _to_bf16(hy_w_out)
    wb_hg_in, wb_hg_out = _to_bf16(hg_w_in), _to_bf16(hg_w_out)
    wb_gla_in, wb_gla_out = _to_bf16(gla_w_in), _to_bf16(gla_w_out)

    for i in range(depth):
        last = i == depth - 1
        kind, j = i % N_MIXERS, i // N_MIXERS
        mv = [mod[i, :, k] for k in range(N_MOD)]
        g1 = norm1_g[i][None, :]
        need_ctx = (not last) or kind != 0
        if kind == 0:
            w_in, w_out = wb_hy_in[j], wb_hy_out[j]
            hy = (hy_fw1[j], hy_fb1[j], hy_ffreq[j], hy_fw2[j], hy_fb2[j], hy_fwout[j])
            u = _hyena_in_lat(xl, g1, mv[0], mv[1], w_in, hy_conv_w[j], b, qh)
            hr, hi = _filter_spectrum_lat(hy, seq, plan_f)
            z1 = _long_conv_lat(u, 0, u, 1, hr, hi, hy_fskip[j, 0][None, :], plan, 0, F32)
            z2 = _long_conv_lat(z1, 0, u, 2, hr, hi, hy_fskip[j, 1][None, :], plan, 1, F32)
            xl = _hyena_out_lat(z2[0], w_out, xl, mv[2], b, qh)
            if need_ctx:
                h_ctx = _norm_modulate(xc, g1, mv[0], mv[1], ctx_map, tm_ctx)
                uc = _hyena_in_ctx(h_ctx, w_in, hy_conv_w[j], b, lc)
                hcr, hci = _filter_spectrum_ctx(hy, lc, dense)
                zc = _hyena_core_ctx(uc, hcr, hci, hy_fskip[j], dense, b, lc)
                xc = _out_res(zc, w_out, xc, mv[2], ctx_map, tm_ctx)
        else:
            if kind == 1:
                heads = d // HGRN_EXPAND
                dk = dv = HGRN_EXPAND
                w_in = wb_hg_in[j]
                lb_cum = jnp.cumsum(jax.nn.softmax(hg_lb_logits.astype(F32), axis=1), axis=1)
                extra = (lb_cum[:, i] - lb_cum[:, 0])[:, None, :]
                gn, w_out, tn = hg_onorm_g[j], wb_hg_out[j], 512
                rkind = "hgrn"
            else:
                heads = GLA_HEADS
                dk, dv = d // 2 // heads, d // heads
                n_in = gla_w_in.shape[-1]
                n_pad = -(-n_in // LANES) * LANES
                w_in = jnp.pad(wb_gla_in[j], ((0, 0), (0, n_pad - n_in)))
                r = GLA_GATE_RANK
                wup = jnp.zeros((2, LANES, heads * dk), F32)
                wup = wup.at[0, :r].set(gla_w_up[j, 0]).at[1, r:2 * r].set(gla_w_up[j, 1]).astype(BF16)
                extra = (wup, gla_b_up[j][:, None, :])
                gn, w_out = gla_onorm_g[j], wb_gla_out[j]
                tn = _lane_tile(n_pad, 1024)
                rkind = "gla"
            (p_ctx, o_ctx), (p_lat, o_lat) = _recurrent_mix(
                rkind, xc, xl, (g1, mv[0], mv[1]), ctx_map, w_in, extra, rec_consts,
                b, lc, seq, d, heads, dk, dv, tn)
            gate_col = 2 if kind == 1 else (2 * heads * dk + heads * dv) // d
            tm_g = 256
            xl = _gated_out(o_lat, p_lat, gate_col, gn[None, :], w_out, xl, mv[2],
                            lambda i: (i * tm_g) // seq, heads, tm_g)
            xc = _gated_out(o_ctx, p_ctx, gate_col, gn[None, :], w_out, xc, mv[2], ctx_map, heads, tm_g)
        g2 = norm2_g[i][None, :]
        xl = _ffn(xl, g2, mv[3], mv[4], mv[5], wb_ffn_in, wb_ffn_out, i, lat_map, tm_lat, 512)
        if not last:
            xc = _ffn(xc, g2, mv[3], mv[4], mv[5], wb_ffn_in, wb_ffn_out, i, ctx_map, tm_ctx, 512)
    return _final_norm(xl, final_g[None, :], tm_lat).reshape(b, seq, d)
```

```python
import functools
import math

import ml_dtypes
import numpy as np
import jax
import jax.numpy as jnp
from jax import lax
from jax.experimental import pallas as pl
from jax.experimental.pallas import tpu as pltpu

F32 = jnp.float32
BF16 = jnp.bfloat16

N_MOD = 6
N_MIXERS = 3
RMS_EPS = 1e-6
GRID_W = 64
HYENA_ORDER = 2
FILTER_BANDS = 16
FILTER_EMB = 1 + 2 * FILTER_BANDS
HYENA_DECAY_MIN = math.log(1e-2) / 1.5
HYENA_DECAY_MAX = math.log(1e-2) / 0.3
HGRN_EXPAND = 128
GLA_HEADS = 4
GLA_GATE_RANK = 16
GLA_GATE_NORM = 16.0

LANES = 128
SUBLANES = 8
V7X_VMEM_LIMIT = 56 * 1024 * 1024

FFT_P = 128
REC_CHUNK = 64
MOD_ROWS = 8


def _cparams(*sem):
    return pltpu.CompilerParams(dimension_semantics=sem, vmem_limit_bytes=V7X_VMEM_LIMIT)


def _dot(a, b):
    return jnp.dot(a, b, preferred_element_type=F32)


def _dot_nt(a, b):
    return lax.dot_general(a, b, (((1,), (1,)), ((), ())), preferred_element_type=F32)


def _dot_tn(a, b):
    return lax.dot_general(a, b, (((0,), (0,)), ((), ())), preferred_element_type=F32)


def _split(x):
    hi = x.astype(BF16)
    lo = (x - hi.astype(F32)).astype(BF16)
    return hi, lo


def _stack3(x, pad_rows=0):
    hi, lo = _split(x)
    parts = [hi, lo, hi]
    if pad_rows:
        parts.append(jnp.zeros((pad_rows, x.shape[1]), BF16))
    return jnp.concatenate(parts, axis=0)


def _const3(c, pad_cols=0):
    hi = c.astype(ml_dtypes.bfloat16)
    lo = (c - hi.astype(np.float64)).astype(ml_dtypes.bfloat16)
    parts = [hi, hi, lo]
    if pad_cols:
        parts.append(np.zeros(c.shape[:-1] + (pad_cols,), ml_dtypes.bfloat16))
    return jnp.asarray(np.concatenate(parts, axis=-1))


def _dot3(a, bh, bl):
    ah, al = _split(a)
    return _dot(ah, bh) + _dot(ah, bl) + _dot(al, bh)


def _sigmoid(x):
    return jax.nn.sigmoid(x)


def _normmod(x, g, sh, sc):
    ms = jnp.mean(x * x, axis=-1, keepdims=True)
    y = x * lax.rsqrt(ms + RMS_EPS) * g
    return y * (1.0 + sc) + sh


def _mod_kernel(c_ref, w_ref, b_ref, o_ref):
    c = c_ref[...]
    s = (c * _sigmoid(c)).astype(BF16)
    o_ref[...] = _dot(s, w_ref[...].astype(BF16)) + b_ref[...]


def _modulation(c8, w_mod, b_mod):
    depth, d, n = w_mod.shape
    tn = 1024
    return pl.pallas_call(
        _mod_kernel,
        grid=(depth, n // tn),
        in_specs=[pl.BlockSpec((MOD_ROWS, d), lambda l, j: (0, 0)),
                  pl.BlockSpec((None, d, tn), lambda l, j: (l, 0, j)),
                  pl.BlockSpec((None, 1, tn), lambda l, j: (l, 0, j))],
        out_specs=pl.BlockSpec((None, MOD_ROWS, tn), lambda l, j: (l, 0, j)),
        out_shape=jax.ShapeDtypeStruct((depth, MOD_ROWS, n), F32),
        compiler_params=_cparams("parallel", "parallel"),
        name="modulation",
    )(c8, w_mod, b_mod.reshape(depth, 1, n))


def _mod_spec(d, bmap):
    return pl.BlockSpec((None, 1, d), lambda *idx: (bmap(idx[0]), 0, 0))


def _normmod_kernel(x_ref, g_ref, sh_ref, sc_ref, o_ref):
    o_ref[...] = _normmod(x_ref[...], g_ref[...], sh_ref[...], sc_ref[...]).astype(BF16)


def _norm_modulate(x, g, sh, sc, bmap, tm):
    m, d = x.shape
    return pl.pallas_call(
        _normmod_kernel,
        grid=(m // tm,),
        in_specs=[pl.BlockSpec((tm, d), lambda i: (i, 0)),
                  pl.BlockSpec((1, d), lambda i: (0, 0)),
                  _mod_spec(d, bmap), _mod_spec(d, bmap)],
        out_specs=pl.BlockSpec((tm, d), lambda i: (i, 0)),
        out_shape=jax.ShapeDtypeStruct((m, d), BF16),
        compiler_params=_cparams("parallel"),
        name="norm_modulate",
    )(x, g, sh, sc)


def _final_norm_kernel(x_ref, g_ref, o_ref):
    x = x_ref[...]
    ms = jnp.mean(x * x, axis=-1, keepdims=True)
    o_ref[...] = x * lax.rsqrt(ms + RMS_EPS) * g_ref[...]


def _final_norm(x, g, tm):
    m, d = x.shape
    return pl.pallas_call(
        _final_norm_kernel,
        grid=(m // tm,),
        in_specs=[pl.BlockSpec((tm, d), lambda i: (i, 0)), pl.BlockSpec((1, d), lambda i: (0, 0))],
        out_specs=pl.BlockSpec((tm, d), lambda i: (i, 0)),
        out_shape=jax.ShapeDtypeStruct((m, d), F32),
        compiler_params=_cparams("parallel"),
        name="final_norm",
    )(x, g)


def _lane_tile(n, cap):
    best = n
    for t in range(LANES, min(n, cap) + 1, LANES):
        if n % t == 0:
            best = t
    return best


def _cast_kernel(x_ref, o_ref):
    o_ref[...] = x_ref[...].astype(o_ref.dtype)


def _to_bf16(w):
    shape = w.shape
    w2 = w.reshape(-1, shape[-1])
    r, c = w2.shape
    tr, tc = min(512, r), _lane_tile(c, 2048)
    out = pl.pallas_call(
        _cast_kernel,
        grid=(r // tr, c // tc),
        in_specs=[pl.BlockSpec((tr, tc), lambda i, j: (i, j))],
        out_specs=pl.BlockSpec((tr, tc), lambda i, j: (i, j)),
        out_shape=jax.ShapeDtypeStruct((r, c), BF16),
        compiler_params=_cparams("parallel", "parallel"),
        name="cast_bf16",
    )(w2)
    return out.reshape(shape)


def _proj_kernel(x_ref, g_ref, sh_ref, sc_ref, w_ref, o_ref, h_s):
    @pl.when(pl.program_id(1) == 0)
    def _():
        h_s[...] = _normmod(x_ref[...], g_ref[...], sh_ref[...], sc_ref[...]).astype(BF16)

    o_ref[...] = _dot(h_s[...], w_ref[...])


def _projection(x, g, sh, sc, w, bmap, tm, tn):
    m, d = x.shape
    n = w.shape[1]
    return pl.pallas_call(
        _proj_kernel,
        grid=(m // tm, n // tn),
        in_specs=[pl.BlockSpec((tm, d), lambda i, j: (i, 0)),
                  pl.BlockSpec((1, d), lambda i, j: (0, 0)),
                  _mod_spec(d, bmap), _mod_spec(d, bmap),
                  pl.BlockSpec((d, tn), lambda i, j: (0, j))],
        out_specs=pl.BlockSpec((tm, tn), lambda i, j: (i, j)),
        out_shape=jax.ShapeDtypeStruct((m, n), F32),
        scratch_shapes=[pltpu.VMEM((tm, d), BF16)],
        compiler_params=_cparams("parallel", "arbitrary"),
        name="projection",
    )(x, g, sh, sc, w)


def _out_res_kernel(a_ref, w_ref, x_ref, mg_ref, o_ref):
    o_ref[...] = x_ref[...] + mg_ref[...] * _dot(a_ref[...], w_ref[...])


def _ffn_kernel(x_ref, g_ref, sh_ref, sc_ref, mg_ref, wg_ref, wu_ref, wo_ref, o_ref, h_s, acc_s):
    j = pl.program_id(1)

    @pl.when(j == 0)
    def _():
        h_s[...] = _normmod(x_ref[...], g_ref[...], sh_ref[...], sc_ref[...]).astype(BF16)
        acc_s[...] = jnp.zeros_like(acc_s)

    h = h_s[...]
    a = _dot(h, wg_ref[...])
    u = _dot(h, wu_ref[...])
    act = (a * _sigmoid(a) * u).astype(BF16)
    acc_s[...] += _dot(act, wo_ref[...])

    @pl.when(j == pl.num_programs(1) - 1)
    def _():
        o_ref[...] = x_ref[...] + mg_ref[...] * acc_s[...]


def _ffn(x, g, sh, sc, mg, w_in, w_out, layer, bmap, tm, tf):
    m, d = x.shape
    f = w_out.shape[1]
    nf = f // tf
    return pl.pallas_call(
        _ffn_kernel,
        grid=(m // tm, nf),
        in_specs=[pl.BlockSpec((tm, d), lambda i, j: (i, 0)),
                  pl.BlockSpec((1, d), lambda i, j: (0, 0)),
                  _mod_spec(d, bmap), _mod_spec(d, bmap), _mod_spec(d, bmap),
                  pl.BlockSpec((None, d, tf), lambda i, j: (layer, 0, j)),
                  pl.BlockSpec((None, d, tf), lambda i, j: (layer, 0, nf + j)),
                  pl.BlockSpec((None, tf, d), lambda i, j: (layer, j, 0))],
        out_specs=pl.BlockSpec((tm, d), lambda i, j: (i, 0)),
        out_shape=jax.ShapeDtypeStruct((m, d), F32),
        scratch_shapes=[pltpu.VMEM((tm, d), BF16), pltpu.VMEM((tm, d), F32)],
        compiler_params=_cparams("parallel", "arbitrary"),
        name="ffn",
    )(x, g, sh, sc, mg, w_in, w_in, w_out)


def _conv3_rows(acc, cw, row_len):
    tm = acc.shape[0]
    rid = lax.broadcasted_iota(jnp.int32, (tm, 1), 0) % row_len
    up = jnp.where(rid == 0, 0.0, pltpu.roll(acc, 1, 0))
    dn = jnp.where(rid == row_len - 1, 0.0, pltpu.roll(acc, tm - 1, 0))
    return cw[0:1] * up + cw[1:2] * acc + cw[2:3] * dn


def _hy_in_kernel(a_ref, w_ref, cw_ref, o_ref, *, row_len):
    o_ref[...] = _conv3_rows(_dot(a_ref[...], w_ref[...]), cw_ref[...], row_len)


def _hy_in_lat_kernel(x_ref, g_ref, sh_ref, sc_ref, w_ref, cw_ref, o_ref, h_s, *, row_len, p):
    @pl.when(pl.program_id(1) == 0)
    def _():
        h_s[...] = _normmod(x_ref[...], g_ref[...], sh_ref[...], sc_ref[...]).astype(BF16)

    y = _conv3_rows(_dot(h_s[...], w_ref[...]), cw_ref[...], row_len)
    tm, tn = y.shape
    o_ref[...] = jnp.swapaxes(y.reshape(tm // p, p, tn), 0, 1)


def _hyena_in_lat(x, g, sh, sc, w, cw, b, qh):
    m, d = x.shape
    p = FFT_P
    th = SUBLANES
    tn = min(1024, d)
    nd = d // tn
    nq = qh // th
    bmap = lambda i: i // nq
    return pl.pallas_call(
        functools.partial(_hy_in_lat_kernel, row_len=GRID_W, p=p),
        grid=(b * nq, 3 * nd),
        in_specs=[pl.BlockSpec((th * p, d), lambda i, j: (i, 0)),
                  pl.BlockSpec((1, d), lambda i, j: (0, 0)),
                  _mod_spec(d, bmap), _mod_spec(d, bmap),
                  pl.BlockSpec((d, tn), lambda i, j: (0, j)),
                  pl.BlockSpec((3, tn), lambda i, j: (0, j))],
        out_specs=pl.BlockSpec((None, None, p, th, tn), lambda i, j: (j // nd, i // nq, 0, i % nq, j % nd)),
        out_shape=jax.ShapeDtypeStruct((3, b, p, qh, d), F32),
        scratch_shapes=[pltpu.VMEM((th * p, d), BF16)],
        compiler_params=_cparams("parallel", "arbitrary"),
        name="hyena_in_lat",
    )(x, g, sh, sc, w, cw)


def _hyena_in_ctx(h, w, cw, b, lc):
    m, d = h.shape
    return pl.pallas_call(
        functools.partial(_hy_in_kernel, row_len=lc),
        grid=(3, b),
        in_specs=[pl.BlockSpec((lc, d), lambda j, i: (i, 0)),
                  pl.BlockSpec((d, d), lambda j, i: (0, j)),
                  pl.BlockSpec((3, d), lambda j, i: (0, j))],
        out_specs=pl.BlockSpec((None, lc, d), lambda j, i: (j, i, 0)),
        out_shape=jax.ShapeDtypeStruct((3, m, d), F32),
        compiler_params=_cparams("parallel", "parallel"),
        name="hyena_in_ctx",
    )(h, w, cw)


def _hy_out_lat_kernel(z_ref, w_ref, x_ref, mg_ref, o_ref, a_s):
    @pl.when(pl.program_id(1) == 0)
    def _():
        a_s[...] = jnp.swapaxes(z_ref[...], 0, 1).reshape(a_s.shape).astype(BF16)

    o_ref[...] = x_ref[...] + mg_ref[...] * _dot(a_s[...], w_ref[...])


def _hyena_out_lat(z, w, x, mg, b, qh):
    m, d = x.shape
    p = FFT_P
    th = SUBLANES
    tn = min(1024, d)
    nq = qh // th
    return pl.pallas_call(
        _hy_out_lat_kernel,
        grid=(b * nq, d // tn),
        in_specs=[pl.BlockSpec((None, p, th, d), lambda i, j: (i // nq, 0, i % nq, 0)),
                  pl.BlockSpec((d, tn), lambda i, j: (0, j)),
                  pl.BlockSpec((th * p, tn), lambda i, j: (i, j)),
                  pl.BlockSpec((None, 1, tn), lambda i, j: (i // nq, 0, j))],
        out_specs=pl.BlockSpec((th * p, tn), lambda i, j: (i, j)),
        out_shape=jax.ShapeDtypeStruct((m, d), F32),
        scratch_shapes=[pltpu.VMEM((th * p, d), BF16)],
        compiler_params=_cparams("parallel", "arbitrary"),
        name="hyena_out_lat",
    )(z, w, x, mg)


def _out_res(a, w, x, mg, bmap, tm):
    m, d = x.shape
    return pl.pallas_call(
        _out_res_kernel,
        grid=(m // tm,),
        in_specs=[pl.BlockSpec((tm, d), lambda i: (i, 0)),
                  pl.BlockSpec((d, d), lambda i: (0, 0)),
                  pl.BlockSpec((tm, d), lambda i: (i, 0)),
                  _mod_spec(d, bmap)],
        out_specs=pl.BlockSpec((tm, d), lambda i: (i, 0)),
        out_shape=jax.ShapeDtypeStruct((m, d), F32),
        compiler_params=_cparams("parallel"),
        name="out_residual",
    )(a, w, x, mg)


def _filter_kernel(z_ref, w1h_ref, w1l_ref, b1_ref, fq_ref, w2h_ref, w2l_ref, b2_ref,
                   woh_ref, wol_ref, dl_ref, h_ref, nrm_ref):
    z = z_ref[...]
    t = z[:, 0:1]
    valid = z[:, FILTER_EMB:FILTER_EMB + 1]
    a1 = _dot3(z, w1h_ref[...], w1l_ref[...]) + b1_ref[...]
    hid = jnp.sin(fq_ref[0:1, :] * a1)
    a2 = _dot3(hid, w2h_ref[...], w2l_ref[...]) + b2_ref[...]
    hid = jnp.sin(fq_ref[1:2, :] * a2)
    h = _dot3(hid, woh_ref[...], wol_ref[...]) * jnp.exp(-t * dl_ref[...]) * valid
    h_ref[...] = h.reshape(h_ref.shape)

    @pl.when((pl.program_id(1) == 0) & (pl.program_id(2) == 0))
    def _():
        nrm_ref[...] = jnp.zeros_like(nrm_ref)

    nrm_ref[...] += jnp.sum(jnp.abs(h), axis=0, keepdims=True)


def _pad2(a, rows, cols):
    return jnp.pad(a, ((0, rows - a.shape[0]), (0, cols - a.shape[1])))


def _filter_positions(seq, perm):
    if perm:
        r_hi = seq // FFT_P
        gt, ng = 2 * SUBLANES, FFT_P // (2 * SUBLANES)
        dirs, g, tl, th = np.meshgrid(np.arange(2), np.arange(ng), np.arange(gt), np.arange(r_hi), indexing="ij")
        n = FFT_P * (r_hi * dirs + th) + g * gt + tl
    else:
        r_hi, gt, ng = seq, 1, 1
        dirs, g, tl, th = np.meshgrid(np.arange(2), np.arange(1), np.arange(1), np.arange(seq), indexing="ij")
        n = seq * dirs + th
    n = n.reshape(-1)
    pos = np.where(n < seq, n, 2 * seq - 1 - n).astype(np.float32)
    z = np.zeros((n.size, LANES), np.float32)
    z[:, 0] = pos / np.float32(max(seq - 1, 1))
    bands = np.arange(1, FILTER_BANDS + 1, dtype=np.float32)
    ang = (np.float32(2.0 * math.pi / seq) * pos)[:, None] * bands[None, :]
    z[:, 1:1 + FILTER_BANDS] = np.cos(ang.astype(np.float64))
    z[:, 1 + FILTER_BANDS:FILTER_EMB] = -np.sin(ang.astype(np.float64))
    z[:, FILTER_EMB] = n != seq
    return jnp.asarray(z), gt, ng, r_hi


def _hyena_filter_time(fw1, fb1, ffreq, fw2, fb2, fwout, seq, perm):
    od = fwout.shape[1] // 2
    d = od // HYENA_ORDER
    z, gt, ng, r_hi = _filter_positions(seq, perm)
    rows = gt * r_hi
    ct = min(2048, od)
    hidden = LANES
    w1h, w1l = _split(_pad2(fw1, LANES, hidden))
    w2h, w2l = _split(_pad2(fw2, hidden, hidden))
    woh, wol = _split(_pad2(fwout, hidden, fwout.shape[1]))
    b1 = _pad2(fb1[None, :], 1, hidden)
    b2 = _pad2(fb2[None, :], 1, hidden)
    fq = _pad2(ffreq, 2, hidden)
    deltas = np.abs(np.linspace(HYENA_DECAY_MIN, HYENA_DECAY_MAX, d, dtype=np.float32))
    dl = jnp.asarray(np.tile(deltas, HYENA_ORDER)[None, :])
    nct = od // ct
    small = lambda shape: pl.BlockSpec(shape, lambda c, di, g: (0, 0))
    wspec = pl.BlockSpec((hidden, ct), lambda c, di, g: (0, di * nct + c))
    return pl.pallas_call(
        _filter_kernel,
        grid=(nct, 2, ng),
        in_specs=[pl.BlockSpec((rows, LANES), lambda c, di, g: (di * ng + g, 0)),
                  small((LANES, hidden)), small((LANES, hidden)), small((1, hidden)), small((2, hidden)),
                  small((hidden, hidden)), small((hidden, hidden)), small((1, hidden)),
                  wspec, wspec,
                  pl.BlockSpec((1, ct), lambda c, di, g: (0, c))],
        out_specs=[pl.BlockSpec((None, gt, None, r_hi, ct), lambda c, di, g: (g, 0, di, 0, c)),
                   pl.BlockSpec((1, ct), lambda c, di, g: (0, c))],
        out_shape=[jax.ShapeDtypeStruct((ng, gt, 2, r_hi, od), F32), jax.ShapeDtypeStruct((1, od), F32)],
        compiler_params=_cparams("parallel", "arbitrary", "arbitrary"),
        name="hyena_filter",
    )(z, w1h, w1l, b1, fq, w2h, w2l, b2, woh, wol, dl)


def _fft_consts(seq, k_in):
    n = 2 * seq
    p = FFT_P
    q = n // p
    qh = q // 2
    ka_used = qh + 1
    ka_pad = -(-ka_used // 8) * 8
    ka = np.arange(ka_used)
    tl = np.arange(p)
    th = np.arange(k_in)
    theta = 2 * np.pi * (ka[None, :, None] * th[None, None, :] / q + ka[None, :, None] * tl[:, None, None] / n)
    f1 = np.zeros((p, 2 * ka_pad, k_in))
    f1[:, :ka_used] = np.cos(theta)
    f1[:, ka_pad:ka_pad + ka_used] = -np.sin(theta)
    k3 = 3 * k_in
    f1c = _const3(f1, pad_cols=-(-k3 // LANES) * LANES - k3)
    ang = 2 * np.pi * np.outer(np.arange(p), np.arange(p)) / p
    cc, sc = np.cos(ang), np.sin(ang)
    m2f_np, m2i_np = np.block([[cc, sc], [-sc, cc]]), np.block([[cc, -sc], [sc, cc]])
    m2f = _const3(m2f_np)
    m2f1, m2i1 = (jnp.asarray(m.astype(ml_dtypes.bfloat16)) for m in (m2f_np, m2i_np))
    tho = np.arange(qh)
    phi = 2 * np.pi * (tho[None, :, None] * ka[None, None, :] / q + ka[None, None, :] * tl[:, None, None] / n)
    wgt = np.where((ka == 0) | (ka == qh), 1.0, 2.0) / n
    kr = 2 * ka_pad
    g = np.zeros((p, qh, -(-kr // LANES) * LANES))
    col = 2 * SUBLANES * (ka // SUBLANES) + ka % SUBLANES
    g[:, :, col] = wgt * np.cos(phi)
    g[:, :, col + SUBLANES] = -wgt * np.sin(phi)
    gc = jnp.asarray(g.astype(ml_dtypes.bfloat16))
    return dict(q=q, qh=qh, ka_used=ka_used, ka_pad=ka_pad, f1=f1c, m2f=m2f, m2f1=m2f1, m2i1=m2i1, g=gc)


def _s1_kernel(x_ref, f_ref, ar_ref, ai_ref, a_s, *, group, ka_pad, pad_rows):
    for j in range(group):
        a_s[j] = _dot(f_ref[j], _stack3(x_ref[j], pad_rows))
    a = jnp.swapaxes(a_s[...], 0, 1)
    ar_ref[...] = a[:ka_pad].astype(ar_ref.dtype)
    ai_ref[...] = a[ka_pad:].astype(ai_ref.dtype)


def _fft_stage1(x5, part, f1c, ka_pad, group, dt, out_dtype):
    _, b, p, k_in, d = x5.shape
    kc = f1c.shape[-1]
    out = jax.ShapeDtypeStruct((b, ka_pad, p, d), out_dtype)
    ospec = pl.BlockSpec((None, ka_pad, group, dt), lambda bi, g, c: (bi, 0, g, c))
    return pl.pallas_call(
        functools.partial(_s1_kernel, group=group, ka_pad=ka_pad, pad_rows=kc - 3 * k_in),
        grid=(b, p // group, d // dt),
        in_specs=[pl.BlockSpec((None, None, group, k_in, dt), lambda bi, g, c: (part, bi, g, 0, c)),
                  pl.BlockSpec((group, 2 * ka_pad, kc), lambda bi, g, c: (g, 0, 0))],
        out_specs=[ospec, ospec],
        out_shape=[out, out],
        scratch_shapes=[pltpu.VMEM((group, 2 * ka_pad, dt), F32)],
        compiler_params=_cparams("parallel", "parallel", "parallel"),
        name="fft_stage1",
    )(x5, f1c)


def _s2_filter_kernel(ar_ref, ai_ref, m_ref, inv_ref, hr_ref, hi_ref, *, p):
    x = jnp.concatenate([ar_ref[...], ai_ref[...]], axis=0)
    y = _dot(m_ref[...], _stack3(x)) * inv_ref[...]
    hr_ref[...] = y[:p]
    hi_ref[...] = y[p:]


def _fft_stage2_filter(ar, ai, m2f, inv_nrm, dt):
    ka, p, d = ar.shape
    spec = pl.BlockSpec((None, p, dt), lambda k, c: (k, 0, c))
    out = jax.ShapeDtypeStruct((ka, p, d), F32)
    return pl.pallas_call(
        functools.partial(_s2_filter_kernel, p=p),
        grid=(ka, d // dt),
        in_specs=[spec, spec,
                  pl.BlockSpec((2 * p, 6 * p), lambda k, c: (0, 0)),
                  pl.BlockSpec((1, dt), lambda k, c: (0, c))],
        out_specs=[spec, spec],
        out_shape=[out, out],
        compiler_params=_cparams("parallel", "parallel"),
        name="fft_stage2_filter",
    )(ar, ai, m2f, inv_nrm)


def _s2_kernel(ar_ref, ai_ref, hr_ref, hi_ref, mf_ref, mi_ref, c_ref, x_s, y_s, c_s, *, p, ka_used):
    kg = pl.program_id(0)
    assert ka_used % SUBLANES == 1

    def run(rows):
        for i in rows:
            x_s[i] = _dot(mf_ref[...], jnp.concatenate([ar_ref[i], ai_ref[i]], axis=0))
        for i in rows:
            xr, xi = x_s[i, :p], x_s[i, p:]
            hr, hi = hr_ref[i], hi_ref[i]
            y_s[i, :p] = (xr * hr - xi * hi).astype(BF16)
            y_s[i, p:] = (xr * hi + xi * hr).astype(BF16)
        for i in rows:
            c = _dot(mi_ref[...], y_s[i])
            c_s[i] = c[:p]
            c_s[SUBLANES + i] = c[p:]

    run(range(1))
    full = (kg + 1) * SUBLANES <= ka_used

    @pl.when(full)
    def _():
        run(range(1, SUBLANES))

    @pl.when(jnp.logical_not(full))
    def _():
        for i in range(1, SUBLANES):
            c_s[i] = jnp.zeros(c_s.shape[1:], F32)
            c_s[SUBLANES + i] = jnp.zeros(c_s.shape[1:], F32)

    c_ref[...] = jnp.swapaxes(c_s[...], 0, 1).astype(c_ref.dtype)


def _fft_stage2(ar, ai, hr, hi, m2f, m2i, ka_used, h_col0, dt):
    b, ka, p, d = ar.shape
    hc = h_col0 // dt
    aspec = pl.BlockSpec((None, SUBLANES, p, dt), lambda k, c, bi: (bi, k, 0, c))
    hspec = pl.BlockSpec((SUBLANES, p, dt), lambda k, c, bi: (k, 0, hc + c))
    mspec = pl.BlockSpec((2 * p, 2 * p), lambda k, c, bi: (0, 0))
    return pl.pallas_call(
        functools.partial(_s2_kernel, p=p, ka_used=ka_used),
        grid=(ka // SUBLANES, d // dt, b),
        in_specs=[aspec, aspec, hspec, hspec, mspec, mspec],
        out_specs=pl.BlockSpec((None, p, 2 * SUBLANES, dt), lambda k, c, bi: (bi, 0, k, c)),
        out_shape=jax.ShapeDtypeStruct((b, p, 2 * ka, d), BF16),
        scratch_shapes=[pltpu.VMEM((SUBLANES, 2 * p, dt), F32), pltpu.VMEM((SUBLANES, 2 * p, dt), BF16),
                        pltpu.VMEM((2 * SUBLANES, p, dt), F32)],
        compiler_params=_cparams("parallel", "parallel", "parallel"),
        name="fft_stage2",
    )(ar, ai, hr, hi, m2f, m2i)


def _s3_kernel(c_ref, g_ref, v_ref, x_ref, sk_ref, z_ref, *, group, pad_rows):
    sk = sk_ref[...]
    zpad = jnp.zeros((pad_rows, c_ref.shape[-1]), BF16)
    for j in range(group):
        y = _dot(g_ref[j], jnp.concatenate([c_ref[j], zpad], axis=0))
        v = v_ref[j]
        z_ref[j] = ((y + v * sk) * x_ref[j]).astype(z_ref.dtype)


def _fft_stage3(c, gc, v5, vpart, x5, xpart, skip, group, dt, out_dtype):
    b, p, kr, d = c.shape
    qh = v5.shape[3]
    kc = gc.shape[-1]
    pspec = lambda part: pl.BlockSpec((None, None, group, qh, dt), lambda bi, g, cc: (part, bi, g, 0, cc))
    return pl.pallas_call(
        functools.partial(_s3_kernel, group=group, pad_rows=kc - kr),
        grid=(b, p // group, d // dt),
        in_specs=[pl.BlockSpec((None, group, kr, dt), lambda bi, g, cc: (bi, g, 0, cc)),
                  pl.BlockSpec((group, qh, kc), lambda bi, g, cc: (g, 0, 0)),
                  pspec(vpart), pspec(xpart),
                  pl.BlockSpec((1, dt), lambda bi, g, cc: (0, cc))],
        out_specs=pspec(0),
        out_shape=jax.ShapeDtypeStruct((1, b, p, qh, d), out_dtype),
        compiler_params=_cparams("parallel", "parallel", "parallel"),
        name="fft_stage3",
    )(c, gc, v5, x5, skip)


def _long_conv_lat(v5, vpart, x5, xpart, hr, hi, skip, plan, order, out_dtype):
    d = v5.shape[-1]
    dt1 = min(1024, d)
    dt2 = min(512, d)
    ar, ai = _fft_stage1(v5, vpart, plan["f1"], plan["ka_pad"], 16, dt1, BF16)
    c = _fft_stage2(ar, ai, hr, hi, plan["m2f1"], plan["m2i1"], plan["ka_used"], order * d, dt2)
    return _fft_stage3(c, plan["g"], v5, vpart, x5, xpart, skip, 16, dt1, out_dtype)


def _filter_spectrum_lat(hy, seq, plan_f):
    fw1, fb1, ffreq, fw2, fb2, fwout = hy
    q = plan_f["q"]
    h_time, nrm = _hyena_filter_time(fw1, fb1, ffreq, fw2, fb2, fwout, seq, True)
    od = h_time.shape[-1]
    h5 = h_time.reshape(1, 1, FFT_P, q, od)
    ar, ai = _fft_stage1(h5, 0, plan_f["f1"], plan_f["ka_pad"], 8, min(1024, od), F32)
    return _fft_stage2_filter(ar[0], ai[0], plan_f["m2f"], 1.0 / nrm, min(1024, od))


def _dense_consts(seq):
    n = 2 * seq
    kf = seq + 1
    kf_pad = -(-kf // LANES) * LANES
    k = np.arange(kf)
    fwd_full = np.zeros((2 * kf_pad, n))
    ang = 2 * np.pi * np.outer(k, np.arange(n)) / n
    fwd_full[:kf] = np.cos(ang)
    fwd_full[kf_pad:kf_pad + kf] = -np.sin(ang)
    wgt = np.where((k == 0) | (k == seq), 1.0, 2.0) / n
    inv = np.zeros((seq, 2 * kf_pad))
    angi = 2 * np.pi * np.outer(np.arange(seq), k) / n
    inv[:, :kf] = wgt * np.cos(angi)
    inv[:, kf_pad:kf_pad + kf] = -wgt * np.sin(angi)
    return dict(kf_pad=kf_pad, fwd_full=_const3(fwd_full), fwd=_const3(fwd_full[:, :seq]), inv=_const3(inv))


def _dense_spec_kernel(h_ref, f_ref, inv_ref, hr_ref, hi_ref, *, kf_pad):
    y = _dot(f_ref[...], _stack3(h_ref[...])) * inv_ref[...]
    hr_ref[...] = y[:kf_pad]
    hi_ref[...] = y[kf_pad:]


def _filter_spectrum_ctx(hy, seq, cons):
    fw1, fb1, ffreq, fw2, fb2, fwout = hy
    h_time, nrm = _hyena_filter_time(fw1, fb1, ffreq, fw2, fb2, fwout, seq, False)
    od = h_time.shape[-1]
    n2 = 2 * seq
    h_time = h_time.reshape(n2, od)
    kf_pad = cons["kf_pad"]
    ct = min(512, od)
    out = jax.ShapeDtypeStruct((kf_pad, od), F32)
    ospec = pl.BlockSpec((kf_pad, ct), lambda c: (0, c))
    return pl.pallas_call(
        functools.partial(_dense_spec_kernel, kf_pad=kf_pad),
        grid=(od // ct,),
        in_specs=[pl.BlockSpec((n2, ct), lambda c: (0, c)),
                  pl.BlockSpec((2 * kf_pad, 3 * n2), lambda c: (0, 0)),
                  pl.BlockSpec((1, ct), lambda c: (0, c))],
        out_specs=[ospec, ospec],
        out_shape=[out, out],
        compiler_params=_cparams("parallel"),
        name="dense_filter_spectrum",
    )(h_time, cons["fwd_full"], 1.0 / nrm)


def _dense_conv_kernel(v_ref, x1_ref, x2_ref, h1r_ref, h1i_ref, h2r_ref, h2i_ref, sk_ref,
                       f_ref, g_ref, z_ref, *, kf_pad):
    def conv(u, hr, hi):
        s = _dot(f_ref[...], _stack3(u))
        sr, si = s[:kf_pad], s[kf_pad:]
        y = jnp.concatenate([sr * hr - si * hi, sr * hi + si * hr], axis=0)
        return _dot(g_ref[...], _stack3(y))

    v = v_ref[...]
    z1 = x1_ref[...] * (conv(v, h1r_ref[...], h1i_ref[...]) + v * sk_ref[0:1, :])
    z2 = x2_ref[...] * (conv(z1, h2r_ref[...], h2i_ref[...]) + z1 * sk_ref[1:2, :])
    z_ref[...] = z2.astype(z_ref.dtype)


def _hyena_core_ctx(u3, hr, hi, fskip, cons, b, seq):
    d = u3.shape[-1]
    dt = min(256, d)
    nd = d // dt
    kf_pad = cons["kf_pad"]
    uspec = lambda part: pl.BlockSpec((None, seq, dt), lambda bi, c: (part, bi, c))
    hspec = lambda order: pl.BlockSpec((kf_pad, dt), lambda bi, c: (0, order * nd + c))
    return pl.pallas_call(
        functools.partial(_dense_conv_kernel, kf_pad=kf_pad),
        grid=(b, nd),
        in_specs=[uspec(0), uspec(1), uspec(2), hspec(0), hspec(0), hspec(1), hspec(1),
                  pl.BlockSpec((HYENA_ORDER, dt), lambda bi, c: (0, c)),
                  pl.BlockSpec((2 * kf_pad, 3 * seq), lambda bi, c: (0, 0)),
                  pl.BlockSpec((seq, 6 * kf_pad), lambda bi, c: (0, 0))],
        out_specs=pl.BlockSpec((seq, dt), lambda bi, c: (bi, c)),
        out_shape=jax.ShapeDtypeStruct((b * seq, d), BF16),
        compiler_params=_cparams("parallel", "parallel"),
        name="hyena_core_ctx",
    )(u3, u3, u3, hr, hi, hr, hi, fskip, cons["fwd"], cons["inv"])


def _rec_consts(chunk):
    t = np.arange(chunk)
    coefs, masks = [], []
    for direction in (0, 1):
        if direction == 0:
            rows = [t[None, :] <= t[:, None], t[None, :] > t[:, None]]
        else:
            rows = [t[None, :] >= t[:, None], t[None, :] < t[:, None]]
        mk = [np.eye(chunk)]
        m = chunk // 2
        while m >= 1:
            blk = t // (2 * m)
            half = (t // m) % 2
            mid = blk * 2 * m + m
            e = np.zeros((chunk, chunk))
            for r in range(chunk):
                if direction == 0:
                    if half[r] == 1:
                        e[r, mid[r]:r + 1] = 1
                    else:
                        e[r, r + 1:mid[r]] = 1
                else:
                    if half[r] == 0:
                        e[r, r:mid[r]] = 1
                    else:
                        e[r, mid[r]:r] = 1
            same = blk[:, None] == blk[None, :]
            if direction == 0:
                mk.append(same & (half[:, None] == 1) & (half[None, :] == 0))
            else:
                mk.append(same & (half[:, None] == 0) & (half[None, :] == 1))
            rows.append(e)
            m //= 2
        rows.append(np.ones((16, chunk)))
        a = np.concatenate([np.asarray(r, np.float64) for r in rows], axis=0)
        a3 = np.concatenate([a, a, a], axis=1)
        pad = -(-a3.shape[1] // LANES) * LANES - a3.shape[1]
        a3 = np.pad(a3, ((0, 0), (0, pad)))
        coefs.append(a3)
        masks.append(np.stack([np.asarray(x, np.float32) for x in mk]))
    return (jnp.asarray(np.stack(coefs), dtype=BF16), jnp.asarray(np.stack(masks), dtype=F32))


def _rec_core(q_s, k_s, g, v_ref, coef_ref, mask_ref, s0_ref, o_ref, sfin_ref, st_s, ex_s, att_s,
              *, chunk, heads, dk, dv):
    c = pl.program_id(2)
    levels = int(math.log2(chunk))
    tot = (2 + levels) * chunk

    @pl.when(c == 0)
    def _():
        st_s[...] = s0_ref[...]

    g1 = g.astype(BF16)
    r1 = g - g1.astype(F32)
    g2 = r1.astype(BF16)
    g3 = (r1 - g2.astype(F32)).astype(BF16)
    pad = coef_ref.shape[-1] - 3 * chunk
    gs = jnp.concatenate([g1, g2, g3, jnp.zeros((pad, g.shape[1]), BF16)], axis=0)
    ex_s[...] = jnp.exp(_dot(coef_ref[...], gs))

    for h in range(heads):
        ks = slice(h * dk, (h + 1) * dk)
        qh = q_s[:, ks]
        kh = k_s[:, ks]
        att = mask_ref[0] * _dot_nt(qh.astype(BF16), kh.astype(BF16))
        for lv in range(levels):
            e = ex_s[(2 + lv) * chunk:(3 + lv) * chunk, ks]
            att += mask_ref[1 + lv] * _dot_nt((qh * e).astype(BF16), (kh * e).astype(BF16))
        att_s[h] = att.astype(BF16)

    for h in range(heads):
        ks = slice(h * dk, (h + 1) * dk)
        vs = slice(h * dv, (h + 1) * dv)
        vh = v_ref[:, vs].astype(BF16)
        st = st_s[h]
        o = _dot(att_s[h], vh)
        o += _dot_nt((q_s[:, ks] * ex_s[0:chunk, ks]).astype(BF16), st.astype(BF16))
        o_ref[:, vs] = o
        kd = (k_s[:, ks] * ex_s[chunk:2 * chunk, ks]).astype(BF16)
        st_s[h] = st * ex_s[tot:tot + 1, ks] + _dot_tn(vh, kd)

    @pl.when(c == pl.num_programs(2) - 1)
    def _():
        sfin_ref[...] = st_s[...]


def _hgrn_rec_kernel(q_ref, v_ref, f_ref, lb_ref, coef_ref, mask_ref, s0_ref, o_ref, sfin_ref,
                     q_s, k_s, st_s, ex_s, att_s, **kw):
    qr = q_ref[...]
    q_s[...] = qr * _sigmoid(qr)
    fr = f_ref[...]
    lb = lb_ref[...]
    sg = _sigmoid(fr)
    g = jnp.log(lb + (1.0 - lb) * sg)
    k_s[...] = (1.0 - lb) * (1.0 - sg)
    _rec_core(q_s, k_s, g, v_ref, coef_ref, mask_ref, s0_ref, o_ref, sfin_ref, st_s, ex_s, att_s, **kw)


def _gla_rec_kernel(q_ref, k_ref, v_ref, a_ref, wup_ref, bup_ref, coef_ref, mask_ref, s0_ref,
                    o_ref, sfin_ref, q_s, k_s, st_s, ex_s, att_s, *, qscale, **kw):
    q_s[...] = q_ref[...] * qscale
    k_s[...] = k_ref[...]
    xg = _dot(a_ref[...].astype(BF16), wup_ref[...]) + bup_ref[...]
    g = (jnp.minimum(xg, 0.0) - jnp.log(1.0 + jnp.exp(-jnp.abs(xg)))) * (1.0 / GLA_GATE_NORM)
    _rec_core(q_s, k_s, g, v_ref, coef_ref, mask_ref, s0_ref, o_ref, sfin_ref, st_s, ex_s, att_s, **kw)


def _rec_call(kind, proj, extra, s0, consts, b, seq, d, heads, dk, dv):
    chunk = REC_CHUNK
    nc = seq // chunk
    coef, masks = consts
    hk = heads * dk
    hv = heads * dv
    levels = int(math.log2(chunk))
    n_rows = (2 + levels) * chunk + 16

    def rmap(col):
        return lambda bi, di, c: (bi * nc + c + di * (nc - 1 - 2 * c), col)

    cspecs = [pl.BlockSpec((None, n_rows, coef.shape[-1]), lambda bi, di, c: (di, 0, 0)),
              pl.BlockSpec((None, levels + 1, chunk, chunk), lambda bi, di, c: (di, 0, 0, 0)),
              pl.BlockSpec((None, None, heads, dv, dk), lambda bi, di, c: (bi, di, 0, 0, 0))]
    if kind == "hgrn":
        lb = extra
        kernel = _hgrn_rec_kernel
        in_specs = [pl.BlockSpec((chunk, d), rmap(0)),
                    pl.BlockSpec((chunk, d), rmap(1)),
                    pl.BlockSpec((chunk, d), lambda bi, di, c: (bi * nc + c + di * (nc - 1 - 2 * c), 3 + di)),
                    pl.BlockSpec((None, 1, d), lambda bi, di, c: (di, 0, 0))] + cspecs
        args = (proj, proj, proj, lb, coef, masks, s0)
        kw = {}
    else:
        wup, bup = extra
        kernel = _gla_rec_kernel
        in_specs = [pl.BlockSpec((chunk, hk), rmap(0)),
                    pl.BlockSpec((chunk, hk), rmap(1)),
                    pl.BlockSpec((chunk, hv), rmap(1)),
                    pl.BlockSpec((chunk, LANES), rmap((2 * hk + 2 * hv) // LANES)),
                    pl.BlockSpec((None, LANES, hk), lambda bi, di, c: (di, 0, 0)),
                    pl.BlockSpec((None, 1, hk), lambda bi, di, c: (di, 0, 0))] + cspecs
        args = (proj, proj, proj, proj, wup, bup, coef, masks, s0)
        kw = dict(qscale=float(dk) ** -0.5)
    return pl.pallas_call(
        functools.partial(kernel, chunk=chunk, heads=heads, dk=dk, dv=dv, **kw),
        grid=(b, 2, nc),
        in_specs=in_specs,
        out_specs=[pl.BlockSpec((None, chunk, hv), lambda bi, di, c: (di, bi * nc + c + di * (nc - 1 - 2 * c), 0)),
                   pl.BlockSpec((None, None, heads, dv, dk), lambda bi, di, c: (bi, di, 0, 0, 0))],
        out_shape=[jax.ShapeDtypeStruct((2, b * seq, hv), F32),
                   jax.ShapeDtypeStruct((b, 2, heads, dv, dk), F32)],
        scratch_shapes=[pltpu.VMEM((chunk, hk), F32), pltpu.VMEM((chunk, hk), F32),
                        pltpu.VMEM((heads, dv, dk), F32), pltpu.VMEM((n_rows, hk), F32),
                        pltpu.VMEM((heads, chunk, chunk), BF16)],
        compiler_params=_cparams("parallel", "parallel", "arbitrary"),
        name=kind + "_recurrence",
    )(*args)


def _gated_out_kernel(o_ref, gate_ref, gn_ref, w_ref, x_ref, mg_ref, out_ref, *, heads):
    o = o_ref[0] + o_ref[1]
    d = o.shape[1]
    dh = d // heads
    parts = []
    for h in range(heads):
        seg = o[:, h * dh:(h + 1) * dh]
        ms = jnp.mean(seg * seg, axis=-1, keepdims=True)
        parts.append(seg * lax.rsqrt(ms + RMS_EPS))
    gate = gate_ref[...]
    y = jnp.concatenate(parts, axis=1) * gn_ref[...] * (gate * _sigmoid(gate))
    out_ref[...] = x_ref[...] + mg_ref[...] * _dot(y.astype(BF16), w_ref[...])


def _gated_out(o2, proj, gate_col, gn, w, x, mg, bmap, heads, tm):
    m, d = x.shape
    return pl.pallas_call(
        functools.partial(_gated_out_kernel, heads=heads),
        grid=(m // tm,),
        in_specs=[pl.BlockSpec((2, tm, d), lambda i: (0, i, 0)),
                  pl.BlockSpec((tm, d), lambda i: (i, gate_col)),
                  pl.BlockSpec((1, d), lambda i: (0, 0)),
                  pl.BlockSpec((d, d), lambda i: (0, 0)),
                  pl.BlockSpec((tm, d), lambda i: (i, 0)),
                  _mod_spec(d, bmap)],
        out_specs=pl.BlockSpec((tm, d), lambda i: (i, 0)),
        out_shape=jax.ShapeDtypeStruct((m, d), F32),
        compiler_params=_cparams("parallel"),
        name="gated_out",
    )(o2, proj, gn, w, x, mg)


def _recurrent_mix(kind, x_ctx, x_lat, norm, ctx_map, w_in, extra, consts, b, lc, seq, d, heads, dk, dv, tn):
    g1, sh, sc = norm
    tm_c, tm_l = min(1024, x_ctx.shape[0]), min(1024, x_lat.shape[0])
    proj_ctx = _projection(x_ctx, g1, sh, sc, w_in, ctx_map, tm_c, tn)
    proj_lat = _projection(x_lat, g1, sh, sc, w_in, lambda i: (i * tm_l) // seq, tm_l, tn)
    s0 = jnp.zeros((b, 2, heads, dv, dk), F32)
    o_ctx, s_ctx = _rec_call(kind, proj_ctx, extra, s0, consts, b, lc, d, heads, dk, dv)
    o_lat, _ = _rec_call(kind, proj_lat, extra, s_ctx, consts, b, seq, d, heads, dk, dv)
    return (proj_ctx, o_ctx), (proj_lat, o_lat)


def kernel(x, c, ctx, c_ctx, w_mod, b_mod, norm1_g, norm2_g, w_ffn_in, w_ffn_out, final_g, hy_w_in, hy_conv_w, hy_fw1, hy_fb1, hy_ffreq, hy_fw2, hy_fb2, hy_fwout, hy_fskip, hy_w_out, hg_w_in, hg_lb_logits, hg_onorm_g, hg_w_out, gla_w_in, gla_w_up, gla_b_up, gla_onorm_g, gla_w_out):
    b, seq, d = x.shape
    lc = ctx.shape[1]
    depth = w_mod.shape[0]
    assert b + 1 <= MOD_ROWS and seq % (FFT_P * 2) == 0 and FFT_P % GRID_W == 0
    m_lat, m_ctx = b * seq, b * lc
    ctx_row = b
    tm_lat, tm_ctx = 512, min(512, m_ctx)
    lat_map = lambda i, tm=tm_lat: (i * tm) // seq
    ctx_map = lambda i: ctx_row

    xl = x.reshape(m_lat, d)
    xc = ctx.reshape(m_ctx, d)
    c8 = jnp.zeros((MOD_ROWS, d), F32).at[:b].set(c).at[ctx_row].set(c_ctx)
    mod = _modulation(c8, w_mod, b_mod).reshape(depth, MOD_ROWS, N_MOD, 1, d)

    plan = _fft_consts(seq, seq // FFT_P)
    plan_f = _fft_consts(seq, 2 * seq // FFT_P)
    dense = _dense_consts(lc)
    rec_consts = _rec_consts(REC_CHUNK)
    qh = plan["qh"]

    wb_ffn_in, wb_ffn_out = _to_bf16(w_ffn_in), _to_bf16(w_ffn_out)
    wb_hy_in, wb_hy_out = _to_bf16(hy_w_in), _to_bf16(hy_w_out)
    wb_hg_in, wb_hg_out = _to_bf16(hg_w_in), _to_bf16(hg_w_out)
    wb_gla_in, wb_gla_out = _to_bf16(gla_w_in), _to_bf16(gla_w_out)

    for i in range(depth):
        last = i == depth - 1
        kind, j = i % N_MIXERS, i // N_MIXERS
        mv = [mod[i, :, k] for k in range(N_MOD)]
        g1 = norm1_g[i][None, :]
        need_ctx = (not last) or kind != 0
        if kind == 0:
            w_in, w_out = wb_hy_in[j], wb_hy_out[j]
            hy = (hy_fw1[j], hy_fb1[j], hy_ffreq[j], hy_fw2[j], hy_fb2[j], hy_fwout[j])
            u = _hyena_in_lat(xl, g1, mv[0], mv[1], w_in, hy_conv_w[j], b, qh)
            hr, hi = _filter_spectrum_lat(hy, seq, plan_f)
            z1 = _long_conv_lat(u, 0, u, 1, hr, hi, hy_fskip[j, 0][None, :], plan, 0, F32)
            z2 = _long_conv_lat(z1, 0, u, 2, hr, hi, hy_fskip[j, 1][None, :], plan, 1, F32)
            xl = _hyena_out_lat(z2[0], w_out, xl, mv[2], b, qh)
            if need_ctx:
                h_ctx = _norm_modulate(xc, g1, mv[0], mv[1], ctx_map, tm_ctx)
                uc = _hyena_in_ctx(h_ctx, w_in, hy_conv_w[j], b, lc)
                hcr, hci = _filter_spectrum_ctx(hy, lc, dense)
                zc = _hyena_core_ctx(uc, hcr, hci, hy_fskip[j], dense, b, lc)
                xc = _out_res(zc, w_out, xc, mv[2], ctx_map, tm_ctx)
        else:
            if kind == 1:
                heads = d // HGRN_EXPAND
                dk = dv = HGRN_EXPAND
                w_in = wb_hg_in[j]
                lb_cum = jnp.cumsum(jax.nn.softmax(hg_lb_logits.astype(F32), axis=1), axis=1)
                extra = (lb_cum[:, i] - lb_cum[:, 0])[:, None, :]
                gn, w_out, tn = hg_onorm_g[j], wb_hg_out[j], 512
                rkind = "hgrn"
            else:
                heads = GLA_HEADS
                dk, dv = d // 2 // heads, d // heads
                n_in = gla_w_in.shape[-1]
                n_pad = -(-n_in // LANES) * LANES
                w_in = jnp.pad(wb_gla_in[j], ((0, 0), (0, n_pad - n_in)))
                r = GLA_GATE_RANK
                wup = jnp.zeros((2, LANES, heads * dk), F32)
                wup = wup.at[0, :r].set(gla_w_up[j, 0]).at[1, r:2 * r].set(gla_w_up[j, 1]).astype(BF16)
                extra = (wup, gla_b_up[j][:, None, :])
                gn, w_out = gla_onorm_g[j], wb_gla_out[j]
                tn = _lane_tile(n_pad, 1024)
                rkind = "gla"
            (p_ctx, o_ctx), (p_lat, o_lat) = _recurrent_mix(
                rkind, xc, xl, (g1, mv[0], mv[1]), ctx_map, w_in, extra, rec_consts,
                b, lc, seq, d, heads, dk, dv, tn)
            gate_col = 2 if kind == 1 else (2 * heads * dk + heads * dv) // d
            tm_g = 256
            xl = _gated_out(o_lat, p_lat, gate_col, gn[None, :], w_out, xl, mv[2],
                            lambda i: (i * tm_g) // seq, heads, tm_g)
            xc = _gated_out(o_ctx, p_ctx, gate_col, gn[None, :], w_out, xc, mv[2], ctx_map, heads, tm_g)
        g2 = norm2_g[i][None, :]
        xl = _ffn(xl, g2, mv[3], mv[4], mv[5], wb_ffn_in, wb_ffn_out, i, lat_map, tm_lat, 512)
        if not last:
            xc = _ffn(xc, g2, mv[3], mv[4], mv[5], wb_ffn_in, wb_ffn_out, i, ctx_map, tm_ctx, 512)
    return _final_norm(xl, final_g[None, :], tm_lat).reshape(b, seq, d)
```

```python
import functools
import math

import ml_dtypes
import numpy as np
import jax
import jax.numpy as jnp
from jax import lax
from jax.experimental import pallas as pl
from jax.experimental.pallas import tpu as pltpu

F32 = jnp.float32
BF16 = jnp.bfloat16

N_MOD = 6
N_MIXERS = 3
RMS_EPS = 1e-6
GRID_W = 64
HYENA_ORDER = 2
FILTER_BANDS = 16
FILTER_EMB = 1 + 2 * FILTER_BANDS
HYENA_DECAY_MIN = math.log(1e-2) / 1.5
HYENA_DECAY_MAX = math.log(1e-2) / 0.3
HGRN_EXPAND = 128
GLA_HEADS = 4
GLA_GATE_RANK = 16
GLA_GATE_NORM = 16.0

LANES = 128
SUBLANES = 8
V7X_VMEM_LIMIT = 56 * 1024 * 1024

FFT_P = 128
REC_CHUNK = 64
MOD_ROWS = 8


def _cparams(*sem):
    return pltpu.CompilerParams(dimension_semantics=sem, vmem_limit_bytes=V7X_VMEM_LIMIT)


def _dot(a, b):
    return jnp.dot(a, b, preferred_element_type=F32)


def _dot_nt(a, b):
    return lax.dot_general(a, b, (((1,), (1,)), ((), ())), preferred_element_type=F32)


def _dot_tn(a, b):
    return lax.dot_general(a, b, (((0,), (0,)), ((), ())), preferred_element_type=F32)


def _split(x):
    hi = x.astype(BF16)
    lo = (x - hi.astype(F32)).astype(BF16)
    return hi, lo


def _stack3(x, pad_rows=0):
    hi, lo = _split(x)
    parts = [hi, lo, hi]
    if pad_rows:
        parts.append(jnp.zeros((pad_rows, x.shape[1]), BF16))
    return jnp.concatenate(parts, axis=0)


def _const3(c, pad_cols=0):
    hi = c.astype(ml_dtypes.bfloat16)
    lo = (c - hi.astype(np.float64)).astype(ml_dtypes.bfloat16)
    parts = [hi, hi, lo]
    if pad_cols:
        parts.append(np.zeros(c.shape[:-1] + (pad_cols,), ml_dtypes.bfloat16))
    return jnp.asarray(np.concatenate(parts, axis=-1))


def _dot3(a, bh, bl):
    ah, al = _split(a)
    return _dot(ah, bh) + _dot(ah, bl) + _dot(al, bh)


def _sigmoid(x):
    return jax.nn.sigmoid(x)


def _normmod(x, g, sh, sc):
    ms = jnp.mean(x * x, axis=-1, keepdims=True)
    y = x * lax.rsqrt(ms + RMS_EPS) * g
    return y * (1.0 + sc) + sh


def _mod_kernel(c_ref, w_ref, b_ref, o_ref):
    c = c_ref[...]
    s = (c * _sigmoid(c)).astype(BF16)
    o_ref[...] = _dot(s, w_ref[...].astype(BF16)) + b_ref[...]


def _modulation(c8, w_mod, b_mod):
    depth, d, n = w_mod.shape
    tn = 1024
    return pl.pallas_call(
        _mod_kernel,
        grid=(depth, n // tn),
        in_specs=[pl.BlockSpec((MOD_ROWS, d), lambda l, j: (0, 0)),
                  pl.BlockSpec((None, d, tn), lambda l, j: (l, 0, j)),
                  pl.BlockSpec((None, 1, tn), lambda l, j: (l, 0, j))],
        out_specs=pl.BlockSpec((None, MOD_ROWS, tn), lambda l, j: (l, 0, j)),
        out_shape=jax.ShapeDtypeStruct((depth, MOD_ROWS, n), F32),
        compiler_params=_cparams("parallel", "parallel"),
        name="modulation",
    )(c8, w_mod, b_mod.reshape(depth, 1, n))


def _mod_spec(d, bmap):
    return pl.BlockSpec((None, 1, d), lambda *idx: (bmap(idx[0]), 0, 0))


def _normmod_kernel(x_ref, g_ref, sh_ref, sc_ref, o_ref):
    o_ref[...] = _normmod(x_ref[...], g_ref[...], sh_ref[...], sc_ref[...]).astype(BF16)


def _norm_modulate(x, g, sh, sc, bmap, tm):
    m, d = x.shape
    return pl.pallas_call(
        _normmod_kernel,
        grid=(m // tm,),
        in_specs=[pl.BlockSpec((tm, d), lambda i: (i, 0)),
                  pl.BlockSpec((1, d), lambda i: (0, 0)),
                  _mod_spec(d, bmap), _mod_spec(d, bmap)],
        out_specs=pl.BlockSpec((tm, d), lambda i: (i, 0)),
        out_shape=jax.ShapeDtypeStruct((m, d), BF16),
        compiler_params=_cparams("parallel"),
        name="norm_modulate",
    )(x, g, sh, sc)


def _final_norm_kernel(x_ref, g_ref, o_ref):
    x = x_ref[...]
    ms = jnp.mean(x * x, axis=-1, keepdims=True)
    o_ref[...] = x * lax.rsqrt(ms + RMS_EPS) * g_ref[...]


def _final_norm(x, g, tm):
    m, d = x.shape
    return pl.pallas_call(
        _final_norm_kernel,
        grid=(m // tm,),
        in_specs=[pl.BlockSpec((tm, d), lambda i: (i, 0)), pl.BlockSpec((1, d), lambda i: (0, 0))],
        out_specs=pl.BlockSpec((tm, d), lambda i: (i, 0)),
        out_shape=jax.ShapeDtypeStruct((m, d), F32),
        compiler_params=_cparams("parallel"),
        name="final_norm",
    )(x, g)


def _lane_tile(n, cap):
    best = n
    for t in range(LANES, min(n, cap) + 1, LANES):
        if n % t == 0:
            best = t
    return best


def _cast_kernel(x_ref, o_ref):
    o_ref[...] = x_ref[...].astype(o_ref.dtype)


def _to_bf16(w):
    shape = w.shape
    w2 = w.reshape(-1, shape[-1])
    r, c = w2.shape
    tr, tc = min(512, r), _lane_tile(c, 2048)
    out = pl.pallas_call(
        _cast_kernel,
        grid=(r // tr, c // tc),
        in_specs=[pl.BlockSpec((tr, tc), lambda i, j: (i, j))],
        out_specs=pl.BlockSpec((tr, tc), lambda i, j: (i, j)),
        out_shape=jax.ShapeDtypeStruct((r, c), BF16),
        compiler_params=_cparams("parallel", "parallel"),
        name="cast_bf16",
    )(w2)
    return out.reshape(shape)


def _proj_kernel(x_ref, g_ref, sh_ref, sc_ref, w_ref, o_ref, h_s):
    @pl.when(pl.program_id(1) == 0)
    def _():
        h_s[...] = _normmod(x_ref[...], g_ref[...], sh_ref[...], sc_ref[...]).astype(BF16)

    o_ref[...] = _dot(h_s[...], w_ref[...])


def _projection(x, g, sh, sc, w, bmap, tm, tn):
    m, d = x.shape
    n = w.shape[1]
    return pl.pallas_call(
        _proj_kernel,
        grid=(m // tm, n // tn),
        in_specs=[pl.BlockSpec((tm, d), lambda i, j: (i, 0)),
                  pl.BlockSpec((1, d), lambda i, j: (0, 0)),
                  _mod_spec(d, bmap), _mod_spec(d, bmap),
                  pl.BlockSpec((d, tn), lambda i, j: (0, j))],
        out_specs=pl.BlockSpec((tm, tn), lambda i, j: (i, j)),
        out_shape=jax.ShapeDtypeStruct((m, n), F32),
        scratch_shapes=[pltpu.VMEM((tm, d), BF16)],
        compiler_params=_cparams("parallel", "arbitrary"),
        name="projection",
    )(x, g, sh, sc, w)


def _out_res_kernel(a_ref, w_ref, x_ref, mg_ref, o_ref):
    o_ref[...] = x_ref[...] + mg_ref[...] * _dot(a_ref[...], w_ref[...])


def _ffn_kernel(x_ref, g_ref, sh_ref, sc_ref, mg_ref, wg_ref, wu_ref, wo_ref, o_ref, h_s, acc_s):
    j = pl.program_id(1)

    @pl.when(j == 0)
    def _():
        h_s[...] = _normmod(x_ref[...], g_ref[...], sh_ref[...], sc_ref[...]).astype(BF16)
        acc_s[...] = jnp.zeros_like(acc_s)

    h = h_s[...]
    a = _dot(h, wg_ref[...])
    u = _dot(h, wu_ref[...])
    act = (a * _sigmoid(a) * u).astype(BF16)
    acc_s[...] += _dot(act, wo_ref[...])

    @pl.when(j == pl.num_programs(1) - 1)
    def _():
        o_ref[...] = x_ref[...] + mg_ref[...] * acc_s[...]


def _ffn(x, g, sh, sc, mg, w_in, w_out, layer, bmap, tm, tf):
    m, d = x.shape
    f = w_out.shape[1]
    nf = f // tf
    return pl.pallas_call(
        _ffn_kernel,
        grid=(m // tm, nf),
        in_specs=[pl.BlockSpec((tm, d), lambda i, j: (i, 0)),
                  pl.BlockSpec((1, d), lambda i, j: (0, 0)),
                  _mod_spec(d, bmap), _mod_spec(d, bmap), _mod_spec(d, bmap),
                  pl.BlockSpec((None, d, tf), lambda i, j: (layer, 0, j)),
                  pl.BlockSpec((None, d, tf), lambda i, j: (layer, 0, nf + j)),
                  pl.BlockSpec((None, tf, d), lambda i, j: (layer, j, 0))],
        out_specs=pl.BlockSpec((tm, d), lambda i, j: (i, 0)),
        out_shape=jax.ShapeDtypeStruct((m, d), F32),
        scratch_shapes=[pltpu.VMEM((tm, d), BF16), pltpu.VMEM((tm, d), F32)],
        compiler_params=_cparams("parallel", "arbitrary"),
        name="ffn",
    )(x, g, sh, sc, mg, w_in, w_in, w_out)


def _conv3_rows(acc, cw, row_len):
    tm = acc.shape[0]
    rid = lax.broadcasted_iota(jnp.int32, (tm, 1), 0) % row_len
    up = jnp.where(rid == 0, 0.0, pltpu.roll(acc, 1, 0))
    dn = jnp.where(rid == row_len - 1, 0.0, pltpu.roll(acc, tm - 1, 0))
    return cw[0:1] * up + cw[1:2] * acc + cw[2:3] * dn


MXU_N = 256


def _hy_in_kernel(x_ref, g_ref, sh_ref, sc_ref, w_ref, cw_ref, o_ref, h_s, *, row_len):
    @pl.when(pl.program_id(1) == 0)
    def _():
        h_s[...] = _normmod(x_ref[...], g_ref[...], sh_ref[...], sc_ref[...]).astype(BF16)

    h = h_s[...]
    for n0 in range(0, o_ref.shape[1], MXU_N):
        ns = slice(n0, n0 + MXU_N)
        o_ref[:, ns] = _conv3_rows(_dot(h, w_ref[:, ns]), cw_ref[:, ns], row_len)


def _hyena_in(x, g, sh, sc, w, cw, bmap, tm, tn, row_len):
    m, d = x.shape
    nd = d // tn
    return pl.pallas_call(
        functools.partial(_hy_in_kernel, row_len=row_len),
        grid=(m // tm, 3 * nd),
        in_specs=[pl.BlockSpec((tm, d), lambda i, j: (i, 0)),
                  pl.BlockSpec((1, d), lambda i, j: (0, 0)),
                  _mod_spec(d, bmap), _mod_spec(d, bmap),
                  pl.BlockSpec((d, tn), lambda i, j: (0, j)),
                  pl.BlockSpec((3, tn), lambda i, j: (0, j))],
        out_specs=pl.BlockSpec((None, tm, tn), lambda i, j: (j // nd, i, j % nd)),
        out_shape=jax.ShapeDtypeStruct((3, m, d), F32),
        scratch_shapes=[pltpu.VMEM((tm, d), BF16)],
        compiler_params=_cparams("parallel", "arbitrary"),
        name="hyena_in",
    )(x, g, sh, sc, w, cw)


def _out_res(a, w, x, mg, bmap, tm):
    m, d = x.shape
    return pl.pallas_call(
        _out_res_kernel,
        grid=(m // tm,),
        in_specs=[pl.BlockSpec((tm, d), lambda i: (i, 0)),
                  pl.BlockSpec((d, d), lambda i: (0, 0)),
                  pl.BlockSpec((tm, d), lambda i: (i, 0)),
                  _mod_spec(d, bmap)],
        out_specs=pl.BlockSpec((tm, d), lambda i: (i, 0)),
        out_shape=jax.ShapeDtypeStruct((m, d), F32),
        compiler_params=_cparams("parallel"),
        name="out_residual",
    )(a, w, x, mg)


def _filter_kernel(z_ref, w1h_ref, w1l_ref, b1_ref, fq_ref, w2h_ref, w2l_ref, b2_ref,
                   woh_ref, wol_ref, dl_ref, h_ref, nrm_ref):
    z = z_ref[...]
    t = z[:, 0:1]
    valid = z[:, FILTER_EMB:FILTER_EMB + 1]
    a1 = _dot3(z, w1h_ref[...], w1l_ref[...]) + b1_ref[...]
    hid = jnp.sin(fq_ref[0:1, :] * a1)
    a2 = _dot3(hid, w2h_ref[...], w2l_ref[...]) + b2_ref[...]
    hid = jnp.sin(fq_ref[1:2, :] * a2)
    h = _dot3(hid, woh_ref[...], wol_ref[...]) * jnp.exp(-t * dl_ref[...]) * valid
    h_ref[...] = h.reshape(h_ref.shape)

    @pl.when((pl.program_id(1) == 0) & (pl.program_id(2) == 0))
    def _():
        nrm_ref[...] = jnp.zeros_like(nrm_ref)

    nrm_ref[...] += jnp.sum(jnp.abs(h), axis=0, keepdims=True)


def _pad2(a, rows, cols):
    return jnp.pad(a, ((0, rows - a.shape[0]), (0, cols - a.shape[1])))


def _filter_positions(seq, perm):
    if perm:
        r_hi = seq // FFT_P
        gt, ng = 2 * SUBLANES, FFT_P // (2 * SUBLANES)
        dirs, g, tl, th = np.meshgrid(np.arange(2), np.arange(ng), np.arange(gt), np.arange(r_hi), indexing="ij")
        n = FFT_P * (r_hi * dirs + th) + g * gt + tl
    else:
        r_hi, gt, ng = seq, 1, 1
        dirs, g, tl, th = np.meshgrid(np.arange(2), np.arange(1), np.arange(1), np.arange(seq), indexing="ij")
        n = seq * dirs + th
    n = n.reshape(-1)
    pos = np.where(n < seq, n, 2 * seq - 1 - n).astype(np.float32)
    z = np.zeros((n.size, LANES), np.float32)
    z[:, 0] = pos / np.float32(max(seq - 1, 1))
    bands = np.arange(1, FILTER_BANDS + 1, dtype=np.float32)
    ang = (np.float32(2.0 * math.pi / seq) * pos)[:, None] * bands[None, :]
    z[:, 1:1 + FILTER_BANDS] = np.cos(ang.astype(np.float64))
    z[:, 1 + FILTER_BANDS:FILTER_EMB] = -np.sin(ang.astype(np.float64))
    z[:, FILTER_EMB] = n != seq
    return jnp.asarray(z), gt, ng, r_hi


def _hyena_filter_time(fw1, fb1, ffreq, fw2, fb2, fwout, seq, perm):
    od = fwout.shape[1] // 2
    d = od // HYENA_ORDER
    z, gt, ng, r_hi = _filter_positions(seq, perm)
    rows = gt * r_hi
    ct = min(2048, od)
    hidden = LANES
    w1h, w1l = _split(_pad2(fw1, LANES, hidden))
    w2h, w2l = _split(_pad2(fw2, hidden, hidden))
    woh, wol = _split(_pad2(fwout, hidden, fwout.shape[1]))
    b1 = _pad2(fb1[None, :], 1, hidden)
    b2 = _pad2(fb2[None, :], 1, hidden)
    fq = _pad2(ffreq, 2, hidden)
    deltas = np.abs(np.linspace(HYENA_DECAY_MIN, HYENA_DECAY_MAX, d, dtype=np.float32))
    dl = jnp.asarray(np.tile(deltas, HYENA_ORDER)[None, :])
    nct = od // ct
    small = lambda shape: pl.BlockSpec(shape, lambda c, di, g: (0, 0))
    wspec = pl.BlockSpec((hidden, ct), lambda c, di, g: (0, di * nct + c))
    return pl.pallas_call(
        _filter_kernel,
        grid=(nct, 2, ng),
        in_specs=[pl.BlockSpec((rows, LANES), lambda c, di, g: (di * ng + g, 0)),
                  small((LANES, hidden)), small((LANES, hidden)), small((1, hidden)), small((2, hidden)),
                  small((hidden, hidden)), small((hidden, hidden)), small((1, hidden)),
                  wspec, wspec,
                  pl.BlockSpec((1, ct), lambda c, di, g: (0, c))],
        out_specs=[pl.BlockSpec((None, gt, None, r_hi, ct), lambda c, di, g: (g, 0, di, 0, c)),
                   pl.BlockSpec((1, ct), lambda c, di, g: (0, c))],
        out_shape=[jax.ShapeDtypeStruct((ng, gt, 2, r_hi, od), F32), jax.ShapeDtypeStruct((1, od), F32)],
        compiler_params=_cparams("parallel", "arbitrary", "arbitrary"),
        name="hyena_filter",
    )(z, w1h, w1l, b1, fq, w2h, w2l, b2, woh, wol, dl)


def _fft_consts(seq, k_in):
    n = 2 * seq
    p = FFT_P
    q = n // p
    qh = q // 2
    ka_used = qh + 1
    ka_pad = -(-ka_used // 8) * 8
    ka = np.arange(ka_used)
    tl = np.arange(p)
    th = np.arange(k_in)
    theta = 2 * np.pi * (ka[None, :, None] * th[None, None, :] / q + ka[None, :, None] * tl[:, None, None] / n)
    f1 = np.zeros((p, 2 * ka_pad, k_in))
    f1[:, :ka_used] = np.cos(theta)
    f1[:, ka_pad:ka_pad + ka_used] = -np.sin(theta)
    k3 = 3 * k_in
    f1c = _const3(f1, pad_cols=-(-k3 // LANES) * LANES - k3)
    ang = 2 * np.pi * np.outer(np.arange(p), np.arange(p)) / p
    cc, sc = np.cos(ang), np.sin(ang)
    m2f_np, m2i_np = np.block([[cc, sc], [-sc, cc]]), np.block([[cc, -sc], [sc, cc]])
    m2f = _const3(m2f_np)
    m2f1, m2i1 = (jnp.asarray(m.astype(ml_dtypes.bfloat16)) for m in (m2f_np, m2i_np))
    tho = np.arange(qh)
    phi = 2 * np.pi * (tho[None, :, None] * ka[None, None, :] / q + ka[None, None, :] * tl[:, None, None] / n)
    wgt = np.where((ka == 0) | (ka == qh), 1.0, 2.0) / n
    kr = 2 * ka_pad
    g = np.zeros((p, qh, -(-kr // LANES) * LANES))
    col = 2 * SUBLANES * (ka // SUBLANES) + ka % SUBLANES
    g[:, :, col] = wgt * np.cos(phi)
    g[:, :, col + SUBLANES] = -wgt * np.sin(phi)
    gc = jnp.asarray(g.astype(ml_dtypes.bfloat16))
    return dict(q=q, qh=qh, ka_used=ka_used, ka_pad=ka_pad, f1=f1c, m2f=m2f, m2f1=m2f1, m2i1=m2i1, g=gc)


def _s1_kernel(x_ref, f_ref, ar_ref, ai_ref, a_s, *, group, ka_pad, pad_rows, natural):
    x = jnp.swapaxes(x_ref[...], 0, 1) if natural else x_ref[...]
    for j in range(group):
        a_s[j] = _dot(f_ref[j], _stack3(x[j], pad_rows))
    a = jnp.swapaxes(a_s[...], 0, 1)
    ar_ref[...] = a[:ka_pad].astype(ar_ref.dtype)
    ai_ref[...] = a[ka_pad:].astype(ai_ref.dtype)


def _fft_stage1(x5, part, f1c, ka_pad, group, dt, out_dtype, natural):
    if natural:
        _, b, k_in, p, d = x5.shape
        xspec = pl.BlockSpec((None, None, k_in, group, dt), lambda bi, g, c: (part, bi, 0, g, c))
    else:
        _, b, p, k_in, d = x5.shape
        xspec = pl.BlockSpec((None, None, group, k_in, dt), lambda bi, g, c: (part, bi, g, 0, c))
    kc = f1c.shape[-1]
    out = jax.ShapeDtypeStruct((b, ka_pad, p, d), out_dtype)
    ospec = pl.BlockSpec((None, ka_pad, group, dt), lambda bi, g, c: (bi, 0, g, c))
    return pl.pallas_call(
        functools.partial(_s1_kernel, group=group, ka_pad=ka_pad, pad_rows=kc - 3 * k_in, natural=natural),
        grid=(b, p // group, d // dt),
        in_specs=[xspec,
                  pl.BlockSpec((group, 2 * ka_pad, kc), lambda bi, g, c: (g, 0, 0))],
        out_specs=[ospec, ospec],
        out_shape=[out, out],
        scratch_shapes=[pltpu.VMEM((group, 2 * ka_pad, dt), F32)],
        compiler_params=_cparams("parallel", "parallel", "parallel"),
        name="fft_stage1",
    )(x5, f1c)


def _s2_filter_kernel(ar_ref, ai_ref, m_ref, inv_ref, hr_ref, hi_ref, *, p):
    x = jnp.concatenate([ar_ref[...], ai_ref[...]], axis=0)
    y = _dot(m_ref[...], x.astype(BF16)) * inv_ref[...]
    hr_ref[...] = y[:p]
    hi_ref[...] = y[p:]


def _fft_stage2_filter(ar, ai, m2f, inv_nrm, dt):
    ka, p, d = ar.shape
    spec = pl.BlockSpec((None, p, dt), lambda k, c: (k, 0, c))
    out = jax.ShapeDtypeStruct((ka, p, d), F32)
    return pl.pallas_call(
        functools.partial(_s2_filter_kernel, p=p),
        grid=(ka, d // dt),
        in_specs=[spec, spec,
                  pl.BlockSpec((2 * p, 2 * p), lambda k, c: (0, 0)),
                  pl.BlockSpec((1, dt), lambda k, c: (0, c))],
        out_specs=[spec, spec],
        out_shape=[out, out],
        compiler_params=_cparams("parallel", "parallel"),
        name="fft_stage2_filter",
    )(ar, ai, m2f, inv_nrm)


def _s2_kernel(ar_ref, ai_ref, hr_ref, hi_ref, mf_ref, mi_ref, c_ref, x_s, y_s, c_s, *, p, ka_used):
    kg = pl.program_id(0)
    assert ka_used % SUBLANES == 1

    def run(rows):
        for i in rows:
            x_s[i] = _dot(mf_ref[...], jnp.concatenate([ar_ref[i], ai_ref[i]], axis=0))
        for i in rows:
            xr, xi = x_s[i, :p], x_s[i, p:]
            hr, hi = hr_ref[i], hi_ref[i]
            y_s[i, :p] = (xr * hr - xi * hi).astype(BF16)
            y_s[i, p:] = (xr * hi + xi * hr).astype(BF16)
        for i in rows:
            c = _dot(mi_ref[...], y_s[i])
            c_s[i] = c[:p]
            c_s[SUBLANES + i] = c[p:]

    run(range(1))
    full = (kg + 1) * SUBLANES <= ka_used

    @pl.when(full)
    def _():
        run(range(1, SUBLANES))

    @pl.when(jnp.logical_not(full))
    def _():
        for i in range(1, SUBLANES):
            c_s[i] = jnp.zeros(c_s.shape[1:], F32)
            c_s[SUBLANES + i] = jnp.zeros(c_s.shape[1:], F32)

    c_ref[...] = jnp.swapaxes(c_s[...], 0, 1).astype(c_ref.dtype)


def _fft_stage2(ar, ai, hr, hi, m2f, m2i, ka_used, h_col0, dt):
    b, ka, p, d = ar.shape
    hc = h_col0 // dt
    aspec = pl.BlockSpec((None, SUBLANES, p, dt), lambda k, c, bi: (bi, k, 0, c))
    hspec = pl.BlockSpec((SUBLANES, p, dt), lambda k, c, bi: (k, 0, hc + c))
    mspec = pl.BlockSpec((2 * p, 2 * p), lambda k, c, bi: (0, 0))
    return pl.pallas_call(
        functools.partial(_s2_kernel, p=p, ka_used=ka_used),
        grid=(ka // SUBLANES, d // dt, b),
        in_specs=[aspec, aspec, hspec, hspec, mspec, mspec],
        out_specs=pl.BlockSpec((None, p, 2 * SUBLANES, dt), lambda k, c, bi: (bi, 0, k, c)),
        out_shape=jax.ShapeDtypeStruct((b, p, 2 * ka, d), BF16),
        scratch_shapes=[pltpu.VMEM((SUBLANES, 2 * p, dt), F32), pltpu.VMEM((SUBLANES, 2 * p, dt), BF16),
                        pltpu.VMEM((2 * SUBLANES, p, dt), F32)],
        compiler_params=_cparams("parallel", "parallel", "parallel"),
        name="fft_stage2",
    )(ar, ai, hr, hi, m2f, m2i)


def _s3_kernel(c_ref, g_ref, v_ref, x_ref, sk_ref, z_ref, y_s, *, group, pad_rows):
    zpad = jnp.zeros((pad_rows, c_ref.shape[-1]), BF16)
    for j in range(group):
        y_s[j] = _dot(g_ref[j], jnp.concatenate([c_ref[j], zpad], axis=0))
    y = jnp.swapaxes(y_s[...], 0, 1)
    v = v_ref[...]
    z_ref[...] = ((y + v * sk_ref[...]) * x_ref[...]).astype(z_ref.dtype)


def _fft_stage3(c, gc, v5, vpart, x5, xpart, skip, group, dt, out_dtype):
    b, p, kr, d = c.shape
    qh = v5.shape[2]
    kc = gc.shape[-1]
    pspec = lambda part: pl.BlockSpec((None, None, qh, group, dt), lambda bi, g, cc: (part, bi, 0, g, cc))
    return pl.pallas_call(
        functools.partial(_s3_kernel, group=group, pad_rows=kc - kr),
        grid=(b, p // group, d // dt),
        in_specs=[pl.BlockSpec((None, group, kr, dt), lambda bi, g, cc: (bi, g, 0, cc)),
                  pl.BlockSpec((group, qh, kc), lambda bi, g, cc: (g, 0, 0)),
                  pspec(vpart), pspec(xpart),
                  pl.BlockSpec((1, dt), lambda bi, g, cc: (0, cc))],
        out_specs=pspec(0),
        out_shape=jax.ShapeDtypeStruct((1, b, qh, p, d), out_dtype),
        scratch_shapes=[pltpu.VMEM((group, qh, dt), F32)],
        compiler_params=_cparams("parallel", "parallel", "parallel"),
        name="fft_stage3",
    )(c, gc, v5, x5, skip)


def _long_conv_lat(v5, vpart, x5, xpart, hr, hi, skip, plan, order, out_dtype):
    d = v5.shape[-1]
    dt1 = min(1024, d)
    dt2 = min(512, d)
    ar, ai = _fft_stage1(v5, vpart, plan["f1"], plan["ka_pad"], 16, dt1, BF16, True)
    c = _fft_stage2(ar, ai, hr, hi, plan["m2f1"], plan["m2i1"], plan["ka_used"], order * d, dt2)
    return _fft_stage3(c, plan["g"], v5, vpart, x5, xpart, skip, 16, dt1, out_dtype)


def _filter_spectrum_lat(hy, seq, plan_f):
    fw1, fb1, ffreq, fw2, fb2, fwout = hy
    q = plan_f["q"]
    h_time, nrm = _hyena_filter_time(fw1, fb1, ffreq, fw2, fb2, fwout, seq, True)
    od = h_time.shape[-1]
    h5 = h_time.reshape(1, 1, FFT_P, q, od)
    ar, ai = _fft_stage1(h5, 0, plan_f["f1"], plan_f["ka_pad"], 8, min(1024, od), F32, False)
    return _fft_stage2_filter(ar[0], ai[0], plan_f["m2f1"], 1.0 / nrm, min(1024, od))


def _dense_consts(seq):
    n = 2 * seq
    kf = seq + 1
    kf_pad = -(-kf // LANES) * LANES
    k = np.arange(kf)
    fwd_full = np.zeros((2 * kf_pad, n))
    ang = 2 * np.pi * np.outer(k, np.arange(n)) / n
    fwd_full[:kf] = np.cos(ang)
    fwd_full[kf_pad:kf_pad + kf] = -np.sin(ang)
    wgt = np.where((k == 0) | (k == seq), 1.0, 2.0) / n
    inv = np.zeros((seq, 2 * kf_pad))
    angi = 2 * np.pi * np.outer(np.arange(seq), k) / n
    inv[:, :kf] = wgt * np.cos(angi)
    inv[:, kf_pad:kf_pad + kf] = -wgt * np.sin(angi)
    return dict(kf_pad=kf_pad, fwd_full=_const3(fwd_full), fwd=_const3(fwd_full[:, :seq]), inv=_const3(inv))


def _dense_spec_kernel(h_ref, f_ref, inv_ref, hr_ref, hi_ref, *, kf_pad):
    y = _dot(f_ref[...], _stack3(h_ref[...])) * inv_ref[...]
    hr_ref[...] = y[:kf_pad]
    hi_ref[...] = y[kf_pad:]


def _filter_spectrum_ctx(hy, seq, cons):
    fw1, fb1, ffreq, fw2, fb2, fwout = hy
    h_time, nrm = _hyena_filter_time(fw1, fb1, ffreq, fw2, fb2, fwout, seq, False)
    od = h_time.shape[-1]
    n2 = 2 * seq
    h_time = h_time.reshape(n2, od)
    kf_pad = cons["kf_pad"]
    ct = min(512, od)
    out = jax.ShapeDtypeStruct((kf_pad, od), F32)
    ospec = pl.BlockSpec((kf_pad, ct), lambda c: (0, c))
    return pl.pallas_call(
        functools.partial(_dense_spec_kernel, kf_pad=kf_pad),
        grid=(od // ct,),
        in_specs=[pl.BlockSpec((n2, ct), lambda c: (0, c)),
                  pl.BlockSpec((2 * kf_pad, 3 * n2), lambda c: (0, 0)),
                  pl.BlockSpec((1, ct), lambda c: (0, c))],
        out_specs=[ospec, ospec],
        out_shape=[out, out],
        compiler_params=_cparams("parallel"),
        name="dense_filter_spectrum",
    )(h_time, cons["fwd_full"], 1.0 / nrm)


def _dense_conv_kernel(v_ref, x1_ref, x2_ref, h1r_ref, h1i_ref, h2r_ref, h2i_ref, sk_ref,
                       f_ref, g_ref, z_ref, *, kf_pad):
    def conv(u, hr, hi):
        s = _dot(f_ref[...], _stack3(u))
        sr, si = s[:kf_pad], s[kf_pad:]
        y = jnp.concatenate([sr * hr - si * hi, sr * hi + si * hr], axis=0)
        return _dot(g_ref[...], _stack3(y))

    v = v_ref[...]
    z1 = x1_ref[...] * (conv(v, h1r_ref[...], h1i_ref[...]) + v * sk_ref[0:1, :])
    z2 = x2_ref[...] * (conv(z1, h2r_ref[...], h2i_ref[...]) + z1 * sk_ref[1:2, :])
    z_ref[...] = z2.astype(z_ref.dtype)


def _hyena_core_ctx(u3, hr, hi, fskip, cons, b, seq):
    d = u3.shape[-1]
    dt = min(256, d)
    nd = d // dt
    kf_pad = cons["kf_pad"]
    uspec = lambda part: pl.BlockSpec((None, seq, dt), lambda bi, c: (part, bi, c))
    hspec = lambda order: pl.BlockSpec((kf_pad, dt), lambda bi, c: (0, order * nd + c))
    return pl.pallas_call(
        functools.partial(_dense_conv_kernel, kf_pad=kf_pad),
        grid=(b, nd),
        in_specs=[uspec(0), uspec(1), uspec(2), hspec(0), hspec(0), hspec(1), hspec(1),
                  pl.BlockSpec((HYENA_ORDER, dt), lambda bi, c: (0, c)),
                  pl.BlockSpec((2 * kf_pad, 3 * seq), lambda bi, c: (0, 0)),
                  pl.BlockSpec((seq, 6 * kf_pad), lambda bi, c: (0, 0))],
        out_specs=pl.BlockSpec((seq, dt), lambda bi, c: (bi, c)),
        out_shape=jax.ShapeDtypeStruct((b * seq, d), BF16),
        compiler_params=_cparams("parallel", "parallel"),
        name="hyena_core_ctx",
    )(u3, u3, u3, hr, hi, hr, hi, fskip, cons["fwd"], cons["inv"])


def _rec_consts(chunk):
    t = np.arange(chunk)
    coefs, masks = [], []
    for direction in (0, 1):
        if direction == 0:
            rows = [t[None, :] <= t[:, None], t[None, :] > t[:, None]]
        else:
            rows = [t[None, :] >= t[:, None], t[None, :] < t[:, None]]
        mk = [np.eye(chunk)]
        m = chunk // 2
        while m >= 1:
            blk = t // (2 * m)
            half = (t // m) % 2
            mid = blk * 2 * m + m
            e = np.zeros((chunk, chunk))
            for r in range(chunk):
                if direction == 0:
                    if half[r] == 1:
                        e[r, mid[r]:r + 1] = 1
                    else:
                        e[r, r + 1:mid[r]] = 1
                else:
                    if half[r] == 0:
                        e[r, r:mid[r]] = 1
                    else:
                        e[r, mid[r]:r] = 1
            same = blk[:, None] == blk[None, :]
            if direction == 0:
                mk.append(same & (half[:, None] == 1) & (half[None, :] == 0))
            else:
                mk.append(same & (half[:, None] == 0) & (half[None, :] == 1))
            rows.append(e)
            m //= 2
        rows.append(np.ones((16, chunk)))
        a = np.concatenate([np.asarray(r, np.float64) for r in rows], axis=0)
        a3 = np.concatenate([a, a, a], axis=1)
        pad = -(-a3.shape[1] // LANES) * LANES - a3.shape[1]
        a3 = np.pad(a3, ((0, 0), (0, pad)))
        coefs.append(a3)
        masks.append(np.stack([np.asarray(x, np.float32) for x in mk]))
    return (jnp.asarray(np.stack(coefs), dtype=BF16), jnp.asarray(np.stack(masks), dtype=F32))


def _rec_core(q_s, k_s, g, v_ref, coef_ref, mask_ref, s0_ref, o_ref, sfin_ref, st_s, ex_s, att_s,
              *, chunk, heads, dk, dv):
    c = pl.program_id(2)
    levels = int(math.log2(chunk))
    tot = (2 + levels) * chunk

    @pl.when(c == 0)
    def _():
        st_s[...] = s0_ref[...]

    g1 = g.astype(BF16)
    r1 = g - g1.astype(F32)
    g2 = r1.astype(BF16)
    g3 = (r1 - g2.astype(F32)).astype(BF16)
    pad = coef_ref.shape[-1] - 3 * chunk
    gs = jnp.concatenate([g1, g2, g3, jnp.zeros((pad, g.shape[1]), BF16)], axis=0)
    ex_s[...] = jnp.exp(_dot(coef_ref[...], gs))

    for h in range(heads):
        ks = slice(h * dk, (h + 1) * dk)
        qh = q_s[:, ks]
        kh = k_s[:, ks]
        att = mask_ref[0] * _dot_nt(qh.astype(BF16), kh.astype(BF16))
        for lv in range(levels):
            e = ex_s[(2 + lv) * chunk:(3 + lv) * chunk, ks]
            att += mask_ref[1 + lv] * _dot_nt((qh * e).astype(BF16), (kh * e).astype(BF16))
        att_s[h] = att.astype(BF16)

    for h in range(heads):
        ks = slice(h * dk, (h + 1) * dk)
        vs = slice(h * dv, (h + 1) * dv)
        vh = v_ref[:, vs].astype(BF16)
        st = st_s[h]
        o = _dot(att_s[h], vh)
        o += _dot_nt((q_s[:, ks] * ex_s[0:chunk, ks]).astype(BF16), st.astype(BF16))
        o_ref[:, vs] = o
        kd = (k_s[:, ks] * ex_s[chunk:2 * chunk, ks]).astype(BF16)
        st_s[h] = st * ex_s[tot:tot + 1, ks] + _dot_tn(vh, kd)

    @pl.when(c == pl.num_programs(2) - 1)
    def _():
        sfin_ref[...] = st_s[...]


def _hgrn_rec_kernel(q_ref, v_ref, f_ref, lb_ref, coef_ref, mask_ref, s0_ref, o_ref, sfin_ref,
                     q_s, k_s, st_s, ex_s, att_s, **kw):
    qr = q_ref[...]
    q_s[...] = qr * _sigmoid(qr)
    fr = f_ref[...]
    lb = lb_ref[...]
    sg = _sigmoid(fr)
    g = jnp.log(lb + (1.0 - lb) * sg)
    k_s[...] = (1.0 - lb) * (1.0 - sg)
    _rec_core(q_s, k_s, g, v_ref, coef_ref, mask_ref, s0_ref, o_ref, sfin_ref, st_s, ex_s, att_s, **kw)


def _gla_rec_kernel(q_ref, k_ref, v_ref, a_ref, wup_ref, bup_ref, coef_ref, mask_ref, s0_ref,
                    o_ref, sfin_ref, q_s, k_s, st_s, ex_s, att_s, *, qscale, **kw):
    q_s[...] = q_ref[...] * qscale
    k_s[...] = k_ref[...]
    xg = _dot(a_ref[...].astype(BF16), wup_ref[...]) + bup_ref[...]
    g = (jnp.minimum(xg, 0.0) - jnp.log(1.0 + jnp.exp(-jnp.abs(xg)))) * (1.0 / GLA_GATE_NORM)
    _rec_core(q_s, k_s, g, v_ref, coef_ref, mask_ref, s0_ref, o_ref, sfin_ref, st_s, ex_s, att_s, **kw)


def _rec_call(kind, proj, extra, s0, consts, b, seq, d, heads, dk, dv):
    chunk = REC_CHUNK
    nc = seq // chunk
    coef, masks = consts
    hk = heads * dk
    hv = heads * dv
    levels = int(math.log2(chunk))
    n_rows = (2 + levels) * chunk + 16

    def rmap(col):
        return lambda bi, di, c: (bi * nc + c + di * (nc - 1 - 2 * c), col)

    cspecs = [pl.BlockSpec((None, n_rows, coef.shape[-1]), lambda bi, di, c: (di, 0, 0)),
              pl.BlockSpec((None, levels + 1, chunk, chunk), lambda bi, di, c: (di, 0, 0, 0)),
              pl.BlockSpec((None, None, heads, dv, dk), lambda bi, di, c: (bi, di, 0, 0, 0))]
    if kind == "hgrn":
        lb = extra
        kernel = _hgrn_rec_kernel
        in_specs = [pl.BlockSpec((chunk, d), rmap(0)),
                    pl.BlockSpec((chunk, d), rmap(1)),
                    pl.BlockSpec((chunk, d), lambda bi, di, c: (bi * nc + c + di * (nc - 1 - 2 * c), 3 + di)),
                    pl.BlockSpec((None, 1, d), lambda bi, di, c: (di, 0, 0))] + cspecs
        args = (proj, proj, proj, lb, coef, masks, s0)
        kw = {}
    else:
        wup, bup = extra
        kernel = _gla_rec_kernel
        in_specs = [pl.BlockSpec((chunk, hk), rmap(0)),
                    pl.BlockSpec((chunk, hk), rmap(1)),
                    pl.BlockSpec((chunk, hv), rmap(1)),
                    pl.BlockSpec((chunk, LANES), rmap((2 * hk + 2 * hv) // LANES)),
                    pl.BlockSpec((None, LANES, hk), lambda bi, di, c: (di, 0, 0)),
                    pl.BlockSpec((None, 1, hk), lambda bi, di, c: (di, 0, 0))] + cspecs
        args = (proj, proj, proj, proj, wup, bup, coef, masks, s0)
        kw = dict(qscale=float(dk) ** -0.5)
    return pl.pallas_call(
        functools.partial(kernel, chunk=chunk, heads=heads, dk=dk, dv=dv, **kw),
        grid=(b, 2, nc),
        in_specs=in_specs,
        out_specs=[pl.BlockSpec((None, chunk, hv), lambda bi, di, c: (di, bi * nc + c + di * (nc - 1 - 2 * c), 0)),
                   pl.BlockSpec((None, None, heads, dv, dk), lambda bi, di, c: (bi, di, 0, 0, 0))],
        out_shape=[jax.ShapeDtypeStruct((2, b * seq, hv), F32),
                   jax.ShapeDtypeStruct((b, 2, heads, dv, dk), F32)],
        scratch_shapes=[pltpu.VMEM((chunk, hk), F32), pltpu.VMEM((chunk, hk), F32),
                        pltpu.VMEM((heads, dv, dk), F32), pltpu.VMEM((n_rows, hk), F32),
                        pltpu.VMEM((heads, chunk, chunk), BF16)],
        compiler_params=_cparams("parallel", "parallel", "arbitrary"),
        name=kind + "_recurrence",
    )(*args)


def _gated_out_kernel(o_ref, gate_ref, gn_ref, w_ref, x_ref, mg_ref, out_ref, *, heads):
    d = x_ref.shape[1]
    dh = d // heads
    kc = max(dh, 2 * MXU_N)
    acc = None
    for k0 in range(0, d, kc):
        parts = []
        for h0 in range(k0, k0 + kc, dh):
            seg = o_ref[0, :, h0:h0 + dh] + o_ref[1, :, h0:h0 + dh]
            ms = jnp.mean(seg * seg, axis=-1, keepdims=True)
            parts.append(seg * lax.rsqrt(ms + RMS_EPS))
        gate = gate_ref[:, k0:k0 + kc]
        y = jnp.concatenate(parts, axis=1) * gn_ref[:, k0:k0 + kc] * (gate * _sigmoid(gate))
        part = _dot(y.astype(BF16), w_ref[k0:k0 + kc, :])
        acc = part if acc is None else acc + part
    out_ref[...] = x_ref[...] + mg_ref[...] * acc


def _gated_out(o2, proj, gate_col, gn, w, x, mg, bmap, heads, tm):
    m, d = x.shape
    return pl.pallas_call(
        functools.partial(_gated_out_kernel, heads=heads),
        grid=(m // tm,),
        in_specs=[pl.BlockSpec((2, tm, d), lambda i: (0, i, 0)),
                  pl.BlockSpec((tm, d), lambda i: (i, gate_col)),
                  pl.BlockSpec((1, d), lambda i: (0, 0)),
                  pl.BlockSpec((d, d), lambda i: (0, 0)),
                  pl.BlockSpec((tm, d), lambda i: (i, 0)),
                  _mod_spec(d, bmap)],
        out_specs=pl.BlockSpec((tm, d), lambda i: (i, 0)),
        out_shape=jax.ShapeDtypeStruct((m, d), F32),
        compiler_params=_cparams("parallel"),
        name="gated_out",
    )(o2, proj, gn, w, x, mg)


def _recurrent_mix(kind, x_ctx, x_lat, norm, ctx_map, w_in, extra, consts, b, lc, seq, d, heads, dk, dv, tn):
    g1, sh, sc = norm
    tm_c, tm_l = min(1024, x_ctx.shape[0]), min(1024, x_lat.shape[0])
    proj_ctx = _projection(x_ctx, g1, sh, sc, w_in, ctx_map, tm_c, tn)
    proj_lat = _projection(x_lat, g1, sh, sc, w_in, lambda i: (i * tm_l) // seq, tm_l, tn)
    s0 = jnp.zeros((b, 2, heads, dv, dk), F32)
    o_ctx, s_ctx = _rec_call(kind, proj_ctx, extra, s0, consts, b, lc, d, heads, dk, dv)
    o_lat, _ = _rec_call(kind, proj_lat, extra, s_ctx, consts, b, seq, d, heads, dk, dv)
    return (proj_ctx, o_ctx), (proj_lat, o_lat)


def kernel(x, c, ctx, c_ctx, w_mod, b_mod, norm1_g, norm2_g, w_ffn_in, w_ffn_out, final_g, hy_w_in, hy_conv_w, hy_fw1, hy_fb1, hy_ffreq, hy_fw2, hy_fb2, hy_fwout, hy_fskip, hy_w_out, hg_w_in, hg_lb_logits, hg_onorm_g, hg_w_out, gla_w_in, gla_w_up, gla_b_up, gla_onorm_g, gla_w_out):
    b, seq, d = x.shape
    lc = ctx.shape[1]
    depth = w_mod.shape[0]
    assert b + 1 <= MOD_ROWS and seq % (FFT_P * 2) == 0 and FFT_P % GRID_W == 0
    m_lat, m_ctx = b * seq, b * lc
    ctx_row = b
    tm_lat, tm_ctx = 512, min(512, m_ctx)
    lat_map = lambda i, tm=tm_lat: (i * tm) // seq
    ctx_map = lambda i: ctx_row

    xl = x.reshape(m_lat, d)
    xc = ctx.reshape(m_ctx, d)
    c8 = jnp.zeros((MOD_ROWS, d), F32).at[:b].set(c).at[ctx_row].set(c_ctx)
    mod = _modulation(c8, w_mod, b_mod).reshape(depth, MOD_ROWS, N_MOD, 1, d)

    plan = _fft_consts(seq, seq // FFT_P)
    plan_f = _fft_consts(seq, 2 * seq // FFT_P)
    dense = _dense_consts(lc)
    rec_consts = _rec_consts(REC_CHUNK)
    qh = plan["qh"]

    wb_ffn_in, wb_ffn_out = _to_bf16(w_ffn_in), _to_bf16(w_ffn_out)
    wb_hy_in, wb_hy_out = _to_bf16(hy_w_in), _to_bf16(hy_w_out)
    wb_hg_in, wb_hg_out = _to_bf16(hg_w_in), _to_bf16(hg_w_out)
    wb_gla_in, wb_gla_out = _to_bf16(gla_w_in), _to_bf16(gla_w_out)

    for i in range(depth):
        last = i == depth - 1
        kind, j = i % N_MIXERS, i // N_MIXERS
        mv = [mod[i, :, k] for k in range(N_MOD)]
        g1 = norm1_g[i][None, :]
        need_ctx = (not last) or kind != 0
        if kind == 0:
            w_in, w_out = wb_hy_in[j], wb_hy_out[j]
            hy = (hy_fw1[j], hy_fb1[j], hy_ffreq[j], hy_fw2[j], hy_fb2[j], hy_fwout[j])
            tm_h = 1024
            u = _hyena_in(xl, g1, mv[0], mv[1], w_in, hy_conv_w[j], lambda i: (i * tm_h) // seq,
                          tm_h, min(1024, d), GRID_W).reshape(3, b, qh, FFT_P, d)
            hr, hi = _filter_spectrum_lat(hy, seq, plan_f)
            z1 = _long_conv_lat(u, 0, u, 1, hr, hi, hy_fskip[j, 0][None, :], plan, 0, F32)
            z2 = _long_conv_lat(z1, 0, u, 2, hr, hi, hy_fskip[j, 1][None, :], plan, 1, BF16)
            xl = _out_res(z2.reshape(m_lat, d), w_out, xl, mv[2], lat_map, tm_lat)
            if need_ctx:
                uc = _hyena_in(xc, g1, mv[0], mv[1], w_in, hy_conv_w[j], ctx_map, lc, min(1024, d), lc)
                hcr, hci = _filter_spectrum_ctx(hy, lc, dense)
                zc = _hyena_core_ctx(uc, hcr, hci, hy_fskip[j], dense, b, lc)
                xc = _out_res(zc, w_out, xc, mv[2], ctx_map, tm_ctx)
        else:
            if kind == 1:
                heads = d // HGRN_EXPAND
                dk = dv = HGRN_EXPAND
                w_in = wb_hg_in[j]
                lb_cum = jnp.cumsum(jax.nn.softmax(hg_lb_logits.astype(F32), axis=1), axis=1)
                extra = (lb_cum[:, i] - lb_cum[:, 0])[:, None, :]
                gn, w_out, tn = hg_onorm_g[j], wb_hg_out[j], 1024
                rkind = "hgrn"
            else:
                heads = GLA_HEADS
                dk, dv = d // 2 // heads, d // heads
                n_in = gla_w_in.shape[-1]
                n_pad = -(-n_in // LANES) * LANES
                w_in = jnp.pad(wb_gla_in[j], ((0, 0), (0, n_pad - n_in)))
                r = GLA_GATE_RANK
                wup = jnp.zeros((2, LANES, heads * dk), F32)
                wup = wup.at[0, :r].set(gla_w_up[j, 0]).at[1, r:2 * r].set(gla_w_up[j, 1]).astype(BF16)
                extra = (wup, gla_b_up[j][:, None, :])
                gn, w_out = gla_onorm_g[j], wb_gla_out[j]
                tn = _lane_tile(n_pad, 1024)
                rkind = "gla"
            (p_ctx, o_ctx), (p_lat, o_lat) = _recurrent_mix(
                rkind, xc, xl, (g1, mv[0], mv[1]), ctx_map, w_in, extra, rec_consts,
                b, lc, seq, d, heads, dk, dv, tn)
            gate_col = 2 if kind == 1 else (2 * heads * dk + heads * dv) // d
            tm_g = 256
            xl = _gated_out(o_lat, p_lat, gate_col, gn[None, :], w_out, xl, mv[2],
                            lambda i: (i * tm_g) // seq, heads, tm_g)
            xc = _gated_out(o_ctx, p_ctx, gate_col, gn[None, :], w_out, xc, mv[2], ctx_map, heads, tm_g)
        g2 = norm2_g[i][None, :]
        xl = _ffn(xl, g2, mv[3], mv[4], mv[5], wb_ffn_in, wb_ffn_out, i, lat_map, tm_lat, 512)
        if not last:
            xc = _ffn(xc, g2, mv[3], mv[4], mv[5], wb_ffn_in, wb_ffn_out, i, ctx_map, tm_ctx, 512)
    return _final_norm(xl, final_g[None, :], tm_lat).reshape(b, seq, d)
```

```python
import functools
import math

import ml_dtypes
import numpy as np
import jax
import jax.numpy as jnp
from jax import lax
from jax.experimental import pallas as pl
from jax.experimental.pallas import tpu as pltpu

F32 = jnp.float32
BF16 = jnp.bfloat16

N_MOD = 6
N_MIXERS = 3
RMS_EPS = 1e-6
GRID_W = 64
HYENA_ORDER = 2
FILTER_BANDS = 16
FILTER_EMB = 1 + 2 * FILTER_BANDS
HYENA_DECAY_MIN = math.log(1e-2) / 1.5
HYENA_DECAY_MAX = math.log(1e-2) / 0.3
HGRN_EXPAND = 128
GLA_HEADS = 4
GLA_GATE_RANK = 16
GLA_GATE_NORM = 16.0

LANES = 128
SUBLANES = 8
V7X_VMEM_LIMIT = 56 * 1024 * 1024

FFT_P = 128
REC_CHUNK = 64
REC_SUBCHUNKS = 4
MOD_ROWS = 8


def _cparams(*sem):
    return pltpu.CompilerParams(dimension_semantics=sem, vmem_limit_bytes=V7X_VMEM_LIMIT)


def _dot(a, b):
    return jnp.dot(a, b, preferred_element_type=F32)


def _dot_nt(a, b):
    return lax.dot_general(a, b, (((1,), (1,)), ((), ())), preferred_element_type=F32)


def _dot_tn(a, b):
    return lax.dot_general(a, b, (((0,), (0,)), ((), ())), preferred_element_type=F32)


def _split(x):
    hi = x.astype(BF16)
    lo = (x - hi.astype(F32)).astype(BF16)
    return hi, lo


def _stack3(x, pad_rows=0):
    hi, lo = _split(x)
    parts = [hi, lo, hi]
    if pad_rows:
        parts.append(jnp.zeros((pad_rows, x.shape[1]), BF16))
    return jnp.concatenate(parts, axis=0)


def _const3(c, pad_cols=0):
    hi = c.astype(ml_dtypes.bfloat16)
    lo = (c - hi.astype(np.float64)).astype(ml_dtypes.bfloat16)
    parts = [hi, hi, lo]
    if pad_cols:
        parts.append(np.zeros(c.shape[:-1] + (pad_cols,), ml_dtypes.bfloat16))
    return jnp.asarray(np.concatenate(parts, axis=-1))


def _dot3(a, bh, bl):
    ah, al = _split(a)
    return _dot(ah, bh) + _dot(ah, bl) + _dot(al, bh)


def _sigmoid(x):
    return jax.nn.sigmoid(x)


def _normmod(x, g, sh, sc):
    ms = jnp.mean(x * x, axis=-1, keepdims=True)
    y = x * lax.rsqrt(ms + RMS_EPS) * g
    return y * (1.0 + sc) + sh


def _mod_kernel(c_ref, w_ref, b_ref, o_ref):
    c = c_ref[...]
    s = (c * _sigmoid(c)).astype(BF16)
    o_ref[...] = _dot(s, w_ref[...].astype(BF16)) + b_ref[...]


def _modulation(c8, w_mod, b_mod):
    depth, d, n = w_mod.shape
    tn = 1024
    return pl.pallas_call(
        _mod_kernel,
        grid=(depth, n // tn),
        in_specs=[pl.BlockSpec((MOD_ROWS, d), lambda l, j: (0, 0)),
                  pl.BlockSpec((None, d, tn), lambda l, j: (l, 0, j)),
                  pl.BlockSpec((None, 1, tn), lambda l, j: (l, 0, j))],
        out_specs=pl.BlockSpec((None, MOD_ROWS, tn), lambda l, j: (l, 0, j)),
        out_shape=jax.ShapeDtypeStruct((depth, MOD_ROWS, n), F32),
        compiler_params=_cparams("parallel", "parallel"),
        name="modulation",
    )(c8, w_mod, b_mod.reshape(depth, 1, n))


def _mod_spec(d, bmap):
    return pl.BlockSpec((None, 1, d), lambda *idx: (bmap(idx[0]), 0, 0))


def _normmod_kernel(x_ref, g_ref, sh_ref, sc_ref, o_ref):
    o_ref[...] = _normmod(x_ref[...], g_ref[...], sh_ref[...], sc_ref[...]).astype(BF16)


def _norm_modulate(x, g, sh, sc, bmap, tm):
    m, d = x.shape
    return pl.pallas_call(
        _normmod_kernel,
        grid=(m // tm,),
        in_specs=[pl.BlockSpec((tm, d), lambda i: (i, 0)),
                  pl.BlockSpec((1, d), lambda i: (0, 0)),
                  _mod_spec(d, bmap), _mod_spec(d, bmap)],
        out_specs=pl.BlockSpec((tm, d), lambda i: (i, 0)),
        out_shape=jax.ShapeDtypeStruct((m, d), BF16),
        compiler_params=_cparams("parallel"),
        name="norm_modulate",
    )(x, g, sh, sc)


def _final_norm_kernel(x_ref, g_ref, o_ref):
    x = x_ref[...]
    ms = jnp.mean(x * x, axis=-1, keepdims=True)
    o_ref[...] = x * lax.rsqrt(ms + RMS_EPS) * g_ref[...]


def _final_norm(x, g, tm):
    m, d = x.shape
    return pl.pallas_call(
        _final_norm_kernel,
        grid=(m // tm,),
        in_specs=[pl.BlockSpec((tm, d), lambda i: (i, 0)), pl.BlockSpec((1, d), lambda i: (0, 0))],
        out_specs=pl.BlockSpec((tm, d), lambda i: (i, 0)),
        out_shape=jax.ShapeDtypeStruct((m, d), F32),
        compiler_params=_cparams("parallel"),
        name="final_norm",
    )(x, g)


def _lane_tile(n, cap):
    best = n
    for t in range(LANES, min(n, cap) + 1, LANES):
        if n % t == 0:
            best = t
    return best


def _cast_kernel(x_ref, o_ref):
    o_ref[...] = x_ref[...].astype(o_ref.dtype)


def _to_bf16(w):
    shape = w.shape
    w2 = w.reshape(-1, shape[-1])
    r, c = w2.shape
    tr, tc = min(512, r), _lane_tile(c, 2048)
    out = pl.pallas_call(
        _cast_kernel,
        grid=(r // tr, c // tc),
        in_specs=[pl.BlockSpec((tr, tc), lambda i, j: (i, j))],
        out_specs=pl.BlockSpec((tr, tc), lambda i, j: (i, j)),
        out_shape=jax.ShapeDtypeStruct((r, c), BF16),
        compiler_params=_cparams("parallel", "parallel"),
        name="cast_bf16",
    )(w2)
    return out.reshape(shape)


def _proj_kernel(x_ref, g_ref, sh_ref, sc_ref, w_ref, o_ref, h_s):
    @pl.when(pl.program_id(1) == 0)
    def _():
        h_s[...] = _normmod(x_ref[...], g_ref[...], sh_ref[...], sc_ref[...]).astype(BF16)

    o_ref[...] = _dot(h_s[...], w_ref[...])


def _projection(x, g, sh, sc, w, bmap, tm, tn):
    m, d = x.shape
    n = w.shape[1]
    return pl.pallas_call(
        _proj_kernel,
        grid=(m // tm, n // tn),
        in_specs=[pl.BlockSpec((tm, d), lambda i, j: (i, 0)),
                  pl.BlockSpec((1, d), lambda i, j: (0, 0)),
                  _mod_spec(d, bmap), _mod_spec(d, bmap),
                  pl.BlockSpec((d, tn), lambda i, j: (0, j))],
        out_specs=pl.BlockSpec((tm, tn), lambda i, j: (i, j)),
        out_shape=jax.ShapeDtypeStruct((m, n), F32),
        scratch_shapes=[pltpu.VMEM((tm, d), BF16)],
        compiler_params=_cparams("parallel", "arbitrary"),
        name="projection",
    )(x, g, sh, sc, w)


def _out_res_kernel(a_ref, w_ref, x_ref, mg_ref, o_ref):
    o_ref[...] = x_ref[...] + mg_ref[...] * _dot(a_ref[...], w_ref[...])


def _ffn_kernel(x_ref, g_ref, sh_ref, sc_ref, mg_ref, wg_ref, wu_ref, wo_ref, o_ref, h_s, acc_s):
    j = pl.program_id(1)

    @pl.when(j == 0)
    def _():
        h_s[...] = _normmod(x_ref[...], g_ref[...], sh_ref[...], sc_ref[...]).astype(BF16)
        acc_s[...] = jnp.zeros_like(acc_s)

    h = h_s[...]
    a = _dot(h, wg_ref[...])
    u = _dot(h, wu_ref[...])
    act = (a * _sigmoid(a) * u).astype(BF16)
    acc_s[...] += _dot(act, wo_ref[...])

    @pl.when(j == pl.num_programs(1) - 1)
    def _():
        o_ref[...] = x_ref[...] + mg_ref[...] * acc_s[...]


def _ffn(x, g, sh, sc, mg, w_in, w_out, layer, bmap, tm, tf):
    m, d = x.shape
    f = w_out.shape[1]
    nf = f // tf
    return pl.pallas_call(
        _ffn_kernel,
        grid=(m // tm, nf),
        in_specs=[pl.BlockSpec((tm, d), lambda i, j: (i, 0)),
                  pl.BlockSpec((1, d), lambda i, j: (0, 0)),
                  _mod_spec(d, bmap), _mod_spec(d, bmap), _mod_spec(d, bmap),
                  pl.BlockSpec((None, d, tf), lambda i, j: (layer, 0, j)),
                  pl.BlockSpec((None, d, tf), lambda i, j: (layer, 0, nf + j)),
                  pl.BlockSpec((None, tf, d), lambda i, j: (layer, j, 0))],
        out_specs=pl.BlockSpec((tm, d), lambda i, j: (i, 0)),
        out_shape=jax.ShapeDtypeStruct((m, d), F32),
        scratch_shapes=[pltpu.VMEM((tm, d), BF16), pltpu.VMEM((tm, d), F32)],
        compiler_params=_cparams("parallel", "arbitrary"),
        name="ffn",
    )(x, g, sh, sc, mg, w_in, w_in, w_out)


def _conv3_rows(acc, cw, row_len):
    tm = acc.shape[0]
    rid = lax.broadcasted_iota(jnp.int32, (tm, 1), 0) % row_len
    up = jnp.where(rid == 0, 0.0, pltpu.roll(acc, 1, 0))
    dn = jnp.where(rid == row_len - 1, 0.0, pltpu.roll(acc, tm - 1, 0))
    return cw[0:1] * up + cw[1:2] * acc + cw[2:3] * dn


MXU_N = 256


def _hy_in_kernel(x_ref, g_ref, sh_ref, sc_ref, w_ref, cw_ref, o_ref, h_s, *, row_len):
    @pl.when(pl.program_id(1) == 0)
    def _():
        h_s[...] = _normmod(x_ref[...], g_ref[...], sh_ref[...], sc_ref[...]).astype(BF16)

    h = h_s[...]
    for n0 in range(0, o_ref.shape[1], MXU_N):
        ns = slice(n0, n0 + MXU_N)
        o_ref[:, ns] = _conv3_rows(_dot(h, w_ref[:, ns]), cw_ref[:, ns], row_len)


def _hyena_in(x, g, sh, sc, w, cw, bmap, tm, tn, row_len):
    m, d = x.shape
    nd = d // tn
    return pl.pallas_call(
        functools.partial(_hy_in_kernel, row_len=row_len),
        grid=(m // tm, 3 * nd),
        in_specs=[pl.BlockSpec((tm, d), lambda i, j: (i, 0)),
                  pl.BlockSpec((1, d), lambda i, j: (0, 0)),
                  _mod_spec(d, bmap), _mod_spec(d, bmap),
                  pl.BlockSpec((d, tn), lambda i, j: (0, j)),
                  pl.BlockSpec((3, tn), lambda i, j: (0, j))],
        out_specs=pl.BlockSpec((None, tm, tn), lambda i, j: (j // nd, i, j % nd)),
        out_shape=jax.ShapeDtypeStruct((3, m, d), F32),
        scratch_shapes=[pltpu.VMEM((tm, d), BF16)],
        compiler_params=_cparams("parallel", "arbitrary"),
        name="hyena_in",
    )(x, g, sh, sc, w, cw)


def _out_res(a, w, x, mg, bmap, tm):
    m, d = x.shape
    return pl.pallas_call(
        _out_res_kernel,
        grid=(m // tm,),
        in_specs=[pl.BlockSpec((tm, d), lambda i: (i, 0)),
                  pl.BlockSpec((d, d), lambda i: (0, 0)),
                  pl.BlockSpec((tm, d), lambda i: (i, 0)),
                  _mod_spec(d, bmap)],
        out_specs=pl.BlockSpec((tm, d), lambda i: (i, 0)),
        out_shape=jax.ShapeDtypeStruct((m, d), F32),
        compiler_params=_cparams("parallel"),
        name="out_residual",
    )(a, w, x, mg)


def _filter_kernel(z_ref, w1h_ref, w1l_ref, b1_ref, fq_ref, w2h_ref, w2l_ref, b2_ref,
                   woh_ref, wol_ref, dl_ref, h_ref, nrm_ref):
    z = z_ref[...]
    t = z[:, 0:1]
    valid = z[:, FILTER_EMB:FILTER_EMB + 1]
    a1 = _dot3(z, w1h_ref[...], w1l_ref[...]) + b1_ref[...]
    hid = jnp.sin(fq_ref[0:1, :] * a1)
    a2 = _dot3(hid, w2h_ref[...], w2l_ref[...]) + b2_ref[...]
    hid = jnp.sin(fq_ref[1:2, :] * a2)
    h = _dot3(hid, woh_ref[...], wol_ref[...]) * jnp.exp(-t * dl_ref[...]) * valid
    h_ref[...] = h.reshape(h_ref.shape)

    @pl.when((pl.program_id(1) == 0) & (pl.program_id(2) == 0))
    def _():
        nrm_ref[...] = jnp.zeros_like(nrm_ref)

    nrm_ref[...] += jnp.sum(jnp.abs(h), axis=0, keepdims=True)


def _pad2(a, rows, cols):
    return jnp.pad(a, ((0, rows - a.shape[0]), (0, cols - a.shape[1])))


def _filter_positions(seq, perm):
    if perm:
        r_hi = seq // FFT_P
        gt, ng = 2 * SUBLANES, FFT_P // (2 * SUBLANES)
        dirs, g, tl, th = np.meshgrid(np.arange(2), np.arange(ng), np.arange(gt), np.arange(r_hi), indexing="ij")
        n = FFT_P * (r_hi * dirs + th) + g * gt + tl
    else:
        r_hi, gt, ng = seq, 1, 1
        dirs, g, tl, th = np.meshgrid(np.arange(2), np.arange(1), np.arange(1), np.arange(seq), indexing="ij")
        n = seq * dirs + th
    n = n.reshape(-1)
    pos = np.where(n < seq, n, 2 * seq - 1 - n).astype(np.float32)
    z = np.zeros((n.size, LANES), np.float32)
    z[:, 0] = pos / np.float32(max(seq - 1, 1))
    bands = np.arange(1, FILTER_BANDS + 1, dtype=np.float32)
    ang = (np.float32(2.0 * math.pi / seq) * pos)[:, None] * bands[None, :]
    z[:, 1:1 + FILTER_BANDS] = np.cos(ang.astype(np.float64))
    z[:, 1 + FILTER_BANDS:FILTER_EMB] = -np.sin(ang.astype(np.float64))
    z[:, FILTER_EMB] = n != seq
    return jnp.asarray(z), gt, ng, r_hi


def _hyena_filter_time(fw1, fb1, ffreq, fw2, fb2, fwout, seq, perm):
    od = fwout.shape[1] // 2
    d = od // HYENA_ORDER
    z, gt, ng, r_hi = _filter_positions(seq, perm)
    rows = gt * r_hi
    ct = min(2048, od)
    hidden = LANES
    w1h, w1l = _split(_pad2(fw1, LANES, hidden))
    w2h, w2l = _split(_pad2(fw2, hidden, hidden))
    woh, wol = _split(_pad2(fwout, hidden, fwout.shape[1]))
    b1 = _pad2(fb1[None, :], 1, hidden)
    b2 = _pad2(fb2[None, :], 1, hidden)
    fq = _pad2(ffreq, 2, hidden)
    deltas = np.abs(np.linspace(HYENA_DECAY_MIN, HYENA_DECAY_MAX, d, dtype=np.float32))
    dl = jnp.asarray(np.tile(deltas, HYENA_ORDER)[None, :])
    nct = od // ct
    small = lambda shape: pl.BlockSpec(shape, lambda c, di, g: (0, 0))
    wspec = pl.BlockSpec((hidden, ct), lambda c, di, g: (0, di * nct + c))
    return pl.pallas_call(
        _filter_kernel,
        grid=(nct, 2, ng),
        in_specs=[pl.BlockSpec((rows, LANES), lambda c, di, g: (di * ng + g, 0)),
                  small((LANES, hidden)), small((LANES, hidden)), small((1, hidden)), small((2, hidden)),
                  small((hidden, hidden)), small((hidden, hidden)), small((1, hidden)),
                  wspec, wspec,
                  pl.BlockSpec((1, ct), lambda c, di, g: (0, c))],
        out_specs=[pl.BlockSpec((None, gt, None, r_hi, ct), lambda c, di, g: (g, 0, di, 0, c)),
                   pl.BlockSpec((1, ct), lambda c, di, g: (0, c))],
        out_shape=[jax.ShapeDtypeStruct((ng, gt, 2, r_hi, od), F32), jax.ShapeDtypeStruct((1, od), F32)],
        compiler_params=_cparams("parallel", "arbitrary", "arbitrary"),
        name="hyena_filter",
    )(z, w1h, w1l, b1, fq, w2h, w2l, b2, woh, wol, dl)


def _fft_consts(seq, k_in):
    n = 2 * seq
    p = FFT_P
    q = n // p
    qh = q // 2
    ka_used = qh + 1
    ka_pad = -(-ka_used // 8) * 8
    ka = np.arange(ka_used)
    tl = np.arange(p)
    th = np.arange(k_in)
    theta = 2 * np.pi * (ka[None, :, None] * th[None, None, :] / q + ka[None, :, None] * tl[:, None, None] / n)
    f1 = np.zeros((p, 2 * ka_pad, k_in))
    f1[:, :ka_used] = np.cos(theta)
    f1[:, ka_pad:ka_pad + ka_used] = -np.sin(theta)
    k3 = 3 * k_in
    f1c = _const3(f1, pad_cols=-(-k3 // LANES) * LANES - k3)
    f1s = jnp.asarray(np.pad(f1, ((0, 0), (0, 0), (0, -(-k_in // LANES) * LANES - k_in))).astype(ml_dtypes.bfloat16))
    ang = 2 * np.pi * np.outer(np.arange(p), np.arange(p)) / p
    cc, sc = np.cos(ang), np.sin(ang)
    m2f_np, m2i_np = np.block([[cc, sc], [-sc, cc]]), np.block([[cc, -sc], [sc, cc]])
    m2f = _const3(m2f_np)
    m2f1, m2i1 = (jnp.asarray(m.astype(ml_dtypes.bfloat16)) for m in (m2f_np, m2i_np))
    tho = np.arange(qh)
    phi = 2 * np.pi * (tho[None, :, None] * ka[None, None, :] / q + ka[None, None, :] * tl[:, None, None] / n)
    wgt = np.where((ka == 0) | (ka == qh), 1.0, 2.0) / n
    kr = 2 * ka_pad
    g = np.zeros((p, qh, -(-kr // LANES) * LANES))
    col = 2 * SUBLANES * (ka // SUBLANES) + ka % SUBLANES
    g[:, :, col] = wgt * np.cos(phi)
    g[:, :, col + SUBLANES] = -wgt * np.sin(phi)
    gc = jnp.asarray(g.astype(ml_dtypes.bfloat16))
    return dict(q=q, qh=qh, ka_used=ka_used, ka_pad=ka_pad, f1=f1c, f1s=f1s, m2f=m2f, m2f1=m2f1, m2i1=m2i1, g=gc)


def _s1_kernel(x_ref, f_ref, ar_ref, ai_ref, a_s, *, group, ka_pad, pad_rows, natural, split):
    x = jnp.swapaxes(x_ref[...], 0, 1) if natural else x_ref[...]
    zpad = None if split else jnp.zeros((pad_rows, x.shape[-1]), BF16)
    for j in range(group):
        xx = _stack3(x[j], pad_rows) if split else jnp.concatenate([x[j].astype(BF16), zpad], axis=0)
        a_s[j] = _dot(f_ref[j], xx)
    a = jnp.swapaxes(a_s[...], 0, 1)
    ar_ref[...] = a[:ka_pad].astype(ar_ref.dtype)
    ai_ref[...] = a[ka_pad:].astype(ai_ref.dtype)


def _fft_stage1(x5, part, f1c, ka_pad, group, dt, out_dtype, natural, split):
    if natural:
        _, b, k_in, p, d = x5.shape
        xspec = pl.BlockSpec((None, None, k_in, group, dt), lambda bi, g, c: (part, bi, 0, g, c))
    else:
        _, b, p, k_in, d = x5.shape
        xspec = pl.BlockSpec((None, None, group, k_in, dt), lambda bi, g, c: (part, bi, g, 0, c))
    kc = f1c.shape[-1]
    out = jax.ShapeDtypeStruct((b, ka_pad, p, d), out_dtype)
    ospec = pl.BlockSpec((None, ka_pad, group, dt), lambda bi, g, c: (bi, 0, g, c))
    pad_rows = kc - (3 if split else 1) * k_in
    return pl.pallas_call(
        functools.partial(_s1_kernel, group=group, ka_pad=ka_pad, pad_rows=pad_rows, natural=natural, split=split),
        grid=(b, p // group, d // dt),
        in_specs=[xspec,
                  pl.BlockSpec((group, 2 * ka_pad, kc), lambda bi, g, c: (g, 0, 0))],
        out_specs=[ospec, ospec],
        out_shape=[out, out],
        scratch_shapes=[pltpu.VMEM((group, 2 * ka_pad, dt), F32)],
        compiler_params=_cparams("parallel", "parallel", "parallel"),
        name="fft_stage1",
    )(x5, f1c)


def _s2_filter_kernel(ar_ref, ai_ref, m_ref, inv_ref, hr_ref, hi_ref, *, p):
    for i in range(ar_ref.shape[0]):
        x = jnp.concatenate([ar_ref[i], ai_ref[i]], axis=0)
        y = _dot(m_ref[...], x.astype(BF16)) * inv_ref[...]
        hr_ref[i] = y[:p]
        hi_ref[i] = y[p:]


def _fft_stage2_filter(ar, ai, m2f, inv_nrm, dt):
    ka, p, d = ar.shape
    spec = pl.BlockSpec((SUBLANES, p, dt), lambda k, c: (k, 0, c))
    out = jax.ShapeDtypeStruct((ka, p, d), F32)
    return pl.pallas_call(
        functools.partial(_s2_filter_kernel, p=p),
        grid=(ka // SUBLANES, d // dt),
        in_specs=[spec, spec,
                  pl.BlockSpec((2 * p, 2 * p), lambda k, c: (0, 0)),
                  pl.BlockSpec((1, dt), lambda k, c: (0, c))],
        out_specs=[spec, spec],
        out_shape=[out, out],
        compiler_params=_cparams("parallel", "parallel"),
        name="fft_stage2_filter",
    )(ar, ai, m2f, inv_nrm)


def _s2_kernel(ar_ref, ai_ref, hr_ref, hi_ref, mf_ref, mi_ref, c_ref, x_s, y_s, c_s, *, p, ka_used):
    kg = pl.program_id(0)
    assert ka_used % SUBLANES == 1

    def run(rows):
        for i in rows:
            x_s[i] = _dot(mf_ref[...], jnp.concatenate([ar_ref[i], ai_ref[i]], axis=0))
        for i in rows:
            xr, xi = x_s[i, :p], x_s[i, p:]
            hr, hi = hr_ref[i], hi_ref[i]
            y_s[i, :p] = (xr * hr - xi * hi).astype(BF16)
            y_s[i, p:] = (xr * hi + xi * hr).astype(BF16)
        for i in rows:
            c = _dot(mi_ref[...], y_s[i])
            c_s[i] = c[:p]
            c_s[SUBLANES + i] = c[p:]

    run(range(1))
    full = (kg + 1) * SUBLANES <= ka_used

    @pl.when(full)
    def _():
        run(range(1, SUBLANES))

    @pl.when(jnp.logical_not(full))
    def _():
        for i in range(1, SUBLANES):
            c_s[i] = jnp.zeros(c_s.shape[1:], F32)
            c_s[SUBLANES + i] = jnp.zeros(c_s.shape[1:], F32)

    c_ref[...] = jnp.swapaxes(c_s[...], 0, 1).astype(c_ref.dtype)


def _fft_stage2(ar, ai, hr, hi, m2f, m2i, ka_used, h_col0, dt):
    b, ka, p, d = ar.shape
    hc = h_col0 // dt
    aspec = pl.BlockSpec((None, SUBLANES, p, dt), lambda k, c, bi: (bi, k, 0, c))
    hspec = pl.BlockSpec((SUBLANES, p, dt), lambda k, c, bi: (k, 0, hc + c))
    mspec = pl.BlockSpec((2 * p, 2 * p), lambda k, c, bi: (0, 0))
    return pl.pallas_call(
        functools.partial(_s2_kernel, p=p, ka_used=ka_used),
        grid=(ka // SUBLANES, d // dt, b),
        in_specs=[aspec, aspec, hspec, hspec, mspec, mspec],
        out_specs=pl.BlockSpec((None, p, 2 * SUBLANES, dt), lambda k, c, bi: (bi, 0, k, c)),
        out_shape=jax.ShapeDtypeStruct((b, p, 2 * ka, d), BF16),
        scratch_shapes=[pltpu.VMEM((SUBLANES, 2 * p, dt), F32), pltpu.VMEM((SUBLANES, 2 * p, dt), BF16),
                        pltpu.VMEM((2 * SUBLANES, p, dt), F32)],
        compiler_params=_cparams("parallel", "parallel", "parallel"),
        name="fft_stage2",
    )(ar, ai, hr, hi, m2f, m2i)


def _s3_kernel(c_ref, g_ref, v_ref, x_ref, sk_ref, z_ref, y_s, *, group, pad_rows):
    zpad = jnp.zeros((pad_rows, c_ref.shape[-1]), BF16)
    for j in range(group):
        y_s[j] = _dot(g_ref[j], jnp.concatenate([c_ref[j], zpad], axis=0))
    y = jnp.swapaxes(y_s[...], 0, 1)
    v = v_ref[...]
    z_ref[...] = ((y + v * sk_ref[...]) * x_ref[...]).astype(z_ref.dtype)


def _fft_stage3(c, gc, v5, vpart, x5, xpart, skip, group, dt, out_dtype):
    b, p, kr, d = c.shape
    qh = v5.shape[2]
    kc = gc.shape[-1]
    pspec = lambda part: pl.BlockSpec((None, None, qh, group, dt), lambda bi, g, cc: (part, bi, 0, g, cc))
    return pl.pallas_call(
        functools.partial(_s3_kernel, group=group, pad_rows=kc - kr),
        grid=(b, p // group, d // dt),
        in_specs=[pl.BlockSpec((None, group, kr, dt), lambda bi, g, cc: (bi, g, 0, cc)),
                  pl.BlockSpec((group, qh, kc), lambda bi, g, cc: (g, 0, 0)),
                  pspec(vpart), pspec(xpart),
                  pl.BlockSpec((1, dt), lambda bi, g, cc: (0, cc))],
        out_specs=pspec(0),
        out_shape=jax.ShapeDtypeStruct((1, b, qh, p, d), out_dtype),
        scratch_shapes=[pltpu.VMEM((group, qh, dt), F32)],
        compiler_params=_cparams("parallel", "parallel", "parallel"),
        name="fft_stage3",
    )(c, gc, v5, x5, skip)


def _long_conv_lat(v5, vpart, x5, xpart, hr, hi, skip, plan, order, out_dtype):
    d = v5.shape[-1]
    dt1 = min(1024, d)
    dt2 = min(512, d)
    ar, ai = _fft_stage1(v5, vpart, plan["f1s"], plan["ka_pad"], 16, dt1, BF16, True, False)
    c = _fft_stage2(ar, ai, hr, hi, plan["m2f1"], plan["m2i1"], plan["ka_used"], order * d, dt2)
    return _fft_stage3(c, plan["g"], v5, vpart, x5, xpart, skip, 16, dt1, out_dtype)


def _filter_spectrum_lat(hy, seq, plan_f):
    fw1, fb1, ffreq, fw2, fb2, fwout = hy
    q = plan_f["q"]
    h_time, nrm = _hyena_filter_time(fw1, fb1, ffreq, fw2, fb2, fwout, seq, True)
    od = h_time.shape[-1]
    h5 = h_time.reshape(1, 1, FFT_P, q, od)
    ar, ai = _fft_stage1(h5, 0, plan_f["f1"], plan_f["ka_pad"], 8, min(1024, od), F32, False, True)
    return _fft_stage2_filter(ar[0], ai[0], plan_f["m2f1"], 1.0 / nrm, min(1024, od))


def _dense_consts(seq):
    n = 2 * seq
    kf = seq + 1
    kf_pad = -(-kf // LANES) * LANES
    k = np.arange(kf)
    fwd_full = np.zeros((2 * kf_pad, n))
    ang = 2 * np.pi * np.outer(k, np.arange(n)) / n
    fwd_full[:kf] = np.cos(ang)
    fwd_full[kf_pad:kf_pad + kf] = -np.sin(ang)
    wgt = np.where((k == 0) | (k == seq), 1.0, 2.0) / n
    inv = np.zeros((seq, 2 * kf_pad))
    angi = 2 * np.pi * np.outer(np.arange(seq), k) / n
    inv[:, :kf] = wgt * np.cos(angi)
    inv[:, kf_pad:kf_pad + kf] = -wgt * np.sin(angi)
    return dict(kf_pad=kf_pad, fwd_full=_const3(fwd_full), fwd=_const3(fwd_full[:, :seq]), inv=_const3(inv))


def _dense_spec_kernel(h_ref, f_ref, inv_ref, hr_ref, hi_ref, *, kf_pad):
    y = _dot(f_ref[...], _stack3(h_ref[...])) * inv_ref[...]
    hr_ref[...] = y[:kf_pad]
    hi_ref[...] = y[kf_pad:]


def _filter_spectrum_ctx(hy, seq, cons):
    fw1, fb1, ffreq, fw2, fb2, fwout = hy
    h_time, nrm = _hyena_filter_time(fw1, fb1, ffreq, fw2, fb2, fwout, seq, False)
    od = h_time.shape[-1]
    n2 = 2 * seq
    h_time = h_time.reshape(n2, od)
    kf_pad = cons["kf_pad"]
    ct = min(512, od)
    out = jax.ShapeDtypeStruct((kf_pad, od), F32)
    ospec = pl.BlockSpec((kf_pad, ct), lambda c: (0, c))
    return pl.pallas_call(
        functools.partial(_dense_spec_kernel, kf_pad=kf_pad),
        grid=(od // ct,),
        in_specs=[pl.BlockSpec((n2, ct), lambda c: (0, c)),
                  pl.BlockSpec((2 * kf_pad, 3 * n2), lambda c: (0, 0)),
                  pl.BlockSpec((1, ct), lambda c: (0, c))],
        out_specs=[ospec, ospec],
        out_shape=[out, out],
        compiler_params=_cparams("parallel"),
        name="dense_filter_spectrum",
    )(h_time, cons["fwd_full"], 1.0 / nrm)


def _dense_conv_kernel(v_ref, x1_ref, x2_ref, h1r_ref, h1i_ref, h2r_ref, h2i_ref, sk_ref,
                       f_ref, g_ref, z_ref, *, kf_pad):
    def conv(u, hr, hi):
        s = _dot(f_ref[...], _stack3(u))
        sr, si = s[:kf_pad], s[kf_pad:]
        y = jnp.concatenate([sr * hr - si * hi, sr * hi + si * hr], axis=0)
        return _dot(g_ref[...], _stack3(y))

    v = v_ref[...]
    z1 = x1_ref[...] * (conv(v, h1r_ref[...], h1i_ref[...]) + v * sk_ref[0:1, :])
    z2 = x2_ref[...] * (conv(z1, h2r_ref[...], h2i_ref[...]) + z1 * sk_ref[1:2, :])
    z_ref[...] = z2.astype(z_ref.dtype)


def _hyena_core_ctx(u3, hr, hi, fskip, cons, b, seq):
    d = u3.shape[-1]
    dt = min(256, d)
    nd = d // dt
    kf_pad = cons["kf_pad"]
    uspec = lambda part: pl.BlockSpec((None, seq, dt), lambda bi, c: (part, bi, c))
    hspec = lambda order: pl.BlockSpec((kf_pad, dt), lambda bi, c: (0, order * nd + c))
    return pl.pallas_call(
        functools.partial(_dense_conv_kernel, kf_pad=kf_pad),
        grid=(b, nd),
        in_specs=[uspec(0), uspec(1), uspec(2), hspec(0), hspec(0), hspec(1), hspec(1),
                  pl.BlockSpec((HYENA_ORDER, dt), lambda bi, c: (0, c)),
                  pl.BlockSpec((2 * kf_pad, 3 * seq), lambda bi, c: (0, 0)),
                  pl.BlockSpec((seq, 6 * kf_pad), lambda bi, c: (0, 0))],
        out_specs=pl.BlockSpec((seq, dt), lambda bi, c: (bi, c)),
        out_shape=jax.ShapeDtypeStruct((b * seq, d), BF16),
        compiler_params=_cparams("parallel", "parallel"),
        name="hyena_core_ctx",
    )(u3, u3, u3, hr, hi, hr, hi, fskip, cons["fwd"], cons["inv"])


def _rec_consts(chunk):
    t = np.arange(chunk)
    coefs, masks = [], []
    for direction in (0, 1):
        if direction == 0:
            rows = [t[None, :] <= t[:, None], t[None, :] > t[:, None]]
        else:
            rows = [t[None, :] >= t[:, None], t[None, :] < t[:, None]]
        mk = [np.eye(chunk)]
        m = chunk // 2
        while m >= 1:
            blk = t // (2 * m)
            half = (t // m) % 2
            mid = blk * 2 * m + m
            e = np.zeros((chunk, chunk))
            for r in range(chunk):
                if direction == 0:
                    if half[r] == 1:
                        e[r, mid[r]:r + 1] = 1
                    else:
                        e[r, r + 1:mid[r]] = 1
                else:
                    if half[r] == 0:
                        e[r, r:mid[r]] = 1
                    else:
                        e[r, mid[r]:r] = 1
            same = blk[:, None] == blk[None, :]
            if direction == 0:
                mk.append(same & (half[:, None] == 1) & (half[None, :] == 0))
            else:
                mk.append(same & (half[:, None] == 0) & (half[None, :] == 1))
            rows.append(e)
            m //= 2
        rows.append(np.ones((16, chunk)))
        a = np.concatenate([np.asarray(r, np.float64) for r in rows], axis=0)
        a3 = np.concatenate([a, a, a], axis=1)
        pad = -(-a3.shape[1] // LANES) * LANES - a3.shape[1]
        a3 = np.pad(a3, ((0, 0), (0, pad)))
        coefs.append(a3)
        masks.append(np.stack([np.asarray(x, np.float32) for x in mk]))
    return (jnp.asarray(np.stack(coefs), dtype=BF16), jnp.asarray(np.stack(masks), dtype=F32))


def _rec_core(load_qkg, v_ref, coef_ref, mask_ref, s0_ref, o_ref, sfin_ref, q_s, k_s, st_s, ex_s, att_s,
              *, chunk, nsub, heads, dk, dv):
    di = pl.program_id(1)
    c = pl.program_id(2)
    levels = int(math.log2(chunk))
    tot = (2 + levels) * chunk

    @pl.when(c == 0)
    def _():
        st_s[...] = s0_ref[...]

    offs = [pl.multiple_of((s + di * (nsub - 1 - 2 * s)) * chunk, chunk) for s in range(nsub)]

    for s in range(nsub):
        q, k, g = load_qkg(pl.ds(offs[s], chunk))
        q_s[s] = q
        k_s[s] = k
        g1 = g.astype(BF16)
        r1 = g - g1.astype(F32)
        g2 = r1.astype(BF16)
        g3 = (r1 - g2.astype(F32)).astype(BF16)
        pad = coef_ref.shape[-1] - 3 * chunk
        gs = jnp.concatenate([g1, g2, g3, jnp.zeros((pad, g.shape[1]), BF16)], axis=0)
        ex_s[s] = jnp.exp(_dot(coef_ref[...], gs))

    for s in range(nsub):
        for h in range(heads):
            ks = slice(h * dk, (h + 1) * dk)
            qh = q_s[s, :, ks]
            kh = k_s[s, :, ks]
            att = mask_ref[0] * _dot_nt(qh.astype(BF16), kh.astype(BF16))
            for lv in range(levels):
                e = ex_s[s, (2 + lv) * chunk:(3 + lv) * chunk, ks]
                att += mask_ref[1 + lv] * _dot_nt((qh * e).astype(BF16), (kh * e).astype(BF16))
            att_s[s, h] = att.astype(BF16)

    for s in range(nsub):
        rows = pl.ds(offs[s], chunk)
        for h in range(heads):
            ks = slice(h * dk, (h + 1) * dk)
            vs = slice(h * dv, (h + 1) * dv)
            vh = v_ref[rows, vs].astype(BF16)
            st = st_s[h]
            o = _dot(att_s[s, h], vh)
            o += _dot_nt((q_s[s, :, ks] * ex_s[s, 0:chunk, ks]).astype(BF16), st.astype(BF16))
            o_ref[rows, vs] = o
            kd = (k_s[s, :, ks] * ex_s[s, chunk:2 * chunk, ks]).astype(BF16)
            st_s[h] = st * ex_s[s, tot:tot + 1, ks] + _dot_tn(vh, kd)

    @pl.when(c == pl.num_programs(2) - 1)
    def _():
        sfin_ref[...] = st_s[...]


def _hgrn_rec_kernel(q_ref, v_ref, f_ref, lb_ref, coef_ref, mask_ref, s0_ref, o_ref, sfin_ref,
                     q_s, k_s, st_s, ex_s, att_s, **kw):
    lb = lb_ref[...]

    def load_qkg(rows):
        qr = q_ref[rows, :]
        sg = _sigmoid(f_ref[rows, :])
        return qr * _sigmoid(qr), (1.0 - lb) * (1.0 - sg), jnp.log(lb + (1.0 - lb) * sg)

    _rec_core(load_qkg, v_ref, coef_ref, mask_ref, s0_ref, o_ref, sfin_ref, q_s, k_s, st_s, ex_s, att_s, **kw)


def _gla_rec_kernel(q_ref, k_ref, v_ref, a_ref, wup_ref, bup_ref, coef_ref, mask_ref, s0_ref,
                    o_ref, sfin_ref, q_s, k_s, st_s, ex_s, att_s, *, qscale, **kw):
    def load_qkg(rows):
        xg = _dot(a_ref[rows, :].astype(BF16), wup_ref[...]) + bup_ref[...]
        g = (jnp.minimum(xg, 0.0) - jnp.log(1.0 + jnp.exp(-jnp.abs(xg)))) * (1.0 / GLA_GATE_NORM)
        return q_ref[rows, :] * qscale, k_ref[rows, :], g

    _rec_core(load_qkg, v_ref, coef_ref, mask_ref, s0_ref, o_ref, sfin_ref, q_s, k_s, st_s, ex_s, att_s, **kw)


def _rec_call(kind, proj, extra, s0, consts, b, seq, d, heads, dk, dv):
    chunk = REC_CHUNK
    nsub = REC_SUBCHUNKS
    rows = chunk * nsub
    nc = seq // rows
    coef, masks = consts
    hk = heads * dk
    hv = heads * dv
    levels = int(math.log2(chunk))
    n_rows = (2 + levels) * chunk + 16

    def rmap(col):
        return lambda bi, di, c: (bi * nc + c + di * (nc - 1 - 2 * c), col)

    cspecs = [pl.BlockSpec((None, n_rows, coef.shape[-1]), lambda bi, di, c: (di, 0, 0)),
              pl.BlockSpec((None, levels + 1, chunk, chunk), lambda bi, di, c: (di, 0, 0, 0)),
              pl.BlockSpec((None, None, heads, dv, dk), lambda bi, di, c: (bi, di, 0, 0, 0))]
    if kind == "hgrn":
        lb = extra
        kernel = _hgrn_rec_kernel
        in_specs = [pl.BlockSpec((rows, d), rmap(0)),
                    pl.BlockSpec((rows, d), rmap(1)),
                    pl.BlockSpec((rows, d), lambda bi, di, c: (bi * nc + c + di * (nc - 1 - 2 * c), 3 + di)),
                    pl.BlockSpec((None, 1, d), lambda bi, di, c: (di, 0, 0))] + cspecs
        args = (proj, proj, proj, lb, coef, masks, s0)
        kw = {}
    else:
        wup, bup = extra
        kernel = _gla_rec_kernel
        in_specs = [pl.BlockSpec((rows, hk), rmap(0)),
                    pl.BlockSpec((rows, hk), rmap(1)),
                    pl.BlockSpec((rows, hv), rmap(1)),
                    pl.BlockSpec((rows, LANES), rmap((2 * hk + 2 * hv) // LANES)),
                    pl.BlockSpec((None, LANES, hk), lambda bi, di, c: (di, 0, 0)),
                    pl.BlockSpec((None, 1, hk), lambda bi, di, c: (di, 0, 0))] + cspecs
        args = (proj, proj, proj, proj, wup, bup, coef, masks, s0)
        kw = dict(qscale=float(dk) ** -0.5)
    return pl.pallas_call(
        functools.partial(kernel, chunk=chunk, nsub=nsub, heads=heads, dk=dk, dv=dv, **kw),
        grid=(b, 2, nc),
        in_specs=in_specs,
        out_specs=[pl.BlockSpec((None, rows, hv), lambda bi, di, c: (di, bi * nc + c + di * (nc - 1 - 2 * c), 0)),
                   pl.BlockSpec((None, None, heads, dv, dk), lambda bi, di, c: (bi, di, 0, 0, 0))],
        out_shape=[jax.ShapeDtypeStruct((2, b * seq, hv), F32),
                   jax.ShapeDtypeStruct((b, 2, heads, dv, dk), F32)],
        scratch_shapes=[pltpu.VMEM((nsub, chunk, hk), F32), pltpu.VMEM((nsub, chunk, hk), F32),
                        pltpu.VMEM((heads, dv, dk), F32), pltpu.VMEM((nsub, n_rows, hk), F32),
                        pltpu.VMEM((nsub, heads, chunk, chunk), BF16)],
        compiler_params=_cparams("parallel", "parallel", "arbitrary"),
        name=kind + "_recurrence",
    )(*args)


def _gated_out_kernel(o_ref, gate_ref, gn_ref, w_ref, x_ref, mg_ref, out_ref, *, heads):
    d = x_ref.shape[1]
    dh = d // heads
    kc = max(dh, 2 * MXU_N)
    acc = None
    for k0 in range(0, d, kc):
        parts = []
        for h0 in range(k0, k0 + kc, dh):
            seg = o_ref[0, :, h0:h0 + dh] + o_ref[1, :, h0:h0 + dh]
            ms = jnp.mean(seg * seg, axis=-1, keepdims=True)
            parts.append(seg * lax.rsqrt(ms + RMS_EPS))
        gate = gate_ref[:, k0:k0 + kc]
        y = jnp.concatenate(parts, axis=1) * gn_ref[:, k0:k0 + kc] * (gate * _sigmoid(gate))
        part = _dot(y.astype(BF16), w_ref[k0:k0 + kc, :])
        acc = part if acc is None else acc + part
    out_ref[...] = x_ref[...] + mg_ref[...] * acc


def _gated_out(o2, proj, gate_col, gn, w, x, mg, bmap, heads, tm):
    m, d = x.shape
    return pl.pallas_call(
        functools.partial(_gated_out_kernel, heads=heads),
        grid=(m // tm,),
        in_specs=[pl.BlockSpec((2, tm, d), lambda i: (0, i, 0)),
                  pl.BlockSpec((tm, d), lambda i: (i, gate_col)),
                  pl.BlockSpec((1, d), lambda i: (0, 0)),
                  pl.BlockSpec((d, d), lambda i: (0, 0)),
                  pl.BlockSpec((tm, d), lambda i: (i, 0)),
                  _mod_spec(d, bmap)],
        out_specs=pl.BlockSpec((tm, d), lambda i: (i, 0)),
        out_shape=jax.ShapeDtypeStruct((m, d), F32),
        compiler_params=_cparams("parallel"),
        name="gated_out",
    )(o2, proj, gn, w, x, mg)


def _recurrent_mix(kind, x_ctx, x_lat, norm, ctx_map, w_in, extra, consts, b, lc, seq, d, heads, dk, dv, tn):
    g1, sh, sc = norm
    tm_c, tm_l = min(1024, x_ctx.shape[0]), min(1024, x_lat.shape[0])
    proj_ctx = _projection(x_ctx, g1, sh, sc, w_in, ctx_map, tm_c, tn)
    proj_lat = _projection(x_lat, g1, sh, sc, w_in, lambda i: (i * tm_l) // seq, tm_l, tn)
    s0 = jnp.zeros((b, 2, heads, dv, dk), F32)
    o_ctx, s_ctx = _rec_call(kind, proj_ctx, extra, s0, consts, b, lc, d, heads, dk, dv)
    o_lat, _ = _rec_call(kind, proj_lat, extra, s_ctx, consts, b, seq, d, heads, dk, dv)
    return (proj_ctx, o_ctx), (proj_lat, o_lat)


def kernel(x, c, ctx, c_ctx, w_mod, b_mod, norm1_g, norm2_g, w_ffn_in, w_ffn_out, final_g, hy_w_in, hy_conv_w, hy_fw1, hy_fb1, hy_ffreq, hy_fw2, hy_fb2, hy_fwout, hy_fskip, hy_w_out, hg_w_in, hg_lb_logits, hg_onorm_g, hg_w_out, gla_w_in, gla_w_up, gla_b_up, gla_onorm_g, gla_w_out):
    b, seq, d = x.shape
    lc = ctx.shape[1]
    depth = w_mod.shape[0]
    assert b + 1 <= MOD_ROWS and seq % (FFT_P * 2) == 0 and FFT_P % GRID_W == 0
    m_lat, m_ctx = b * seq, b * lc
    ctx_row = b
    tm_lat, tm_ctx = 512, min(512, m_ctx)
    lat_map = lambda i, tm=tm_lat: (i * tm) // seq
    ctx_map = lambda i: ctx_row

    xl = x.reshape(m_lat, d)
    xc = ctx.reshape(m_ctx, d)
    c8 = jnp.zeros((MOD_ROWS, d), F32).at[:b].set(c).at[ctx_row].set(c_ctx)
    mod = _modulation(c8, w_mod, b_mod).reshape(depth, MOD_ROWS, N_MOD, 1, d)

    plan = _fft_consts(seq, seq // FFT_P)
    plan_f = _fft_consts(seq, 2 * seq // FFT_P)
    dense = _dense_consts(lc)
    rec_consts = _rec_consts(REC_CHUNK)
    qh = plan["qh"]

    wb_ffn_in, wb_ffn_out = _to_bf16(w_ffn_in), _to_bf16(w_ffn_out)
    wb_hy_in, wb_hy_out = _to_bf16(hy_w_in), _to_bf16(hy_w_out)
    wb_hg_in, wb_hg_out = _to_bf16(hg_w_in), _to_bf16(hg_w_out)
    wb_gla_in, wb_gla_out = _to_bf16(gla_w_in), _to_bf16(gla_w_out)

    for i in range(depth):
        last = i == depth - 1
        kind, j = i % N_MIXERS, i // N_MIXERS
        mv = [mod[i, :, k] for k in range(N_MOD)]
        g1 = norm1_g[i][None, :]
        need_ctx = (not last) or kind != 0
        if kind == 0:
            w_in, w_out = wb_hy_in[j], wb_hy_out[j]
            hy = (hy_fw1[j], hy_fb1[j], hy_ffreq[j], hy_fw2[j], hy_fb2[j], hy_fwout[j])
            tm_h = 1024
            u = _hyena_in(xl, g1, mv[0], mv[1], w_in, hy_conv_w[j], lambda i: (i * tm_h) // seq,
                          tm_h, min(1024, d), GRID_W).reshape(3, b, qh, FFT_P, d)
            hr, hi = _filter_spectrum_lat(hy, seq, plan_f)
            z1 = _long_conv_lat(u, 0, u, 1, hr, hi, hy_fskip[j, 0][None, :], plan, 0, F32)
            z2 = _long_conv_lat(z1, 0, u, 2, hr, hi, hy_fskip[j, 1][None, :], plan, 1, BF16)
            xl = _out_res(z2.reshape(m_lat, d), w_out, xl, mv[2], lat_map, tm_lat)
            if need_ctx:
                uc = _hyena_in(xc, g1, mv[0], mv[1], w_in, hy_conv_w[j], ctx_map, lc, min(1024, d), lc)
                hcr, hci = _filter_spectrum_ctx(hy, lc, dense)
                zc = _hyena_core_ctx(uc, hcr, hci, hy_fskip[j], dense, b, lc)
                xc = _out_res(zc, w_out, xc, mv[2], ctx_map, tm_ctx)
        else:
            if kind == 1:
                heads = d // HGRN_EXPAND
                dk = dv = HGRN_EXPAND
                w_in = wb_hg_in[j]
                lb_cum = jnp.cumsum(jax.nn.softmax(hg_lb_logits.astype(F32), axis=1), axis=1)
                extra = (lb_cum[:, i] - lb_cum[:, 0])[:, None, :]
                gn, w_out, tn = hg_onorm_g[j], wb_hg_out[j], 1024
                rkind = "hgrn"
            else:
                heads = GLA_HEADS
                dk, dv = d // 2 // heads, d // heads
                n_in = gla_w_in.shape[-1]
                n_pad = -(-n_in // LANES) * LANES
                w_in = jnp.pad(wb_gla_in[j], ((0, 0), (0, n_pad - n_in)))
                r = GLA_GATE_RANK
                wup = jnp.zeros((2, LANES, heads * dk), F32)
                wup = wup.at[0, :r].set(gla_w_up[j, 0]).at[1, r:2 * r].set(gla_w_up[j, 1]).astype(BF16)
                extra = (wup, gla_b_up[j][:, None, :])
                gn, w_out = gla_onorm_g[j], wb_gla_out[j]
                tn = _lane_tile(n_pad, 1024)
                rkind = "gla"
            (p_ctx, o_ctx), (p_lat, o_lat) = _recurrent_mix(
                rkind, xc, xl, (g1, mv[0], mv[1]), ctx_map, w_in, extra, rec_consts,
                b, lc, seq, d, heads, dk, dv, tn)
            gate_col = 2 if kind == 1 else (2 * heads * dk + heads * dv) // d
            tm_g = 256
            xl = _gated_out(o_lat, p_lat, gate_col, gn[None, :], w_out, xl, mv[2],
                            lambda i: (i * tm_g) // seq, heads, tm_g)
            xc = _gated_out(o_ctx, p_ctx, gate_col, gn[None, :], w_out, xc, mv[2], ctx_map, heads, tm_g)
        g2 = norm2_g[i][None, :]
        xl = _ffn(xl, g2, mv[3], mv[4], mv[5], wb_ffn_in, wb_ffn_out, i, lat_map, tm_lat, 512)
        if not last:
            xc = _ffn(xc, g2, mv[3], mv[4], mv[5], wb_ffn_in, wb_ffn_out, i, ctx_map, tm_ctx, 512)
    return _final_norm(xl, final_g[None, :], tm_lat).reshape(b, seq, d)
```

```python
import functools
import math

import ml_dtypes
import numpy as np
import jax
import jax.numpy as jnp
from jax import lax
from jax.experimental import pallas as pl
from jax.experimental.pallas import tpu as pltpu

F32 = jnp.float32
BF16 = jnp.bfloat16

N_MOD = 6
N_MIXERS = 3
RMS_EPS = 1e-6
GRID_W = 64
HYENA_ORDER = 2
FILTER_BANDS = 16
FILTER_EMB = 1 + 2 * FILTER_BANDS
HYENA_DECAY_MIN = math.log(1e-2) / 1.5
HYENA_DECAY_MAX = math.log(1e-2) / 0.3
HGRN_EXPAND = 128
GLA_HEADS = 4
GLA_GATE_RANK = 16
GLA_GATE_NORM = 16.0

LANES = 128
SUBLANES = 8
MXU_N = 256
V7X_VMEM_LIMIT = 56 * 1024 * 1024

FFT_P = 128
REC_CHUNK = 64
REC_SUBCHUNKS = 4
MOD_ROWS = 8


def _cparams(*sem):
    return pltpu.CompilerParams(dimension_semantics=sem, vmem_limit_bytes=V7X_VMEM_LIMIT)


def _dot(a, b):
    return jnp.dot(a, b, preferred_element_type=F32)


def _dot_nt(a, b):
    return lax.dot_general(a, b, (((1,), (1,)), ((), ())), preferred_element_type=F32)


def _dot_tn(a, b):
    return lax.dot_general(a, b, (((0,), (0,)), ((), ())), preferred_element_type=F32)


def _split(x):
    hi = x.astype(BF16)
    lo = (x - hi.astype(F32)).astype(BF16)
    return hi, lo


def _stack3(x, pad_rows=0):
    hi, lo = _split(x)
    parts = [hi, lo, hi]
    if pad_rows:
        parts.append(jnp.zeros((pad_rows, x.shape[1]), BF16))
    return jnp.concatenate(parts, axis=0)


def _const3(c, pad_cols=0):
    hi = c.astype(ml_dtypes.bfloat16)
    lo = (c - hi.astype(np.float64)).astype(ml_dtypes.bfloat16)
    parts = [hi, hi, lo]
    if pad_cols:
        parts.append(np.zeros(c.shape[:-1] + (pad_cols,), ml_dtypes.bfloat16))
    return jnp.asarray(np.concatenate(parts, axis=-1))


def _dot3(a, bh, bl):
    ah, al = _split(a)
    return _dot(ah, bh) + _dot(ah, bl) + _dot(al, bh)


def _sigmoid(x):
    return jax.nn.sigmoid(x)


def _normmod(x, g, sh, sc):
    ms = jnp.mean(x * x, axis=-1, keepdims=True)
    y = x * lax.rsqrt(ms + RMS_EPS) * g
    return y * (1.0 + sc) + sh


def _mod_kernel(c_ref, w_ref, b_ref, o_ref):
    c = c_ref[...]
    s = (c * _sigmoid(c)).astype(BF16)
    o_ref[...] = _dot(s, w_ref[...].astype(BF16)) + b_ref[...]


def _modulation(c8, w_mod, b_mod):
    depth, d, n = w_mod.shape
    tn = 1024
    return pl.pallas_call(
        _mod_kernel,
        grid=(depth, n // tn),
        in_specs=[pl.BlockSpec((MOD_ROWS, d), lambda l, j: (0, 0)),
                  pl.BlockSpec((None, d, tn), lambda l, j: (l, 0, j)),
                  pl.BlockSpec((None, 1, tn), lambda l, j: (l, 0, j))],
        out_specs=pl.BlockSpec((None, MOD_ROWS, tn), lambda l, j: (l, 0, j)),
        out_shape=jax.ShapeDtypeStruct((depth, MOD_ROWS, n), F32),
        compiler_params=_cparams("parallel", "parallel"),
        name="modulation",
    )(c8, w_mod, b_mod.reshape(depth, 1, n))


def _mod_spec(d, bmap):
    return pl.BlockSpec((None, 1, d), lambda *idx: (bmap(idx[0]), 0, 0))


def _lane_tile(n, cap):
    best = n
    for t in range(LANES, min(n, cap) + 1, LANES):
        if n % t == 0:
            best = t
    return best


def _cast_kernel(x_ref, o_ref):
    o_ref[...] = x_ref[...].astype(o_ref.dtype)


def _to_bf16(w):
    shape = w.shape
    w2 = w.reshape(-1, shape[-1])
    r, c = w2.shape
    tr, tc = min(512, r), _lane_tile(c, 2048)
    out = pl.pallas_call(
        _cast_kernel,
        grid=(r // tr, c // tc),
        in_specs=[pl.BlockSpec((tr, tc), lambda i, j: (i, j))],
        out_specs=pl.BlockSpec((tr, tc), lambda i, j: (i, j)),
        out_shape=jax.ShapeDtypeStruct((r, c), BF16),
        compiler_params=_cparams("parallel", "parallel"),
        name="cast_bf16",
    )(w2)
    return out.reshape(shape)


def _proj_kernel(x_ref, g_ref, sh_ref, sc_ref, w_ref, o_ref, h_s):
    @pl.when(pl.program_id(1) == 0)
    def _():
        h_s[...] = _normmod(x_ref[...], g_ref[...], sh_ref[...], sc_ref[...]).astype(BF16)

    o_ref[...] = _dot(h_s[...], w_ref[...])


def _projection(x, g, sh, sc, w, bmap, tm, tn):
    m, d = x.shape
    n = w.shape[1]
    return pl.pallas_call(
        _proj_kernel,
        grid=(m // tm, n // tn),
        in_specs=[pl.BlockSpec((tm, d), lambda i, j: (i, 0)),
                  pl.BlockSpec((1, d), lambda i, j: (0, 0)),
                  _mod_spec(d, bmap), _mod_spec(d, bmap),
                  pl.BlockSpec((d, tn), lambda i, j: (0, j))],
        out_specs=pl.BlockSpec((tm, tn), lambda i, j: (i, j)),
        out_shape=jax.ShapeDtypeStruct((m, n), F32),
        scratch_shapes=[pltpu.VMEM((tm, d), BF16)],
        compiler_params=_cparams("parallel", "arbitrary"),
        name="projection",
    )(x, g, sh, sc, w)


def _out_res_kernel(a_ref, w_ref, x_ref, mg_ref, o_ref):
    o_ref[...] = x_ref[...] + mg_ref[...] * _dot(a_ref[...], w_ref[...])


def _ffn_kernel(x_ref, g_ref, sh_ref, sc_ref, mg_ref, wg_ref, wu_ref, wo_ref, fg_ref, o_ref, h_s, acc_s,
                *, final_norm):
    j = pl.program_id(1)

    @pl.when(j == 0)
    def _():
        h_s[...] = _normmod(x_ref[...], g_ref[...], sh_ref[...], sc_ref[...]).astype(BF16)
        acc_s[...] = jnp.zeros_like(acc_s)

    h = h_s[...]
    a = _dot(h, wg_ref[...])
    u = _dot(h, wu_ref[...])
    act = (a * _sigmoid(a) * u).astype(BF16)
    acc_s[...] += _dot(act, wo_ref[...])

    @pl.when(j == pl.num_programs(1) - 1)
    def _():
        y = x_ref[...] + mg_ref[...] * acc_s[...]
        if final_norm:
            y = y * lax.rsqrt(jnp.mean(y * y, axis=-1, keepdims=True) + RMS_EPS) * fg_ref[...]
        o_ref[...] = y


def _ffn(x, g, sh, sc, mg, w_in, w_out, layer, final_g, final_norm, bmap, tm, tf):
    m, d = x.shape
    f = w_out.shape[1]
    nf = f // tf
    return pl.pallas_call(
        functools.partial(_ffn_kernel, final_norm=final_norm),
        grid=(m // tm, nf),
        in_specs=[pl.BlockSpec((tm, d), lambda i, j: (i, 0)),
                  pl.BlockSpec((1, d), lambda i, j: (0, 0)),
                  _mod_spec(d, bmap), _mod_spec(d, bmap), _mod_spec(d, bmap),
                  pl.BlockSpec((None, d, tf), lambda i, j: (layer, 0, j)),
                  pl.BlockSpec((None, d, tf), lambda i, j: (layer, 0, nf + j)),
                  pl.BlockSpec((None, tf, d), lambda i, j: (layer, j, 0)),
                  pl.BlockSpec((1, d), lambda i, j: (0, 0))],
        out_specs=pl.BlockSpec((tm, d), lambda i, j: (i, 0)),
        out_shape=jax.ShapeDtypeStruct((m, d), F32),
        scratch_shapes=[pltpu.VMEM((tm, d), BF16), pltpu.VMEM((tm, d), F32)],
        compiler_params=_cparams("parallel", "arbitrary"),
        name="ffn",
    )(x, g, sh, sc, mg, w_in, w_in, w_out, final_g)


def _conv3_rows(acc, cw, row_len):
    tm = acc.shape[0]
    rid = lax.broadcasted_iota(jnp.int32, (tm, 1), 0) % row_len
    up = jnp.where(rid == 0, 0.0, pltpu.roll(acc, 1, 0))
    dn = jnp.where(rid == row_len - 1, 0.0, pltpu.roll(acc, tm - 1, 0))
    return cw[0:1] * up + cw[1:2] * acc + cw[2:3] * dn


def _hy_in_kernel(x_ref, g_ref, sh_ref, sc_ref, w_ref, cw_ref, o_ref, h_s, *, row_len):
    @pl.when(pl.program_id(1) == 0)
    def _():
        h_s[...] = _normmod(x_ref[...], g_ref[...], sh_ref[...], sc_ref[...]).astype(BF16)

    h = h_s[...]
    for n0 in range(0, o_ref.shape[1], MXU_N):
        ns = slice(n0, n0 + MXU_N)
        o_ref[:, ns] = _conv3_rows(_dot(h, w_ref[:, ns]), cw_ref[:, ns], row_len).astype(o_ref.dtype)


def _hyena_in(x, g, sh, sc, w, cw, bmap, tm, tn, row_len, out_dtype):
    m, d = x.shape
    nd = d // tn
    return pl.pallas_call(
        functools.partial(_hy_in_kernel, row_len=row_len),
        grid=(m // tm, 3 * nd),
        in_specs=[pl.BlockSpec((tm, d), lambda i, j: (i, 0)),
                  pl.BlockSpec((1, d), lambda i, j: (0, 0)),
                  _mod_spec(d, bmap), _mod_spec(d, bmap),
                  pl.BlockSpec((d, tn), lambda i, j: (0, j)),
                  pl.BlockSpec((3, tn), lambda i, j: (0, j))],
        out_specs=pl.BlockSpec((None, tm, tn), lambda i, j: (j // nd, i, j % nd)),
        out_shape=jax.ShapeDtypeStruct((3, m, d), out_dtype),
        scratch_shapes=[pltpu.VMEM((tm, d), BF16)],
        compiler_params=_cparams("parallel", "arbitrary"),
        name="hyena_in",
    )(x, g, sh, sc, w, cw)


def _out_res(a, w, x, mg, bmap, tm):
    m, d = x.shape
    return pl.pallas_call(
        _out_res_kernel,
        grid=(m // tm,),
        in_specs=[pl.BlockSpec((tm, d), lambda i: (i, 0)),
                  pl.BlockSpec((d, d), lambda i: (0, 0)),
                  pl.BlockSpec((tm, d), lambda i: (i, 0)),
                  _mod_spec(d, bmap)],
        out_specs=pl.BlockSpec((tm, d), lambda i: (i, 0)),
        out_shape=jax.ShapeDtypeStruct((m, d), F32),
        compiler_params=_cparams("parallel"),
        name="out_residual",
    )(a, w, x, mg)


def _filter_kernel(z_ref, w1h_ref, w1l_ref, b1_ref, fq_ref, w2h_ref, w2l_ref, b2_ref,
                   woh_ref, wol_ref, dl_ref, h_ref, nrm_ref):
    z = z_ref[...]
    t = z[:, 0:1]
    valid = z[:, FILTER_EMB:FILTER_EMB + 1]
    a1 = _dot3(z, w1h_ref[...], w1l_ref[...]) + b1_ref[...]
    hid = jnp.sin(fq_ref[0:1, :] * a1)
    a2 = _dot3(hid, w2h_ref[...], w2l_ref[...]) + b2_ref[...]
    hid = jnp.sin(fq_ref[1:2, :] * a2)
    h = _dot3(hid, woh_ref[...], wol_ref[...]) * jnp.exp(-t * dl_ref[...]) * valid
    h_ref[...] = h.reshape(h_ref.shape)

    @pl.when((pl.program_id(1) == 0) & (pl.program_id(2) == 0))
    def _():
        nrm_ref[...] = jnp.zeros_like(nrm_ref)

    nrm_ref[...] += jnp.sum(jnp.abs(h), axis=0, keepdims=True)


def _pad2(a, rows, cols):
    return jnp.pad(a, ((0, rows - a.shape[0]), (0, cols - a.shape[1])))


def _filter_positions(seq, perm):
    if perm:
        r_hi = seq // FFT_P
        gt, ng = 2 * SUBLANES, FFT_P // (2 * SUBLANES)
        dirs, g, tl, th = np.meshgrid(np.arange(2), np.arange(ng), np.arange(gt), np.arange(r_hi), indexing="ij")
        n = FFT_P * (r_hi * dirs + th) + g * gt + tl
    else:
        r_hi, gt, ng = seq, 1, 1
        dirs, g, tl, th = np.meshgrid(np.arange(2), np.arange(1), np.arange(1), np.arange(seq), indexing="ij")
        n = seq * dirs + th
    n = n.reshape(-1)
    pos = np.where(n < seq, n, 2 * seq - 1 - n).astype(np.float32)
    z = np.zeros((n.size, LANES), np.float32)
    z[:, 0] = pos / np.float32(max(seq - 1, 1))
    bands = np.arange(1, FILTER_BANDS + 1, dtype=np.float32)
    ang = (np.float32(2.0 * math.pi / seq) * pos)[:, None] * bands[None, :]
    z[:, 1:1 + FILTER_BANDS] = np.cos(ang.astype(np.float64))
    z[:, 1 + FILTER_BANDS:FILTER_EMB] = -np.sin(ang.astype(np.float64))
    z[:, FILTER_EMB] = n != seq
    return jnp.asarray(z), gt, ng, r_hi


def _hyena_filter_time(fw1, fb1, ffreq, fw2, fb2, fwout, seq, perm):
    od = fwout.shape[1] // 2
    d = od // HYENA_ORDER
    z, gt, ng, r_hi = _filter_positions(seq, perm)
    rows = gt * r_hi
    ct = min(2048, od)
    hidden = LANES
    w1h, w1l = _split(_pad2(fw1, LANES, hidden))
    w2h, w2l = _split(_pad2(fw2, hidden, hidden))
    woh, wol = _split(_pad2(fwout, hidden, fwout.shape[1]))
    b1 = _pad2(fb1[None, :], 1, hidden)
    b2 = _pad2(fb2[None, :], 1, hidden)
    fq = _pad2(ffreq, 2, hidden)
    deltas = np.abs(np.linspace(HYENA_DECAY_MIN, HYENA_DECAY_MAX, d, dtype=np.float32))
    dl = jnp.asarray(np.tile(deltas, HYENA_ORDER)[None, :])
    nct = od // ct
    small = lambda shape: pl.BlockSpec(shape, lambda c, di, g: (0, 0))
    wspec = pl.BlockSpec((hidden, ct), lambda c, di, g: (0, di * nct + c))
    return pl.pallas_call(
        _filter_kernel,
        grid=(nct, 2, ng),
        in_specs=[pl.BlockSpec((rows, LANES), lambda c, di, g: (di * ng + g, 0)),
                  small((LANES, hidden)), small((LANES, hidden)), small((1, hidden)), small((2, hidden)),
                  small((hidden, hidden)), small((hidden, hidden)), small((1, hidden)),
                  wspec, wspec,
                  pl.BlockSpec((1, ct), lambda c, di, g: (0, c))],
        out_specs=[pl.BlockSpec((None, gt, None, r_hi, ct), lambda c, di, g: (g, 0, di, 0, c)),
                   pl.BlockSpec((1, ct), lambda c, di, g: (0, c))],
        out_shape=[jax.ShapeDtypeStruct((ng, gt, 2, r_hi, od), F32), jax.ShapeDtypeStruct((1, od), F32)],
        compiler_params=_cparams("parallel", "arbitrary", "arbitrary"),
        name="hyena_filter",
    )(z, w1h, w1l, b1, fq, w2h, w2l, b2, woh, wol, dl)


def _fft_consts(seq, k_in):
    n = 2 * seq
    p = FFT_P
    q = n // p
    qh = q // 2
    ka_used = qh + 1
    ka_pad = -(-ka_used // 8) * 8
    ka = np.arange(ka_used)
    tl = np.arange(p)
    th = np.arange(k_in)
    theta = 2 * np.pi * (ka[None, :, None] * th[None, None, :] / q + ka[None, :, None] * tl[:, None, None] / n)
    f1 = np.zeros((p, 2 * ka_pad, k_in))
    f1[:, :ka_used] = np.cos(theta)
    f1[:, ka_pad:ka_pad + ka_used] = -np.sin(theta)
    k3 = 3 * k_in
    f1c = _const3(f1, pad_cols=-(-k3 // LANES) * LANES - k3)
    f1s = jnp.asarray(np.pad(f1, ((0, 0), (0, 0), (0, -(-k_in // LANES) * LANES - k_in))).astype(ml_dtypes.bfloat16))
    ang = 2 * np.pi * np.outer(np.arange(p), np.arange(p)) / p
    cc, sc = np.cos(ang), np.sin(ang)
    m2f_np, m2i_np = np.block([[cc, sc], [-sc, cc]]), np.block([[cc, -sc], [sc, cc]])
    m2f = _const3(m2f_np)
    m2f1, m2i1 = (jnp.asarray(m.astype(ml_dtypes.bfloat16)) for m in (m2f_np, m2i_np))
    tho = np.arange(qh)
    phi = 2 * np.pi * (tho[None, :, None] * ka[None, None, :] / q + ka[None, None, :] * tl[:, None, None] / n)
    wgt = np.where((ka == 0) | (ka == qh), 1.0, 2.0) / n
    kr = 2 * ka_pad
    g = np.zeros((p, qh, -(-kr // LANES) * LANES))
    col = 2 * SUBLANES * (ka // SUBLANES) + ka % SUBLANES
    g[:, :, col] = wgt * np.cos(phi)
    g[:, :, col + SUBLANES] = -wgt * np.sin(phi)
    gc = jnp.asarray(g.astype(ml_dtypes.bfloat16))
    return dict(q=q, qh=qh, ka_used=ka_used, ka_pad=ka_pad, f1=f1c, f1s=f1s, m2f=m2f, m2f1=m2f1, m2i1=m2i1, g=gc)


def _s1_kernel(x_ref, f_ref, ar_ref, ai_ref, a_s, *, group, ka_pad, pad_rows, natural, split):
    x = x_ref[...].astype(F32)
    if natural:
        x = jnp.swapaxes(x, 0, 1)
    zpad = None if split else jnp.zeros((pad_rows, x.shape[-1]), BF16)
    for j in range(group):
        xx = _stack3(x[j], pad_rows) if split else jnp.concatenate([x[j].astype(BF16), zpad], axis=0)
        a_s[j] = _dot(f_ref[j], xx)
    a = jnp.swapaxes(a_s[...], 0, 1)
    ar_ref[...] = a[:ka_pad].astype(ar_ref.dtype)
    ai_ref[...] = a[ka_pad:].astype(ai_ref.dtype)


def _fft_stage1(x5, part, f1c, ka_pad, group, dt, out_dtype, natural, split):
    if natural:
        _, b, k_in, p, d = x5.shape
        xspec = pl.BlockSpec((None, None, k_in, group, dt), lambda bi, g, c: (part, bi, 0, g, c))
    else:
        _, b, p, k_in, d = x5.shape
        xspec = pl.BlockSpec((None, None, group, k_in, dt), lambda bi, g, c: (part, bi, g, 0, c))
    kc = f1c.shape[-1]
    out = jax.ShapeDtypeStruct((b, ka_pad, p, d), out_dtype)
    ospec = pl.BlockSpec((None, ka_pad, group, dt), lambda bi, g, c: (bi, 0, g, c))
    pad_rows = kc - (3 if split else 1) * k_in
    return pl.pallas_call(
        functools.partial(_s1_kernel, group=group, ka_pad=ka_pad, pad_rows=pad_rows, natural=natural, split=split),
        grid=(b, p // group, d // dt),
        in_specs=[xspec,
                  pl.BlockSpec((group, 2 * ka_pad, kc), lambda bi, g, c: (g, 0, 0))],
        out_specs=[ospec, ospec],
        out_shape=[out, out],
        scratch_shapes=[pltpu.VMEM((group, 2 * ka_pad, dt), F32)],
        compiler_params=_cparams("parallel", "parallel", "parallel"),
        name="fft_stage1",
    )(x5, f1c)


def _s2_filter_kernel(ar_ref, ai_ref, m_ref, inv_ref, hr_ref, hi_ref, *, p):
    for i in range(ar_ref.shape[0]):
        x = jnp.concatenate([ar_ref[i], ai_ref[i]], axis=0)
        y = _dot(m_ref[...], x.astype(BF16)) * inv_ref[...]
        hr_ref[i] = y[:p]
        hi_ref[i] = y[p:]


def _fft_stage2_filter(ar, ai, m2f, inv_nrm, dt):
    ka, p, d = ar.shape
    spec = pl.BlockSpec((SUBLANES, p, dt), lambda k, c: (k, 0, c))
    out = jax.ShapeDtypeStruct((ka, p, d), F32)
    return pl.pallas_call(
        functools.partial(_s2_filter_kernel, p=p),
        grid=(ka // SUBLANES, d // dt),
        in_specs=[spec, spec,
                  pl.BlockSpec((2 * p, 2 * p), lambda k, c: (0, 0)),
                  pl.BlockSpec((1, dt), lambda k, c: (0, c))],
        out_specs=[spec, spec],
        out_shape=[out, out],
        compiler_params=_cparams("parallel", "parallel"),
        name="fft_stage2_filter",
    )(ar, ai, m2f, inv_nrm)


def _s2_kernel(ar_ref, ai_ref, hr_ref, hi_ref, mf_ref, mi_ref, c_ref, x_s, y_s, c_s, *, p, ka_used):
    kg = pl.program_id(0)
    assert ka_used % SUBLANES == 1

    def run(rows):
        for i in rows:
            x_s[i] = _dot(mf_ref[...], jnp.concatenate([ar_ref[i], ai_ref[i]], axis=0))
        for i in rows:
            xr, xi = x_s[i, :p], x_s[i, p:]
            hr, hi = hr_ref[i], hi_ref[i]
            y_s[i, :p] = (xr * hr - xi * hi).astype(BF16)
            y_s[i, p:] = (xr * hi + xi * hr).astype(BF16)
        for i in rows:
            c = _dot(mi_ref[...], y_s[i])
            c_s[i] = c[:p]
            c_s[SUBLANES + i] = c[p:]

    run(range(1))
    full = (kg + 1) * SUBLANES <= ka_used

    @pl.when(full)
    def _():
        run(range(1, SUBLANES))

    @pl.when(jnp.logical_not(full))
    def _():
        for i in range(1, SUBLANES):
            c_s[i] = jnp.zeros(c_s.shape[1:], F32)
            c_s[SUBLANES + i] = jnp.zeros(c_s.shape[1:], F32)

    c_ref[...] = jnp.swapaxes(c_s[...], 0, 1).astype(c_ref.dtype)


def _fft_stage2(ar, ai, hr, hi, m2f, m2i, ka_used, h_col0, dt):
    b, ka, p, d = ar.shape
    hc = h_col0 // dt
    aspec = pl.BlockSpec((None, SUBLANES, p, dt), lambda k, c, bi: (bi, k, 0, c))
    hspec = pl.BlockSpec((SUBLANES, p, dt), lambda k, c, bi: (k, 0, hc + c))
    mspec = pl.BlockSpec((2 * p, 2 * p), lambda k, c, bi: (0, 0))
    return pl.pallas_call(
        functools.partial(_s2_kernel, p=p, ka_used=ka_used),
        grid=(ka // SUBLANES, d // dt, b),
        in_specs=[aspec, aspec, hspec, hspec, mspec, mspec],
        out_specs=pl.BlockSpec((None, p, 2 * SUBLANES, dt), lambda k, c, bi: (bi, 0, k, c)),
        out_shape=jax.ShapeDtypeStruct((b, p, 2 * ka, d), BF16),
        scratch_shapes=[pltpu.VMEM((SUBLANES, 2 * p, dt), F32), pltpu.VMEM((SUBLANES, 2 * p, dt), BF16),
                        pltpu.VMEM((2 * SUBLANES, p, dt), F32)],
        compiler_params=_cparams("parallel", "parallel", "parallel"),
        name="fft_stage2",
    )(ar, ai, hr, hi, m2f, m2i)


def _s3_kernel(c_ref, g_ref, v_ref, x_ref, sk_ref, z_ref, y_s, *, group, pad_rows):
    zpad = jnp.zeros((pad_rows, c_ref.shape[-1]), BF16)
    for j in range(group):
        y_s[j] = _dot(g_ref[j], jnp.concatenate([c_ref[j], zpad], axis=0))
    y = jnp.swapaxes(y_s[...], 0, 1)
    v = v_ref[...].astype(F32)
    z_ref[...] = ((y + v * sk_ref[...]) * x_ref[...].astype(F32)).astype(z_ref.dtype)


def _fft_stage3(c, gc, v5, vpart, x5, xpart, skip, group, dt, out_dtype):
    b, p, kr, d = c.shape
    qh = v5.shape[2]
    kc = gc.shape[-1]
    pspec = lambda part: pl.BlockSpec((None, None, qh, group, dt), lambda bi, g, cc: (part, bi, 0, g, cc))
    return pl.pallas_call(
        functools.partial(_s3_kernel, group=group, pad_rows=kc - kr),
        grid=(b, p // group, d // dt),
        in_specs=[pl.BlockSpec((None, group, kr, dt), lambda bi, g, cc: (bi, g, 0, cc)),
                  pl.BlockSpec((group, qh, kc), lambda bi, g, cc: (g, 0, 0)),
                  pspec(vpart), pspec(xpart),
                  pl.BlockSpec((1, dt), lambda bi, g, cc: (0, cc))],
        out_specs=pspec(0),
        out_shape=jax.ShapeDtypeStruct((1, b, qh, p, d), out_dtype),
        scratch_shapes=[pltpu.VMEM((group, qh, dt), F32)],
        compiler_params=_cparams("parallel", "parallel", "parallel"),
        name="fft_stage3",
    )(c, gc, v5, x5, skip)


def _long_conv_lat(v5, vpart, x5, xpart, hr, hi, skip, plan, order, out_dtype):
    d = v5.shape[-1]
    dt1 = min(1024, d)
    dt2 = min(512, d)
    ar, ai = _fft_stage1(v5, vpart, plan["f1s"], plan["ka_pad"], 16, dt1, BF16, True, False)
    c = _fft_stage2(ar, ai, hr, hi, plan["m2f1"], plan["m2i1"], plan["ka_used"], order * d, dt2)
    return _fft_stage3(c, plan["g"], v5, vpart, x5, xpart, skip, 16, dt1, out_dtype)


def _filter_spectrum_lat(hy, seq, plan_f):
    fw1, fb1, ffreq, fw2, fb2, fwout = hy
    q = plan_f["q"]
    h_time, nrm = _hyena_filter_time(fw1, fb1, ffreq, fw2, fb2, fwout, seq, True)
    od = h_time.shape[-1]
    h5 = h_time.reshape(1, 1, FFT_P, q, od)
    ar, ai = _fft_stage1(h5, 0, plan_f["f1"], plan_f["ka_pad"], 8, min(1024, od), F32, False, True)
    return _fft_stage2_filter(ar[0], ai[0], plan_f["m2f1"], 1.0 / nrm, min(1024, od))


def _dense_consts(seq):
    n = 2 * seq
    kf = seq + 1
    kf_pad = -(-kf // LANES) * LANES
    k = np.arange(kf)
    fwd_full = np.zeros((2 * kf_pad, n))
    ang = 2 * np.pi * np.outer(k, np.arange(n)) / n
    fwd_full[:kf] = np.cos(ang)
    fwd_full[kf_pad:kf_pad + kf] = -np.sin(ang)
    wgt = np.where((k == 0) | (k == seq), 1.0, 2.0) / n
    inv = np.zeros((seq, 2 * kf_pad))
    angi = 2 * np.pi * np.outer(np.arange(seq), k) / n
    inv[:, :kf] = wgt * np.cos(angi)
    inv[:, kf_pad:kf_pad + kf] = -wgt * np.sin(angi)
    return dict(kf_pad=kf_pad, fwd_full=_const3(fwd_full), fwd=_const3(fwd_full[:, :seq]), inv=_const3(inv))


def _dense_spec_kernel(h_ref, f_ref, inv_ref, hr_ref, hi_ref, *, kf_pad):
    y = _dot(f_ref[...], _stack3(h_ref[...])) * inv_ref[...]
    hr_ref[...] = y[:kf_pad]
    hi_ref[...] = y[kf_pad:]


def _filter_spectrum_ctx(hy, seq, cons):
    fw1, fb1, ffreq, fw2, fb2, fwout = hy
    h_time, nrm = _hyena_filter_time(fw1, fb1, ffreq, fw2, fb2, fwout, seq, False)
    od = h_time.shape[-1]
    n2 = 2 * seq
    h_time = h_time.reshape(n2, od)
    kf_pad = cons["kf_pad"]
    ct = min(512, od)
    out = jax.ShapeDtypeStruct((kf_pad, od), F32)
    ospec = pl.BlockSpec((kf_pad, ct), lambda c: (0, c))
    return pl.pallas_call(
        functools.partial(_dense_spec_kernel, kf_pad=kf_pad),
        grid=(od // ct,),
        in_specs=[pl.BlockSpec((n2, ct), lambda c: (0, c)),
                  pl.BlockSpec((2 * kf_pad, 3 * n2), lambda c: (0, 0)),
                  pl.BlockSpec((1, ct), lambda c: (0, c))],
        out_specs=[ospec, ospec],
        out_shape=[out, out],
        compiler_params=_cparams("parallel"),
        name="dense_filter_spectrum",
    )(h_time, cons["fwd_full"], 1.0 / nrm)


def _dense_conv_kernel(v_ref, x1_ref, x2_ref, h1r_ref, h1i_ref, h2r_ref, h2i_ref, sk_ref,
                       f_ref, g_ref, z_ref, *, kf_pad):
    def conv(u, hr, hi):
        s = _dot(f_ref[...], _stack3(u))
        sr, si = s[:kf_pad], s[kf_pad:]
        y = jnp.concatenate([sr * hr - si * hi, sr * hi + si * hr], axis=0)
        return _dot(g_ref[...], _stack3(y))

    v = v_ref[...]
    z1 = x1_ref[...] * (conv(v, h1r_ref[...], h1i_ref[...]) + v * sk_ref[0:1, :])
    z2 = x2_ref[...] * (conv(z1, h2r_ref[...], h2i_ref[...]) + z1 * sk_ref[1:2, :])
    z_ref[...] = z2.astype(z_ref.dtype)


def _hyena_core_ctx(u3, hr, hi, fskip, cons, b, seq):
    d = u3.shape[-1]
    dt = min(256, d)
    nd = d // dt
    kf_pad = cons["kf_pad"]
    uspec = lambda part: pl.BlockSpec((None, seq, dt), lambda bi, c: (part, bi, c))
    hspec = lambda order: pl.BlockSpec((kf_pad, dt), lambda bi, c: (0, order * nd + c))
    return pl.pallas_call(
        functools.partial(_dense_conv_kernel, kf_pad=kf_pad),
        grid=(b, nd),
        in_specs=[uspec(0), uspec(1), uspec(2), hspec(0), hspec(0), hspec(1), hspec(1),
                  pl.BlockSpec((HYENA_ORDER, dt), lambda bi, c: (0, c)),
                  pl.BlockSpec((2 * kf_pad, 3 * seq), lambda bi, c: (0, 0)),
                  pl.BlockSpec((seq, 6 * kf_pad), lambda bi, c: (0, 0))],
        out_specs=pl.BlockSpec((seq, dt), lambda bi, c: (bi, c)),
        out_shape=jax.ShapeDtypeStruct((b * seq, d), BF16),
        compiler_params=_cparams("parallel", "parallel"),
        name="hyena_core_ctx",
    )(u3, u3, u3, hr, hi, hr, hi, fskip, cons["fwd"], cons["inv"])


def _rec_consts(chunk):
    t = np.arange(chunk)
    coefs, masks = [], []
    for direction in (0, 1):
        if direction == 0:
            rows = [t[None, :] <= t[:, None], t[None, :] > t[:, None]]
        else:
            rows = [t[None, :] >= t[:, None], t[None, :] < t[:, None]]
        mk = [np.eye(chunk)]
        m = chunk // 2
        while m >= 1:
            blk = t // (2 * m)
            half = (t // m) % 2
            mid = blk * 2 * m + m
            e = np.zeros((chunk, chunk))
            for r in range(chunk):
                if direction == 0:
                    if half[r] == 1:
                        e[r, mid[r]:r + 1] = 1
                    else:
                        e[r, r + 1:mid[r]] = 1
                else:
                    if half[r] == 0:
                        e[r, r:mid[r]] = 1
                    else:
                        e[r, mid[r]:r] = 1
            same = blk[:, None] == blk[None, :]
            if direction == 0:
                mk.append(same & (half[:, None] == 1) & (half[None, :] == 0))
            else:
                mk.append(same & (half[:, None] == 0) & (half[None, :] == 1))
            rows.append(e)
            m //= 2
        rows.append(np.ones((16, chunk)))
        a = np.concatenate([np.asarray(r, np.float64) for r in rows], axis=0)
        a3 = np.concatenate([a, a, a], axis=1)
        pad = -(-a3.shape[1] // LANES) * LANES - a3.shape[1]
        a3 = np.pad(a3, ((0, 0), (0, pad)))
        coefs.append(a3)
        masks.append(np.stack([np.asarray(x, np.float32) for x in mk]))
    return (jnp.asarray(np.stack(coefs), dtype=BF16), jnp.asarray(np.stack(masks), dtype=F32))


def _rec_core(load_qkg, v_ref, coef_ref, mask_ref, s0_ref, o_ref, sfin_ref, q_s, k_s, st_s, ex_s, att_s,
              *, chunk, nsub, heads, dk, dv):
    di = pl.program_id(1)
    c = pl.program_id(2)
    levels = int(math.log2(chunk))
    tot = (2 + levels) * chunk

    @pl.when(c == 0)
    def _():
        st_s[...] = s0_ref[...]

    offs = [pl.multiple_of((s + di * (nsub - 1 - 2 * s)) * chunk, chunk) for s in range(nsub)]

    for s in range(nsub):
        q, k, g = load_qkg(pl.ds(offs[s], chunk))
        q_s[s] = q
        k_s[s] = k
        g1 = g.astype(BF16)
        r1 = g - g1.astype(F32)
        g2 = r1.astype(BF16)
        g3 = (r1 - g2.astype(F32)).astype(BF16)
        pad = coef_ref.shape[-1] - 3 * chunk
        gs = jnp.concatenate([g1, g2, g3, jnp.zeros((pad, g.shape[1]), BF16)], axis=0)
        ex_s[s] = jnp.exp(_dot(coef_ref[...], gs))

    for s in range(nsub):
        for h in range(heads):
            ks = slice(h * dk, (h + 1) * dk)
            qh = q_s[s, :, ks]
            kh = k_s[s, :, ks]
            att = mask_ref[0] * _dot_nt(qh.astype(BF16), kh.astype(BF16))
            for lv in range(levels):
                e = ex_s[s, (2 + lv) * chunk:(3 + lv) * chunk, ks]
                att += mask_ref[1 + lv] * _dot_nt((qh * e).astype(BF16), (kh * e).astype(BF16))
            att_s[s, h] = att.astype(BF16)

    for s in range(nsub):
        rows = pl.ds(offs[s], chunk)
        for h in range(heads):
            ks = slice(h * dk, (h + 1) * dk)
            vs = slice(h * dv, (h + 1) * dv)
            vh = v_ref[rows, vs].astype(BF16)
            st = st_s[h]
            o = _dot(att_s[s, h], vh)
            o += _dot_nt((q_s[s, :, ks] * ex_s[s, 0:chunk, ks]).astype(BF16), st.astype(BF16))
            o_ref[rows, vs] = o.astype(o_ref.dtype)
            kd = (k_s[s, :, ks] * ex_s[s, chunk:2 * chunk, ks]).astype(BF16)
            st_s[h] = st * ex_s[s, tot:tot + 1, ks] + _dot_tn(vh, kd)

    @pl.when(c == pl.num_programs(2) - 1)
    def _():
        sfin_ref[...] = st_s[...]


def _hgrn_rec_kernel(q_ref, v_ref, f_ref, lb_ref, coef_ref, mask_ref, s0_ref, o_ref, sfin_ref,
                     q_s, k_s, st_s, ex_s, att_s, **kw):
    lb = lb_ref[...]

    def load_qkg(rows):
        qr = q_ref[rows, :]
        sg = _sigmoid(f_ref[rows, :])
        return qr * _sigmoid(qr), (1.0 - lb) * (1.0 - sg), jnp.log(lb + (1.0 - lb) * sg)

    _rec_core(load_qkg, v_ref, coef_ref, mask_ref, s0_ref, o_ref, sfin_ref, q_s, k_s, st_s, ex_s, att_s, **kw)


def _gla_rec_kernel(q_ref, k_ref, v_ref, a_ref, wup_ref, bup_ref, coef_ref, mask_ref, s0_ref,
                    o_ref, sfin_ref, q_s, k_s, st_s, ex_s, att_s, *, qscale, **kw):
    def load_qkg(rows):
        xg = _dot(a_ref[rows, :].astype(BF16), wup_ref[...]) + bup_ref[...]
        g = (jnp.minimum(xg, 0.0) - jnp.log(1.0 + jnp.exp(-jnp.abs(xg)))) * (1.0 / GLA_GATE_NORM)
        return q_ref[rows, :] * qscale, k_ref[rows, :], g

    _rec_core(load_qkg, v_ref, coef_ref, mask_ref, s0_ref, o_ref, sfin_ref, q_s, k_s, st_s, ex_s, att_s, **kw)


def _rec_call(kind, proj, extra, s0, consts, b, seq, d, heads, dk, dv):
    chunk = REC_CHUNK
    nsub = REC_SUBCHUNKS
    rows = chunk * nsub
    nc = seq // rows
    coef, masks = consts
    hk = heads * dk
    hv = heads * dv
    levels = int(math.log2(chunk))
    n_rows = (2 + levels) * chunk + 16

    def rmap(col):
        return lambda bi, di, c: (bi * nc + c + di * (nc - 1 - 2 * c), col)

    cspecs = [pl.BlockSpec((None, n_rows, coef.shape[-1]), lambda bi, di, c: (di, 0, 0)),
              pl.BlockSpec((None, levels + 1, chunk, chunk), lambda bi, di, c: (di, 0, 0, 0)),
              pl.BlockSpec((None, None, heads, dv, dk), lambda bi, di, c: (bi, di, 0, 0, 0))]
    if kind == "hgrn":
        lb = extra
        kernel = _hgrn_rec_kernel
        in_specs = [pl.BlockSpec((rows, d), rmap(0)),
                    pl.BlockSpec((rows, d), rmap(1)),
                    pl.BlockSpec((rows, d), lambda bi, di, c: (bi * nc + c + di * (nc - 1 - 2 * c), 3 + di)),
                    pl.BlockSpec((None, 1, d), lambda bi, di, c: (di, 0, 0))] + cspecs
        args = (proj, proj, proj, lb, coef, masks, s0)
        kw = {}
    else:
        wup, bup = extra
        kernel = _gla_rec_kernel
        in_specs = [pl.BlockSpec((rows, hk), rmap(0)),
                    pl.BlockSpec((rows, hk), rmap(1)),
                    pl.BlockSpec((rows, hv), rmap(1)),
                    pl.BlockSpec((rows, LANES), rmap((2 * hk + 2 * hv) // LANES)),
                    pl.BlockSpec((None, LANES, hk), lambda bi, di, c: (di, 0, 0)),
                    pl.BlockSpec((None, 1, hk), lambda bi, di, c: (di, 0, 0))] + cspecs
        args = (proj, proj, proj, proj, wup, bup, coef, masks, s0)
        kw = dict(qscale=float(dk) ** -0.5)
    return pl.pallas_call(
        functools.partial(kernel, chunk=chunk, nsub=nsub, heads=heads, dk=dk, dv=dv, **kw),
        grid=(b, 2, nc),
        in_specs=in_specs,
        out_specs=[pl.BlockSpec((None, rows, hv), lambda bi, di, c: (di, bi * nc + c + di * (nc - 1 - 2 * c), 0)),
                   pl.BlockSpec((None, None, heads, dv, dk), lambda bi, di, c: (bi, di, 0, 0, 0))],
        out_shape=[jax.ShapeDtypeStruct((2, b * seq, hv), BF16),
                   jax.ShapeDtypeStruct((b, 2, heads, dv, dk), F32)],
        scratch_shapes=[pltpu.VMEM((nsub, chunk, hk), F32), pltpu.VMEM((nsub, chunk, hk), F32),
                        pltpu.VMEM((heads, dv, dk), F32), pltpu.VMEM((nsub, n_rows, hk), F32),
                        pltpu.VMEM((nsub, heads, chunk, chunk), BF16)],
        compiler_params=_cparams("parallel", "parallel", "arbitrary"),
        name=kind + "_recurrence",
    )(*args)


def _gated_out_kernel(o_ref, gate_ref, gn_ref, w_ref, x_ref, mg_ref, out_ref, *, heads):
    d = x_ref.shape[1]
    dh = d // heads
    kc = max(dh, 2 * MXU_N)
    acc = None
    for k0 in range(0, d, kc):
        parts = []
        for h0 in range(k0, k0 + kc, dh):
            seg = o_ref[0, :, h0:h0 + dh].astype(F32) + o_ref[1, :, h0:h0 + dh].astype(F32)
            ms = jnp.mean(seg * seg, axis=-1, keepdims=True)
            parts.append(seg * lax.rsqrt(ms + RMS_EPS))
        gate = gate_ref[:, k0:k0 + kc]
        y = jnp.concatenate(parts, axis=1) * gn_ref[:, k0:k0 + kc] * (gate * _sigmoid(gate))
        part = _dot(y.astype(BF16), w_ref[k0:k0 + kc, :])
        acc = part if acc is None else acc + part
    out_ref[...] = x_ref[...] + mg_ref[...] * acc


def _gated_out(o2, proj, gate_col, gn, w, x, mg, bmap, heads, tm):
    m, d = x.shape
    return pl.pallas_call(
        functools.partial(_gated_out_kernel, heads=heads),
        grid=(m // tm,),
        in_specs=[pl.BlockSpec((2, tm, d), lambda i: (0, i, 0)),
                  pl.BlockSpec((tm, d), lambda i: (i, gate_col)),
                  pl.BlockSpec((1, d), lambda i: (0, 0)),
                  pl.BlockSpec((d, d), lambda i: (0, 0)),
                  pl.BlockSpec((tm, d), lambda i: (i, 0)),
                  _mod_spec(d, bmap)],
        out_specs=pl.BlockSpec((tm, d), lambda i: (i, 0)),
        out_shape=jax.ShapeDtypeStruct((m, d), F32),
        compiler_params=_cparams("parallel"),
        name="gated_out",
    )(o2, proj, gn, w, x, mg)


def _recurrent_mix(kind, x_ctx, x_lat, norm, ctx_map, w_in, extra, consts, b, lc, seq, d, heads, dk, dv, tn):
    g1, sh, sc = norm
    tm_c, tm_l = min(1024, x_ctx.shape[0]), min(1024, x_lat.shape[0])
    proj_ctx = _projection(x_ctx, g1, sh, sc, w_in, ctx_map, tm_c, tn)
    proj_lat = _projection(x_lat, g1, sh, sc, w_in, lambda i: (i * tm_l) // seq, tm_l, tn)
    s0 = jnp.zeros((b, 2, heads, dv, dk), F32)
    o_ctx, s_ctx = _rec_call(kind, proj_ctx, extra, s0, consts, b, lc, d, heads, dk, dv)
    o_lat, _ = _rec_call(kind, proj_lat, extra, s_ctx, consts, b, seq, d, heads, dk, dv)
    return (proj_ctx, o_ctx), (proj_lat, o_lat)


def kernel(x, c, ctx, c_ctx, w_mod, b_mod, norm1_g, norm2_g, w_ffn_in, w_ffn_out, final_g, hy_w_in, hy_conv_w, hy_fw1, hy_fb1, hy_ffreq, hy_fw2, hy_fb2, hy_fwout, hy_fskip, hy_w_out, hg_w_in, hg_lb_logits, hg_onorm_g, hg_w_out, gla_w_in, gla_w_up, gla_b_up, gla_onorm_g, gla_w_out):
    b, seq, d = x.shape
    lc = ctx.shape[1]
    depth = w_mod.shape[0]
    assert b + 1 <= MOD_ROWS and seq % (FFT_P * 2) == 0 and FFT_P % GRID_W == 0
    m_lat, m_ctx = b * seq, b * lc
    ctx_row = b
    tm_lat, tm_ctx = 512, min(512, m_ctx)
    lat_map = lambda i, tm=tm_lat: (i * tm) // seq
    ctx_map = lambda i: ctx_row

    xl = x.reshape(m_lat, d)
    xc = ctx.reshape(m_ctx, d)
    c8 = jnp.zeros((MOD_ROWS, d), F32).at[:b].set(c).at[ctx_row].set(c_ctx)
    mod = _modulation(c8, w_mod, b_mod).reshape(depth, MOD_ROWS, N_MOD, 1, d)

    plan = _fft_consts(seq, seq // FFT_P)
    plan_f = _fft_consts(seq, 2 * seq // FFT_P)
    dense = _dense_consts(lc)
    rec_consts = _rec_consts(REC_CHUNK)
    qh = plan["qh"]

    wb_ffn_in, wb_ffn_out = _to_bf16(w_ffn_in), _to_bf16(w_ffn_out)
    wb_hy_in, wb_hy_out = _to_bf16(hy_w_in), _to_bf16(hy_w_out)
    wb_hg_in, wb_hg_out = _to_bf16(hg_w_in), _to_bf16(hg_w_out)
    wb_gla_in, wb_gla_out = _to_bf16(gla_w_in), _to_bf16(gla_w_out)

    for i in range(depth):
        last = i == depth - 1
        kind, j = i % N_MIXERS, i // N_MIXERS
        mv = [mod[i, :, k] for k in range(N_MOD)]
        g1 = norm1_g[i][None, :]
        need_ctx = (not last) or kind != 0
        if kind == 0:
            w_in, w_out = wb_hy_in[j], wb_hy_out[j]
            hy = (hy_fw1[j], hy_fb1[j], hy_ffreq[j], hy_fw2[j], hy_fb2[j], hy_fwout[j])
            tm_h = 1024
            u = _hyena_in(xl, g1, mv[0], mv[1], w_in, hy_conv_w[j], lambda i: (i * tm_h) // seq,
                          tm_h, min(1024, d), GRID_W, BF16).reshape(3, b, qh, FFT_P, d)
            hr, hi = _filter_spectrum_lat(hy, seq, plan_f)
            z1 = _long_conv_lat(u, 0, u, 1, hr, hi, hy_fskip[j, 0][None, :], plan, 0, BF16)
            z2 = _long_conv_lat(z1, 0, u, 2, hr, hi, hy_fskip[j, 1][None, :], plan, 1, BF16)
            xl = _out_res(z2.reshape(m_lat, d), w_out, xl, mv[2], lat_map, tm_lat)
            if need_ctx:
                uc = _hyena_in(xc, g1, mv[0], mv[1], w_in, hy_conv_w[j], ctx_map, lc, min(1024, d), lc, F32)
                hcr, hci = _filter_spectrum_ctx(hy, lc, dense)
                zc = _hyena_core_ctx(uc, hcr, hci, hy_fskip[j], dense, b, lc)
                xc = _out_res(zc, w_out, xc, mv[2], ctx_map, tm_ctx)
        else:
            if kind == 1:
                heads = d // HGRN_EXPAND
                dk = dv = HGRN_EXPAND
                w_in = wb_hg_in[j]
                lb_cum = jnp.cumsum(jax.nn.softmax(hg_lb_logits.astype(F32), axis=1), axis=1)
                extra = (lb_cum[:, i] - lb_cum[:, 0])[:, None, :]
                gn, w_out, tn = hg_onorm_g[j], wb_hg_out[j], 1024
                rkind = "hgrn"
            else:
                heads = GLA_HEADS
                dk, dv = d // 2 // heads, d // heads
                n_in = gla_w_in.shape[-1]
                n_pad = -(-n_in // LANES) * LANES
                w_in = jnp.pad(wb_gla_in[j], ((0, 0), (0, n_pad - n_in)))
                r = GLA_GATE_RANK
                wup = jnp.zeros((2, LANES, heads * dk), F32)
                wup = wup.at[0, :r].set(gla_w_up[j, 0]).at[1, r:2 * r].set(gla_w_up[j, 1]).astype(BF16)
                extra = (wup, gla_b_up[j][:, None, :])
                gn, w_out = gla_onorm_g[j], wb_gla_out[j]
                tn = _lane_tile(n_pad, 1024)
                rkind = "gla"
            (p_ctx, o_ctx), (p_lat, o_lat) = _recurrent_mix(
                rkind, xc, xl, (g1, mv[0], mv[1]), ctx_map, w_in, extra, rec_consts,
                b, lc, seq, d, heads, dk, dv, tn)
            gate_col = 2 if kind == 1 else (2 * heads * dk + heads * dv) // d
            tm_g = 512
            xl = _gated_out(o_lat, p_lat, gate_col, gn[None, :], w_out, xl, mv[2],
                            lambda i: (i * tm_g) // seq, heads, tm_g)
            xc = _gated_out(o_ctx, p_ctx, gate_col, gn[None, :], w_out, xc, mv[2], ctx_map, heads, tm_g)
        g2 = norm2_g[i][None, :]
        fg = final_g[None, :]
        xl = _ffn(xl, g2, mv[3], mv[4], mv[5], wb_ffn_in, wb_ffn_out, i, fg, last, lat_map, tm_lat, 512)
        if not last:
            xc = _ffn(xc, g2, mv[3], mv[4], mv[5], wb_ffn_in, wb_ffn_out, i, fg, False, ctx_map, tm_ctx, 512)
    return xl.reshape(b, seq, d)
```

```python
import functools
import math

import ml_dtypes
import numpy as np
import jax
import jax.numpy as jnp
from jax import lax
from jax.experimental import pallas as pl
from jax.experimental.pallas import tpu as pltpu

F32 = jnp.float32
BF16 = jnp.bfloat16

N_MOD = 6
N_MIXERS = 3
RMS_EPS = 1e-6
GRID_W = 64
HYENA_ORDER = 2
FILTER_BANDS = 16
FILTER_EMB = 1 + 2 * FILTER_BANDS
HYENA_DECAY_MIN = math.log(1e-2) / 1.5
HYENA_DECAY_MAX = math.log(1e-2) / 0.3
HGRN_EXPAND = 128
GLA_HEADS = 4
GLA_GATE_RANK = 16
GLA_GATE_NORM = 16.0

LANES = 128
SUBLANES = 8
MXU_N = 256
V7X_VMEM_LIMIT = 56 * 1024 * 1024

FFT_P = 128
FFT_GROUP = 2 * SUBLANES
FFT_DT_OUTER = 1024
FFT_DT_INNER = 512
REC_STEP_ROWS = 256
HGRN_CHUNK = 64
GLA_CHUNK = 128
REC_GATE_TERMS = 2
REC_TOTAL_ROWS = 2 * SUBLANES
MOD_ROWS = 8

TM_ROWS = 512
TM_PROJ = 1024
TN_PROJ = 1024
TN_HYENA_IN = 1024
TF_FFN = 512
TN_MOD = 1024


def _cparams(*sem):
    return pltpu.CompilerParams(dimension_semantics=sem, vmem_limit_bytes=V7X_VMEM_LIMIT)


def _dot(a, b):
    return jnp.dot(a, b, preferred_element_type=F32)


def _dot_nt(a, b):
    return lax.dot_general(a, b, (((1,), (1,)), ((), ())), preferred_element_type=F32)


def _dot_tn(a, b):
    return lax.dot_general(a, b, (((0,), (0,)), ((), ())), preferred_element_type=F32)


def _split(x):
    hi = x.astype(BF16)
    lo = (x - hi.astype(F32)).astype(BF16)
    return hi, lo


def _stack3(x, pad_rows=0):
    hi, lo = _split(x)
    parts = [hi, lo, hi]
    if pad_rows:
        parts.append(jnp.zeros((pad_rows, x.shape[1]), BF16))
    return jnp.concatenate(parts, axis=0)


def _const3(c, pad_cols=0):
    hi = c.astype(ml_dtypes.bfloat16)
    lo = (c - hi.astype(np.float64)).astype(ml_dtypes.bfloat16)
    parts = [hi, hi, lo]
    if pad_cols:
        parts.append(np.zeros(c.shape[:-1] + (pad_cols,), ml_dtypes.bfloat16))
    return jnp.asarray(np.concatenate(parts, axis=-1))


def _dot3(a, bh, bl):
    ah, al = _split(a)
    return _dot(ah, bh) + _dot(ah, bl) + _dot(al, bh)


def _sigmoid(x):
    return jax.nn.sigmoid(x)


def _normmod(x, g, sh, sc):
    ms = jnp.mean(x * x, axis=-1, keepdims=True)
    y = x * lax.rsqrt(ms + RMS_EPS) * g
    return y * (1.0 + sc) + sh


def _mod_kernel(c_ref, w_ref, b_ref, o_ref):
    c = c_ref[...]
    s = (c * _sigmoid(c)).astype(BF16)
    o_ref[...] = _dot(s, w_ref[...].astype(BF16)) + b_ref[...]


def _modulation(c8, w_mod, b_mod):
    depth, d, n = w_mod.shape
    tn = TN_MOD
    return pl.pallas_call(
        _mod_kernel,
        grid=(depth, n // tn),
        in_specs=[pl.BlockSpec((MOD_ROWS, d), lambda l, j: (0, 0)),
                  pl.BlockSpec((None, d, tn), lambda l, j: (l, 0, j)),
                  pl.BlockSpec((None, 1, tn), lambda l, j: (l, 0, j))],
        out_specs=pl.BlockSpec((None, MOD_ROWS, tn), lambda l, j: (l, 0, j)),
        out_shape=jax.ShapeDtypeStruct((depth, MOD_ROWS, n), F32),
        compiler_params=_cparams("parallel", "parallel"),
        name="modulation",
    )(c8, w_mod, b_mod.reshape(depth, 1, n))


def _mod_spec(d, bmap):
    return pl.BlockSpec((None, 1, d), lambda *idx: (bmap(idx[0]), 0, 0))


def _lane_tile(n, cap):
    best = n
    for t in range(LANES, min(n, cap) + 1, LANES):
        if n % t == 0:
            best = t
    return best


def _cast_kernel(x_ref, o_ref):
    o_ref[...] = x_ref[...].astype(o_ref.dtype)


def _to_bf16(w):
    shape = w.shape
    w2 = w.reshape(-1, shape[-1])
    r, c = w2.shape
    tr, tc = min(TM_ROWS, r), _lane_tile(c, 2 * TN_PROJ)
    out = pl.pallas_call(
        _cast_kernel,
        grid=(r // tr, c // tc),
        in_specs=[pl.BlockSpec((tr, tc), lambda i, j: (i, j))],
        out_specs=pl.BlockSpec((tr, tc), lambda i, j: (i, j)),
        out_shape=jax.ShapeDtypeStruct((r, c), BF16),
        compiler_params=_cparams("parallel", "parallel"),
        name="cast_bf16",
    )(w2)
    return out.reshape(shape)


def _proj_kernel(x_ref, g_ref, sh_ref, sc_ref, w_ref, o_ref, h_s):
    @pl.when(pl.program_id(1) == 0)
    def _():
        h_s[...] = _normmod(x_ref[...], g_ref[...], sh_ref[...], sc_ref[...]).astype(BF16)

    o_ref[...] = _dot(h_s[...], w_ref[...])


def _projection(x, g, sh, sc, w, bmap, tm, tn):
    m, d = x.shape
    n = w.shape[1]
    return pl.pallas_call(
        _proj_kernel,
        grid=(m // tm, n // tn),
        in_specs=[pl.BlockSpec((tm, d), lambda i, j: (i, 0)),
                  pl.BlockSpec((1, d), lambda i, j: (0, 0)),
                  _mod_spec(d, bmap), _mod_spec(d, bmap),
                  pl.BlockSpec((d, tn), lambda i, j: (0, j))],
        out_specs=pl.BlockSpec((tm, tn), lambda i, j: (i, j)),
        out_shape=jax.ShapeDtypeStruct((m, n), F32),
        scratch_shapes=[pltpu.VMEM((tm, d), BF16)],
        compiler_params=_cparams("parallel", "arbitrary"),
        name="projection",
    )(x, g, sh, sc, w)


def _out_res_kernel(a_ref, w_ref, x_ref, mg_ref, o_ref):
    o_ref[...] = x_ref[...] + mg_ref[...] * _dot(a_ref[...], w_ref[...])


def _ffn_kernel(x_ref, g_ref, sh_ref, sc_ref, mg_ref, wg_ref, wu_ref, wo_ref, fg_ref, o_ref, h_s, acc_s,
                *, final_norm):
    j = pl.program_id(1)

    @pl.when(j == 0)
    def _():
        h_s[...] = _normmod(x_ref[...], g_ref[...], sh_ref[...], sc_ref[...]).astype(BF16)
        acc_s[...] = jnp.zeros_like(acc_s)

    h = h_s[...]
    a = _dot(h, wg_ref[...])
    u = _dot(h, wu_ref[...])
    act = (a * _sigmoid(a) * u).astype(BF16)
    acc_s[...] += _dot(act, wo_ref[...])

    @pl.when(j == pl.num_programs(1) - 1)
    def _():
        y = x_ref[...] + mg_ref[...] * acc_s[...]
        if final_norm:
            y = y * lax.rsqrt(jnp.mean(y * y, axis=-1, keepdims=True) + RMS_EPS) * fg_ref[...]
        o_ref[...] = y


def _ffn(x, g, sh, sc, mg, w_in, w_out, layer, final_g, final_norm, bmap, tm, tf):
    m, d = x.shape
    f = w_out.shape[1]
    nf = f // tf
    return pl.pallas_call(
        functools.partial(_ffn_kernel, final_norm=final_norm),
        grid=(m // tm, nf),
        in_specs=[pl.BlockSpec((tm, d), lambda i, j: (i, 0)),
                  pl.BlockSpec((1, d), lambda i, j: (0, 0)),
                  _mod_spec(d, bmap), _mod_spec(d, bmap), _mod_spec(d, bmap),
                  pl.BlockSpec((None, d, tf), lambda i, j: (layer, 0, j)),
                  pl.BlockSpec((None, d, tf), lambda i, j: (layer, 0, nf + j)),
                  pl.BlockSpec((None, tf, d), lambda i, j: (layer, j, 0)),
                  pl.BlockSpec((1, d), lambda i, j: (0, 0))],
        out_specs=pl.BlockSpec((tm, d), lambda i, j: (i, 0)),
        out_shape=jax.ShapeDtypeStruct((m, d), F32),
        scratch_shapes=[pltpu.VMEM((tm, d), BF16), pltpu.VMEM((tm, d), F32)],
        compiler_params=_cparams("parallel", "arbitrary"),
        name="ffn",
    )(x, g, sh, sc, mg, w_in, w_in, w_out, final_g)


def _conv3_rows(acc, cw, row_len):
    tm = acc.shape[0]
    rid = lax.broadcasted_iota(jnp.int32, (tm, 1), 0) % row_len
    up = jnp.where(rid == 0, 0.0, pltpu.roll(acc, 1, 0))
    dn = jnp.where(rid == row_len - 1, 0.0, pltpu.roll(acc, tm - 1, 0))
    return cw[0:1] * up + cw[1:2] * acc + cw[2:3] * dn


def _hy_in_kernel(x_ref, g_ref, sh_ref, sc_ref, w_ref, cw_ref, o_ref, h_s, *, row_len):
    @pl.when(pl.program_id(1) == 0)
    def _():
        h_s[...] = _normmod(x_ref[...], g_ref[...], sh_ref[...], sc_ref[...]).astype(BF16)

    h = h_s[...]
    for n0 in range(0, o_ref.shape[1], MXU_N):
        ns = slice(n0, n0 + MXU_N)
        o_ref[:, ns] = _conv3_rows(_dot(h, w_ref[:, ns]), cw_ref[:, ns], row_len).astype(o_ref.dtype)


def _hyena_in(x, g, sh, sc, w, cw, bmap, tm, tn, row_len, out_dtype):
    m, d = x.shape
    nd = d // tn
    return pl.pallas_call(
        functools.partial(_hy_in_kernel, row_len=row_len),
        grid=(m // tm, 3 * nd),
        in_specs=[pl.BlockSpec((tm, d), lambda i, j: (i, 0)),
                  pl.BlockSpec((1, d), lambda i, j: (0, 0)),
                  _mod_spec(d, bmap), _mod_spec(d, bmap),
                  pl.BlockSpec((d, tn), lambda i, j: (0, j)),
                  pl.BlockSpec((3, tn), lambda i, j: (0, j))],
        out_specs=pl.BlockSpec((None, tm, tn), lambda i, j: (j // nd, i, j % nd)),
        out_shape=jax.ShapeDtypeStruct((3, m, d), out_dtype),
        scratch_shapes=[pltpu.VMEM((tm, d), BF16)],
        compiler_params=_cparams("parallel", "arbitrary"),
        name="hyena_in",
    )(x, g, sh, sc, w, cw)


def _out_res(a, w, x, mg, bmap, tm):
    m, d = x.shape
    return pl.pallas_call(
        _out_res_kernel,
        grid=(m // tm,),
        in_specs=[pl.BlockSpec((tm, d), lambda i: (i, 0)),
                  pl.BlockSpec((d, d), lambda i: (0, 0)),
                  pl.BlockSpec((tm, d), lambda i: (i, 0)),
                  _mod_spec(d, bmap)],
        out_specs=pl.BlockSpec((tm, d), lambda i: (i, 0)),
        out_shape=jax.ShapeDtypeStruct((m, d), F32),
        compiler_params=_cparams("parallel"),
        name="out_residual",
    )(a, w, x, mg)


def _filter_kernel(z_ref, w1h_ref, w1l_ref, b1_ref, fq_ref, w2h_ref, w2l_ref, b2_ref,
                   wo_ref, dl_ref, h_ref, nrm_ref):
    z = z_ref[...]
    t = z[:, 0:1]
    valid = z[:, FILTER_EMB:FILTER_EMB + 1]
    a1 = _dot3(z, w1h_ref[...], w1l_ref[...]) + b1_ref[...]
    hid = jnp.sin(fq_ref[0:1, :] * a1)
    a2 = _dot3(hid, w2h_ref[...], w2l_ref[...]) + b2_ref[...]
    hid = jnp.sin(fq_ref[1:2, :] * a2)
    hh, hl = _split(hid)
    h = _dot(jnp.concatenate([hh, hl], axis=1), wo_ref[...]) * jnp.exp(-t * dl_ref[...]) * valid
    h_ref[...] = h.reshape(h_ref.shape)

    @pl.when((pl.program_id(1) == 0) & (pl.program_id(2) == 0))
    def _():
        nrm_ref[...] = jnp.zeros_like(nrm_ref)

    nrm_ref[...] += jnp.sum(jnp.abs(h), axis=0, keepdims=True)


def _pad2(a, rows, cols):
    return jnp.pad(a, ((0, rows - a.shape[0]), (0, cols - a.shape[1])))


def _filter_positions(seq, perm):
    if perm:
        r_hi = seq // FFT_P
        gt, ng = 2 * SUBLANES, FFT_P // (2 * SUBLANES)
        dirs, g, tl, th = np.meshgrid(np.arange(2), np.arange(ng), np.arange(gt), np.arange(r_hi), indexing="ij")
        n = FFT_P * (r_hi * dirs + th) + g * gt + tl
    else:
        r_hi, gt, ng = seq, 1, 1
        dirs, g, tl, th = np.meshgrid(np.arange(2), np.arange(1), np.arange(1), np.arange(seq), indexing="ij")
        n = seq * dirs + th
    n = n.reshape(-1)
    pos = np.where(n < seq, n, 2 * seq - 1 - n).astype(np.float32)
    z = np.zeros((n.size, LANES), np.float32)
    z[:, 0] = pos / np.float32(max(seq - 1, 1))
    bands = np.arange(1, FILTER_BANDS + 1, dtype=np.float32)
    ang = (np.float32(2.0 * math.pi / seq) * pos)[:, None] * bands[None, :]
    z[:, 1:1 + FILTER_BANDS] = np.cos(ang.astype(np.float64))
    z[:, 1 + FILTER_BANDS:FILTER_EMB] = -np.sin(ang.astype(np.float64))
    z[:, FILTER_EMB] = n != seq
    return jnp.asarray(z), gt, ng, r_hi


def _hyena_filter_time(fw1, fb1, ffreq, fw2, fb2, fwout, seq, perm):
    od = fwout.shape[1] // 2
    d = od // HYENA_ORDER
    z, gt, ng, r_hi = _filter_positions(seq, perm)
    rows = gt * r_hi
    ct = min(2 * FFT_DT_OUTER, od)
    hidden = LANES
    w1h, w1l = _split(_pad2(fw1, LANES, hidden))
    w2h, w2l = _split(_pad2(fw2, hidden, hidden))
    wo = _pad2(fwout, hidden, fwout.shape[1]).astype(BF16)
    wo2 = jnp.concatenate([wo, wo], axis=0)
    b1 = _pad2(fb1[None, :], 1, hidden)
    b2 = _pad2(fb2[None, :], 1, hidden)
    fq = _pad2(ffreq, 2, hidden)
    deltas = np.abs(np.linspace(HYENA_DECAY_MIN, HYENA_DECAY_MAX, d, dtype=np.float32))
    dl = jnp.asarray(np.tile(deltas, HYENA_ORDER)[None, :])
    nct = od // ct
    small = lambda shape: pl.BlockSpec(shape, lambda c, di, g: (0, 0))
    wspec = pl.BlockSpec((2 * hidden, ct), lambda c, di, g: (0, di * nct + c))
    return pl.pallas_call(
        _filter_kernel,
        grid=(nct, 2, ng),
        in_specs=[pl.BlockSpec((rows, LANES), lambda c, di, g: (di * ng + g, 0)),
                  small((LANES, hidden)), small((LANES, hidden)), small((1, hidden)), small((2, hidden)),
                  small((hidden, hidden)), small((hidden, hidden)), small((1, hidden)),
                  wspec,
                  pl.BlockSpec((1, ct), lambda c, di, g: (0, c))],
        out_specs=[pl.BlockSpec((None, gt, None, r_hi, ct), lambda c, di, g: (g, 0, di, 0, c)),
                   pl.BlockSpec((1, ct), lambda c, di, g: (0, c))],
        out_shape=[jax.ShapeDtypeStruct((ng, gt, 2, r_hi, od), F32), jax.ShapeDtypeStruct((1, od), F32)],
        compiler_params=_cparams("parallel", "arbitrary", "arbitrary"),
        name="hyena_filter",
    )(z, w1h, w1l, b1, fq, w2h, w2l, b2, wo2, dl)


def _fft_consts(seq, k_in):
    n = 2 * seq
    p = FFT_P
    q = n // p
    qh = q // 2
    ka_used = qh + 1
    ka_pad = -(-ka_used // SUBLANES) * SUBLANES
    ka = np.arange(ka_used)
    tl = np.arange(p)
    th = np.arange(k_in)
    theta = 2 * np.pi * (ka[None, :, None] * th[None, None, :] / q + ka[None, :, None] * tl[:, None, None] / n)
    f1 = np.zeros((p, 2 * ka_pad, k_in))
    f1[:, :ka_used] = np.cos(theta)
    f1[:, ka_pad:ka_pad + ka_used] = -np.sin(theta)
    k3 = 3 * k_in
    f1c = _const3(f1, pad_cols=-(-k3 // LANES) * LANES - k3)
    f1s = jnp.asarray(np.pad(f1, ((0, 0), (0, 0), (0, -(-k_in // LANES) * LANES - k_in))).astype(ml_dtypes.bfloat16))
    ang = 2 * np.pi * np.outer(np.arange(p), np.arange(p)) / p
    cc, sc = np.cos(ang), np.sin(ang)
    m2f_np, m2i_np = np.block([[cc, sc], [-sc, cc]]), np.block([[cc, -sc], [sc, cc]])
    m2f1, m2i1 = (jnp.asarray(m.astype(ml_dtypes.bfloat16)) for m in (m2f_np, m2i_np))
    tho = np.arange(qh)
    phi = 2 * np.pi * (tho[None, :, None] * ka[None, None, :] / q + ka[None, None, :] * tl[:, None, None] / n)
    wgt = np.where((ka == 0) | (ka == qh), 1.0, 2.0) / n
    kr = 2 * ka_pad
    g = np.zeros((p, qh, -(-kr // LANES) * LANES))
    col = 2 * SUBLANES * (ka // SUBLANES) + ka % SUBLANES
    g[:, :, col] = wgt * np.cos(phi)
    g[:, :, col + SUBLANES] = -wgt * np.sin(phi)
    gc = jnp.asarray(g.astype(ml_dtypes.bfloat16))
    return dict(q=q, qh=qh, ka_used=ka_used, ka_pad=ka_pad, f1=f1c, f1s=f1s, m2f1=m2f1, m2i1=m2i1, g=gc)


def _s1_kernel(x_ref, f_ref, ar_ref, ai_ref, a_s, *, group, ka_pad, pad_rows, natural, split):
    x = x_ref[...].astype(F32)
    if natural:
        x = jnp.swapaxes(x, 0, 1)
    zpad = None if split else jnp.zeros((pad_rows, x.shape[-1]), BF16)
    for j in range(group):
        xx = _stack3(x[j], pad_rows) if split else jnp.concatenate([x[j].astype(BF16), zpad], axis=0)
        a_s[j] = _dot(f_ref[j], xx)
    a = jnp.swapaxes(a_s[...], 0, 1)
    ar_ref[...] = a[:ka_pad].astype(ar_ref.dtype)
    ai_ref[...] = a[ka_pad:].astype(ai_ref.dtype)


def _fft_stage1(x5, part, f1c, ka_pad, group, dt, out_dtype, natural, split):
    if natural:
        _, b, k_in, p, d = x5.shape
        xspec = pl.BlockSpec((None, None, k_in, group, dt), lambda bi, g, c: (part, bi, 0, g, c))
    else:
        _, b, p, k_in, d = x5.shape
        xspec = pl.BlockSpec((None, None, group, k_in, dt), lambda bi, g, c: (part, bi, g, 0, c))
    kc = f1c.shape[-1]
    out = jax.ShapeDtypeStruct((b, ka_pad, p, d), out_dtype)
    ospec = pl.BlockSpec((None, ka_pad, group, dt), lambda bi, g, c: (bi, 0, g, c))
    pad_rows = kc - (3 if split else 1) * k_in
    return pl.pallas_call(
        functools.partial(_s1_kernel, group=group, ka_pad=ka_pad, pad_rows=pad_rows, natural=natural, split=split),
        grid=(b, p // group, d // dt),
        in_specs=[xspec,
                  pl.BlockSpec((group, 2 * ka_pad, kc), lambda bi, g, c: (g, 0, 0))],
        out_specs=[ospec, ospec],
        out_shape=[out, out],
        scratch_shapes=[pltpu.VMEM((group, 2 * ka_pad, dt), F32)],
        compiler_params=_cparams("parallel", "parallel", "parallel"),
        name="fft_stage1",
    )(x5, f1c)


def _s2_filter_kernel(ar_ref, ai_ref, m_ref, inv_ref, hr_ref, hi_ref, *, p):
    for i in range(ar_ref.shape[0]):
        x = jnp.concatenate([ar_ref[i], ai_ref[i]], axis=0)
        y = _dot(m_ref[...], x.astype(BF16)) * inv_ref[...]
        hr_ref[i] = y[:p]
        hi_ref[i] = y[p:]


def _fft_stage2_filter(ar, ai, m2f, inv_nrm, dt):
    ka, p, d = ar.shape
    spec = pl.BlockSpec((SUBLANES, p, dt), lambda k, c: (k, 0, c))
    out = jax.ShapeDtypeStruct((ka, p, d), F32)
    return pl.pallas_call(
        functools.partial(_s2_filter_kernel, p=p),
        grid=(ka // SUBLANES, d // dt),
        in_specs=[spec, spec,
                  pl.BlockSpec((2 * p, 2 * p), lambda k, c: (0, 0)),
                  pl.BlockSpec((1, dt), lambda k, c: (0, c))],
        out_specs=[spec, spec],
        out_shape=[out, out],
        compiler_params=_cparams("parallel", "parallel"),
        name="fft_stage2_filter",
    )(ar, ai, m2f, inv_nrm)


def _s2_kernel(ar_ref, ai_ref, hr_ref, hi_ref, mf_ref, mi_ref, c_ref, x_s, y_s, c_s, *, p, ka_used):
    kg = pl.program_id(0)
    assert ka_used % SUBLANES == 1

    def run(rows):
        for i in rows:
            x_s[i] = _dot(mf_ref[...], jnp.concatenate([ar_ref[i], ai_ref[i]], axis=0))
        for i in rows:
            xr, xi = x_s[i, :p], x_s[i, p:]
            hr, hi = hr_ref[i], hi_ref[i]
            y_s[i, :p] = (xr * hr - xi * hi).astype(BF16)
            y_s[i, p:] = (xr * hi + xi * hr).astype(BF16)
        for i in rows:
            c = _dot(mi_ref[...], y_s[i])
            c_s[i] = c[:p]
            c_s[SUBLANES + i] = c[p:]

    run(range(1))
    full = (kg + 1) * SUBLANES <= ka_used

    @pl.when(full)
    def _():
        run(range(1, SUBLANES))

    @pl.when(jnp.logical_not(full))
    def _():
        for i in range(1, SUBLANES):
            c_s[i] = jnp.zeros(c_s.shape[1:], F32)
            c_s[SUBLANES + i] = jnp.zeros(c_s.shape[1:], F32)

    c_ref[...] = jnp.swapaxes(c_s[...], 0, 1).astype(c_ref.dtype)


def _fft_stage2(ar, ai, hr, hi, m2f, m2i, ka_used, h_col0, dt):
    b, ka, p, d = ar.shape
    hc = h_col0 // dt
    aspec = pl.BlockSpec((None, SUBLANES, p, dt), lambda k, c, bi: (bi, k, 0, c))
    hspec = pl.BlockSpec((SUBLANES, p, dt), lambda k, c, bi: (k, 0, hc + c))
    mspec = pl.BlockSpec((2 * p, 2 * p), lambda k, c, bi: (0, 0))
    return pl.pallas_call(
        functools.partial(_s2_kernel, p=p, ka_used=ka_used),
        grid=(ka // SUBLANES, d // dt, b),
        in_specs=[aspec, aspec, hspec, hspec, mspec, mspec],
        out_specs=pl.BlockSpec((None, p, 2 * SUBLANES, dt), lambda k, c, bi: (bi, 0, k, c)),
        out_shape=jax.ShapeDtypeStruct((b, p, 2 * ka, d), BF16),
        scratch_shapes=[pltpu.VMEM((SUBLANES, 2 * p, dt), F32), pltpu.VMEM((SUBLANES, 2 * p, dt), BF16),
                        pltpu.VMEM((2 * SUBLANES, p, dt), F32)],
        compiler_params=_cparams("parallel", "parallel", "parallel"),
        name="fft_stage2",
    )(ar, ai, hr, hi, m2f, m2i)


def _s3_kernel(c_ref, g_ref, v_ref, x_ref, sk_ref, z_ref, y_s, *, group, pad_rows):
    zpad = jnp.zeros((pad_rows, c_ref.shape[-1]), BF16)
    for j in range(group):
        y_s[j] = _dot(g_ref[j], jnp.concatenate([c_ref[j], zpad], axis=0))
    y = jnp.swapaxes(y_s[...], 0, 1)
    v = v_ref[...].astype(F32)
    z_ref[...] = ((y + v * sk_ref[...]) * x_ref[...].astype(F32)).astype(z_ref.dtype)


def _fft_stage3(c, gc, v5, vpart, x5, xpart, skip, group, dt, out_dtype):
    b, p, kr, d = c.shape
    qh = v5.shape[2]
    kc = gc.shape[-1]
    pspec = lambda part: pl.BlockSpec((None, None, qh, group, dt), lambda bi, g, cc: (part, bi, 0, g, cc))
    return pl.pallas_call(
        functools.partial(_s3_kernel, group=group, pad_rows=kc - kr),
        grid=(b, p // group, d // dt),
        in_specs=[pl.BlockSpec((None, group, kr, dt), lambda bi, g, cc: (bi, g, 0, cc)),
                  pl.BlockSpec((group, qh, kc), lambda bi, g, cc: (g, 0, 0)),
                  pspec(vpart), pspec(xpart),
                  pl.BlockSpec((1, dt), lambda bi, g, cc: (0, cc))],
        out_specs=pspec(0),
        out_shape=jax.ShapeDtypeStruct((1, b, qh, p, d), out_dtype),
        scratch_shapes=[pltpu.VMEM((group, qh, dt), F32)],
        compiler_params=_cparams("parallel", "parallel", "parallel"),
        name="fft_stage3",
    )(c, gc, v5, x5, skip)


def _long_conv_lat(v5, vpart, x5, xpart, hr, hi, skip, plan, order, out_dtype):
    d = v5.shape[-1]
    dt1 = min(FFT_DT_OUTER, d)
    dt2 = min(FFT_DT_INNER, d)
    ar, ai = _fft_stage1(v5, vpart, plan["f1s"], plan["ka_pad"], FFT_GROUP, dt1, BF16, True, False)
    c = _fft_stage2(ar, ai, hr, hi, plan["m2f1"], plan["m2i1"], plan["ka_used"], order * d, dt2)
    return _fft_stage3(c, plan["g"], v5, vpart, x5, xpart, skip, FFT_GROUP, dt1, out_dtype)


def _filter_spectrum_lat(hy, seq, plan_f):
    fw1, fb1, ffreq, fw2, fb2, fwout = hy
    q = plan_f["q"]
    h_time, nrm = _hyena_filter_time(fw1, fb1, ffreq, fw2, fb2, fwout, seq, True)
    od = h_time.shape[-1]
    h5 = h_time.reshape(1, 1, FFT_P, q, od)
    dt = min(FFT_DT_OUTER, od)
    ar, ai = _fft_stage1(h5, 0, plan_f["f1"], plan_f["ka_pad"], SUBLANES, dt, F32, False, True)
    return _fft_stage2_filter(ar[0], ai[0], plan_f["m2f1"], 1.0 / nrm, dt)


def _dense_consts(seq):
    n = 2 * seq
    kf = seq + 1
    kf_pad = -(-kf // LANES) * LANES
    k = np.arange(kf)
    fwd_full = np.zeros((2 * kf_pad, n))
    ang = 2 * np.pi * np.outer(k, np.arange(n)) / n
    fwd_full[:kf] = np.cos(ang)
    fwd_full[kf_pad:kf_pad + kf] = -np.sin(ang)
    wgt = np.where((k == 0) | (k == seq), 1.0, 2.0) / n
    inv = np.zeros((seq, 2 * kf_pad))
    angi = 2 * np.pi * np.outer(np.arange(seq), k) / n
    inv[:, :kf] = wgt * np.cos(angi)
    inv[:, kf_pad:kf_pad + kf] = -wgt * np.sin(angi)
    return dict(kf_pad=kf_pad, fwd_full=_const3(fwd_full), fwd=_const3(fwd_full[:, :seq]), inv=_const3(inv))


def _dense_spec_kernel(h_ref, f_ref, inv_ref, hr_ref, hi_ref, *, kf_pad):
    y = _dot(f_ref[...], _stack3(h_ref[...])) * inv_ref[...]
    hr_ref[...] = y[:kf_pad]
    hi_ref[...] = y[kf_pad:]


def _filter_spectrum_ctx(hy, seq, cons):
    fw1, fb1, ffreq, fw2, fb2, fwout = hy
    h_time, nrm = _hyena_filter_time(fw1, fb1, ffreq, fw2, fb2, fwout, seq, False)
    od = h_time.shape[-1]
    n2 = 2 * seq
    h_time = h_time.reshape(n2, od)
    kf_pad = cons["kf_pad"]
    ct = min(FFT_DT_INNER, od)
    out = jax.ShapeDtypeStruct((kf_pad, od), F32)
    ospec = pl.BlockSpec((kf_pad, ct), lambda c: (0, c))
    return pl.pallas_call(
        functools.partial(_dense_spec_kernel, kf_pad=kf_pad),
        grid=(od // ct,),
        in_specs=[pl.BlockSpec((n2, ct), lambda c: (0, c)),
                  pl.BlockSpec((2 * kf_pad, 3 * n2), lambda c: (0, 0)),
                  pl.BlockSpec((1, ct), lambda c: (0, c))],
        out_specs=[ospec, ospec],
        out_shape=[out, out],
        compiler_params=_cparams("parallel"),
        name="dense_filter_spectrum",
    )(h_time, cons["fwd_full"], 1.0 / nrm)


def _dense_conv_kernel(v_ref, x1_ref, x2_ref, h1r_ref, h1i_ref, h2r_ref, h2i_ref, sk_ref,
                       f_ref, g_ref, z_ref, *, kf_pad):
    def conv(u, hr, hi):
        s = _dot(f_ref[...], _stack3(u))
        sr, si = s[:kf_pad], s[kf_pad:]
        y = jnp.concatenate([sr * hr - si * hi, sr * hi + si * hr], axis=0)
        return _dot(g_ref[...], _stack3(y))

    v = v_ref[...]
    z1 = x1_ref[...] * (conv(v, h1r_ref[...], h1i_ref[...]) + v * sk_ref[0:1, :])
    z2 = x2_ref[...] * (conv(z1, h2r_ref[...], h2i_ref[...]) + z1 * sk_ref[1:2, :])
    z_ref[...] = z2.astype(z_ref.dtype)


def _hyena_core_ctx(u3, hr, hi, fskip, cons, b, seq):
    d = u3.shape[-1]
    dt = min(MXU_N, d)
    nd = d // dt
    kf_pad = cons["kf_pad"]
    uspec = lambda part: pl.BlockSpec((None, seq, dt), lambda bi, c: (part, bi, c))
    hspec = lambda order: pl.BlockSpec((kf_pad, dt), lambda bi, c: (0, order * nd + c))
    return pl.pallas_call(
        functools.partial(_dense_conv_kernel, kf_pad=kf_pad),
        grid=(b, nd),
        in_specs=[uspec(0), uspec(1), uspec(2), hspec(0), hspec(0), hspec(1), hspec(1),
                  pl.BlockSpec((HYENA_ORDER, dt), lambda bi, c: (0, c)),
                  pl.BlockSpec((2 * kf_pad, 3 * seq), lambda bi, c: (0, 0)),
                  pl.BlockSpec((seq, 6 * kf_pad), lambda bi, c: (0, 0))],
        out_specs=pl.BlockSpec((seq, dt), lambda bi, c: (bi, c)),
        out_shape=jax.ShapeDtypeStruct((b * seq, d), BF16),
        compiler_params=_cparams("parallel", "parallel"),
        name="hyena_core_ctx",
    )(u3, u3, u3, hr, hi, hr, hi, fskip, cons["fwd"], cons["inv"])


def _rec_consts(chunk):
    t = np.arange(chunk)
    coefs, masks = [], []
    for direction in (0, 1):
        if direction == 0:
            rows = [t[None, :] <= t[:, None], t[None, :] > t[:, None]]
        else:
            rows = [t[None, :] >= t[:, None], t[None, :] < t[:, None]]
        mk = [np.eye(chunk)]
        m = chunk // 2
        while m >= 1:
            blk = t // (2 * m)
            half = (t // m) % 2
            mid = blk * 2 * m + m
            e = np.zeros((chunk, chunk))
            for r in range(chunk):
                if direction == 0:
                    if half[r] == 1:
                        e[r, mid[r]:r + 1] = 1
                    else:
                        e[r, r + 1:mid[r]] = 1
                else:
                    if half[r] == 0:
                        e[r, r:mid[r]] = 1
                    else:
                        e[r, mid[r]:r] = 1
            same = blk[:, None] == blk[None, :]
            if direction == 0:
                mk.append(same & (half[:, None] == 1) & (half[None, :] == 0))
            else:
                mk.append(same & (half[:, None] == 0) & (half[None, :] == 1))
            rows.append(e)
            m //= 2
        rows.append(np.ones((REC_TOTAL_ROWS, chunk)))
        a = np.concatenate([np.asarray(r, np.float64) for r in rows], axis=0)
        a3 = np.concatenate([a] * REC_GATE_TERMS, axis=1)
        pad = -(-a3.shape[1] // LANES) * LANES - a3.shape[1]
        a3 = np.pad(a3, ((0, 0), (0, pad)))
        coefs.append(a3)
        masks.append(np.stack([np.asarray(x, np.float32) for x in mk]))
    return (jnp.asarray(np.stack(coefs), dtype=BF16), jnp.asarray(np.stack(masks), dtype=F32))


def _rec_core(load_qkg, v_ref, coef_ref, mask_ref, s0_ref, o_ref, sfin_ref, q_s, k_s, st_s, ex_s, att_s,
              *, chunk, nsub, heads, dk, dv):
    di = pl.program_id(1)
    c = pl.program_id(2)
    levels = int(math.log2(chunk))
    tot = (2 + levels) * chunk

    @pl.when(c == 0)
    def _():
        st_s[...] = s0_ref[...]

    offs = [pl.multiple_of((s + di * (nsub - 1 - 2 * s)) * chunk, chunk) for s in range(nsub)]

    for s in range(nsub):
        q, k, g = load_qkg(pl.ds(offs[s], chunk))
        q_s[s] = q
        k_s[s] = k
        terms, rest = [], g
        for _ in range(REC_GATE_TERMS):
            terms.append(rest.astype(BF16))
            rest = rest - terms[-1].astype(F32)
        pad = coef_ref.shape[-1] - REC_GATE_TERMS * chunk
        if pad:
            terms.append(jnp.zeros((pad, g.shape[1]), BF16))
        ex_s[s] = jnp.exp(_dot(coef_ref[...], jnp.concatenate(terms, axis=0)))

    for s in range(nsub):
        for h in range(heads):
            ks = slice(h * dk, (h + 1) * dk)
            qh = q_s[s, :, ks]
            kh = k_s[s, :, ks]
            att = mask_ref[0] * _dot_nt(qh.astype(BF16), kh.astype(BF16))
            for lv in range(levels):
                e = ex_s[s, (2 + lv) * chunk:(3 + lv) * chunk, ks]
                att += mask_ref[1 + lv] * _dot_nt((qh * e).astype(BF16), (kh * e).astype(BF16))
            att_s[s, h] = att.astype(BF16)

    for s in range(nsub):
        rows = pl.ds(offs[s], chunk)
        for h in range(heads):
            ks = slice(h * dk, (h + 1) * dk)
            vs = slice(h * dv, (h + 1) * dv)
            vh = v_ref[rows, vs].astype(BF16)
            st = st_s[h]
            o = _dot(att_s[s, h], vh)
            o += _dot_nt((q_s[s, :, ks] * ex_s[s, 0:chunk, ks]).astype(BF16), st.astype(BF16))
            o_ref[rows, vs] = o.astype(o_ref.dtype)
            kd = (k_s[s, :, ks] * ex_s[s, chunk:2 * chunk, ks]).astype(BF16)
            st_s[h] = st * ex_s[s, tot:tot + 1, ks] + _dot_tn(vh, kd)

    @pl.when(c == pl.num_programs(2) - 1)
    def _():
        sfin_ref[...] = st_s[...]


def _hgrn_rec_kernel(q_ref, v_ref, f_ref, lb_ref, coef_ref, mask_ref, s0_ref, o_ref, sfin_ref,
                     q_s, k_s, st_s, ex_s, att_s, **kw):
    log_lb = lb_ref[0:1, :]
    log_1mlb = lb_ref[1:2, :]

    def load_qkg(rows):
        qr = q_ref[rows, :]
        fr = f_ref[rows, :]
        t2 = log_1mlb + jnp.minimum(fr, 0.0) - jnp.log(1.0 + jnp.exp(-jnp.abs(fr)))
        mx = jnp.maximum(log_lb, t2)
        g = mx + jnp.log(1.0 + jnp.exp(-jnp.abs(log_lb - t2)))
        return qr * _sigmoid(qr), 1.0 - jnp.exp(g), g

    _rec_core(load_qkg, v_ref, coef_ref, mask_ref, s0_ref, o_ref, sfin_ref, q_s, k_s, st_s, ex_s, att_s, **kw)


def _gla_rec_kernel(q_ref, k_ref, v_ref, a_ref, wup_ref, bup_ref, coef_ref, mask_ref, s0_ref,
                    o_ref, sfin_ref, q_s, k_s, st_s, ex_s, att_s, *, qscale, **kw):
    def load_qkg(rows):
        xg = _dot(a_ref[rows, :].astype(BF16), wup_ref[...]) + bup_ref[...]
        g = (jnp.minimum(xg, 0.0) - jnp.log(1.0 + jnp.exp(-jnp.abs(xg)))) * (1.0 / GLA_GATE_NORM)
        return q_ref[rows, :] * qscale, k_ref[rows, :], g

    _rec_core(load_qkg, v_ref, coef_ref, mask_ref, s0_ref, o_ref, sfin_ref, q_s, k_s, st_s, ex_s, att_s, **kw)


def _rec_call(kind, proj, extra, s0, consts, b, seq, d, heads, dk, dv):
    coef, masks = consts
    chunk = masks.shape[-1]
    rows = REC_STEP_ROWS
    nsub = rows // chunk
    nc = seq // rows
    hk = heads * dk
    hv = heads * dv
    levels = int(math.log2(chunk))
    n_rows = (2 + levels) * chunk + REC_TOTAL_ROWS

    def rmap(col):
        return lambda bi, di, c: (bi * nc + c + di * (nc - 1 - 2 * c), col)

    cspecs = [pl.BlockSpec((None, n_rows, coef.shape[-1]), lambda bi, di, c: (di, 0, 0)),
              pl.BlockSpec((None, levels + 1, chunk, chunk), lambda bi, di, c: (di, 0, 0, 0)),
              pl.BlockSpec((None, None, heads, dv, dk), lambda bi, di, c: (bi, di, 0, 0, 0))]
    if kind == "hgrn":
        lb = extra
        kernel = _hgrn_rec_kernel
        in_specs = [pl.BlockSpec((rows, d), rmap(0)),
                    pl.BlockSpec((rows, d), rmap(1)),
                    pl.BlockSpec((rows, d), lambda bi, di, c: (bi * nc + c + di * (nc - 1 - 2 * c), 3 + di)),
                    pl.BlockSpec((None, 2, d), lambda bi, di, c: (di, 0, 0))] + cspecs
        args = (proj, proj, proj, lb, coef, masks, s0)
        kw = {}
    else:
        wup, bup = extra
        kernel = _gla_rec_kernel
        in_specs = [pl.BlockSpec((rows, hk), rmap(0)),
                    pl.BlockSpec((rows, hk), rmap(1)),
                    pl.BlockSpec((rows, hv), rmap(1)),
                    pl.BlockSpec((rows, LANES), rmap((2 * hk + 2 * hv) // LANES)),
                    pl.BlockSpec((None, LANES, hk), lambda bi, di, c: (di, 0, 0)),
                    pl.BlockSpec((None, 1, hk), lambda bi, di, c: (di, 0, 0))] + cspecs
        args = (proj, proj, proj, proj, wup, bup, coef, masks, s0)
        kw = dict(qscale=float(dk) ** -0.5)
    return pl.pallas_call(
        functools.partial(kernel, chunk=chunk, nsub=nsub, heads=heads, dk=dk, dv=dv, **kw),
        grid=(b, 2, nc),
        in_specs=in_specs,
        out_specs=[pl.BlockSpec((None, rows, hv), lambda bi, di, c: (di, bi * nc + c + di * (nc - 1 - 2 * c), 0)),
                   pl.BlockSpec((None, None, heads, dv, dk), lambda bi, di, c: (bi, di, 0, 0, 0))],
        out_shape=[jax.ShapeDtypeStruct((2, b * seq, hv), BF16),
                   jax.ShapeDtypeStruct((b, 2, heads, dv, dk), F32)],
        scratch_shapes=[pltpu.VMEM((nsub, chunk, hk), F32), pltpu.VMEM((nsub, chunk, hk), F32),
                        pltpu.VMEM((heads, dv, dk), F32), pltpu.VMEM((nsub, n_rows, hk), F32),
                        pltpu.VMEM((nsub, heads, chunk, chunk), BF16)],
        compiler_params=_cparams("parallel", "parallel", "arbitrary"),
        name=kind + "_recurrence",
    )(*args)


def _gated_out_kernel(o_ref, gate_ref, gn_ref, w_ref, x_ref, mg_ref, out_ref, *, heads):
    d = x_ref.shape[1]
    dh = d // heads
    kc = max(dh, 2 * MXU_N)
    acc = None
    for k0 in range(0, d, kc):
        parts = []
        for h0 in range(k0, k0 + kc, dh):
            seg = o_ref[0, :, h0:h0 + dh].astype(F32) + o_ref[1, :, h0:h0 + dh].astype(F32)
            ms = jnp.mean(seg * seg, axis=-1, keepdims=True)
            parts.append(seg * lax.rsqrt(ms + RMS_EPS))
        gate = gate_ref[:, k0:k0 + kc]
        y = jnp.concatenate(parts, axis=1) * gn_ref[:, k0:k0 + kc] * (gate * _sigmoid(gate))
        part = _dot(y.astype(BF16), w_ref[k0:k0 + kc, :])
        acc = part if acc is None else acc + part
    out_ref[...] = x_ref[...] + mg_ref[...] * acc


def _gated_out(o2, proj, gate_col, gn, w, x, mg, bmap, heads, tm):
    m, d = x.shape
    return pl.pallas_call(
        functools.partial(_gated_out_kernel, heads=heads),
        grid=(m // tm,),
        in_specs=[pl.BlockSpec((2, tm, d), lambda i: (0, i, 0)),
                  pl.BlockSpec((tm, d), lambda i: (i, gate_col)),
                  pl.BlockSpec((1, d), lambda i: (0, 0)),
                  pl.BlockSpec((d, d), lambda i: (0, 0)),
                  pl.BlockSpec((tm, d), lambda i: (i, 0)),
                  _mod_spec(d, bmap)],
        out_specs=pl.BlockSpec((tm, d), lambda i: (i, 0)),
        out_shape=jax.ShapeDtypeStruct((m, d), F32),
        compiler_params=_cparams("parallel"),
        name="gated_out",
    )(o2, proj, gn, w, x, mg)


def _recurrent_mix(kind, x_ctx, x_lat, norm, ctx_map, w_in, extra, consts, b, lc, seq, d, heads, dk, dv, tn):
    g1, sh, sc = norm
    tm_c, tm_l = min(TM_PROJ, x_ctx.shape[0]), min(TM_PROJ, x_lat.shape[0])
    proj_ctx = _projection(x_ctx, g1, sh, sc, w_in, ctx_map, tm_c, tn)
    proj_lat = _projection(x_lat, g1, sh, sc, w_in, lambda i: (i * tm_l) // seq, tm_l, tn)
    s0 = jnp.zeros((b, 2, heads, dv, dk), F32)
    o_ctx, s_ctx = _rec_call(kind, proj_ctx, extra, s0, consts, b, lc, d, heads, dk, dv)
    o_lat, _ = _rec_call(kind, proj_lat, extra, s_ctx, consts, b, seq, d, heads, dk, dv)
    return (proj_ctx, o_ctx), (proj_lat, o_lat)


def kernel(x, c, ctx, c_ctx, w_mod, b_mod, norm1_g, norm2_g, w_ffn_in, w_ffn_out, final_g, hy_w_in, hy_conv_w, hy_fw1, hy_fb1, hy_ffreq, hy_fw2, hy_fb2, hy_fwout, hy_fskip, hy_w_out, hg_w_in, hg_lb_logits, hg_onorm_g, hg_w_out, gla_w_in, gla_w_up, gla_b_up, gla_onorm_g, gla_w_out):
    b, seq, d = x.shape
    lc = ctx.shape[1]
    depth = w_mod.shape[0]
    assert b + 1 <= MOD_ROWS and seq % (FFT_P * 2) == 0 and FFT_P % GRID_W == 0
    m_lat, m_ctx = b * seq, b * lc
    ctx_row = b
    tm_lat, tm_ctx = TM_ROWS, min(TM_ROWS, m_ctx)
    tm_in = TM_PROJ
    lat_map = lambda i: (i * tm_lat) // seq
    lat_map_in = lambda i: (i * tm_in) // seq
    ctx_map = lambda i: ctx_row

    xl = x.reshape(m_lat, d)
    xc = ctx.reshape(m_ctx, d)
    c8 = jnp.zeros((MOD_ROWS, d), F32).at[:b].set(c).at[ctx_row].set(c_ctx)
    mod = _modulation(c8, w_mod, b_mod).reshape(depth, MOD_ROWS, N_MOD, 1, d)

    plan = _fft_consts(seq, seq // FFT_P)
    plan_f = _fft_consts(seq, 2 * seq // FFT_P)
    dense = _dense_consts(lc)
    rec_consts = {1: _rec_consts(HGRN_CHUNK), 2: _rec_consts(GLA_CHUNK)}
    qh = plan["qh"]

    wb_ffn_in, wb_ffn_out = _to_bf16(w_ffn_in), _to_bf16(w_ffn_out)
    wb_hy_in, wb_hy_out = _to_bf16(hy_w_in), _to_bf16(hy_w_out)
    wb_hg_in, wb_hg_out = _to_bf16(hg_w_in), _to_bf16(hg_w_out)
    wb_gla_in, wb_gla_out = _to_bf16(gla_w_in), _to_bf16(gla_w_out)

    for i in range(depth):
        last = i == depth - 1
        kind, j = i % N_MIXERS, i // N_MIXERS
        mv = [mod[i, :, k] for k in range(N_MOD)]
        g1 = norm1_g[i][None, :]
        need_ctx = (not last) or kind != 0
        if kind == 0:
            w_in, w_out = wb_hy_in[j], wb_hy_out[j]
            hy = (hy_fw1[j], hy_fb1[j], hy_ffreq[j], hy_fw2[j], hy_fb2[j], hy_fwout[j])
            u = _hyena_in(xl, g1, mv[0], mv[1], w_in, hy_conv_w[j], lat_map_in,
                          tm_in, min(TN_HYENA_IN, d), GRID_W, BF16).reshape(3, b, qh, FFT_P, d)
            hr, hi = _filter_spectrum_lat(hy, seq, plan_f)
            z1 = _long_conv_lat(u, 0, u, 1, hr, hi, hy_fskip[j, 0][None, :], plan, 0, BF16)
            z2 = _long_conv_lat(z1, 0, u, 2, hr, hi, hy_fskip[j, 1][None, :], plan, 1, BF16)
            xl = _out_res(z2.reshape(m_lat, d), w_out, xl, mv[2], lat_map, tm_lat)
            if need_ctx:
                uc = _hyena_in(xc, g1, mv[0], mv[1], w_in, hy_conv_w[j], ctx_map, lc, min(TN_PROJ, d), lc, F32)
                hcr, hci = _filter_spectrum_ctx(hy, lc, dense)
                zc = _hyena_core_ctx(uc, hcr, hci, hy_fskip[j], dense, b, lc)
                xc = _out_res(zc, w_out, xc, mv[2], ctx_map, tm_ctx)
        else:
            if kind == 1:
                heads = d // HGRN_EXPAND
                dk = dv = HGRN_EXPAND
                w_in = wb_hg_in[j]
                lb_cum = jnp.cumsum(jax.nn.softmax(hg_lb_logits.astype(F32), axis=1), axis=1)
                lb = lb_cum[:, i] - lb_cum[:, 0]
                extra = jnp.stack([jnp.log(lb), jnp.log1p(-lb)], axis=1)
                gn, w_out, tn = hg_onorm_g[j], wb_hg_out[j], TN_PROJ
                rkind = "hgrn"
            else:
                heads = GLA_HEADS
                dk, dv = d // 2 // heads, d // heads
                n_in = gla_w_in.shape[-1]
                n_pad = -(-n_in // LANES) * LANES
                w_in = jnp.pad(wb_gla_in[j], ((0, 0), (0, n_pad - n_in)))
                r = GLA_GATE_RANK
                wup = jnp.zeros((2, LANES, heads * dk), F32)
                wup = wup.at[0, :r].set(gla_w_up[j, 0]).at[1, r:2 * r].set(gla_w_up[j, 1]).astype(BF16)
                extra = (wup, gla_b_up[j][:, None, :])
                gn, w_out = gla_onorm_g[j], wb_gla_out[j]
                tn = _lane_tile(n_pad, TN_PROJ)
                rkind = "gla"
            (p_ctx, o_ctx), (p_lat, o_lat) = _recurrent_mix(
                rkind, xc, xl, (g1, mv[0], mv[1]), ctx_map, w_in, extra, rec_consts[kind],
                b, lc, seq, d, heads, dk, dv, tn)
            gate_col = 2 if kind == 1 else (2 * heads * dk + heads * dv) // d
            xl = _gated_out(o_lat, p_lat, gate_col, gn[None, :], w_out, xl, mv[2], lat_map, heads, tm_lat)
            xc = _gated_out(o_ctx, p_ctx, gate_col, gn[None, :], w_out, xc, mv[2], ctx_map, heads, tm_ctx)
        g2 = norm2_g[i][None, :]
        fg = final_g[None, :]
        xl = _ffn(xl, g2, mv[3], mv[4], mv[5], wb_ffn_in, wb_ffn_out, i, fg, last, lat_map, tm_lat, TF_FFN)
        if not last:
            xc = _ffn(xc, g2, mv[3], mv[4], mv[5], wb_ffn_in, wb_ffn_out, i, fg, False, ctx_map, tm_ctx, TF_FFN)
    return xl.reshape(b, seq, d)
```

```python
import functools
import math

import ml_dtypes
import numpy as np
import jax
import jax.numpy as jnp
from jax import lax
from jax.experimental import pallas as pl
from jax.experimental.pallas import tpu as pltpu

F32 = jnp.float32
BF16 = jnp.bfloat16

N_MOD = 6
N_MIXERS = 3
RMS_EPS = 1e-6
GRID_W = 64
HYENA_ORDER = 2
FILTER_BANDS = 16
FILTER_EMB = 1 + 2 * FILTER_BANDS
HYENA_DECAY_MIN = math.log(1e-2) / 1.5
HYENA_DECAY_MAX = math.log(1e-2) / 0.3
HGRN_EXPAND = 128
GLA_HEADS = 4
GLA_GATE_RANK = 16
GLA_GATE_NORM = 16.0

LANES = 128
SUBLANES = 8
MXU_N = 256
V7X_VMEM_LIMIT = 56 * 1024 * 1024

FFT_P = 128
FFT_GROUP = 4 * SUBLANES
FFT_DT_OUTER = 1024
FFT_DT_INNER = 512
REC_STEP_ROWS = 256
HGRN_CHUNK = 64
GLA_CHUNK = 128
REC_GATE_TERMS = 2
REC_TOTAL_ROWS = 2 * SUBLANES
MOD_ROWS = 8

TM_ROWS = 512
TM_PROJ = 1024
TN_PROJ = 1024
TN_HYENA_IN = 1024
TF_FFN = 512
TN_MOD = 1024


def _cparams(*sem):
    return pltpu.CompilerParams(dimension_semantics=sem, vmem_limit_bytes=V7X_VMEM_LIMIT)


def _dot(a, b):
    return jnp.dot(a, b, preferred_element_type=F32)


def _dot_nt(a, b):
    return lax.dot_general(a, b, (((1,), (1,)), ((), ())), preferred_element_type=F32)


def _dot_tn(a, b):
    return lax.dot_general(a, b, (((0,), (0,)), ((), ())), preferred_element_type=F32)


def _split(x):
    hi = x.astype(BF16)
    lo = (x - hi.astype(F32)).astype(BF16)
    return hi, lo


def _stack3(x, pad_rows=0):
    hi, lo = _split(x)
    parts = [hi, lo, hi]
    if pad_rows:
        parts.append(jnp.zeros((pad_rows, x.shape[1]), BF16))
    return jnp.concatenate(parts, axis=0)


def _const3(c, pad_cols=0):
    hi = c.astype(ml_dtypes.bfloat16)
    lo = (c - hi.astype(np.float64)).astype(ml_dtypes.bfloat16)
    parts = [hi, hi, lo]
    if pad_cols:
        parts.append(np.zeros(c.shape[:-1] + (pad_cols,), ml_dtypes.bfloat16))
    return jnp.asarray(np.concatenate(parts, axis=-1))


def _dot3(a, bh, bl):
    ah, al = _split(a)
    return _dot(ah, bh) + _dot(ah, bl) + _dot(al, bh)


def _sigmoid(x):
    return jax.nn.sigmoid(x)


def _normmod(x, g, sh, sc):
    ms = jnp.mean(x * x, axis=-1, keepdims=True)
    y = x * lax.rsqrt(ms + RMS_EPS) * g
    return y * (1.0 + sc) + sh


def _mod_kernel(c_ref, w_ref, b_ref, o_ref):
    c = c_ref[...]
    s = (c * _sigmoid(c)).astype(BF16)
    o_ref[...] = _dot(s, w_ref[...].astype(BF16)) + b_ref[...]


def _modulation(c8, w_mod, b_mod):
    depth, d, n = w_mod.shape
    tn = TN_MOD
    return pl.pallas_call(
        _mod_kernel,
        grid=(depth, n // tn),
        in_specs=[pl.BlockSpec((MOD_ROWS, d), lambda l, j: (0, 0)),
                  pl.BlockSpec((None, d, tn), lambda l, j: (l, 0, j)),
                  pl.BlockSpec((None, 1, tn), lambda l, j: (l, 0, j))],
        out_specs=pl.BlockSpec((None, MOD_ROWS, tn), lambda l, j: (l, 0, j)),
        out_shape=jax.ShapeDtypeStruct((depth, MOD_ROWS, n), F32),
        compiler_params=_cparams("parallel", "parallel"),
        name="modulation",
    )(c8, w_mod, b_mod.reshape(depth, 1, n))


def _mod_spec(d, bmap):
    return pl.BlockSpec((None, 1, d), lambda *idx: (bmap(idx[0]), 0, 0))


def _lane_tile(n, cap):
    best = n
    for t in range(LANES, min(n, cap) + 1, LANES):
        if n % t == 0:
            best = t
    return best


def _cast_kernel(x_ref, o_ref):
    o_ref[...] = x_ref[...].astype(o_ref.dtype)


def _to_bf16(w):
    shape = w.shape
    w2 = w.reshape(-1, shape[-1])
    r, c = w2.shape
    tr, tc = min(TM_ROWS, r), _lane_tile(c, 2 * TN_PROJ)
    out = pl.pallas_call(
        _cast_kernel,
        grid=(r // tr, c // tc),
        in_specs=[pl.BlockSpec((tr, tc), lambda i, j: (i, j))],
        out_specs=pl.BlockSpec((tr, tc), lambda i, j: (i, j)),
        out_shape=jax.ShapeDtypeStruct((r, c), BF16),
        compiler_params=_cparams("parallel", "parallel"),
        name="cast_bf16",
    )(w2)
    return out.reshape(shape)


def _proj_kernel(x_ref, g_ref, sh_ref, sc_ref, w_ref, o_ref, h_s):
    @pl.when(pl.program_id(1) == 0)
    def _():
        h_s[...] = _normmod(x_ref[...], g_ref[...], sh_ref[...], sc_ref[...]).astype(BF16)

    o_ref[...] = _dot(h_s[...], w_ref[...])


def _projection(x, g, sh, sc, w, bmap, tm, tn):
    m, d = x.shape
    n = w.shape[1]
    return pl.pallas_call(
        _proj_kernel,
        grid=(m // tm, n // tn),
        in_specs=[pl.BlockSpec((tm, d), lambda i, j: (i, 0)),
                  pl.BlockSpec((1, d), lambda i, j: (0, 0)),
                  _mod_spec(d, bmap), _mod_spec(d, bmap),
                  pl.BlockSpec((d, tn), lambda i, j: (0, j))],
        out_specs=pl.BlockSpec((tm, tn), lambda i, j: (i, j)),
        out_shape=jax.ShapeDtypeStruct((m, n), F32),
        scratch_shapes=[pltpu.VMEM((tm, d), BF16)],
        compiler_params=_cparams("parallel", "arbitrary"),
        name="projection",
    )(x, g, sh, sc, w)


def _out_res_kernel(a_ref, w_ref, x_ref, mg_ref, o_ref):
    o_ref[...] = x_ref[...] + mg_ref[...] * _dot(a_ref[...], w_ref[...])


def _ffn_kernel(x_ref, g_ref, sh_ref, sc_ref, mg_ref, wg_ref, wu_ref, wo_ref, fg_ref, o_ref, h_s, acc_s,
                *, final_norm):
    j = pl.program_id(1)

    @pl.when(j == 0)
    def _():
        h_s[...] = _normmod(x_ref[...], g_ref[...], sh_ref[...], sc_ref[...]).astype(BF16)
        acc_s[...] = jnp.zeros_like(acc_s)

    h = h_s[...]
    a = _dot(h, wg_ref[...])
    u = _dot(h, wu_ref[...])
    act = (a * _sigmoid(a) * u).astype(BF16)
    acc_s[...] += _dot(act, wo_ref[...])

    @pl.when(j == pl.num_programs(1) - 1)
    def _():
        y = x_ref[...] + mg_ref[...] * acc_s[...]
        if final_norm:
            y = y * lax.rsqrt(jnp.mean(y * y, axis=-1, keepdims=True) + RMS_EPS) * fg_ref[...]
        o_ref[...] = y


def _ffn(x, g, sh, sc, mg, w_in, w_out, layer, final_g, final_norm, bmap, tm, tf):
    m, d = x.shape
    f = w_out.shape[1]
    nf = f // tf
    return pl.pallas_call(
        functools.partial(_ffn_kernel, final_norm=final_norm),
        grid=(m // tm, nf),
        in_specs=[pl.BlockSpec((tm, d), lambda i, j: (i, 0)),
                  pl.BlockSpec((1, d), lambda i, j: (0, 0)),
                  _mod_spec(d, bmap), _mod_spec(d, bmap), _mod_spec(d, bmap),
                  pl.BlockSpec((None, d, tf), lambda i, j: (layer, 0, j)),
                  pl.BlockSpec((None, d, tf), lambda i, j: (layer, 0, nf + j)),
                  pl.BlockSpec((None, tf, d), lambda i, j: (layer, j, 0)),
                  pl.BlockSpec((1, d), lambda i, j: (0, 0))],
        out_specs=pl.BlockSpec((tm, d), lambda i, j: (i, 0)),
        out_shape=jax.ShapeDtypeStruct((m, d), F32),
        scratch_shapes=[pltpu.VMEM((tm, d), BF16), pltpu.VMEM((tm, d), F32)],
        compiler_params=_cparams("parallel", "arbitrary"),
        name="ffn",
    )(x, g, sh, sc, mg, w_in, w_in, w_out, final_g)


def _conv3_rows(acc, cw, row_len):
    tm = acc.shape[0]
    rid = lax.broadcasted_iota(jnp.int32, (tm, 1), 0) % row_len
    up = jnp.where(rid == 0, 0.0, pltpu.roll(acc, 1, 0))
    dn = jnp.where(rid == row_len - 1, 0.0, pltpu.roll(acc, tm - 1, 0))
    return cw[0:1] * up + cw[1:2] * acc + cw[2:3] * dn


def _hy_in_kernel(x_ref, g_ref, sh_ref, sc_ref, w_ref, cw_ref, o_ref, h_s, *, row_len):
    @pl.when(pl.program_id(1) == 0)
    def _():
        h_s[...] = _normmod(x_ref[...], g_ref[...], sh_ref[...], sc_ref[...]).astype(BF16)

    h = h_s[...]
    for n0 in range(0, o_ref.shape[1], MXU_N):
        ns = slice(n0, n0 + MXU_N)
        o_ref[:, ns] = _conv3_rows(_dot(h, w_ref[:, ns]), cw_ref[:, ns], row_len).astype(o_ref.dtype)


def _hyena_in(x, g, sh, sc, w, cw, bmap, tm, tn, row_len, out_dtype):
    m, d = x.shape
    nd = d // tn
    return pl.pallas_call(
        functools.partial(_hy_in_kernel, row_len=row_len),
        grid=(m // tm, 3 * nd),
        in_specs=[pl.BlockSpec((tm, d), lambda i, j: (i, 0)),
                  pl.BlockSpec((1, d), lambda i, j: (0, 0)),
                  _mod_spec(d, bmap), _mod_spec(d, bmap),
                  pl.BlockSpec((d, tn), lambda i, j: (0, j)),
                  pl.BlockSpec((3, tn), lambda i, j: (0, j))],
        out_specs=pl.BlockSpec((None, tm, tn), lambda i, j: (j // nd, i, j % nd)),
        out_shape=jax.ShapeDtypeStruct((3, m, d), out_dtype),
        scratch_shapes=[pltpu.VMEM((tm, d), BF16)],
        compiler_params=_cparams("parallel", "arbitrary"),
        name="hyena_in",
    )(x, g, sh, sc, w, cw)


def _out_res(a, w, x, mg, bmap, tm):
    m, d = x.shape
    return pl.pallas_call(
        _out_res_kernel,
        grid=(m // tm,),
        in_specs=[pl.BlockSpec((tm, d), lambda i: (i, 0)),
                  pl.BlockSpec((d, d), lambda i: (0, 0)),
                  pl.BlockSpec((tm, d), lambda i: (i, 0)),
                  _mod_spec(d, bmap)],
        out_specs=pl.BlockSpec((tm, d), lambda i: (i, 0)),
        out_shape=jax.ShapeDtypeStruct((m, d), F32),
        compiler_params=_cparams("parallel"),
        name="out_residual",
    )(a, w, x, mg)


def _filter_kernel(z_ref, w1h_ref, w1l_ref, b1_ref, fq_ref, w2h_ref, w2l_ref, b2_ref,
                   wo_ref, dl_ref, h_ref, nrm_ref):
    z = z_ref[...]
    t = z[:, 0:1]
    valid = z[:, FILTER_EMB:FILTER_EMB + 1]
    a1 = _dot3(z, w1h_ref[...], w1l_ref[...]) + b1_ref[...]
    hid = jnp.sin(fq_ref[0:1, :] * a1)
    a2 = _dot3(hid, w2h_ref[...], w2l_ref[...]) + b2_ref[...]
    hid = jnp.sin(fq_ref[1:2, :] * a2)
    hh, hl = _split(hid)
    h = _dot(jnp.concatenate([hh, hl], axis=1), wo_ref[...]) * jnp.exp(-t * dl_ref[...]) * valid
    h_ref[...] = h.reshape(h_ref.shape)

    @pl.when((pl.program_id(1) == 0) & (pl.program_id(2) == 0))
    def _():
        nrm_ref[...] = jnp.zeros_like(nrm_ref)

    nrm_ref[...] += jnp.sum(jnp.abs(h), axis=0, keepdims=True)


def _pad2(a, rows, cols):
    return jnp.pad(a, ((0, rows - a.shape[0]), (0, cols - a.shape[1])))


def _filter_positions(seq, perm):
    if perm:
        r_hi = seq // FFT_P
        gt, ng = 2 * SUBLANES, FFT_P // (2 * SUBLANES)
        dirs, g, tl, th = np.meshgrid(np.arange(2), np.arange(ng), np.arange(gt), np.arange(r_hi), indexing="ij")
        n = FFT_P * (r_hi * dirs + th) + g * gt + tl
    else:
        r_hi, gt, ng = seq, 1, 1
        dirs, g, tl, th = np.meshgrid(np.arange(2), np.arange(1), np.arange(1), np.arange(seq), indexing="ij")
        n = seq * dirs + th
    n = n.reshape(-1)
    pos = np.where(n < seq, n, 2 * seq - 1 - n).astype(np.float32)
    z = np.zeros((n.size, LANES), np.float32)
    z[:, 0] = pos / np.float32(max(seq - 1, 1))
    bands = np.arange(1, FILTER_BANDS + 1, dtype=np.float32)
    ang = (np.float32(2.0 * math.pi / seq) * pos)[:, None] * bands[None, :]
    z[:, 1:1 + FILTER_BANDS] = np.cos(ang.astype(np.float64))
    z[:, 1 + FILTER_BANDS:FILTER_EMB] = -np.sin(ang.astype(np.float64))
    z[:, FILTER_EMB] = n != seq
    return jnp.asarray(z), gt, ng, r_hi


def _hyena_filter_time(fw1, fb1, ffreq, fw2, fb2, fwout, seq, perm):
    od = fwout.shape[1] // 2
    d = od // HYENA_ORDER
    z, gt, ng, r_hi = _filter_positions(seq, perm)
    rows = gt * r_hi
    ct = min(2 * FFT_DT_OUTER, od)
    hidden = LANES
    w1h, w1l = _split(_pad2(fw1, LANES, hidden))
    w2h, w2l = _split(_pad2(fw2, hidden, hidden))
    wo = _pad2(fwout, hidden, fwout.shape[1]).astype(BF16)
    wo2 = jnp.concatenate([wo, wo], axis=0)
    b1 = _pad2(fb1[None, :], 1, hidden)
    b2 = _pad2(fb2[None, :], 1, hidden)
    fq = _pad2(ffreq, 2, hidden)
    deltas = np.abs(np.linspace(HYENA_DECAY_MIN, HYENA_DECAY_MAX, d, dtype=np.float32))
    dl = jnp.asarray(np.tile(deltas, HYENA_ORDER)[None, :])
    nct = od // ct
    small = lambda shape: pl.BlockSpec(shape, lambda c, di, g: (0, 0))
    wspec = pl.BlockSpec((2 * hidden, ct), lambda c, di, g: (0, di * nct + c))
    return pl.pallas_call(
        _filter_kernel,
        grid=(nct, 2, ng),
        in_specs=[pl.BlockSpec((rows, LANES), lambda c, di, g: (di * ng + g, 0)),
                  small((LANES, hidden)), small((LANES, hidden)), small((1, hidden)), small((2, hidden)),
                  small((hidden, hidden)), small((hidden, hidden)), small((1, hidden)),
                  wspec,
                  pl.BlockSpec((1, ct), lambda c, di, g: (0, c))],
        out_specs=[pl.BlockSpec((None, gt, None, r_hi, ct), lambda c, di, g: (g, 0, di, 0, c)),
                   pl.BlockSpec((1, ct), lambda c, di, g: (0, c))],
        out_shape=[jax.ShapeDtypeStruct((ng, gt, 2, r_hi, od), F32), jax.ShapeDtypeStruct((1, od), F32)],
        compiler_params=_cparams("parallel", "arbitrary", "arbitrary"),
        name="hyena_filter",
    )(z, w1h, w1l, b1, fq, w2h, w2l, b2, wo2, dl)


def _fft_consts(seq, k_in):
    n = 2 * seq
    p = FFT_P
    q = n // p
    qh = q // 2
    ka_used = qh + 1
    ka_pad = -(-ka_used // SUBLANES) * SUBLANES
    ka = np.arange(ka_used)
    tl = np.arange(p)
    th = np.arange(k_in)
    theta = 2 * np.pi * (ka[None, :, None] * th[None, None, :] / q + ka[None, :, None] * tl[:, None, None] / n)
    f1 = np.zeros((p, 2 * ka_pad, k_in))
    f1[:, :ka_used] = np.cos(theta)
    f1[:, ka_pad:ka_pad + ka_used] = -np.sin(theta)
    k3 = 3 * k_in
    f1c = _const3(f1, pad_cols=-(-k3 // LANES) * LANES - k3)
    f1s = jnp.asarray(np.pad(f1, ((0, 0), (0, 0), (0, -(-k_in // LANES) * LANES - k_in))).astype(ml_dtypes.bfloat16))
    ang = 2 * np.pi * np.outer(np.arange(p), np.arange(p)) / p
    cc, sc = np.cos(ang), np.sin(ang)
    m2f_np, m2i_np = np.block([[cc, sc], [-sc, cc]]), np.block([[cc, -sc], [sc, cc]])
    m2f1, m2i1 = (jnp.asarray(m.astype(ml_dtypes.bfloat16)) for m in (m2f_np, m2i_np))
    tho = np.arange(qh)
    phi = 2 * np.pi * (tho[None, :, None] * ka[None, None, :] / q + ka[None, None, :] * tl[:, None, None] / n)
    wgt = np.where((ka == 0) | (ka == qh), 1.0, 2.0) / n
    kr = 2 * ka_pad
    g = np.zeros((p, qh, -(-kr // LANES) * LANES))
    col = 2 * SUBLANES * (ka // SUBLANES) + ka % SUBLANES
    g[:, :, col] = wgt * np.cos(phi)
    g[:, :, col + SUBLANES] = -wgt * np.sin(phi)
    gc = jnp.asarray(g.astype(ml_dtypes.bfloat16))
    return dict(q=q, qh=qh, ka_used=ka_used, ka_pad=ka_pad, f1=f1c, f1s=f1s, m2f1=m2f1, m2i1=m2i1, g=gc)


def _s1_kernel(x_ref, f_ref, ar_ref, ai_ref, a_s, *, group, ka_pad, pad_rows, natural, split):
    x = x_ref[...].astype(F32)
    if natural:
        x = jnp.swapaxes(x, 0, 1)
    zpad = None if split else jnp.zeros((pad_rows, x.shape[-1]), BF16)
    for j in range(group):
        xx = _stack3(x[j], pad_rows) if split else jnp.concatenate([x[j].astype(BF16), zpad], axis=0)
        a_s[j] = _dot(f_ref[j], xx)
    a = jnp.swapaxes(a_s[...], 0, 1)
    ar_ref[...] = a[:ka_pad].astype(ar_ref.dtype)
    ai_ref[...] = a[ka_pad:].astype(ai_ref.dtype)


def _fft_stage1(x5, part, f1c, ka_pad, group, dt, out_dtype, natural, split):
    if natural:
        _, b, k_in, p, d = x5.shape
        xspec = pl.BlockSpec((None, None, k_in, group, dt), lambda bi, g, c: (part, bi, 0, g, c))
    else:
        _, b, p, k_in, d = x5.shape
        xspec = pl.BlockSpec((None, None, group, k_in, dt), lambda bi, g, c: (part, bi, g, 0, c))
    kc = f1c.shape[-1]
    out = jax.ShapeDtypeStruct((b, ka_pad, p, d), out_dtype)
    ospec = pl.BlockSpec((None, ka_pad, group, dt), lambda bi, g, c: (bi, 0, g, c))
    pad_rows = kc - (3 if split else 1) * k_in
    return pl.pallas_call(
        functools.partial(_s1_kernel, group=group, ka_pad=ka_pad, pad_rows=pad_rows, natural=natural, split=split),
        grid=(b, p // group, d // dt),
        in_specs=[xspec,
                  pl.BlockSpec((group, 2 * ka_pad, kc), lambda bi, g, c: (g, 0, 0))],
        out_specs=[ospec, ospec],
        out_shape=[out, out],
        scratch_shapes=[pltpu.VMEM((group, 2 * ka_pad, dt), F32)],
        compiler_params=_cparams("parallel", "parallel", "parallel"),
        name="fft_stage1",
    )(x5, f1c)


def _s2_filter_kernel(ar_ref, ai_ref, m_ref, inv_ref, hr_ref, hi_ref, *, p):
    for i in range(ar_ref.shape[0]):
        x = jnp.concatenate([ar_ref[i], ai_ref[i]], axis=0)
        y = _dot(m_ref[...], x.astype(BF16)) * inv_ref[...]
        hr_ref[i] = y[:p]
        hi_ref[i] = y[p:]


def _fft_stage2_filter(ar, ai, m2f, inv_nrm, dt):
    ka, p, d = ar.shape
    spec = pl.BlockSpec((SUBLANES, p, dt), lambda k, c: (k, 0, c))
    out = jax.ShapeDtypeStruct((ka, p, d), F32)
    return pl.pallas_call(
        functools.partial(_s2_filter_kernel, p=p),
        grid=(ka // SUBLANES, d // dt),
        in_specs=[spec, spec,
                  pl.BlockSpec((2 * p, 2 * p), lambda k, c: (0, 0)),
                  pl.BlockSpec((1, dt), lambda k, c: (0, c))],
        out_specs=[spec, spec],
        out_shape=[out, out],
        compiler_params=_cparams("parallel", "parallel"),
        name="fft_stage2_filter",
    )(ar, ai, m2f, inv_nrm)


def _s2_kernel(ar_ref, ai_ref, hr_ref, hi_ref, mf_ref, mi_ref, c_ref, x_s, y_s, c_s, *, p, ka_used):
    kg = pl.program_id(0)
    assert ka_used % SUBLANES == 1

    def run(rows):
        for i in rows:
            x_s[i] = _dot(mf_ref[...], jnp.concatenate([ar_ref[i], ai_ref[i]], axis=0))
        for i in rows:
            xr, xi = x_s[i, :p], x_s[i, p:]
            hr, hi = hr_ref[i], hi_ref[i]
            y_s[i, :p] = (xr * hr - xi * hi).astype(BF16)
            y_s[i, p:] = (xr * hi + xi * hr).astype(BF16)
        for i in rows:
            c = _dot(mi_ref[...], y_s[i])
            c_s[i] = c[:p]
            c_s[SUBLANES + i] = c[p:]

    run(range(1))
    full = (kg + 1) * SUBLANES <= ka_used

    @pl.when(full)
    def _():
        run(range(1, SUBLANES))

    @pl.when(jnp.logical_not(full))
    def _():
        for i in range(1, SUBLANES):
            c_s[i] = jnp.zeros(c_s.shape[1:], F32)
            c_s[SUBLANES + i] = jnp.zeros(c_s.shape[1:], F32)

    c_ref[...] = jnp.swapaxes(c_s[...], 0, 1).astype(c_ref.dtype)


def _fft_stage2(ar, ai, hr, hi, m2f, m2i, ka_used, h_col0, dt):
    b, ka, p, d = ar.shape
    hc = h_col0 // dt
    aspec = pl.BlockSpec((None, SUBLANES, p, dt), lambda k, c, bi: (bi, k, 0, c))
    hspec = pl.BlockSpec((SUBLANES, p, dt), lambda k, c, bi: (k, 0, hc + c))
    mspec = pl.BlockSpec((2 * p, 2 * p), lambda k, c, bi: (0, 0))
    return pl.pallas_call(
        functools.partial(_s2_kernel, p=p, ka_used=ka_used),
        grid=(ka // SUBLANES, d // dt, b),
        in_specs=[aspec, aspec, hspec, hspec, mspec, mspec],
        out_specs=pl.BlockSpec((None, p, 2 * SUBLANES, dt), lambda k, c, bi: (bi, 0, k, c)),
        out_shape=jax.ShapeDtypeStruct((b, p, 2 * ka, d), BF16),
        scratch_shapes=[pltpu.VMEM((SUBLANES, 2 * p, dt), F32), pltpu.VMEM((SUBLANES, 2 * p, dt), BF16),
                        pltpu.VMEM((2 * SUBLANES, p, dt), F32)],
        compiler_params=_cparams("parallel", "parallel", "parallel"),
        name="fft_stage2",
    )(ar, ai, hr, hi, m2f, m2i)


def _s3_kernel(c_ref, g_ref, v_ref, x_ref, sk_ref, z_ref, y_s, *, group, pad_rows):
    zpad = jnp.zeros((pad_rows, c_ref.shape[-1]), BF16)
    for j in range(group):
        y_s[j] = _dot(g_ref[j], jnp.concatenate([c_ref[j], zpad], axis=0))
    y = jnp.swapaxes(y_s[...], 0, 1)
    v = v_ref[...].astype(F32)
    z_ref[...] = ((y + v * sk_ref[...]) * x_ref[...].astype(F32)).astype(z_ref.dtype)


def _fft_stage3(c, gc, v5, vpart, x5, xpart, skip, group, dt, out_dtype):
    b, p, kr, d = c.shape
    qh = v5.shape[2]
    kc = gc.shape[-1]
    pspec = lambda part: pl.BlockSpec((None, None, qh, group, dt), lambda bi, g, cc: (part, bi, 0, g, cc))
    return pl.pallas_call(
        functools.partial(_s3_kernel, group=group, pad_rows=kc - kr),
        grid=(b, p // group, d // dt),
        in_specs=[pl.BlockSpec((None, group, kr, dt), lambda bi, g, cc: (bi, g, 0, cc)),
                  pl.BlockSpec((group, qh, kc), lambda bi, g, cc: (g, 0, 0)),
                  pspec(vpart), pspec(xpart),
                  pl.BlockSpec((1, dt), lambda bi, g, cc: (0, cc))],
        out_specs=pspec(0),
        out_shape=jax.ShapeDtypeStruct((1, b, qh, p, d), out_dtype),
        scratch_shapes=[pltpu.VMEM((group, qh, dt), F32)],
        compiler_params=_cparams("parallel", "parallel", "parallel"),
        name="fft_stage3",
    )(c, gc, v5, x5, skip)


def _long_conv_lat(v5, vpart, x5, xpart, hr, hi, skip, plan, order, out_dtype):
    d = v5.shape[-1]
    dt1 = min(FFT_DT_OUTER, d)
    dt2 = min(FFT_DT_INNER, d)
    ar, ai = _fft_stage1(v5, vpart, plan["f1s"], plan["ka_pad"], FFT_GROUP, dt1, BF16, True, False)
    c = _fft_stage2(ar, ai, hr, hi, plan["m2f1"], plan["m2i1"], plan["ka_used"], order * d, dt2)
    return _fft_stage3(c, plan["g"], v5, vpart, x5, xpart, skip, FFT_GROUP, dt1, out_dtype)


def _filter_spectrum_lat(hy, seq, plan_f):
    fw1, fb1, ffreq, fw2, fb2, fwout = hy
    q = plan_f["q"]
    h_time, nrm = _hyena_filter_time(fw1, fb1, ffreq, fw2, fb2, fwout, seq, True)
    od = h_time.shape[-1]
    h5 = h_time.reshape(1, 1, FFT_P, q, od)
    dt = min(FFT_DT_OUTER, od)
    ar, ai = _fft_stage1(h5, 0, plan_f["f1"], plan_f["ka_pad"], SUBLANES, dt, F32, False, True)
    return _fft_stage2_filter(ar[0], ai[0], plan_f["m2f1"], 1.0 / nrm, dt)


def _dense_consts(seq):
    n = 2 * seq
    kf = seq + 1
    kf_pad = -(-kf // LANES) * LANES
    k = np.arange(kf)
    fwd_full = np.zeros((2 * kf_pad, n))
    ang = 2 * np.pi * np.outer(k, np.arange(n)) / n
    fwd_full[:kf] = np.cos(ang)
    fwd_full[kf_pad:kf_pad + kf] = -np.sin(ang)
    wgt = np.where((k == 0) | (k == seq), 1.0, 2.0) / n
    inv = np.zeros((seq, 2 * kf_pad))
    angi = 2 * np.pi * np.outer(np.arange(seq), k) / n
    inv[:, :kf] = wgt * np.cos(angi)
    inv[:, kf_pad:kf_pad + kf] = -wgt * np.sin(angi)
    return dict(kf_pad=kf_pad, fwd_full=_const3(fwd_full), fwd=_const3(fwd_full[:, :seq]), inv=_const3(inv))


def _dense_spec_kernel(h_ref, f_ref, inv_ref, hr_ref, hi_ref, *, kf_pad):
    y = _dot(f_ref[...], _stack3(h_ref[...])) * inv_ref[...]
    hr_ref[...] = y[:kf_pad]
    hi_ref[...] = y[kf_pad:]


def _filter_spectrum_ctx(hy, seq, cons):
    fw1, fb1, ffreq, fw2, fb2, fwout = hy
    h_time, nrm = _hyena_filter_time(fw1, fb1, ffreq, fw2, fb2, fwout, seq, False)
    od = h_time.shape[-1]
    n2 = 2 * seq
    h_time = h_time.reshape(n2, od)
    kf_pad = cons["kf_pad"]
    ct = min(FFT_DT_INNER, od)
    out = jax.ShapeDtypeStruct((kf_pad, od), F32)
    ospec = pl.BlockSpec((kf_pad, ct), lambda c: (0, c))
    return pl.pallas_call(
        functools.partial(_dense_spec_kernel, kf_pad=kf_pad),
        grid=(od // ct,),
        in_specs=[pl.BlockSpec((n2, ct), lambda c: (0, c)),
                  pl.BlockSpec((2 * kf_pad, 3 * n2), lambda c: (0, 0)),
                  pl.BlockSpec((1, ct), lambda c: (0, c))],
        out_specs=[ospec, ospec],
        out_shape=[out, out],
        compiler_params=_cparams("parallel"),
        name="dense_filter_spectrum",
    )(h_time, cons["fwd_full"], 1.0 / nrm)


def _dense_conv_kernel(v_ref, x1_ref, x2_ref, h1r_ref, h1i_ref, h2r_ref, h2i_ref, sk_ref,
                       f_ref, g_ref, z_ref, *, kf_pad):
    def conv(u, hr, hi):
        s = _dot(f_ref[...], _stack3(u))
        sr, si = s[:kf_pad], s[kf_pad:]
        y = jnp.concatenate([sr * hr - si * hi, sr * hi + si * hr], axis=0)
        return _dot(g_ref[...], _stack3(y))

    v = v_ref[...]
    z1 = x1_ref[...] * (conv(v, h1r_ref[...], h1i_ref[...]) + v * sk_ref[0:1, :])
    z2 = x2_ref[...] * (conv(z1, h2r_ref[...], h2i_ref[...]) + z1 * sk_ref[1:2, :])
    z_ref[...] = z2.astype(z_ref.dtype)


def _hyena_core_ctx(u3, hr, hi, fskip, cons, b, seq):
    d = u3.shape[-1]
    dt = min(MXU_N, d)
    nd = d // dt
    kf_pad = cons["kf_pad"]
    uspec = lambda part: pl.BlockSpec((None, seq, dt), lambda bi, c: (part, bi, c))
    hspec = lambda order: pl.BlockSpec((kf_pad, dt), lambda bi, c: (0, order * nd + c))
    return pl.pallas_call(
        functools.partial(_dense_conv_kernel, kf_pad=kf_pad),
        grid=(b, nd),
        in_specs=[uspec(0), uspec(1), uspec(2), hspec(0), hspec(0), hspec(1), hspec(1),
                  pl.BlockSpec((HYENA_ORDER, dt), lambda bi, c: (0, c)),
                  pl.BlockSpec((2 * kf_pad, 3 * seq), lambda bi, c: (0, 0)),
                  pl.BlockSpec((seq, 6 * kf_pad), lambda bi, c: (0, 0))],
        out_specs=pl.BlockSpec((seq, dt), lambda bi, c: (bi, c)),
        out_shape=jax.ShapeDtypeStruct((b * seq, d), BF16),
        compiler_params=_cparams("parallel", "parallel"),
        name="hyena_core_ctx",
    )(u3, u3, u3, hr, hi, hr, hi, fskip, cons["fwd"], cons["inv"])


def _rec_consts(chunk):
    t = np.arange(chunk)
    coefs, masks = [], []
    for direction in (0, 1):
        if direction == 0:
            rows = [t[None, :] <= t[:, None], t[None, :] > t[:, None]]
        else:
            rows = [t[None, :] >= t[:, None], t[None, :] < t[:, None]]
        mk = [np.eye(chunk)]
        m = chunk // 2
        while m >= 1:
            blk = t // (2 * m)
            half = (t // m) % 2
            mid = blk * 2 * m + m
            e = np.zeros((chunk, chunk))
            for r in range(chunk):
                if direction == 0:
                    if half[r] == 1:
                        e[r, mid[r]:r + 1] = 1
                    else:
                        e[r, r + 1:mid[r]] = 1
                else:
                    if half[r] == 0:
                        e[r, r:mid[r]] = 1
                    else:
                        e[r, mid[r]:r] = 1
            same = blk[:, None] == blk[None, :]
            if direction == 0:
                mk.append(same & (half[:, None] == 1) & (half[None, :] == 0))
            else:
                mk.append(same & (half[:, None] == 0) & (half[None, :] == 1))
            rows.append(e)
            m //= 2
        rows.append(np.ones((REC_TOTAL_ROWS, chunk)))
        a = np.concatenate([np.asarray(r, np.float64) for r in rows], axis=0)
        a3 = np.concatenate([a] * REC_GATE_TERMS, axis=1)
        pad = -(-a3.shape[1] // LANES) * LANES - a3.shape[1]
        a3 = np.pad(a3, ((0, 0), (0, pad)))
        coefs.append(a3)
        masks.append(np.stack([np.asarray(x, np.float32) for x in mk]))
    return (jnp.asarray(np.stack(coefs), dtype=BF16), jnp.asarray(np.stack(masks), dtype=F32))


def _rec_core(load_qkg, v_ref, coef_ref, mask_ref, s0_ref, o_ref, sfin_ref, q_s, k_s, st_s, ex_s, att_s,
              *, chunk, nsub, heads, dk, dv):
    di = pl.program_id(1)
    c = pl.program_id(2)
    levels = int(math.log2(chunk))
    tot = (2 + levels) * chunk

    @pl.when(c == 0)
    def _():
        st_s[...] = s0_ref[...]

    offs = [pl.multiple_of((s + di * (nsub - 1 - 2 * s)) * chunk, chunk) for s in range(nsub)]

    for s in range(nsub):
        q, k, g = load_qkg(pl.ds(offs[s], chunk))
        q_s[s] = q
        k_s[s] = k
        terms, rest = [], g
        for _ in range(REC_GATE_TERMS):
            terms.append(rest.astype(BF16))
            rest = rest - terms[-1].astype(F32)
        pad = coef_ref.shape[-1] - REC_GATE_TERMS * chunk
        if pad:
            terms.append(jnp.zeros((pad, g.shape[1]), BF16))
        ex_s[s] = jnp.exp(_dot(coef_ref[...], jnp.concatenate(terms, axis=0)))

    for s in range(nsub):
        for h in range(heads):
            ks = slice(h * dk, (h + 1) * dk)
            qh = q_s[s, :, ks]
            kh = k_s[s, :, ks]
            att = mask_ref[0] * _dot_nt(qh.astype(BF16), kh.astype(BF16))
            for lv in range(levels):
                e = ex_s[s, (2 + lv) * chunk:(3 + lv) * chunk, ks]
                att += mask_ref[1 + lv] * _dot_nt((qh * e).astype(BF16), (kh * e).astype(BF16))
            att_s[s, h] = att.astype(BF16)

    for s in range(nsub):
        rows = pl.ds(offs[s], chunk)
        for h in range(heads):
            ks = slice(h * dk, (h + 1) * dk)
            vs = slice(h * dv, (h + 1) * dv)
            vh = v_ref[rows, vs].astype(BF16)
            st = st_s[h]
            o = _dot(att_s[s, h], vh)
            o += _dot_nt((q_s[s, :, ks] * ex_s[s, 0:chunk, ks]).astype(BF16), st.astype(BF16))
            o_ref[rows, vs] = o.astype(o_ref.dtype)
            kd = (k_s[s, :, ks] * ex_s[s, chunk:2 * chunk, ks]).astype(BF16)
            st_s[h] = st * ex_s[s, tot:tot + 1, ks] + _dot_tn(vh, kd)

    @pl.when(c == pl.num_programs(2) - 1)
    def _():
        sfin_ref[...] = st_s[...]


def _hgrn_rec_kernel(q_ref, v_ref, f_ref, lb_ref, coef_ref, mask_ref, s0_ref, o_ref, sfin_ref,
                     q_s, k_s, st_s, ex_s, att_s, **kw):
    log_lb = lb_ref[0:1, :]
    log_1mlb = lb_ref[1:2, :]

    def load_qkg(rows):
        qr = q_ref[rows, :]
        fr = f_ref[rows, :]
        t2 = log_1mlb + jnp.minimum(fr, 0.0) - jnp.log(1.0 + jnp.exp(-jnp.abs(fr)))
        mx = jnp.maximum(log_lb, t2)
        g = mx + jnp.log(1.0 + jnp.exp(-jnp.abs(log_lb - t2)))
        return qr * _sigmoid(qr), 1.0 - jnp.exp(g), g

    _rec_core(load_qkg, v_ref, coef_ref, mask_ref, s0_ref, o_ref, sfin_ref, q_s, k_s, st_s, ex_s, att_s, **kw)


def _gla_rec_kernel(q_ref, k_ref, v_ref, a_ref, wup_ref, bup_ref, coef_ref, mask_ref, s0_ref,
                    o_ref, sfin_ref, q_s, k_s, st_s, ex_s, att_s, *, qscale, **kw):
    def load_qkg(rows):
        xg = _dot(a_ref[rows, :].astype(BF16), wup_ref[...]) + bup_ref[...]
        g = (jnp.minimum(xg, 0.0) - jnp.log(1.0 + jnp.exp(-jnp.abs(xg)))) * (1.0 / GLA_GATE_NORM)
        return q_ref[rows, :] * qscale, k_ref[rows, :], g

    _rec_core(load_qkg, v_ref, coef_ref, mask_ref, s0_ref, o_ref, sfin_ref, q_s, k_s, st_s, ex_s, att_s, **kw)


def _rec_call(kind, proj, extra, s0, consts, b, seq, d, heads, dk, dv):
    coef, masks = consts
    chunk = masks.shape[-1]
    rows = REC_STEP_ROWS
    nsub = rows // chunk
    nc = seq // rows
    hk = heads * dk
    hv = heads * dv
    levels = int(math.log2(chunk))
    n_rows = (2 + levels) * chunk + REC_TOTAL_ROWS

    def rmap(col):
        return lambda bi, di, c: (bi * nc + c + di * (nc - 1 - 2 * c), col)

    cspecs = [pl.BlockSpec((None, n_rows, coef.shape[-1]), lambda bi, di, c: (di, 0, 0)),
              pl.BlockSpec((None, levels + 1, chunk, chunk), lambda bi, di, c: (di, 0, 0, 0)),
              pl.BlockSpec((None, None, heads, dv, dk), lambda bi, di, c: (bi, di, 0, 0, 0))]
    if kind == "hgrn":
        lb = extra
        kernel = _hgrn_rec_kernel
        in_specs = [pl.BlockSpec((rows, d), rmap(0)),
                    pl.BlockSpec((rows, d), rmap(1)),
                    pl.BlockSpec((rows, d), lambda bi, di, c: (bi * nc + c + di * (nc - 1 - 2 * c), 3 + di)),
                    pl.BlockSpec((None, 2, d), lambda bi, di, c: (di, 0, 0))] + cspecs
        args = (proj, proj, proj, lb, coef, masks, s0)
        kw = {}
    else:
        wup, bup = extra
        kernel = _gla_rec_kernel
        in_specs = [pl.BlockSpec((rows, hk), rmap(0)),
                    pl.BlockSpec((rows, hk), rmap(1)),
                    pl.BlockSpec((rows, hv), rmap(1)),
                    pl.BlockSpec((rows, LANES), rmap((2 * hk + 2 * hv) // LANES)),
                    pl.BlockSpec((None, LANES, hk), lambda bi, di, c: (di, 0, 0)),
                    pl.BlockSpec((None, 1, hk), lambda bi, di, c: (di, 0, 0))] + cspecs
        args = (proj, proj, proj, proj, wup, bup, coef, masks, s0)
        kw = dict(qscale=float(dk) ** -0.5)
    return pl.pallas_call(
        functools.partial(kernel, chunk=chunk, nsub=nsub, heads=heads, dk=dk, dv=dv, **kw),
        grid=(b, 2, nc),
        in_specs=in_specs,
        out_specs=[pl.BlockSpec((None, rows, hv), lambda bi, di, c: (di, bi * nc + c + di * (nc - 1 - 2 * c), 0)),
                   pl.BlockSpec((None, None, heads, dv, dk), lambda bi, di, c: (bi, di, 0, 0, 0))],
        out_shape=[jax.ShapeDtypeStruct((2, b * seq, hv), BF16),
                   jax.ShapeDtypeStruct((b, 2, heads, dv, dk), F32)],
        scratch_shapes=[pltpu.VMEM((nsub, chunk, hk), F32), pltpu.VMEM((nsub, chunk, hk), F32),
                        pltpu.VMEM((heads, dv, dk), F32), pltpu.VMEM((nsub, n_rows, hk), F32),
                        pltpu.VMEM((nsub, heads, chunk, chunk), BF16)],
        compiler_params=_cparams("parallel", "parallel", "arbitrary"),
        name=kind + "_recurrence",
    )(*args)


def _gated_out_kernel(o_ref, gate_ref, gn_ref, w_ref, x_ref, mg_ref, out_ref, *, heads):
    d = x_ref.shape[1]
    dh = d // heads
    kc = max(dh, 2 * MXU_N)
    acc = None
    for k0 in range(0, d, kc):
        parts = []
        for h0 in range(k0, k0 + kc, dh):
            seg = o_ref[0, :, h0:h0 + dh].astype(F32) + o_ref[1, :, h0:h0 + dh].astype(F32)
            ms = jnp.mean(seg * seg, axis=-1, keepdims=True)
            parts.append(seg * lax.rsqrt(ms + RMS_EPS))
        gate = gate_ref[:, k0:k0 + kc]
        y = jnp.concatenate(parts, axis=1) * gn_ref[:, k0:k0 + kc] * (gate * _sigmoid(gate))
        part = _dot(y.astype(BF16), w_ref[k0:k0 + kc, :])
        acc = part if acc is None else acc + part
    out_ref[...] = x_ref[...] + mg_ref[...] * acc


def _gated_out(o2, proj, gate_col, gn, w, x, mg, bmap, heads, tm):
    m, d = x.shape
    return pl.pallas_call(
        functools.partial(_gated_out_kernel, heads=heads),
        grid=(m // tm,),
        in_specs=[pl.BlockSpec((2, tm, d), lambda i: (0, i, 0)),
                  pl.BlockSpec((tm, d), lambda i: (i, gate_col)),
                  pl.BlockSpec((1, d), lambda i: (0, 0)),
                  pl.BlockSpec((d, d), lambda i: (0, 0)),
                  pl.BlockSpec((tm, d), lambda i: (i, 0)),
                  _mod_spec(d, bmap)],
        out_specs=pl.BlockSpec((tm, d), lambda i: (i, 0)),
        out_shape=jax.ShapeDtypeStruct((m, d), F32),
        compiler_params=_cparams("parallel"),
        name="gated_out",
    )(o2, proj, gn, w, x, mg)


def _recurrent_mix(kind, x_ctx, x_lat, norm, ctx_map, w_in, extra, consts, b, lc, seq, d, heads, dk, dv, tn):
    g1, sh, sc = norm
    tm_c, tm_l = min(TM_PROJ, x_ctx.shape[0]), min(TM_PROJ, x_lat.shape[0])
    proj_ctx = _projection(x_ctx, g1, sh, sc, w_in, ctx_map, tm_c, tn)
    proj_lat = _projection(x_lat, g1, sh, sc, w_in, lambda i: (i * tm_l) // seq, tm_l, tn)
    s0 = jnp.zeros((b, 2, heads, dv, dk), F32)
    o_ctx, s_ctx = _rec_call(kind, proj_ctx, extra, s0, consts, b, lc, d, heads, dk, dv)
    o_lat, _ = _rec_call(kind, proj_lat, extra, s_ctx, consts, b, seq, d, heads, dk, dv)
    return (proj_ctx, o_ctx), (proj_lat, o_lat)


def kernel(x, c, ctx, c_ctx, w_mod, b_mod, norm1_g, norm2_g, w_ffn_in, w_ffn_out, final_g, hy_w_in, hy_conv_w, hy_fw1, hy_fb1, hy_ffreq, hy_fw2, hy_fb2, hy_fwout, hy_fskip, hy_w_out, hg_w_in, hg_lb_logits, hg_onorm_g, hg_w_out, gla_w_in, gla_w_up, gla_b_up, gla_onorm_g, gla_w_out):
    b, seq, d = x.shape
    lc = ctx.shape[1]
    depth = w_mod.shape[0]
    assert b + 1 <= MOD_ROWS and seq % (FFT_P * 2) == 0 and FFT_P % GRID_W == 0
    m_lat, m_ctx = b * seq, b * lc
    ctx_row = b
    tm_lat, tm_ctx = TM_ROWS, min(TM_ROWS, m_ctx)
    tm_in = TM_PROJ
    lat_map = lambda i: (i * tm_lat) // seq
    lat_map_in = lambda i: (i * tm_in) // seq
    ctx_map = lambda i: ctx_row

    xl = x.reshape(m_lat, d)
    xc = ctx.reshape(m_ctx, d)
    c8 = jnp.zeros((MOD_ROWS, d), F32).at[:b].set(c).at[ctx_row].set(c_ctx)
    mod = _modulation(c8, w_mod, b_mod).reshape(depth, MOD_ROWS, N_MOD, 1, d)

    plan = _fft_consts(seq, seq // FFT_P)
    plan_f = _fft_consts(seq, 2 * seq // FFT_P)
    dense = _dense_consts(lc)
    rec_consts = {1: _rec_consts(HGRN_CHUNK), 2: _rec_consts(GLA_CHUNK)}
    qh = plan["qh"]

    wb_ffn_in, wb_ffn_out = _to_bf16(w_ffn_in), _to_bf16(w_ffn_out)
    wb_hy_in, wb_hy_out = _to_bf16(hy_w_in), _to_bf16(hy_w_out)
    wb_hg_in, wb_hg_out = _to_bf16(hg_w_in), _to_bf16(hg_w_out)
    wb_gla_in, wb_gla_out = _to_bf16(gla_w_in), _to_bf16(gla_w_out)

    for i in range(depth):
        last = i == depth - 1
        kind, j = i % N_MIXERS, i // N_MIXERS
        mv = [mod[i, :, k] for k in range(N_MOD)]
        g1 = norm1_g[i][None, :]
        need_ctx = (not last) or kind != 0
        if kind == 0:
            w_in, w_out = wb_hy_in[j], wb_hy_out[j]
            hy = (hy_fw1[j], hy_fb1[j], hy_ffreq[j], hy_fw2[j], hy_fb2[j], hy_fwout[j])
            u = _hyena_in(xl, g1, mv[0], mv[1], w_in, hy_conv_w[j], lat_map_in,
                          tm_in, min(TN_HYENA_IN, d), GRID_W, BF16).reshape(3, b, qh, FFT_P, d)
            hr, hi = _filter_spectrum_lat(hy, seq, plan_f)
            z1 = _long_conv_lat(u, 0, u, 1, hr, hi, hy_fskip[j, 0][None, :], plan, 0, BF16)
            z2 = _long_conv_lat(z1, 0, u, 2, hr, hi, hy_fskip[j, 1][None, :], plan, 1, BF16)
            xl = _out_res(z2.reshape(m_lat, d), w_out, xl, mv[2], lat_map, tm_lat)
            if need_ctx:
                uc = _hyena_in(xc, g1, mv[0], mv[1], w_in, hy_conv_w[j], ctx_map, lc, min(TN_PROJ, d), lc, F32)
                hcr, hci = _filter_spectrum_ctx(hy, lc, dense)
                zc = _hyena_core_ctx(uc, hcr, hci, hy_fskip[j], dense, b, lc)
                xc = _out_res(zc, w_out, xc, mv[2], ctx_map, tm_ctx)
        else:
            if kind == 1:
                heads = d // HGRN_EXPAND
                dk = dv = HGRN_EXPAND
                w_in = wb_hg_in[j]
                lb_cum = jnp.cumsum(jax.nn.softmax(hg_lb_logits.astype(F32), axis=1), axis=1)
                lb = lb_cum[:, i] - lb_cum[:, 0]
                extra = jnp.stack([jnp.log(lb), jnp.log1p(-lb)], axis=1)
                gn, w_out, tn = hg_onorm_g[j], wb_hg_out[j], TN_PROJ
                rkind = "hgrn"
            else:
                heads = GLA_HEADS
                dk, dv = d // 2 // heads, d // heads
                n_in = gla_w_in.shape[-1]
                n_pad = -(-n_in // LANES) * LANES
                w_in = jnp.pad(wb_gla_in[j], ((0, 0), (0, n_pad - n_in)))
                r = GLA_GATE_RANK
                wup = jnp.zeros((2, LANES, heads * dk), F32)
                wup = wup.at[0, :r].set(gla_w_up[j, 0]).at[1, r:2 * r].set(gla_w_up[j, 1]).astype(BF16)
                extra = (wup, gla_b_up[j][:, None, :])
                gn, w_out = gla_onorm_g[j], wb_gla_out[j]
                tn = _lane_tile(n_pad, TN_PROJ)
                rkind = "gla"
            (p_ctx, o_ctx), (p_lat, o_lat) = _recurrent_mix(
                rkind, xc, xl, (g1, mv[0], mv[1]), ctx_map, w_in, extra, rec_consts[kind],
                b, lc, seq, d, heads, dk, dv, tn)
            gate_col = 2 if kind == 1 else (2 * heads * dk + heads * dv) // d
            xl = _gated_out(o_lat, p_lat, gate_col, gn[None, :], w_out, xl, mv[2], lat_map, heads, tm_lat)
            xc = _gated_out(o_ctx, p_ctx, gate_col, gn[None, :], w_out, xc, mv[2], ctx_map, heads, tm_ctx)
        g2 = norm2_g[i][None, :]
        fg = final_g[None, :]
        xl = _ffn(xl, g2, mv[3], mv[4], mv[5], wb_ffn_in, wb_ffn_out, i, fg, last, lat_map, tm_lat, TF_FFN)
        if not last:
            xc = _ffn(xc, g2, mv[3], mv[4], mv[5], wb_ffn_in, wb_ffn_out, i, fg, False, ctx_map, tm_ctx, TF_FFN)
    return xl.reshape(b, seq, d)
```

```python
import functools
import math

import ml_dtypes
import numpy as np
import jax
import jax.numpy as jnp
from jax import lax
from jax.experimental import pallas as pl
from jax.experimental.pallas import tpu as pltpu

F32 = jnp.float32
BF16 = jnp.bfloat16

N_MOD = 6
N_MIXERS = 3
RMS_EPS = 1e-6
GRID_W = 64
HYENA_ORDER = 2
FILTER_BANDS = 16
FILTER_EMB = 1 + 2 * FILTER_BANDS
HYENA_DECAY_MIN = math.log(1e-2) / 1.5
HYENA_DECAY_MAX = math.log(1e-2) / 0.3
HGRN_EXPAND = 128
GLA_HEADS = 4
GLA_GATE_RANK = 16
GLA_GATE_NORM = 16.0

LANES = 128
SUBLANES = 8
MXU_N = 256
V7X_VMEM_LIMIT = 56 * 1024 * 1024

FFT_P = 128
FFT_GROUP = 4 * SUBLANES
FFT_DT_OUTER = 1024
FFT_DT_INNER = 512
REC_STEP_ROWS = 256
HGRN_CHUNK = 64
GLA_CHUNK = 128
REC_GATE_TERMS = 2
REC_TOTAL_ROWS = 2 * SUBLANES
MOD_ROWS = 8

TM_ROWS = 512
TM_PROJ = 1024
TN_PROJ = 1024
TN_GLA_PROJ = 5 * MXU_N
TN_HYENA_IN = 1024
TF_FFN = 512
TN_MOD = 1024


def _cparams(*sem):
    return pltpu.CompilerParams(dimension_semantics=sem, vmem_limit_bytes=V7X_VMEM_LIMIT)


def _dot(a, b):
    return jnp.dot(a, b, preferred_element_type=F32)


def _dot_nt(a, b):
    return lax.dot_general(a, b, (((1,), (1,)), ((), ())), preferred_element_type=F32)


def _dot_tn(a, b):
    return lax.dot_general(a, b, (((0,), (0,)), ((), ())), preferred_element_type=F32)


def _split(x):
    hi = x.astype(BF16)
    lo = (x - hi.astype(F32)).astype(BF16)
    return hi, lo


def _stack3(x, pad_rows=0):
    hi, lo = _split(x)
    parts = [hi, lo, hi]
    if pad_rows:
        parts.append(jnp.zeros((pad_rows, x.shape[1]), BF16))
    return jnp.concatenate(parts, axis=0)


def _const3(c, pad_cols=0):
    hi = c.astype(ml_dtypes.bfloat16)
    lo = (c - hi.astype(np.float64)).astype(ml_dtypes.bfloat16)
    parts = [hi, hi, lo]
    if pad_cols:
        parts.append(np.zeros(c.shape[:-1] + (pad_cols,), ml_dtypes.bfloat16))
    return jnp.asarray(np.concatenate(parts, axis=-1))


def _dot3(a, bh, bl):
    ah, al = _split(a)
    return _dot(ah, bh) + _dot(ah, bl) + _dot(al, bh)


def _sigmoid(x):
    return jax.nn.sigmoid(x)


def _normmod(x, g, sh, sc):
    ms = jnp.mean(x * x, axis=-1, keepdims=True)
    y = x * lax.rsqrt(ms + RMS_EPS) * g
    return y * (1.0 + sc) + sh


def _mod_kernel(c_ref, w_ref, b_ref, o_ref):
    c = c_ref[...]
    s = (c * _sigmoid(c)).astype(BF16)
    o_ref[...] = _dot(s, w_ref[...].astype(BF16)) + b_ref[...]


def _modulation(c8, w_mod, b_mod):
    depth, d, n = w_mod.shape
    tn = TN_MOD
    return pl.pallas_call(
        _mod_kernel,
        grid=(depth, n // tn),
        in_specs=[pl.BlockSpec((MOD_ROWS, d), lambda l, j: (0, 0)),
                  pl.BlockSpec((None, d, tn), lambda l, j: (l, 0, j)),
                  pl.BlockSpec((None, 1, tn), lambda l, j: (l, 0, j))],
        out_specs=pl.BlockSpec((None, MOD_ROWS, tn), lambda l, j: (l, 0, j)),
        out_shape=jax.ShapeDtypeStruct((depth, MOD_ROWS, n), F32),
        compiler_params=_cparams("parallel", "parallel"),
        name="modulation",
    )(c8, w_mod, b_mod.reshape(depth, 1, n))


def _mod_spec(d, bmap):
    return pl.BlockSpec((None, 1, d), lambda *idx: (bmap(idx[0]), 0, 0))


def _lane_tile(n, cap):
    best = n
    for t in range(LANES, min(n, cap) + 1, LANES):
        if n % t == 0:
            best = t
    return best


def _cast_kernel(x_ref, o_ref):
    o_ref[...] = x_ref[...].astype(o_ref.dtype)


def _to_bf16(w):
    shape = w.shape
    w2 = w.reshape(-1, shape[-1])
    r, c = w2.shape
    tr, tc = min(TM_ROWS, r), _lane_tile(c, 2 * TN_PROJ)
    out = pl.pallas_call(
        _cast_kernel,
        grid=(r // tr, c // tc),
        in_specs=[pl.BlockSpec((tr, tc), lambda i, j: (i, j))],
        out_specs=pl.BlockSpec((tr, tc), lambda i, j: (i, j)),
        out_shape=jax.ShapeDtypeStruct((r, c), BF16),
        compiler_params=_cparams("parallel", "parallel"),
        name="cast_bf16",
    )(w2)
    return out.reshape(shape)


def _proj_kernel(x_ref, g_ref, sh_ref, sc_ref, w_ref, o_ref, h_s):
    @pl.when(pl.program_id(1) == 0)
    def _():
        h_s[...] = _normmod(x_ref[...], g_ref[...], sh_ref[...], sc_ref[...]).astype(BF16)

    o_ref[...] = _dot(h_s[...], w_ref[...])


def _projection(x, g, sh, sc, w, bmap, tm, tn):
    m, d = x.shape
    n = w.shape[1]
    return pl.pallas_call(
        _proj_kernel,
        grid=(m // tm, n // tn),
        in_specs=[pl.BlockSpec((tm, d), lambda i, j: (i, 0)),
                  pl.BlockSpec((1, d), lambda i, j: (0, 0)),
                  _mod_spec(d, bmap), _mod_spec(d, bmap),
                  pl.BlockSpec((d, tn), lambda i, j: (0, j))],
        out_specs=pl.BlockSpec((tm, tn), lambda i, j: (i, j)),
        out_shape=jax.ShapeDtypeStruct((m, n), F32),
        scratch_shapes=[pltpu.VMEM((tm, d), BF16)],
        compiler_params=_cparams("parallel", "arbitrary"),
        name="projection",
    )(x, g, sh, sc, w)


def _out_res_kernel(a_ref, w_ref, x_ref, mg_ref, o_ref):
    o_ref[...] = x_ref[...] + mg_ref[...] * _dot(a_ref[...], w_ref[...])


def _ffn_kernel(x_ref, g_ref, sh_ref, sc_ref, mg_ref, wg_ref, wu_ref, wo_ref, fg_ref, o_ref, h_s, acc_s,
                *, final_norm):
    j = pl.program_id(1)

    @pl.when(j == 0)
    def _():
        h_s[...] = _normmod(x_ref[...], g_ref[...], sh_ref[...], sc_ref[...]).astype(BF16)
        acc_s[...] = jnp.zeros_like(acc_s)

    h = h_s[...]
    a = _dot(h, wg_ref[...])
    u = _dot(h, wu_ref[...])
    act = (a * _sigmoid(a) * u).astype(BF16)
    acc_s[...] += _dot(act, wo_ref[...])

    @pl.when(j == pl.num_programs(1) - 1)
    def _():
        y = x_ref[...] + mg_ref[...] * acc_s[...]
        if final_norm:
            y = y * lax.rsqrt(jnp.mean(y * y, axis=-1, keepdims=True) + RMS_EPS) * fg_ref[...]
        o_ref[...] = y


def _ffn(x, g, sh, sc, mg, w_in, w_out, layer, final_g, final_norm, bmap, tm, tf):
    m, d = x.shape
    f = w_out.shape[1]
    nf = f // tf
    return pl.pallas_call(
        functools.partial(_ffn_kernel, final_norm=final_norm),
        grid=(m // tm, nf),
        in_specs=[pl.BlockSpec((tm, d), lambda i, j: (i, 0)),
                  pl.BlockSpec((1, d), lambda i, j: (0, 0)),
                  _mod_spec(d, bmap), _mod_spec(d, bmap), _mod_spec(d, bmap),
                  pl.BlockSpec((None, d, tf), lambda i, j: (layer, 0, j)),
                  pl.BlockSpec((None, d, tf), lambda i, j: (layer, 0, nf + j)),
                  pl.BlockSpec((None, tf, d), lambda i, j: (layer, j, 0)),
                  pl.BlockSpec((1, d), lambda i, j: (0, 0))],
        out_specs=pl.BlockSpec((tm, d), lambda i, j: (i, 0)),
        out_shape=jax.ShapeDtypeStruct((m, d), F32),
        scratch_shapes=[pltpu.VMEM((tm, d), BF16), pltpu.VMEM((tm, d), F32)],
        compiler_params=_cparams("parallel", "arbitrary"),
        name="ffn",
    )(x, g, sh, sc, mg, w_in, w_in, w_out, final_g)


def _conv3_rows(acc, cw, row_len):
    tm = acc.shape[0]
    rid = lax.broadcasted_iota(jnp.int32, (tm, 1), 0) % row_len
    up = jnp.where(rid == 0, 0.0, pltpu.roll(acc, 1, 0))
    dn = jnp.where(rid == row_len - 1, 0.0, pltpu.roll(acc, tm - 1, 0))
    return cw[0:1] * up + cw[1:2] * acc + cw[2:3] * dn


def _hy_in_kernel(x_ref, g_ref, sh_ref, sc_ref, w_ref, cw_ref, o_ref, h_s, *, row_len):
    @pl.when(pl.program_id(1) == 0)
    def _():
        h_s[...] = _normmod(x_ref[...], g_ref[...], sh_ref[...], sc_ref[...]).astype(BF16)

    h = h_s[...]
    for n0 in range(0, o_ref.shape[1], MXU_N):
        ns = slice(n0, n0 + MXU_N)
        o_ref[:, ns] = _conv3_rows(_dot(h, w_ref[:, ns]), cw_ref[:, ns], row_len).astype(o_ref.dtype)


def _hyena_in(x, g, sh, sc, w, cw, bmap, tm, tn, row_len, out_dtype):
    m, d = x.shape
    nd = d // tn
    return pl.pallas_call(
        functools.partial(_hy_in_kernel, row_len=row_len),
        grid=(m // tm, 3 * nd),
        in_specs=[pl.BlockSpec((tm, d), lambda i, j: (i, 0)),
                  pl.BlockSpec((1, d), lambda i, j: (0, 0)),
                  _mod_spec(d, bmap), _mod_spec(d, bmap),
                  pl.BlockSpec((d, tn), lambda i, j: (0, j)),
                  pl.BlockSpec((3, tn), lambda i, j: (0, j))],
        out_specs=pl.BlockSpec((None, tm, tn), lambda i, j: (j // nd, i, j % nd)),
        out_shape=jax.ShapeDtypeStruct((3, m, d), out_dtype),
        scratch_shapes=[pltpu.VMEM((tm, d), BF16)],
        compiler_params=_cparams("parallel", "arbitrary"),
        name="hyena_in",
    )(x, g, sh, sc, w, cw)


def _out_res(a, w, x, mg, bmap, tm):
    m, d = x.shape
    return pl.pallas_call(
        _out_res_kernel,
        grid=(m // tm,),
        in_specs=[pl.BlockSpec((tm, d), lambda i: (i, 0)),
                  pl.BlockSpec((d, d), lambda i: (0, 0)),
                  pl.BlockSpec((tm, d), lambda i: (i, 0)),
                  _mod_spec(d, bmap)],
        out_specs=pl.BlockSpec((tm, d), lambda i: (i, 0)),
        out_shape=jax.ShapeDtypeStruct((m, d), F32),
        compiler_params=_cparams("parallel"),
        name="out_residual",
    )(a, w, x, mg)


def _filter_kernel(z_ref, w1h_ref, w1l_ref, b1_ref, fq_ref, w2h_ref, w2l_ref, b2_ref,
                   wo_ref, dl_ref, h_ref, nrm_ref):
    z = z_ref[...]
    t = z[:, 0:1]
    valid = z[:, FILTER_EMB:FILTER_EMB + 1]
    a1 = _dot3(z, w1h_ref[...], w1l_ref[...]) + b1_ref[...]
    hid = jnp.sin(fq_ref[0:1, :] * a1)
    a2 = _dot3(hid, w2h_ref[...], w2l_ref[...]) + b2_ref[...]
    hid = jnp.sin(fq_ref[1:2, :] * a2)
    hh, hl = _split(hid)
    h = _dot(jnp.concatenate([hh, hl], axis=1), wo_ref[...]) * jnp.exp(-t * dl_ref[...]) * valid
    h_ref[...] = h.reshape(h_ref.shape)

    @pl.when((pl.program_id(1) == 0) & (pl.program_id(2) == 0))
    def _():
        nrm_ref[...] = jnp.zeros_like(nrm_ref)

    nrm_ref[...] += jnp.sum(jnp.abs(h), axis=0, keepdims=True)


def _pad2(a, rows, cols):
    return jnp.pad(a, ((0, rows - a.shape[0]), (0, cols - a.shape[1])))


def _filter_positions(seq, perm):
    if perm:
        r_hi = seq // FFT_P
        gt, ng = 2 * SUBLANES, FFT_P // (2 * SUBLANES)
        dirs, g, tl, th = np.meshgrid(np.arange(2), np.arange(ng), np.arange(gt), np.arange(r_hi), indexing="ij")
        n = FFT_P * (r_hi * dirs + th) + g * gt + tl
    else:
        r_hi, gt, ng = seq, 1, 1
        dirs, g, tl, th = np.meshgrid(np.arange(2), np.arange(1), np.arange(1), np.arange(seq), indexing="ij")
        n = seq * dirs + th
    n = n.reshape(-1)
    pos = np.where(n < seq, n, 2 * seq - 1 - n).astype(np.float32)
    z = np.zeros((n.size, LANES), np.float32)
    z[:, 0] = pos / np.float32(max(seq - 1, 1))
    bands = np.arange(1, FILTER_BANDS + 1, dtype=np.float32)
    ang = (np.float32(2.0 * math.pi / seq) * pos)[:, None] * bands[None, :]
    z[:, 1:1 + FILTER_BANDS] = np.cos(ang.astype(np.float64))
    z[:, 1 + FILTER_BANDS:FILTER_EMB] = -np.sin(ang.astype(np.float64))
    z[:, FILTER_EMB] = n != seq
    return jnp.asarray(z), gt, ng, r_hi


def _hyena_filter_time(fw1, fb1, ffreq, fw2, fb2, fwout, seq, perm):
    od = fwout.shape[1] // 2
    d = od // HYENA_ORDER
    z, gt, ng, r_hi = _filter_positions(seq, perm)
    rows = gt * r_hi
    ct = min(2 * FFT_DT_OUTER, od)
    hidden = LANES
    w1h, w1l = _split(_pad2(fw1, LANES, hidden))
    w2h, w2l = _split(_pad2(fw2, hidden, hidden))
    wo = _pad2(fwout, hidden, fwout.shape[1]).astype(BF16)
    wo2 = jnp.concatenate([wo, wo], axis=0)
    b1 = _pad2(fb1[None, :], 1, hidden)
    b2 = _pad2(fb2[None, :], 1, hidden)
    fq = _pad2(ffreq, 2, hidden)
    deltas = np.abs(np.linspace(HYENA_DECAY_MIN, HYENA_DECAY_MAX, d, dtype=np.float32))
    dl = jnp.asarray(np.tile(deltas, HYENA_ORDER)[None, :])
    nct = od // ct
    small = lambda shape: pl.BlockSpec(shape, lambda c, di, g: (0, 0))
    wspec = pl.BlockSpec((2 * hidden, ct), lambda c, di, g: (0, di * nct + c))
    return pl.pallas_call(
        _filter_kernel,
        grid=(nct, 2, ng),
        in_specs=[pl.BlockSpec((rows, LANES), lambda c, di, g: (di * ng + g, 0)),
                  small((LANES, hidden)), small((LANES, hidden)), small((1, hidden)), small((2, hidden)),
                  small((hidden, hidden)), small((hidden, hidden)), small((1, hidden)),
                  wspec,
                  pl.BlockSpec((1, ct), lambda c, di, g: (0, c))],
        out_specs=[pl.BlockSpec((None, gt, None, r_hi, ct), lambda c, di, g: (g, 0, di, 0, c)),
                   pl.BlockSpec((1, ct), lambda c, di, g: (0, c))],
        out_shape=[jax.ShapeDtypeStruct((ng, gt, 2, r_hi, od), F32), jax.ShapeDtypeStruct((1, od), F32)],
        compiler_params=_cparams("parallel", "arbitrary", "arbitrary"),
        name="hyena_filter",
    )(z, w1h, w1l, b1, fq, w2h, w2l, b2, wo2, dl)


def _fft_consts(seq, k_in):
    n = 2 * seq
    p = FFT_P
    q = n // p
    qh = q // 2
    ka_used = qh + 1
    ka_pad = -(-ka_used // SUBLANES) * SUBLANES
    ka = np.arange(ka_used)
    tl = np.arange(p)
    th = np.arange(k_in)
    theta = 2 * np.pi * (ka[None, :, None] * th[None, None, :] / q + ka[None, :, None] * tl[:, None, None] / n)
    f1 = np.zeros((p, 2 * ka_pad, k_in))
    f1[:, :ka_used] = np.cos(theta)
    f1[:, ka_pad:ka_pad + ka_used] = -np.sin(theta)
    k3 = 3 * k_in
    f1c = _const3(f1, pad_cols=-(-k3 // LANES) * LANES - k3)
    f1s = jnp.asarray(np.pad(f1, ((0, 0), (0, 0), (0, -(-k_in // LANES) * LANES - k_in))).astype(ml_dtypes.bfloat16))
    ang = 2 * np.pi * np.outer(np.arange(p), np.arange(p)) / p
    cc, sc = np.cos(ang), np.sin(ang)
    m2f_np, m2i_np = np.block([[cc, sc], [-sc, cc]]), np.block([[cc, -sc], [sc, cc]])
    m2f1, m2i1 = (jnp.asarray(m.astype(ml_dtypes.bfloat16)) for m in (m2f_np, m2i_np))
    tho = np.arange(qh)
    phi = 2 * np.pi * (tho[None, :, None] * ka[None, None, :] / q + ka[None, None, :] * tl[:, None, None] / n)
    wgt = np.where((ka == 0) | (ka == qh), 1.0, 2.0) / n
    kr = 2 * ka_pad
    g = np.zeros((p, qh, -(-kr // LANES) * LANES))
    col = 2 * SUBLANES * (ka // SUBLANES) + ka % SUBLANES
    g[:, :, col] = wgt * np.cos(phi)
    g[:, :, col + SUBLANES] = -wgt * np.sin(phi)
    gc = jnp.asarray(g.astype(ml_dtypes.bfloat16))
    return dict(q=q, qh=qh, ka_used=ka_used, ka_pad=ka_pad, f1=f1c, f1s=f1s, m2f1=m2f1, m2i1=m2i1, g=gc)


def _s1_kernel(x_ref, f_ref, ar_ref, ai_ref, a_s, *, group, ka_pad, pad_rows, natural, split):
    x = x_ref[...].astype(F32)
    if natural:
        x = jnp.swapaxes(x, 0, 1)
    zpad = None if split else jnp.zeros((pad_rows, x.shape[-1]), BF16)
    for j in range(group):
        xx = _stack3(x[j], pad_rows) if split else jnp.concatenate([x[j].astype(BF16), zpad], axis=0)
        a_s[j] = _dot(f_ref[j], xx)
    a = jnp.swapaxes(a_s[...], 0, 1)
    ar_ref[...] = a[:ka_pad].astype(ar_ref.dtype)
    ai_ref[...] = a[ka_pad:].astype(ai_ref.dtype)


def _fft_stage1(x5, part, f1c, ka_pad, group, dt, out_dtype, natural, split):
    if natural:
        _, b, k_in, p, d = x5.shape
        xspec = pl.BlockSpec((None, None, k_in, group, dt), lambda bi, g, c: (part, bi, 0, g, c))
    else:
        _, b, p, k_in, d = x5.shape
        xspec = pl.BlockSpec((None, None, group, k_in, dt), lambda bi, g, c: (part, bi, g, 0, c))
    kc = f1c.shape[-1]
    out = jax.ShapeDtypeStruct((b, ka_pad, p, d), out_dtype)
    ospec = pl.BlockSpec((None, ka_pad, group, dt), lambda bi, g, c: (bi, 0, g, c))
    pad_rows = kc - (3 if split else 1) * k_in
    return pl.pallas_call(
        functools.partial(_s1_kernel, group=group, ka_pad=ka_pad, pad_rows=pad_rows, natural=natural, split=split),
        grid=(b, p // group, d // dt),
        in_specs=[xspec,
                  pl.BlockSpec((group, 2 * ka_pad, kc), lambda bi, g, c: (g, 0, 0))],
        out_specs=[ospec, ospec],
        out_shape=[out, out],
        scratch_shapes=[pltpu.VMEM((group, 2 * ka_pad, dt), F32)],
        compiler_params=_cparams("parallel", "parallel", "parallel"),
        name="fft_stage1",
    )(x5, f1c)


def _s2_filter_kernel(ar_ref, ai_ref, m_ref, inv_ref, hr_ref, hi_ref, *, p):
    for i in range(ar_ref.shape[0]):
        x = jnp.concatenate([ar_ref[i], ai_ref[i]], axis=0)
        y = _dot(m_ref[...], x.astype(BF16)) * inv_ref[...]
        hr_ref[i] = y[:p]
        hi_ref[i] = y[p:]


def _fft_stage2_filter(ar, ai, m2f, inv_nrm, dt):
    ka, p, d = ar.shape
    spec = pl.BlockSpec((SUBLANES, p, dt), lambda k, c: (k, 0, c))
    out = jax.ShapeDtypeStruct((ka, p, d), F32)
    return pl.pallas_call(
        functools.partial(_s2_filter_kernel, p=p),
        grid=(ka // SUBLANES, d // dt),
        in_specs=[spec, spec,
                  pl.BlockSpec((2 * p, 2 * p), lambda k, c: (0, 0)),
                  pl.BlockSpec((1, dt), lambda k, c: (0, c))],
        out_specs=[spec, spec],
        out_shape=[out, out],
        compiler_params=_cparams("parallel", "parallel"),
        name="fft_stage2_filter",
    )(ar, ai, m2f, inv_nrm)


def _s2_kernel(ar_ref, ai_ref, hr_ref, hi_ref, mf_ref, mi_ref, c_ref, x_s, y_s, c_s, *, p, ka_used):
    kg = pl.program_id(0)
    assert ka_used % SUBLANES == 1

    def run(rows):
        for i in rows:
            x_s[i] = _dot(mf_ref[...], jnp.concatenate([ar_ref[i], ai_ref[i]], axis=0))
        for i in rows:
            xr, xi = x_s[i, :p], x_s[i, p:]
            hr, hi = hr_ref[i], hi_ref[i]
            y_s[i, :p] = (xr * hr - xi * hi).astype(BF16)
            y_s[i, p:] = (xr * hi + xi * hr).astype(BF16)
        for i in rows:
            c = _dot(mi_ref[...], y_s[i])
            c_s[i] = c[:p]
            c_s[SUBLANES + i] = c[p:]

    run(range(1))
    full = (kg + 1) * SUBLANES <= ka_used

    @pl.when(full)
    def _():
        run(range(1, SUBLANES))

    @pl.when(jnp.logical_not(full))
    def _():
        for i in range(1, SUBLANES):
            c_s[i] = jnp.zeros(c_s.shape[1:], F32)
            c_s[SUBLANES + i] = jnp.zeros(c_s.shape[1:], F32)

    c_ref[...] = jnp.swapaxes(c_s[...], 0, 1).astype(c_ref.dtype)


def _fft_stage2(ar, ai, hr, hi, m2f, m2i, ka_used, h_col0, dt):
    b, ka, p, d = ar.shape
    hc = h_col0 // dt
    aspec = pl.BlockSpec((None, SUBLANES, p, dt), lambda k, c, bi: (bi, k, 0, c))
    hspec = pl.BlockSpec((SUBLANES, p, dt), lambda k, c, bi: (k, 0, hc + c))
    mspec = pl.BlockSpec((2 * p, 2 * p), lambda k, c, bi: (0, 0))
    return pl.pallas_call(
        functools.partial(_s2_kernel, p=p, ka_used=ka_used),
        grid=(ka // SUBLANES, d // dt, b),
        in_specs=[aspec, aspec, hspec, hspec, mspec, mspec],
        out_specs=pl.BlockSpec((None, p, 2 * SUBLANES, dt), lambda k, c, bi: (bi, 0, k, c)),
        out_shape=jax.ShapeDtypeStruct((b, p, 2 * ka, d), BF16),
        scratch_shapes=[pltpu.VMEM((SUBLANES, 2 * p, dt), F32), pltpu.VMEM((SUBLANES, 2 * p, dt), BF16),
                        pltpu.VMEM((2 * SUBLANES, p, dt), F32)],
        compiler_params=_cparams("parallel", "parallel", "parallel"),
        name="fft_stage2",
    )(ar, ai, hr, hi, m2f, m2i)


def _s3_kernel(c_ref, g_ref, v_ref, x_ref, sk_ref, z_ref, y_s, *, group, pad_rows):
    zpad = jnp.zeros((pad_rows, c_ref.shape[-1]), BF16)
    for j in range(group):
        y_s[j] = _dot(g_ref[j], jnp.concatenate([c_ref[j], zpad], axis=0))
    y = jnp.swapaxes(y_s[...], 0, 1)
    v = v_ref[...].astype(F32)
    z_ref[...] = ((y + v * sk_ref[...]) * x_ref[...].astype(F32)).astype(z_ref.dtype)


def _fft_stage3(c, gc, v5, vpart, x5, xpart, skip, group, dt, out_dtype):
    b, p, kr, d = c.shape
    qh = v5.shape[2]
    kc = gc.shape[-1]
    pspec = lambda part: pl.BlockSpec((None, None, qh, group, dt), lambda bi, g, cc: (part, bi, 0, g, cc))
    return pl.pallas_call(
        functools.partial(_s3_kernel, group=group, pad_rows=kc - kr),
        grid=(b, p // group, d // dt),
        in_specs=[pl.BlockSpec((None, group, kr, dt), lambda bi, g, cc: (bi, g, 0, cc)),
                  pl.BlockSpec((group, qh, kc), lambda bi, g, cc: (g, 0, 0)),
                  pspec(vpart), pspec(xpart),
                  pl.BlockSpec((1, dt), lambda bi, g, cc: (0, cc))],
        out_specs=pspec(0),
        out_shape=jax.ShapeDtypeStruct((1, b, qh, p, d), out_dtype),
        scratch_shapes=[pltpu.VMEM((group, qh, dt), F32)],
        compiler_params=_cparams("parallel", "parallel", "parallel"),
        name="fft_stage3",
    )(c, gc, v5, x5, skip)


def _long_conv_lat(v5, vpart, x5, xpart, hr, hi, skip, plan, order, out_dtype):
    d = v5.shape[-1]
    dt1 = min(FFT_DT_OUTER, d)
    dt2 = min(FFT_DT_INNER, d)
    ar, ai = _fft_stage1(v5, vpart, plan["f1s"], plan["ka_pad"], FFT_GROUP, dt1, BF16, True, False)
    c = _fft_stage2(ar, ai, hr, hi, plan["m2f1"], plan["m2i1"], plan["ka_used"], order * d, dt2)
    return _fft_stage3(c, plan["g"], v5, vpart, x5, xpart, skip, FFT_GROUP, dt1, out_dtype)


def _filter_spectrum_lat(hy, seq, plan_f):
    fw1, fb1, ffreq, fw2, fb2, fwout = hy
    q = plan_f["q"]
    h_time, nrm = _hyena_filter_time(fw1, fb1, ffreq, fw2, fb2, fwout, seq, True)
    od = h_time.shape[-1]
    h5 = h_time.reshape(1, 1, FFT_P, q, od)
    dt = min(FFT_DT_OUTER, od)
    ar, ai = _fft_stage1(h5, 0, plan_f["f1"], plan_f["ka_pad"], SUBLANES, dt, F32, False, True)
    return _fft_stage2_filter(ar[0], ai[0], plan_f["m2f1"], 1.0 / nrm, dt)


def _dense_consts(seq):
    n = 2 * seq
    kf = seq + 1
    kf_pad = -(-kf // LANES) * LANES
    k = np.arange(kf)
    fwd_full = np.zeros((2 * kf_pad, n))
    ang = 2 * np.pi * np.outer(k, np.arange(n)) / n
    fwd_full[:kf] = np.cos(ang)
    fwd_full[kf_pad:kf_pad + kf] = -np.sin(ang)
    wgt = np.where((k == 0) | (k == seq), 1.0, 2.0) / n
    inv = np.zeros((seq, 2 * kf_pad))
    angi = 2 * np.pi * np.outer(np.arange(seq), k) / n
    inv[:, :kf] = wgt * np.cos(angi)
    inv[:, kf_pad:kf_pad + kf] = -wgt * np.sin(angi)
    return dict(kf_pad=kf_pad, fwd_full=_const3(fwd_full), fwd=_const3(fwd_full[:, :seq]), inv=_const3(inv))


def _dense_spec_kernel(h_ref, f_ref, inv_ref, hr_ref, hi_ref, *, kf_pad):
    y = _dot(f_ref[...], _stack3(h_ref[...])) * inv_ref[...]
    hr_ref[...] = y[:kf_pad]
    hi_ref[...] = y[kf_pad:]


def _filter_spectrum_ctx(hy, seq, cons):
    fw1, fb1, ffreq, fw2, fb2, fwout = hy
    h_time, nrm = _hyena_filter_time(fw1, fb1, ffreq, fw2, fb2, fwout, seq, False)
    od = h_time.shape[-1]
    n2 = 2 * seq
    h_time = h_time.reshape(n2, od)
    kf_pad = cons["kf_pad"]
    ct = min(FFT_DT_INNER, od)
    out = jax.ShapeDtypeStruct((kf_pad, od), F32)
    ospec = pl.BlockSpec((kf_pad, ct), lambda c: (0, c))
    return pl.pallas_call(
        functools.partial(_dense_spec_kernel, kf_pad=kf_pad),
        grid=(od // ct,),
        in_specs=[pl.BlockSpec((n2, ct), lambda c: (0, c)),
                  pl.BlockSpec((2 * kf_pad, 3 * n2), lambda c: (0, 0)),
                  pl.BlockSpec((1, ct), lambda c: (0, c))],
        out_specs=[ospec, ospec],
        out_shape=[out, out],
        compiler_params=_cparams("parallel"),
        name="dense_filter_spectrum",
    )(h_time, cons["fwd_full"], 1.0 / nrm)


def _dense_conv_kernel(v_ref, x1_ref, x2_ref, h1r_ref, h1i_ref, h2r_ref, h2i_ref, sk_ref,
                       f_ref, g_ref, z_ref, *, kf_pad):
    def conv(u, hr, hi):
        s = _dot(f_ref[...], _stack3(u))
        sr, si = s[:kf_pad], s[kf_pad:]
        y = jnp.concatenate([sr * hr - si * hi, sr * hi + si * hr], axis=0)
        return _dot(g_ref[...], _stack3(y))

    v = v_ref[...]
    z1 = x1_ref[...] * (conv(v, h1r_ref[...], h1i_ref[...]) + v * sk_ref[0:1, :])
    z2 = x2_ref[...] * (conv(z1, h2r_ref[...], h2i_ref[...]) + z1 * sk_ref[1:2, :])
    z_ref[...] = z2.astype(z_ref.dtype)


def _hyena_core_ctx(u3, hr, hi, fskip, cons, b, seq):
    d = u3.shape[-1]
    dt = min(MXU_N, d)
    nd = d // dt
    kf_pad = cons["kf_pad"]
    uspec = lambda part: pl.BlockSpec((None, seq, dt), lambda bi, c: (part, bi, c))
    hspec = lambda order: pl.BlockSpec((kf_pad, dt), lambda bi, c: (0, order * nd + c))
    return pl.pallas_call(
        functools.partial(_dense_conv_kernel, kf_pad=kf_pad),
        grid=(b, nd),
        in_specs=[uspec(0), uspec(1), uspec(2), hspec(0), hspec(0), hspec(1), hspec(1),
                  pl.BlockSpec((HYENA_ORDER, dt), lambda bi, c: (0, c)),
                  pl.BlockSpec((2 * kf_pad, 3 * seq), lambda bi, c: (0, 0)),
                  pl.BlockSpec((seq, 6 * kf_pad), lambda bi, c: (0, 0))],
        out_specs=pl.BlockSpec((seq, dt), lambda bi, c: (bi, c)),
        out_shape=jax.ShapeDtypeStruct((b * seq, d), BF16),
        compiler_params=_cparams("parallel", "parallel"),
        name="hyena_core_ctx",
    )(u3, u3, u3, hr, hi, hr, hi, fskip, cons["fwd"], cons["inv"])


def _rec_consts(chunk):
    t = np.arange(chunk)
    coefs, masks = [], []
    for direction in (0, 1):
        if direction == 0:
            rows = [t[None, :] <= t[:, None], t[None, :] > t[:, None]]
        else:
            rows = [t[None, :] >= t[:, None], t[None, :] < t[:, None]]
        mk = [np.eye(chunk)]
        m = chunk // 2
        while m >= 1:
            blk = t // (2 * m)
            half = (t // m) % 2
            mid = blk * 2 * m + m
            e = np.zeros((chunk, chunk))
            for r in range(chunk):
                if direction == 0:
                    if half[r] == 1:
                        e[r, mid[r]:r + 1] = 1
                    else:
                        e[r, r + 1:mid[r]] = 1
                else:
                    if half[r] == 0:
                        e[r, r:mid[r]] = 1
                    else:
                        e[r, mid[r]:r] = 1
            same = blk[:, None] == blk[None, :]
            if direction == 0:
                mk.append(same & (half[:, None] == 1) & (half[None, :] == 0))
            else:
                mk.append(same & (half[:, None] == 0) & (half[None, :] == 1))
            rows.append(e)
            m //= 2
        rows.append(np.ones((REC_TOTAL_ROWS, chunk)))
        a = np.concatenate([np.asarray(r, np.float64) for r in rows], axis=0)
        a3 = np.concatenate([a] * REC_GATE_TERMS, axis=1)
        pad = -(-a3.shape[1] // LANES) * LANES - a3.shape[1]
        a3 = np.pad(a3, ((0, 0), (0, pad)))
        coefs.append(a3)
        masks.append(np.stack([np.asarray(x, np.float32) for x in mk]))
    return (jnp.asarray(np.stack(coefs), dtype=BF16), jnp.asarray(np.stack(masks), dtype=F32))


def _rec_core(load_qkg, v_ref, coef_ref, mask_ref, s0_ref, o_ref, sfin_ref, q_s, k_s, st_s, ex_s, att_s,
              *, chunk, nsub, heads, dk, dv):
    di = pl.program_id(1)
    c = pl.program_id(2)
    levels = int(math.log2(chunk))
    tot = (2 + levels) * chunk

    @pl.when(c == 0)
    def _():
        st_s[...] = s0_ref[...]

    offs = [pl.multiple_of((s + di * (nsub - 1 - 2 * s)) * chunk, chunk) for s in range(nsub)]

    for s in range(nsub):
        q, k, g = load_qkg(pl.ds(offs[s], chunk))
        q_s[s] = q
        k_s[s] = k
        terms, rest = [], g
        for _ in range(REC_GATE_TERMS):
            terms.append(rest.astype(BF16))
            rest = rest - terms[-1].astype(F32)
        pad = coef_ref.shape[-1] - REC_GATE_TERMS * chunk
        if pad:
            terms.append(jnp.zeros((pad, g.shape[1]), BF16))
        ex_s[s] = jnp.exp(_dot(coef_ref[...], jnp.concatenate(terms, axis=0)))

    for s in range(nsub):
        for h in range(heads):
            ks = slice(h * dk, (h + 1) * dk)
            qh = q_s[s, :, ks]
            kh = k_s[s, :, ks]
            att = mask_ref[0] * _dot_nt(qh.astype(BF16), kh.astype(BF16))
            for lv in range(levels):
                e = ex_s[s, (2 + lv) * chunk:(3 + lv) * chunk, ks]
                att += mask_ref[1 + lv] * _dot_nt((qh * e).astype(BF16), (kh * e).astype(BF16))
            att_s[s, h] = att.astype(BF16)

    for s in range(nsub):
        rows = pl.ds(offs[s], chunk)
        for h in range(heads):
            ks = slice(h * dk, (h + 1) * dk)
            vs = slice(h * dv, (h + 1) * dv)
            vh = v_ref[rows, vs].astype(BF16)
            st = st_s[h]
            o = _dot(att_s[s, h], vh)
            o += _dot_nt((q_s[s, :, ks] * ex_s[s, 0:chunk, ks]).astype(BF16), st.astype(BF16))
            o_ref[rows, vs] = o.astype(o_ref.dtype)
            kd = (k_s[s, :, ks] * ex_s[s, chunk:2 * chunk, ks]).astype(BF16)
            st_s[h] = st * ex_s[s, tot:tot + 1, ks] + _dot_tn(vh, kd)

    @pl.when(c == pl.num_programs(2) - 1)
    def _():
        sfin_ref[...] = st_s[...]


def _hgrn_rec_kernel(q_ref, v_ref, f_ref, lb_ref, coef_ref, mask_ref, s0_ref, o_ref, sfin_ref,
                     q_s, k_s, st_s, ex_s, att_s, **kw):
    log_lb = lb_ref[0:1, :]
    log_1mlb = lb_ref[1:2, :]

    def load_qkg(rows):
        qr = q_ref[rows, :]
        fr = f_ref[rows, :]
        t2 = log_1mlb + jnp.minimum(fr, 0.0) - jnp.log(1.0 + jnp.exp(-jnp.abs(fr)))
        mx = jnp.maximum(log_lb, t2)
        g = mx + jnp.log(1.0 + jnp.exp(-jnp.abs(log_lb - t2)))
        return qr * _sigmoid(qr), 1.0 - jnp.exp(g), g

    _rec_core(load_qkg, v_ref, coef_ref, mask_ref, s0_ref, o_ref, sfin_ref, q_s, k_s, st_s, ex_s, att_s, **kw)


def _gla_rec_kernel(q_ref, k_ref, v_ref, a_ref, wup_ref, bup_ref, coef_ref, mask_ref, s0_ref,
                    o_ref, sfin_ref, q_s, k_s, st_s, ex_s, att_s, *, qscale, **kw):
    def load_qkg(rows):
        xg = _dot(a_ref[rows, :].astype(BF16), wup_ref[...]) + bup_ref[...]
        g = (jnp.minimum(xg, 0.0) - jnp.log(1.0 + jnp.exp(-jnp.abs(xg)))) * (1.0 / GLA_GATE_NORM)
        return q_ref[rows, :] * qscale, k_ref[rows, :], g

    _rec_core(load_qkg, v_ref, coef_ref, mask_ref, s0_ref, o_ref, sfin_ref, q_s, k_s, st_s, ex_s, att_s, **kw)


def _rec_call(kind, proj, extra, s0, consts, b, seq, d, heads, dk, dv):
    coef, masks = consts
    chunk = masks.shape[-1]
    rows = REC_STEP_ROWS
    nsub = rows // chunk
    nc = seq // rows
    hk = heads * dk
    hv = heads * dv
    levels = int(math.log2(chunk))
    n_rows = (2 + levels) * chunk + REC_TOTAL_ROWS

    def rmap(col):
        return lambda bi, di, c: (bi * nc + c + di * (nc - 1 - 2 * c), col)

    cspecs = [pl.BlockSpec((None, n_rows, coef.shape[-1]), lambda bi, di, c: (di, 0, 0)),
              pl.BlockSpec((None, levels + 1, chunk, chunk), lambda bi, di, c: (di, 0, 0, 0)),
              pl.BlockSpec((None, None, heads, dv, dk), lambda bi, di, c: (bi, di, 0, 0, 0))]
    if kind == "hgrn":
        lb = extra
        kernel = _hgrn_rec_kernel
        in_specs = [pl.BlockSpec((rows, d), rmap(0)),
                    pl.BlockSpec((rows, d), rmap(1)),
                    pl.BlockSpec((rows, d), lambda bi, di, c: (bi * nc + c + di * (nc - 1 - 2 * c), 3 + di)),
                    pl.BlockSpec((None, 2, d), lambda bi, di, c: (di, 0, 0))] + cspecs
        args = (proj, proj, proj, lb, coef, masks, s0)
        kw = {}
    else:
        wup, bup = extra
        kernel = _gla_rec_kernel
        in_specs = [pl.BlockSpec((rows, hk), rmap(0)),
                    pl.BlockSpec((rows, hk), rmap(1)),
                    pl.BlockSpec((rows, hv), rmap(1)),
                    pl.BlockSpec((rows, LANES), rmap((2 * hk + 2 * hv) // LANES)),
                    pl.BlockSpec((None, LANES, hk), lambda bi, di, c: (di, 0, 0)),
                    pl.BlockSpec((None, 1, hk), lambda bi, di, c: (di, 0, 0))] + cspecs
        args = (proj, proj, proj, proj, wup, bup, coef, masks, s0)
        kw = dict(qscale=float(dk) ** -0.5)
    return pl.pallas_call(
        functools.partial(kernel, chunk=chunk, nsub=nsub, heads=heads, dk=dk, dv=dv, **kw),
        grid=(b, 2, nc),
        in_specs=in_specs,
        out_specs=[pl.BlockSpec((None, rows, hv), lambda bi, di, c: (di, bi * nc + c + di * (nc - 1 - 2 * c), 0)),
                   pl.BlockSpec((None, None, heads, dv, dk), lambda bi, di, c: (bi, di, 0, 0, 0))],
        out_shape=[jax.ShapeDtypeStruct((2, b * seq, hv), BF16),
                   jax.ShapeDtypeStruct((b, 2, heads, dv, dk), F32)],
        scratch_shapes=[pltpu.VMEM((nsub, chunk, hk), F32), pltpu.VMEM((nsub, chunk, hk), F32),
                        pltpu.VMEM((heads, dv, dk), F32), pltpu.VMEM((nsub, n_rows, hk), F32),
                        pltpu.VMEM((nsub, heads, chunk, chunk), BF16)],
        compiler_params=_cparams("parallel", "parallel", "arbitrary"),
        name=kind + "_recurrence",
    )(*args)


def _gated_out_kernel(o_ref, gate_ref, gn_ref, w_ref, x_ref, mg_ref, out_ref, *, heads):
    d = x_ref.shape[1]
    dh = d // heads
    kc = max(dh, 2 * MXU_N)
    acc = None
    for k0 in range(0, d, kc):
        parts = []
        for h0 in range(k0, k0 + kc, dh):
            seg = o_ref[0, :, h0:h0 + dh].astype(F32) + o_ref[1, :, h0:h0 + dh].astype(F32)
            ms = jnp.mean(seg * seg, axis=-1, keepdims=True)
            parts.append(seg * lax.rsqrt(ms + RMS_EPS))
        gate = gate_ref[:, k0:k0 + kc]
        y = jnp.concatenate(parts, axis=1) * gn_ref[:, k0:k0 + kc] * (gate * _sigmoid(gate))
        part = _dot(y.astype(BF16), w_ref[k0:k0 + kc, :])
        acc = part if acc is None else acc + part
    out_ref[...] = x_ref[...] + mg_ref[...] * acc


def _gated_out(o2, proj, gate_col, gn, w, x, mg, bmap, heads, tm):
    m, d = x.shape
    return pl.pallas_call(
        functools.partial(_gated_out_kernel, heads=heads),
        grid=(m // tm,),
        in_specs=[pl.BlockSpec((2, tm, d), lambda i: (0, i, 0)),
                  pl.BlockSpec((tm, d), lambda i: (i, gate_col)),
                  pl.BlockSpec((1, d), lambda i: (0, 0)),
                  pl.BlockSpec((d, d), lambda i: (0, 0)),
                  pl.BlockSpec((tm, d), lambda i: (i, 0)),
                  _mod_spec(d, bmap)],
        out_specs=pl.BlockSpec((tm, d), lambda i: (i, 0)),
        out_shape=jax.ShapeDtypeStruct((m, d), F32),
        compiler_params=_cparams("parallel"),
        name="gated_out",
    )(o2, proj, gn, w, x, mg)


def _recurrent_mix(kind, x_ctx, x_lat, norm, ctx_map, w_in, extra, consts, b, lc, seq, d, heads, dk, dv, tn):
    g1, sh, sc = norm
    tm_c, tm_l = min(TM_PROJ, x_ctx.shape[0]), min(TM_PROJ, x_lat.shape[0])
    proj_ctx = _projection(x_ctx, g1, sh, sc, w_in, ctx_map, tm_c, tn)
    proj_lat = _projection(x_lat, g1, sh, sc, w_in, lambda i: (i * tm_l) // seq, tm_l, tn)
    s0 = jnp.zeros((b, 2, heads, dv, dk), F32)
    o_ctx, s_ctx = _rec_call(kind, proj_ctx, extra, s0, consts, b, lc, d, heads, dk, dv)
    o_lat, _ = _rec_call(kind, proj_lat, extra, s_ctx, consts, b, seq, d, heads, dk, dv)
    return (proj_ctx, o_ctx), (proj_lat, o_lat)


def kernel(x, c, ctx, c_ctx, w_mod, b_mod, norm1_g, norm2_g, w_ffn_in, w_ffn_out, final_g, hy_w_in, hy_conv_w, hy_fw1, hy_fb1, hy_ffreq, hy_fw2, hy_fb2, hy_fwout, hy_fskip, hy_w_out, hg_w_in, hg_lb_logits, hg_onorm_g, hg_w_out, gla_w_in, gla_w_up, gla_b_up, gla_onorm_g, gla_w_out):
    b, seq, d = x.shape
    lc = ctx.shape[1]
    depth = w_mod.shape[0]
    assert b + 1 <= MOD_ROWS and seq % (FFT_P * 2) == 0 and FFT_P % GRID_W == 0
    m_lat, m_ctx = b * seq, b * lc
    ctx_row = b
    tm_lat, tm_ctx = TM_ROWS, min(TM_ROWS, m_ctx)
    tm_in = TM_PROJ
    lat_map = lambda i: (i * tm_lat) // seq
    lat_map_in = lambda i: (i * tm_in) // seq
    ctx_map = lambda i: ctx_row

    xl = x.reshape(m_lat, d)
    xc = ctx.reshape(m_ctx, d)
    c8 = jnp.zeros((MOD_ROWS, d), F32).at[:b].set(c).at[ctx_row].set(c_ctx)
    mod = _modulation(c8, w_mod, b_mod).reshape(depth, MOD_ROWS, N_MOD, 1, d)

    plan = _fft_consts(seq, seq // FFT_P)
    plan_f = _fft_consts(seq, 2 * seq // FFT_P)
    dense = _dense_consts(lc)
    rec_consts = {1: _rec_consts(HGRN_CHUNK), 2: _rec_consts(GLA_CHUNK)}
    qh = plan["qh"]

    wb_ffn_in, wb_ffn_out = _to_bf16(w_ffn_in), _to_bf16(w_ffn_out)
    wb_hy_in, wb_hy_out = _to_bf16(hy_w_in), _to_bf16(hy_w_out)
    wb_hg_in, wb_hg_out = _to_bf16(hg_w_in), _to_bf16(hg_w_out)
    wb_gla_in, wb_gla_out = _to_bf16(gla_w_in), _to_bf16(gla_w_out)

    for i in range(depth):
        last = i == depth - 1
        kind, j = i % N_MIXERS, i // N_MIXERS
        mv = [mod[i, :, k] for k in range(N_MOD)]
        g1 = norm1_g[i][None, :]
        need_ctx = (not last) or kind != 0
        if kind == 0:
            w_in, w_out = wb_hy_in[j], wb_hy_out[j]
            hy = (hy_fw1[j], hy_fb1[j], hy_ffreq[j], hy_fw2[j], hy_fb2[j], hy_fwout[j])
            u = _hyena_in(xl, g1, mv[0], mv[1], w_in, hy_conv_w[j], lat_map_in,
                          tm_in, min(TN_HYENA_IN, d), GRID_W, BF16).reshape(3, b, qh, FFT_P, d)
            hr, hi = _filter_spectrum_lat(hy, seq, plan_f)
            z1 = _long_conv_lat(u, 0, u, 1, hr, hi, hy_fskip[j, 0][None, :], plan, 0, BF16)
            z2 = _long_conv_lat(z1, 0, u, 2, hr, hi, hy_fskip[j, 1][None, :], plan, 1, BF16)
            xl = _out_res(z2.reshape(m_lat, d), w_out, xl, mv[2], lat_map, tm_lat)
            if need_ctx:
                uc = _hyena_in(xc, g1, mv[0], mv[1], w_in, hy_conv_w[j], ctx_map, lc, min(TN_PROJ, d), lc, F32)
                hcr, hci = _filter_spectrum_ctx(hy, lc, dense)
                zc = _hyena_core_ctx(uc, hcr, hci, hy_fskip[j], dense, b, lc)
                xc = _out_res(zc, w_out, xc, mv[2], ctx_map, tm_ctx)
        else:
            if kind == 1:
                heads = d // HGRN_EXPAND
                dk = dv = HGRN_EXPAND
                w_in = wb_hg_in[j]
                lb_cum = jnp.cumsum(jax.nn.softmax(hg_lb_logits.astype(F32), axis=1), axis=1)
                lb = lb_cum[:, i] - lb_cum[:, 0]
                extra = jnp.stack([jnp.log(lb), jnp.log1p(-lb)], axis=1)
                gn, w_out, tn = hg_onorm_g[j], wb_hg_out[j], TN_PROJ
                rkind = "hgrn"
            else:
                heads = GLA_HEADS
                dk, dv = d // 2 // heads, d // heads
                n_in = gla_w_in.shape[-1]
                tn = TN_GLA_PROJ
                n_pad = -(-n_in // tn) * tn
                w_in = jnp.pad(wb_gla_in[j], ((0, 0), (0, n_pad - n_in)))
                r = GLA_GATE_RANK
                wup = jnp.zeros((2, LANES, heads * dk), F32)
                wup = wup.at[0, :r].set(gla_w_up[j, 0]).at[1, r:2 * r].set(gla_w_up[j, 1]).astype(BF16)
                extra = (wup, gla_b_up[j][:, None, :])
                gn, w_out = gla_onorm_g[j], wb_gla_out[j]
                rkind = "gla"
            (p_ctx, o_ctx), (p_lat, o_lat) = _recurrent_mix(
                rkind, xc, xl, (g1, mv[0], mv[1]), ctx_map, w_in, extra, rec_consts[kind],
                b, lc, seq, d, heads, dk, dv, tn)
            gate_col = 2 if kind == 1 else (2 * heads * dk + heads * dv) // d
            xl = _gated_out(o_lat, p_lat, gate_col, gn[None, :], w_out, xl, mv[2], lat_map, heads, tm_lat)
            xc = _gated_out(o_ctx, p_ctx, gate_col, gn[None, :], w_out, xc, mv[2], ctx_map, heads, tm_ctx)
        g2 = norm2_g[i][None, :]
        fg = final_g[None, :]
        xl = _ffn(xl, g2, mv[3], mv[4], mv[5], wb_ffn_in, wb_ffn_out, i, fg, last, lat_map, tm_lat, TF_FFN)
        if not last:
            xc = _ffn(xc, g2, mv[3], mv[4], mv[5], wb_ffn_in, wb_ffn_out, i, fg, False, ctx_map, tm_ctx, TF_FFN)
    return xl.reshape(b, seq, d)
```

```python
import functools
import math

import ml_dtypes
import numpy as np
import jax
import jax.numpy as jnp
from jax import lax
from jax.experimental import pallas as pl
from jax.experimental.pallas import tpu as pltpu

F32 = jnp.float32
BF16 = jnp.bfloat16

N_MOD = 6
N_MIXERS = 3
RMS_EPS = 1e-6
GRID_W = 64
HYENA_ORDER = 2
FILTER_BANDS = 16
FILTER_EMB = 1 + 2 * FILTER_BANDS
HYENA_DECAY_MIN = math.log(1e-2) / 1.5
HYENA_DECAY_MAX = math.log(1e-2) / 0.3
HGRN_EXPAND = 128
GLA_HEADS = 4
GLA_GATE_RANK = 16
GLA_GATE_NORM = 16.0

LANES = 128
SUBLANES = 8
MXU_N = 256
V7X_VMEM_LIMIT = 56 * 1024 * 1024

FFT_P = 128
FFT_GROUP = 4 * SUBLANES
FFT_DT_OUTER = 1024
FFT_DT_INNER = 512
REC_STEP_ROWS = 256
HGRN_CHUNK = 64
GLA_CHUNK = 128
REC_GATE_TERMS = 2
REC_TOTAL_ROWS = 2 * SUBLANES
MOD_ROWS = 8

TM_ROWS = 512
TM_PROJ = 1024
TN_PROJ = 1024
TN_GLA_PROJ = 5 * MXU_N
TN_HYENA_IN = 1024
TF_FFN = 512
NORM_CHUNK = 2 * MXU_N
TN_MOD = 1024


def _cparams(*sem):
    return pltpu.CompilerParams(dimension_semantics=sem, vmem_limit_bytes=V7X_VMEM_LIMIT)


def _dot(a, b):
    return jnp.dot(a, b, preferred_element_type=F32)


def _dot_nt(a, b):
    return lax.dot_general(a, b, (((1,), (1,)), ((), ())), preferred_element_type=F32)


def _dot_tn(a, b):
    return lax.dot_general(a, b, (((0,), (0,)), ((), ())), preferred_element_type=F32)


def _split(x):
    hi = x.astype(BF16)
    lo = (x - hi.astype(F32)).astype(BF16)
    return hi, lo


def _stack3(x, pad_rows=0):
    hi, lo = _split(x)
    parts = [hi, lo, hi]
    if pad_rows:
        parts.append(jnp.zeros((pad_rows, x.shape[1]), BF16))
    return jnp.concatenate(parts, axis=0)


def _const3(c, pad_cols=0):
    hi = c.astype(ml_dtypes.bfloat16)
    lo = (c - hi.astype(np.float64)).astype(ml_dtypes.bfloat16)
    parts = [hi, hi, lo]
    if pad_cols:
        parts.append(np.zeros(c.shape[:-1] + (pad_cols,), ml_dtypes.bfloat16))
    return jnp.asarray(np.concatenate(parts, axis=-1))


def _dot3(a, bh, bl):
    ah, al = _split(a)
    return _dot(ah, bh) + _dot(ah, bl) + _dot(al, bh)


def _sigmoid(x):
    return jax.nn.sigmoid(x)


def _normmod(x, g, sh, sc):
    ms = jnp.mean(x * x, axis=-1, keepdims=True)
    y = x * lax.rsqrt(ms + RMS_EPS) * g
    return y * (1.0 + sc) + sh


def _mod_kernel(c_ref, w_ref, b_ref, o_ref):
    c = c_ref[...]
    s = (c * _sigmoid(c)).astype(BF16)
    o_ref[...] = _dot(s, w_ref[...].astype(BF16)) + b_ref[...]


def _modulation(c8, w_mod, b_mod):
    depth, d, n = w_mod.shape
    tn = TN_MOD
    return pl.pallas_call(
        _mod_kernel,
        grid=(depth, n // tn),
        in_specs=[pl.BlockSpec((MOD_ROWS, d), lambda l, j: (0, 0)),
                  pl.BlockSpec((None, d, tn), lambda l, j: (l, 0, j)),
                  pl.BlockSpec((None, 1, tn), lambda l, j: (l, 0, j))],
        out_specs=pl.BlockSpec((None, MOD_ROWS, tn), lambda l, j: (l, 0, j)),
        out_shape=jax.ShapeDtypeStruct((depth, MOD_ROWS, n), F32),
        compiler_params=_cparams("parallel", "parallel"),
        name="modulation",
    )(c8, w_mod, b_mod.reshape(depth, 1, n))


def _mod_spec(d, bmap):
    return pl.BlockSpec((None, 1, d), lambda *idx: (bmap(idx[0]), 0, 0))


def _lane_tile(n, cap):
    best = n
    for t in range(LANES, min(n, cap) + 1, LANES):
        if n % t == 0:
            best = t
    return best


def _cast_kernel(x_ref, o_ref):
    o_ref[...] = x_ref[...].astype(o_ref.dtype)


def _to_bf16(w):
    shape = w.shape
    w2 = w.reshape(-1, shape[-1])
    r, c = w2.shape
    tr, tc = min(TM_ROWS, r), _lane_tile(c, 2 * TN_PROJ)
    out = pl.pallas_call(
        _cast_kernel,
        grid=(r // tr, c // tc),
        in_specs=[pl.BlockSpec((tr, tc), lambda i, j: (i, j))],
        out_specs=pl.BlockSpec((tr, tc), lambda i, j: (i, j)),
        out_shape=jax.ShapeDtypeStruct((r, c), BF16),
        compiler_params=_cparams("parallel", "parallel"),
        name="cast_bf16",
    )(w2)
    return out.reshape(shape)


def _proj_kernel(x_ref, g_ref, sh_ref, sc_ref, w_ref, o_ref, h_s):
    j = pl.program_id(1)

    @pl.when(j == 0)
    def _():
        x = x_ref[...]
        inv = lax.rsqrt(jnp.mean(x * x, axis=-1, keepdims=True) + RMS_EPS)
        acc = None
        for k0 in range(0, x.shape[1], NORM_CHUNK):
            ks = slice(k0, k0 + NORM_CHUNK)
            hk = ((x[:, ks] * inv * g_ref[:, ks]) * (1.0 + sc_ref[:, ks]) + sh_ref[:, ks]).astype(BF16)
            h_s[:, ks] = hk
            part = _dot(hk, w_ref[ks, :])
            acc = part if acc is None else acc + part
        o_ref[...] = acc

    @pl.when(j > 0)
    def _():
        o_ref[...] = _dot(h_s[...], w_ref[...])


def _projection(x, g, sh, sc, w, bmap, tm, tn):
    m, d = x.shape
    n = w.shape[1]
    return pl.pallas_call(
        _proj_kernel,
        grid=(m // tm, n // tn),
        in_specs=[pl.BlockSpec((tm, d), lambda i, j: (i, 0)),
                  pl.BlockSpec((1, d), lambda i, j: (0, 0)),
                  _mod_spec(d, bmap), _mod_spec(d, bmap),
                  pl.BlockSpec((d, tn), lambda i, j: (0, j))],
        out_specs=pl.BlockSpec((tm, tn), lambda i, j: (i, j)),
        out_shape=jax.ShapeDtypeStruct((m, n), F32),
        scratch_shapes=[pltpu.VMEM((tm, d), BF16)],
        compiler_params=_cparams("parallel", "arbitrary"),
        name="projection",
    )(x, g, sh, sc, w)


def _out_res_kernel(a_ref, w_ref, x_ref, mg_ref, o_ref):
    o_ref[...] = x_ref[...] + mg_ref[...] * _dot(a_ref[...], w_ref[...])


def _ffn_kernel(x_ref, g_ref, sh_ref, sc_ref, mg_ref, wg_ref, wu_ref, wo_ref, fg_ref, o_ref, h_s, acc_s,
                *, final_norm):
    j = pl.program_id(1)

    def swiglu(a, u):
        return _dot((a * _sigmoid(a) * u).astype(BF16), wo_ref[...])

    @pl.when(j == 0)
    def _():
        x = x_ref[...]
        inv = lax.rsqrt(jnp.mean(x * x, axis=-1, keepdims=True) + RMS_EPS)
        a = u = None
        for k0 in range(0, x.shape[1], NORM_CHUNK):
            ks = slice(k0, k0 + NORM_CHUNK)
            hk = ((x[:, ks] * inv * g_ref[:, ks]) * (1.0 + sc_ref[:, ks]) + sh_ref[:, ks]).astype(BF16)
            h_s[:, ks] = hk
            pa, pu = _dot(hk, wg_ref[ks, :]), _dot(hk, wu_ref[ks, :])
            a, u = (pa, pu) if a is None else (a + pa, u + pu)
        acc_s[...] = swiglu(a, u)

    @pl.when(j > 0)
    def _():
        h = h_s[...]
        acc_s[...] += swiglu(_dot(h, wg_ref[...]), _dot(h, wu_ref[...]))

    @pl.when(j == pl.num_programs(1) - 1)
    def _():
        y = x_ref[...] + mg_ref[...] * acc_s[...]
        if final_norm:
            y = y * lax.rsqrt(jnp.mean(y * y, axis=-1, keepdims=True) + RMS_EPS) * fg_ref[...]
        o_ref[...] = y


def _ffn(x, g, sh, sc, mg, w_in, w_out, layer, final_g, final_norm, bmap, tm, tf):
    m, d = x.shape
    f = w_out.shape[1]
    nf = f // tf
    return pl.pallas_call(
        functools.partial(_ffn_kernel, final_norm=final_norm),
        grid=(m // tm, nf),
        in_specs=[pl.BlockSpec((tm, d), lambda i, j: (i, 0)),
                  pl.BlockSpec((1, d), lambda i, j: (0, 0)),
                  _mod_spec(d, bmap), _mod_spec(d, bmap), _mod_spec(d, bmap),
                  pl.BlockSpec((None, d, tf), lambda i, j: (layer, 0, j)),
                  pl.BlockSpec((None, d, tf), lambda i, j: (layer, 0, nf + j)),
                  pl.BlockSpec((None, tf, d), lambda i, j: (layer, j, 0)),
                  pl.BlockSpec((1, d), lambda i, j: (0, 0))],
        out_specs=pl.BlockSpec((tm, d), lambda i, j: (i, 0)),
        out_shape=jax.ShapeDtypeStruct((m, d), F32),
        scratch_shapes=[pltpu.VMEM((tm, d), BF16), pltpu.VMEM((tm, d), F32)],
        compiler_params=_cparams("parallel", "arbitrary"),
        name="ffn",
    )(x, g, sh, sc, mg, w_in, w_in, w_out, final_g)


def _conv3_rows(acc, cw, row_len):
    tm = acc.shape[0]
    rid = lax.broadcasted_iota(jnp.int32, (tm, 1), 0) % row_len
    up = jnp.where(rid == 0, 0.0, pltpu.roll(acc, 1, 0))
    dn = jnp.where(rid == row_len - 1, 0.0, pltpu.roll(acc, tm - 1, 0))
    return cw[0:1] * up + cw[1:2] * acc + cw[2:3] * dn


def _hy_in_kernel(x_ref, g_ref, sh_ref, sc_ref, w_ref, cw_ref, o_ref, h_s, *, row_len):
    @pl.when(pl.program_id(1) == 0)
    def _():
        h_s[...] = _normmod(x_ref[...], g_ref[...], sh_ref[...], sc_ref[...]).astype(BF16)

    h = h_s[...]
    for n0 in range(0, o_ref.shape[1], MXU_N):
        ns = slice(n0, n0 + MXU_N)
        o_ref[:, ns] = _conv3_rows(_dot(h, w_ref[:, ns]), cw_ref[:, ns], row_len).astype(o_ref.dtype)


def _hyena_in(x, g, sh, sc, w, cw, bmap, tm, tn, row_len, out_dtype):
    m, d = x.shape
    nd = d // tn
    return pl.pallas_call(
        functools.partial(_hy_in_kernel, row_len=row_len),
        grid=(m // tm, 3 * nd),
        in_specs=[pl.BlockSpec((tm, d), lambda i, j: (i, 0)),
                  pl.BlockSpec((1, d), lambda i, j: (0, 0)),
                  _mod_spec(d, bmap), _mod_spec(d, bmap),
                  pl.BlockSpec((d, tn), lambda i, j: (0, j)),
                  pl.BlockSpec((3, tn), lambda i, j: (0, j))],
        out_specs=pl.BlockSpec((None, tm, tn), lambda i, j: (j // nd, i, j % nd)),
        out_shape=jax.ShapeDtypeStruct((3, m, d), out_dtype),
        scratch_shapes=[pltpu.VMEM((tm, d), BF16)],
        compiler_params=_cparams("parallel", "arbitrary"),
        name="hyena_in",
    )(x, g, sh, sc, w, cw)


def _out_res(a, w, x, mg, bmap, tm):
    m, d = x.shape
    return pl.pallas_call(
        _out_res_kernel,
        grid=(m // tm,),
        in_specs=[pl.BlockSpec((tm, d), lambda i: (i, 0)),
                  pl.BlockSpec((d, d), lambda i: (0, 0)),
                  pl.BlockSpec((tm, d), lambda i: (i, 0)),
                  _mod_spec(d, bmap)],
        out_specs=pl.BlockSpec((tm, d), lambda i: (i, 0)),
        out_shape=jax.ShapeDtypeStruct((m, d), F32),
        compiler_params=_cparams("parallel"),
        name="out_residual",
    )(a, w, x, mg)


def _filter_kernel(z_ref, w1h_ref, w1l_ref, b1_ref, fq_ref, w2h_ref, w2l_ref, b2_ref,
                   wo_ref, dl_ref, h_ref, nrm_ref):
    z = z_ref[...]
    t = z[:, 0:1]
    valid = z[:, FILTER_EMB:FILTER_EMB + 1]
    a1 = _dot3(z, w1h_ref[...], w1l_ref[...]) + b1_ref[...]
    hid = jnp.sin(fq_ref[0:1, :] * a1)
    a2 = _dot3(hid, w2h_ref[...], w2l_ref[...]) + b2_ref[...]
    hid = jnp.sin(fq_ref[1:2, :] * a2)
    hh, hl = _split(hid)
    h = _dot(jnp.concatenate([hh, hl], axis=1), wo_ref[...]) * jnp.exp(-t * dl_ref[...]) * valid
    h_ref[...] = h.reshape(h_ref.shape)

    @pl.when((pl.program_id(1) == 0) & (pl.program_id(2) == 0))
    def _():
        nrm_ref[...] = jnp.zeros_like(nrm_ref)

    nrm_ref[...] += jnp.sum(jnp.abs(h), axis=0, keepdims=True)


def _pad2(a, rows, cols):
    return jnp.pad(a, ((0, rows - a.shape[0]), (0, cols - a.shape[1])))


def _filter_positions(seq, perm):
    if perm:
        r_hi = seq // FFT_P
        gt, ng = 2 * SUBLANES, FFT_P // (2 * SUBLANES)
        dirs, g, tl, th = np.meshgrid(np.arange(2), np.arange(ng), np.arange(gt), np.arange(r_hi), indexing="ij")
        n = FFT_P * (r_hi * dirs + th) + g * gt + tl
    else:
        r_hi, gt, ng = seq, 1, 1
        dirs, g, tl, th = np.meshgrid(np.arange(2), np.arange(1), np.arange(1), np.arange(seq), indexing="ij")
        n = seq * dirs + th
    n = n.reshape(-1)
    pos = np.where(n < seq, n, 2 * seq - 1 - n).astype(np.float32)
    z = np.zeros((n.size, LANES), np.float32)
    z[:, 0] = pos / np.float32(max(seq - 1, 1))
    bands = np.arange(1, FILTER_BANDS + 1, dtype=np.float32)
    ang = (np.float32(2.0 * math.pi / seq) * pos)[:, None] * bands[None, :]
    z[:, 1:1 + FILTER_BANDS] = np.cos(ang.astype(np.float64))
    z[:, 1 + FILTER_BANDS:FILTER_EMB] = -np.sin(ang.astype(np.float64))
    z[:, FILTER_EMB] = n != seq
    return jnp.asarray(z), gt, ng, r_hi


def _hyena_filter_time(fw1, fb1, ffreq, fw2, fb2, fwout, seq, perm):
    od = fwout.shape[1] // 2
    d = od // HYENA_ORDER
    z, gt, ng, r_hi = _filter_positions(seq, perm)
    rows = gt * r_hi
    ct = min(2 * FFT_DT_OUTER, od)
    hidden = LANES
    w1h, w1l = _split(_pad2(fw1, LANES, hidden))
    w2h, w2l = _split(_pad2(fw2, hidden, hidden))
    wo = _pad2(fwout, hidden, fwout.shape[1]).astype(BF16)
    wo2 = jnp.concatenate([wo, wo], axis=0)
    b1 = _pad2(fb1[None, :], 1, hidden)
    b2 = _pad2(fb2[None, :], 1, hidden)
    fq = _pad2(ffreq, 2, hidden)
    deltas = np.abs(np.linspace(HYENA_DECAY_MIN, HYENA_DECAY_MAX, d, dtype=np.float32))
    dl = jnp.asarray(np.tile(deltas, HYENA_ORDER)[None, :])
    nct = od // ct
    small = lambda shape: pl.BlockSpec(shape, lambda c, di, g: (0, 0))
    wspec = pl.BlockSpec((2 * hidden, ct), lambda c, di, g: (0, di * nct + c))
    return pl.pallas_call(
        _filter_kernel,
        grid=(nct, 2, ng),
        in_specs=[pl.BlockSpec((rows, LANES), lambda c, di, g: (di * ng + g, 0)),
                  small((LANES, hidden)), small((LANES, hidden)), small((1, hidden)), small((2, hidden)),
                  small((hidden, hidden)), small((hidden, hidden)), small((1, hidden)),
                  wspec,
                  pl.BlockSpec((1, ct), lambda c, di, g: (0, c))],
        out_specs=[pl.BlockSpec((None, gt, None, r_hi, ct), lambda c, di, g: (g, 0, di, 0, c)),
                   pl.BlockSpec((1, ct), lambda c, di, g: (0, c))],
        out_shape=[jax.ShapeDtypeStruct((ng, gt, 2, r_hi, od), F32), jax.ShapeDtypeStruct((1, od), F32)],
        compiler_params=_cparams("parallel", "arbitrary", "arbitrary"),
        name="hyena_filter",
    )(z, w1h, w1l, b1, fq, w2h, w2l, b2, wo2, dl)


def _fft_consts(seq, k_in):
    n = 2 * seq
    p = FFT_P
    q = n // p
    qh = q // 2
    ka_used = qh + 1
    ka_pad = -(-ka_used // SUBLANES) * SUBLANES
    ka = np.arange(ka_used)
    tl = np.arange(p)
    th = np.arange(k_in)
    theta = 2 * np.pi * (ka[None, :, None] * th[None, None, :] / q + ka[None, :, None] * tl[:, None, None] / n)
    f1 = np.zeros((p, 2 * ka_pad, k_in))
    f1[:, :ka_used] = np.cos(theta)
    f1[:, ka_pad:ka_pad + ka_used] = -np.sin(theta)
    k3 = 3 * k_in
    f1c = _const3(f1, pad_cols=-(-k3 // LANES) * LANES - k3)
    f1s = jnp.asarray(np.pad(f1, ((0, 0), (0, 0), (0, -(-k_in // LANES) * LANES - k_in))).astype(ml_dtypes.bfloat16))
    ang = 2 * np.pi * np.outer(np.arange(p), np.arange(p)) / p
    cc, sc = np.cos(ang), np.sin(ang)
    m2f_np, m2i_np = np.block([[cc, sc], [-sc, cc]]), np.block([[cc, -sc], [sc, cc]])
    m2f1, m2i1 = (jnp.asarray(m.astype(ml_dtypes.bfloat16)) for m in (m2f_np, m2i_np))
    tho = np.arange(qh)
    phi = 2 * np.pi * (tho[None, :, None] * ka[None, None, :] / q + ka[None, None, :] * tl[:, None, None] / n)
    wgt = np.where((ka == 0) | (ka == qh), 1.0, 2.0) / n
    kr = 2 * ka_pad
    g = np.zeros((p, qh, -(-kr // LANES) * LANES))
    col = 2 * SUBLANES * (ka // SUBLANES) + ka % SUBLANES
    g[:, :, col] = wgt * np.cos(phi)
    g[:, :, col + SUBLANES] = -wgt * np.sin(phi)
    gc = jnp.asarray(g.astype(ml_dtypes.bfloat16))
    return dict(q=q, qh=qh, ka_used=ka_used, ka_pad=ka_pad, f1=f1c, f1s=f1s, m2f1=m2f1, m2i1=m2i1, g=gc)


def _s1_kernel(x_ref, f_ref, ar_ref, ai_ref, a_s, *, group, ka_pad, pad_rows, natural, split):
    x = x_ref[...].astype(F32)
    if natural:
        x = jnp.swapaxes(x, 0, 1)
    zpad = None if split else jnp.zeros((pad_rows, x.shape[-1]), BF16)
    for j in range(group):
        xx = _stack3(x[j], pad_rows) if split else jnp.concatenate([x[j].astype(BF16), zpad], axis=0)
        a_s[j] = _dot(f_ref[j], xx)
    a = jnp.swapaxes(a_s[...], 0, 1)
    ar_ref[...] = a[:ka_pad].astype(ar_ref.dtype)
    ai_ref[...] = a[ka_pad:].astype(ai_ref.dtype)


def _fft_stage1(x5, part, f1c, ka_pad, group, dt, out_dtype, natural, split):
    if natural:
        _, b, k_in, p, d = x5.shape
        xspec = pl.BlockSpec((None, None, k_in, group, dt), lambda bi, g, c: (part, bi, 0, g, c))
    else:
        _, b, p, k_in, d = x5.shape
        xspec = pl.BlockSpec((None, None, group, k_in, dt), lambda bi, g, c: (part, bi, g, 0, c))
    kc = f1c.shape[-1]
    out = jax.ShapeDtypeStruct((b, ka_pad, p, d), out_dtype)
    ospec = pl.BlockSpec((None, ka_pad, group, dt), lambda bi, g, c: (bi, 0, g, c))
    pad_rows = kc - (3 if split else 1) * k_in
    return pl.pallas_call(
        functools.partial(_s1_kernel, group=group, ka_pad=ka_pad, pad_rows=pad_rows, natural=natural, split=split),
        grid=(b, p // group, d // dt),
        in_specs=[xspec,
                  pl.BlockSpec((group, 2 * ka_pad, kc), lambda bi, g, c: (g, 0, 0))],
        out_specs=[ospec, ospec],
        out_shape=[out, out],
        scratch_shapes=[pltpu.VMEM((group, 2 * ka_pad, dt), F32)],
        compiler_params=_cparams("parallel", "parallel", "parallel"),
        name="fft_stage1",
    )(x5, f1c)


def _s2_filter_kernel(ar_ref, ai_ref, m_ref, inv_ref, hr_ref, hi_ref, *, p):
    for i in range(ar_ref.shape[0]):
        x = jnp.concatenate([ar_ref[i], ai_ref[i]], axis=0)
        y = _dot(m_ref[...], x.astype(BF16)) * inv_ref[...]
        hr_ref[i] = y[:p]
        hi_ref[i] = y[p:]


def _fft_stage2_filter(ar, ai, m2f, inv_nrm, dt):
    ka, p, d = ar.shape
    spec = pl.BlockSpec((SUBLANES, p, dt), lambda k, c: (k, 0, c))
    out = jax.ShapeDtypeStruct((ka, p, d), F32)
    return pl.pallas_call(
        functools.partial(_s2_filter_kernel, p=p),
        grid=(ka // SUBLANES, d // dt),
        in_specs=[spec, spec,
                  pl.BlockSpec((2 * p, 2 * p), lambda k, c: (0, 0)),
                  pl.BlockSpec((1, dt), lambda k, c: (0, c))],
        out_specs=[spec, spec],
        out_shape=[out, out],
        compiler_params=_cparams("parallel", "parallel"),
        name="fft_stage2_filter",
    )(ar, ai, m2f, inv_nrm)


def _s2_kernel(ar_ref, ai_ref, hr_ref, hi_ref, mf_ref, mi_ref, c_ref, x_s, y_s, c_s, *, p, ka_used):
    kg = pl.program_id(0)
    assert ka_used % SUBLANES == 1

    def run(rows):
        for i in rows:
            x_s[i] = _dot(mf_ref[...], jnp.concatenate([ar_ref[i], ai_ref[i]], axis=0))
        for i in rows:
            xr, xi = x_s[i, :p], x_s[i, p:]
            hr, hi = hr_ref[i], hi_ref[i]
            y_s[i, :p] = (xr * hr - xi * hi).astype(BF16)
            y_s[i, p:] = (xr * hi + xi * hr).astype(BF16)
        for i in rows:
            c = _dot(mi_ref[...], y_s[i])
            c_s[i] = c[:p]
            c_s[SUBLANES + i] = c[p:]

    run(range(1))
    full = (kg + 1) * SUBLANES <= ka_used

    @pl.when(full)
    def _():
        run(range(1, SUBLANES))

    @pl.when(jnp.logical_not(full))
    def _():
        for i in range(1, SUBLANES):
            c_s[i] = jnp.zeros(c_s.shape[1:], F32)
            c_s[SUBLANES + i] = jnp.zeros(c_s.shape[1:], F32)

    c_ref[...] = jnp.swapaxes(c_s[...], 0, 1).astype(c_ref.dtype)


def _fft_stage2(ar, ai, hr, hi, m2f, m2i, ka_used, h_col0, dt):
    b, ka, p, d = ar.shape
    hc = h_col0 // dt
    aspec = pl.BlockSpec((None, SUBLANES, p, dt), lambda k, c, bi: (bi, k, 0, c))
    hspec = pl.BlockSpec((SUBLANES, p, dt), lambda k, c, bi: (k, 0, hc + c))
    mspec = pl.BlockSpec((2 * p, 2 * p), lambda k, c, bi: (0, 0))
    return pl.pallas_call(
        functools.partial(_s2_kernel, p=p, ka_used=ka_used),
        grid=(ka // SUBLANES, d // dt, b),
        in_specs=[aspec, aspec, hspec, hspec, mspec, mspec],
        out_specs=pl.BlockSpec((None, p, 2 * SUBLANES, dt), lambda k, c, bi: (bi, 0, k, c)),
        out_shape=jax.ShapeDtypeStruct((b, p, 2 * ka, d), BF16),
        scratch_shapes=[pltpu.VMEM((SUBLANES, 2 * p, dt), F32), pltpu.VMEM((SUBLANES, 2 * p, dt), BF16),
                        pltpu.VMEM((2 * SUBLANES, p, dt), F32)],
        compiler_params=_cparams("parallel", "parallel", "parallel"),
        name="fft_stage2",
    )(ar, ai, hr, hi, m2f, m2i)


def _s3_kernel(c_ref, g_ref, v_ref, x_ref, sk_ref, z_ref, y_s, *, group, pad_rows):
    zpad = jnp.zeros((pad_rows, c_ref.shape[-1]), BF16)
    for j in range(group):
        y_s[j] = _dot(g_ref[j], jnp.concatenate([c_ref[j], zpad], axis=0))
    y = jnp.swapaxes(y_s[...], 0, 1)
    v = v_ref[...].astype(F32)
    z_ref[...] = ((y + v * sk_ref[...]) * x_ref[...].astype(F32)).astype(z_ref.dtype)


def _fft_stage3(c, gc, v5, vpart, x5, xpart, skip, group, dt, out_dtype):
    b, p, kr, d = c.shape
    qh = v5.shape[2]
    kc = gc.shape[-1]
    pspec = lambda part: pl.BlockSpec((None, None, qh, group, dt), lambda bi, g, cc: (part, bi, 0, g, cc))
    return pl.pallas_call(
        functools.partial(_s3_kernel, group=group, pad_rows=kc - kr),
        grid=(b, p // group, d // dt),
        in_specs=[pl.BlockSpec((None, group, kr, dt), lambda bi, g, cc: (bi, g, 0, cc)),
                  pl.BlockSpec((group, qh, kc), lambda bi, g, cc: (g, 0, 0)),
                  pspec(vpart), pspec(xpart),
                  pl.BlockSpec((1, dt), lambda bi, g, cc: (0, cc))],
        out_specs=pspec(0),
        out_shape=jax.ShapeDtypeStruct((1, b, qh, p, d), out_dtype),
        scratch_shapes=[pltpu.VMEM((group, qh, dt), F32)],
        compiler_params=_cparams("parallel", "parallel", "parallel"),
        name="fft_stage3",
    )(c, gc, v5, x5, skip)


def _long_conv_lat(v5, vpart, x5, xpart, hr, hi, skip, plan, order, out_dtype):
    d = v5.shape[-1]
    dt1 = min(FFT_DT_OUTER, d)
    dt2 = min(FFT_DT_INNER, d)
    ar, ai = _fft_stage1(v5, vpart, plan["f1s"], plan["ka_pad"], FFT_GROUP, dt1, BF16, True, False)
    c = _fft_stage2(ar, ai, hr, hi, plan["m2f1"], plan["m2i1"], plan["ka_used"], order * d, dt2)
    return _fft_stage3(c, plan["g"], v5, vpart, x5, xpart, skip, FFT_GROUP, dt1, out_dtype)


def _filter_spectrum_lat(hy, seq, plan_f):
    fw1, fb1, ffreq, fw2, fb2, fwout = hy
    q = plan_f["q"]
    h_time, nrm = _hyena_filter_time(fw1, fb1, ffreq, fw2, fb2, fwout, seq, True)
    od = h_time.shape[-1]
    h5 = h_time.reshape(1, 1, FFT_P, q, od)
    dt = min(FFT_DT_OUTER, od)
    ar, ai = _fft_stage1(h5, 0, plan_f["f1"], plan_f["ka_pad"], SUBLANES, dt, F32, False, True)
    return _fft_stage2_filter(ar[0], ai[0], plan_f["m2f1"], 1.0 / nrm, dt)


def _dense_consts(seq):
    n = 2 * seq
    kf = seq + 1
    kf_pad = -(-kf // LANES) * LANES
    k = np.arange(kf)
    fwd_full = np.zeros((2 * kf_pad, n))
    ang = 2 * np.pi * np.outer(k, np.arange(n)) / n
    fwd_full[:kf] = np.cos(ang)
    fwd_full[kf_pad:kf_pad + kf] = -np.sin(ang)
    wgt = np.where((k == 0) | (k == seq), 1.0, 2.0) / n
    inv = np.zeros((seq, 2 * kf_pad))
    angi = 2 * np.pi * np.outer(np.arange(seq), k) / n
    inv[:, :kf] = wgt * np.cos(angi)
    inv[:, kf_pad:kf_pad + kf] = -wgt * np.sin(angi)
    return dict(kf_pad=kf_pad, fwd_full=_const3(fwd_full), fwd=_const3(fwd_full[:, :seq]), inv=_const3(inv))


def _dense_spec_kernel(h_ref, f_ref, inv_ref, hr_ref, hi_ref, *, kf_pad):
    y = _dot(f_ref[...], _stack3(h_ref[...])) * inv_ref[...]
    hr_ref[...] = y[:kf_pad]
    hi_ref[...] = y[kf_pad:]


def _filter_spectrum_ctx(hy, seq, cons):
    fw1, fb1, ffreq, fw2, fb2, fwout = hy
    h_time, nrm = _hyena_filter_time(fw1, fb1, ffreq, fw2, fb2, fwout, seq, False)
    od = h_time.shape[-1]
    n2 = 2 * seq
    h_time = h_time.reshape(n2, od)
    kf_pad = cons["kf_pad"]
    ct = min(FFT_DT_INNER, od)
    out = jax.ShapeDtypeStruct((kf_pad, od), F32)
    ospec = pl.BlockSpec((kf_pad, ct), lambda c: (0, c))
    return pl.pallas_call(
        functools.partial(_dense_spec_kernel, kf_pad=kf_pad),
        grid=(od // ct,),
        in_specs=[pl.BlockSpec((n2, ct), lambda c: (0, c)),
                  pl.BlockSpec((2 * kf_pad, 3 * n2), lambda c: (0, 0)),
                  pl.BlockSpec((1, ct), lambda c: (0, c))],
        out_specs=[ospec, ospec],
        out_shape=[out, out],
        compiler_params=_cparams("parallel"),
        name="dense_filter_spectrum",
    )(h_time, cons["fwd_full"], 1.0 / nrm)


def _dense_conv_kernel(v_ref, x1_ref, x2_ref, h1r_ref, h1i_ref, h2r_ref, h2i_ref, sk_ref,
                       f_ref, g_ref, z_ref, *, kf_pad):
    def conv(u, hr, hi):
        s = _dot(f_ref[...], _stack3(u))
        sr, si = s[:kf_pad], s[kf_pad:]
        y = jnp.concatenate([sr * hr - si * hi, sr * hi + si * hr], axis=0)
        return _dot(g_ref[...], _stack3(y))

    v = v_ref[...]
    z1 = x1_ref[...] * (conv(v, h1r_ref[...], h1i_ref[...]) + v * sk_ref[0:1, :])
    z2 = x2_ref[...] * (conv(z1, h2r_ref[...], h2i_ref[...]) + z1 * sk_ref[1:2, :])
    z_ref[...] = z2.astype(z_ref.dtype)


def _hyena_core_ctx(u3, hr, hi, fskip, cons, b, seq):
    d = u3.shape[-1]
    dt = min(MXU_N, d)
    nd = d // dt
    kf_pad = cons["kf_pad"]
    uspec = lambda part: pl.BlockSpec((None, seq, dt), lambda bi, c: (part, bi, c))
    hspec = lambda order: pl.BlockSpec((kf_pad, dt), lambda bi, c: (0, order * nd + c))
    return pl.pallas_call(
        functools.partial(_dense_conv_kernel, kf_pad=kf_pad),
        grid=(b, nd),
        in_specs=[uspec(0), uspec(1), uspec(2), hspec(0), hspec(0), hspec(1), hspec(1),
                  pl.BlockSpec((HYENA_ORDER, dt), lambda bi, c: (0, c)),
                  pl.BlockSpec((2 * kf_pad, 3 * seq), lambda bi, c: (0, 0)),
                  pl.BlockSpec((seq, 6 * kf_pad), lambda bi, c: (0, 0))],
        out_specs=pl.BlockSpec((seq, dt), lambda bi, c: (bi, c)),
        out_shape=jax.ShapeDtypeStruct((b * seq, d), BF16),
        compiler_params=_cparams("parallel", "parallel"),
        name="hyena_core_ctx",
    )(u3, u3, u3, hr, hi, hr, hi, fskip, cons["fwd"], cons["inv"])


def _rec_consts(chunk):
    t = np.arange(chunk)
    coefs, masks = [], []
    for direction in (0, 1):
        if direction == 0:
            rows = [t[None, :] <= t[:, None], t[None, :] > t[:, None]]
        else:
            rows = [t[None, :] >= t[:, None], t[None, :] < t[:, None]]
        mk = [np.eye(chunk)]
        m = chunk // 2
        while m >= 1:
            blk = t // (2 * m)
            half = (t // m) % 2
            mid = blk * 2 * m + m
            e = np.zeros((chunk, chunk))
            for r in range(chunk):
                if direction == 0:
                    if half[r] == 1:
                        e[r, mid[r]:r + 1] = 1
                    else:
                        e[r, r + 1:mid[r]] = 1
                else:
                    if half[r] == 0:
                        e[r, r:mid[r]] = 1
                    else:
                        e[r, mid[r]:r] = 1
            same = blk[:, None] == blk[None, :]
            if direction == 0:
                mk.append(same & (half[:, None] == 1) & (half[None, :] == 0))
            else:
                mk.append(same & (half[:, None] == 0) & (half[None, :] == 1))
            rows.append(e)
            m //= 2
        rows.append(np.ones((REC_TOTAL_ROWS, chunk)))
        a = np.concatenate([np.asarray(r, np.float64) for r in rows], axis=0)
        a3 = np.concatenate([a] * REC_GATE_TERMS, axis=1)
        pad = -(-a3.shape[1] // LANES) * LANES - a3.shape[1]
        a3 = np.pad(a3, ((0, 0), (0, pad)))
        coefs.append(a3)
        masks.append(np.stack([np.asarray(x, np.float32) for x in mk]))
    return (jnp.asarray(np.stack(coefs), dtype=BF16), jnp.asarray(np.stack(masks), dtype=F32))


def _rec_core(load_qkg, v_ref, coef_ref, mask_ref, s0_ref, o_ref, sfin_ref, q_s, k_s, st_s, ex_s, att_s,
              *, chunk, nsub, heads, dk, dv):
    di = pl.program_id(1)
    c = pl.program_id(2)
    levels = int(math.log2(chunk))
    tot = (2 + levels) * chunk

    @pl.when(c == 0)
    def _():
        st_s[...] = s0_ref[...]

    offs = [pl.multiple_of((s + di * (nsub - 1 - 2 * s)) * chunk, chunk) for s in range(nsub)]

    for s in range(nsub):
        q, k, g = load_qkg(pl.ds(offs[s], chunk))
        q_s[s] = q
        k_s[s] = k
        terms, rest = [], g
        for _ in range(REC_GATE_TERMS):
            terms.append(rest.astype(BF16))
            rest = rest - terms[-1].astype(F32)
        pad = coef_ref.shape[-1] - REC_GATE_TERMS * chunk
        if pad:
            terms.append(jnp.zeros((pad, g.shape[1]), BF16))
        ex_s[s] = jnp.exp(_dot(coef_ref[...], jnp.concatenate(terms, axis=0)))

    for s in range(nsub):
        for h in range(heads):
            ks = slice(h * dk, (h + 1) * dk)
            qh = q_s[s, :, ks]
            kh = k_s[s, :, ks]
            att = mask_ref[0] * _dot_nt(qh.astype(BF16), kh.astype(BF16))
            for lv in range(levels):
                e = ex_s[s, (2 + lv) * chunk:(3 + lv) * chunk, ks]
                att += mask_ref[1 + lv] * _dot_nt((qh * e).astype(BF16), (kh * e).astype(BF16))
            att_s[s, h] = att.astype(BF16)

    for s in range(nsub):
        rows = pl.ds(offs[s], chunk)
        for h in range(heads):
            ks = slice(h * dk, (h + 1) * dk)
            vs = slice(h * dv, (h + 1) * dv)
            vh = v_ref[rows, vs].astype(BF16)
            st = st_s[h]
            o = _dot(att_s[s, h], vh)
            o += _dot_nt((q_s[s, :, ks] * ex_s[s, 0:chunk, ks]).astype(BF16), st.astype(BF16))
            o_ref[rows, vs] = o.astype(o_ref.dtype)
            kd = (k_s[s, :, ks] * ex_s[s, chunk:2 * chunk, ks]).astype(BF16)
            st_s[h] = st * ex_s[s, tot:tot + 1, ks] + _dot_tn(vh, kd)

    @pl.when(c == pl.num_programs(2) - 1)
    def _():
        sfin_ref[...] = st_s[...]


def _hgrn_rec_kernel(q_ref, v_ref, f_ref, lb_ref, coef_ref, mask_ref, s0_ref, o_ref, sfin_ref,
                     q_s, k_s, st_s, ex_s, att_s, **kw):
    log_lb = lb_ref[0:1, :]
    log_1mlb = lb_ref[1:2, :]

    def load_qkg(rows):
        qr = q_ref[rows, :]
        fr = f_ref[rows, :]
        t2 = log_1mlb + jnp.minimum(fr, 0.0) - jnp.log(1.0 + jnp.exp(-jnp.abs(fr)))
        mx = jnp.maximum(log_lb, t2)
        g = mx + jnp.log(1.0 + jnp.exp(-jnp.abs(log_lb - t2)))
        return qr * _sigmoid(qr), 1.0 - jnp.exp(g), g

    _rec_core(load_qkg, v_ref, coef_ref, mask_ref, s0_ref, o_ref, sfin_ref, q_s, k_s, st_s, ex_s, att_s, **kw)


def _gla_rec_kernel(q_ref, k_ref, v_ref, a_ref, wup_ref, bup_ref, coef_ref, mask_ref, s0_ref,
                    o_ref, sfin_ref, q_s, k_s, st_s, ex_s, att_s, *, qscale, **kw):
    def load_qkg(rows):
        xg = _dot(a_ref[rows, :].astype(BF16), wup_ref[...]) + bup_ref[...]
        g = (jnp.minimum(xg, 0.0) - jnp.log(1.0 + jnp.exp(-jnp.abs(xg)))) * (1.0 / GLA_GATE_NORM)
        return q_ref[rows, :] * qscale, k_ref[rows, :], g

    _rec_core(load_qkg, v_ref, coef_ref, mask_ref, s0_ref, o_ref, sfin_ref, q_s, k_s, st_s, ex_s, att_s, **kw)


def _rec_call(kind, proj, extra, s0, consts, b, seq, d, heads, dk, dv):
    coef, masks = consts
    chunk = masks.shape[-1]
    rows = REC_STEP_ROWS
    nsub = rows // chunk
    nc = seq // rows
    hk = heads * dk
    hv = heads * dv
    levels = int(math.log2(chunk))
    n_rows = (2 + levels) * chunk + REC_TOTAL_ROWS

    def rmap(col):
        return lambda bi, di, c: (bi * nc + c + di * (nc - 1 - 2 * c), col)

    cspecs = [pl.BlockSpec((None, n_rows, coef.shape[-1]), lambda bi, di, c: (di, 0, 0)),
              pl.BlockSpec((None, levels + 1, chunk, chunk), lambda bi, di, c: (di, 0, 0, 0)),
              pl.BlockSpec((None, None, heads, dv, dk), lambda bi, di, c: (bi, di, 0, 0, 0))]
    if kind == "hgrn":
        lb = extra
        kernel = _hgrn_rec_kernel
        in_specs = [pl.BlockSpec((rows, d), rmap(0)),
                    pl.BlockSpec((rows, d), rmap(1)),
                    pl.BlockSpec((rows, d), lambda bi, di, c: (bi * nc + c + di * (nc - 1 - 2 * c), 3 + di)),
                    pl.BlockSpec((None, 2, d), lambda bi, di, c: (di, 0, 0))] + cspecs
        args = (proj, proj, proj, lb, coef, masks, s0)
        kw = {}
    else:
        wup, bup = extra
        kernel = _gla_rec_kernel
        in_specs = [pl.BlockSpec((rows, hk), rmap(0)),
                    pl.BlockSpec((rows, hk), rmap(1)),
                    pl.BlockSpec((rows, hv), rmap(1)),
                    pl.BlockSpec((rows, LANES), rmap((2 * hk + 2 * hv) // LANES)),
                    pl.BlockSpec((None, LANES, hk), lambda bi, di, c: (di, 0, 0)),
                    pl.BlockSpec((None, 1, hk), lambda bi, di, c: (di, 0, 0))] + cspecs
        args = (proj, proj, proj, proj, wup, bup, coef, masks, s0)
        kw = dict(qscale=float(dk) ** -0.5)
    return pl.pallas_call(
        functools.partial(kernel, chunk=chunk, nsub=nsub, heads=heads, dk=dk, dv=dv, **kw),
        grid=(b, 2, nc),
        in_specs=in_specs,
        out_specs=[pl.BlockSpec((None, rows, hv), lambda bi, di, c: (di, bi * nc + c + di * (nc - 1 - 2 * c), 0)),
                   pl.BlockSpec((None, None, heads, dv, dk), lambda bi, di, c: (bi, di, 0, 0, 0))],
        out_shape=[jax.ShapeDtypeStruct((2, b * seq, hv), BF16),
                   jax.ShapeDtypeStruct((b, 2, heads, dv, dk), F32)],
        scratch_shapes=[pltpu.VMEM((nsub, chunk, hk), F32), pltpu.VMEM((nsub, chunk, hk), F32),
                        pltpu.VMEM((heads, dv, dk), F32), pltpu.VMEM((nsub, n_rows, hk), F32),
                        pltpu.VMEM((nsub, heads, chunk, chunk), BF16)],
        compiler_params=_cparams("parallel", "parallel", "arbitrary"),
        name=kind + "_recurrence",
    )(*args)


def _gated_out_kernel(o_ref, gate_ref, gn_ref, w_ref, x_ref, mg_ref, out_ref, *, heads):
    d = x_ref.shape[1]
    dh = d // heads
    kc = max(dh, 2 * MXU_N)
    acc = None
    for k0 in range(0, d, kc):
        parts = []
        for h0 in range(k0, k0 + kc, dh):
            seg = o_ref[0, :, h0:h0 + dh].astype(F32) + o_ref[1, :, h0:h0 + dh].astype(F32)
            ms = jnp.mean(seg * seg, axis=-1, keepdims=True)
            parts.append(seg * lax.rsqrt(ms + RMS_EPS))
        gate = gate_ref[:, k0:k0 + kc]
        y = jnp.concatenate(parts, axis=1) * gn_ref[:, k0:k0 + kc] * (gate * _sigmoid(gate))
        part = _dot(y.astype(BF16), w_ref[k0:k0 + kc, :])
        acc = part if acc is None else acc + part
    out_ref[...] = x_ref[...] + mg_ref[...] * acc


def _gated_out(o2, proj, gate_col, gn, w, x, mg, bmap, heads, tm):
    m, d = x.shape
    return pl.pallas_call(
        functools.partial(_gated_out_kernel, heads=heads),
        grid=(m // tm,),
        in_specs=[pl.BlockSpec((2, tm, d), lambda i: (0, i, 0)),
                  pl.BlockSpec((tm, d), lambda i: (i, gate_col)),
                  pl.BlockSpec((1, d), lambda i: (0, 0)),
                  pl.BlockSpec((d, d), lambda i: (0, 0)),
                  pl.BlockSpec((tm, d), lambda i: (i, 0)),
                  _mod_spec(d, bmap)],
        out_specs=pl.BlockSpec((tm, d), lambda i: (i, 0)),
        out_shape=jax.ShapeDtypeStruct((m, d), F32),
        compiler_params=_cparams("parallel"),
        name="gated_out",
    )(o2, proj, gn, w, x, mg)


def _recurrent_mix(kind, x_ctx, x_lat, norm, ctx_map, w_in, extra, consts, b, lc, seq, d, heads, dk, dv, tn):
    g1, sh, sc = norm
    tm_c, tm_l = min(TM_PROJ, x_ctx.shape[0]), min(TM_PROJ, x_lat.shape[0])
    proj_ctx = _projection(x_ctx, g1, sh, sc, w_in, ctx_map, tm_c, tn)
    proj_lat = _projection(x_lat, g1, sh, sc, w_in, lambda i: (i * tm_l) // seq, tm_l, tn)
    s0 = jnp.zeros((b, 2, heads, dv, dk), F32)
    o_ctx, s_ctx = _rec_call(kind, proj_ctx, extra, s0, consts, b, lc, d, heads, dk, dv)
    o_lat, _ = _rec_call(kind, proj_lat, extra, s_ctx, consts, b, seq, d, heads, dk, dv)
    return (proj_ctx, o_ctx), (proj_lat, o_lat)


def kernel(x, c, ctx, c_ctx, w_mod, b_mod, norm1_g, norm2_g, w_ffn_in, w_ffn_out, final_g, hy_w_in, hy_conv_w, hy_fw1, hy_fb1, hy_ffreq, hy_fw2, hy_fb2, hy_fwout, hy_fskip, hy_w_out, hg_w_in, hg_lb_logits, hg_onorm_g, hg_w_out, gla_w_in, gla_w_up, gla_b_up, gla_onorm_g, gla_w_out):
    b, seq, d = x.shape
    lc = ctx.shape[1]
    depth = w_mod.shape[0]
    assert b + 1 <= MOD_ROWS and seq % (FFT_P * 2) == 0 and FFT_P % GRID_W == 0
    m_lat, m_ctx = b * seq, b * lc
    ctx_row = b
    tm_lat, tm_ctx = TM_ROWS, min(TM_ROWS, m_ctx)
    tm_in = TM_PROJ
    lat_map = lambda i: (i * tm_lat) // seq
    lat_map_in = lambda i: (i * tm_in) // seq
    ctx_map = lambda i: ctx_row

    xl = x.reshape(m_lat, d)
    xc = ctx.reshape(m_ctx, d)
    c8 = jnp.zeros((MOD_ROWS, d), F32).at[:b].set(c).at[ctx_row].set(c_ctx)
    mod = _modulation(c8, w_mod, b_mod).reshape(depth, MOD_ROWS, N_MOD, 1, d)

    plan = _fft_consts(seq, seq // FFT_P)
    plan_f = _fft_consts(seq, 2 * seq // FFT_P)
    dense = _dense_consts(lc)
    rec_consts = {1: _rec_consts(HGRN_CHUNK), 2: _rec_consts(GLA_CHUNK)}
    qh = plan["qh"]

    wb_ffn_in, wb_ffn_out = _to_bf16(w_ffn_in), _to_bf16(w_ffn_out)
    wb_hy_in, wb_hy_out = _to_bf16(hy_w_in), _to_bf16(hy_w_out)
    wb_hg_in, wb_hg_out = _to_bf16(hg_w_in), _to_bf16(hg_w_out)
    wb_gla_in, wb_gla_out = _to_bf16(gla_w_in), _to_bf16(gla_w_out)

    for i in range(depth):
        last = i == depth - 1
        kind, j = i % N_MIXERS, i // N_MIXERS
        mv = [mod[i, :, k] for k in range(N_MOD)]
        g1 = norm1_g[i][None, :]
        need_ctx = (not last) or kind != 0
        if kind == 0:
            w_in, w_out = wb_hy_in[j], wb_hy_out[j]
            hy = (hy_fw1[j], hy_fb1[j], hy_ffreq[j], hy_fw2[j], hy_fb2[j], hy_fwout[j])
            u = _hyena_in(xl, g1, mv[0], mv[1], w_in, hy_conv_w[j], lat_map_in,
                          tm_in, min(TN_HYENA_IN, d), GRID_W, BF16).reshape(3, b, qh, FFT_P, d)
            hr, hi = _filter_spectrum_lat(hy, seq, plan_f)
            z1 = _long_conv_lat(u, 0, u, 1, hr, hi, hy_fskip[j, 0][None, :], plan, 0, BF16)
            z2 = _long_conv_lat(z1, 0, u, 2, hr, hi, hy_fskip[j, 1][None, :], plan, 1, BF16)
            xl = _out_res(z2.reshape(m_lat, d), w_out, xl, mv[2], lat_map, tm_lat)
            if need_ctx:
                uc = _hyena_in(xc, g1, mv[0], mv[1], w_in, hy_conv_w[j], ctx_map, lc, min(TN_PROJ, d), lc, F32)
                hcr, hci = _filter_spectrum_ctx(hy, lc, dense)
                zc = _hyena_core_ctx(uc, hcr, hci, hy_fskip[j], dense, b, lc)
                xc = _out_res(zc, w_out, xc, mv[2], ctx_map, tm_ctx)
        else:
            if kind == 1:
                heads = d // HGRN_EXPAND
                dk = dv = HGRN_EXPAND
                w_in = wb_hg_in[j]
                lb_cum = jnp.cumsum(jax.nn.softmax(hg_lb_logits.astype(F32), axis=1), axis=1)
                lb = lb_cum[:, i] - lb_cum[:, 0]
                extra = jnp.stack([jnp.log(lb), jnp.log1p(-lb)], axis=1)
                gn, w_out, tn = hg_onorm_g[j], wb_hg_out[j], TN_PROJ
                rkind = "hgrn"
            else:
                heads = GLA_HEADS
                dk, dv = d // 2 // heads, d // heads
                n_in = gla_w_in.shape[-1]
                tn = TN_GLA_PROJ
                n_pad = -(-n_in // tn) * tn
                w_in = jnp.pad(wb_gla_in[j], ((0, 0), (0, n_pad - n_in)))
                r = GLA_GATE_RANK
                wup = jnp.zeros((2, LANES, heads * dk), F32)
                wup = wup.at[0, :r].set(gla_w_up[j, 0]).at[1, r:2 * r].set(gla_w_up[j, 1]).astype(BF16)
                extra = (wup, gla_b_up[j][:, None, :])
                gn, w_out = gla_onorm_g[j], wb_gla_out[j]
                rkind = "gla"
            (p_ctx, o_ctx), (p_lat, o_lat) = _recurrent_mix(
                rkind, xc, xl, (g1, mv[0], mv[1]), ctx_map, w_in, extra, rec_consts[kind],
                b, lc, seq, d, heads, dk, dv, tn)
            gate_col = 2 if kind == 1 else (2 * heads * dk + heads * dv) // d
            xl = _gated_out(o_lat, p_lat, gate_col, gn[None, :], w_out, xl, mv[2], lat_map, heads, tm_lat)
            xc = _gated_out(o_ctx, p_ctx, gate_col, gn[None, :], w_out, xc, mv[2], ctx_map, heads, tm_ctx)
        g2 = norm2_g[i][None, :]
        fg = final_g[None, :]
        xl = _ffn(xl, g2, mv[3], mv[4], mv[5], wb_ffn_in, wb_ffn_out, i, fg, last, lat_map, tm_lat, TF_FFN)
        if not last:
            xc = _ffn(xc, g2, mv[3], mv[4], mv[5], wb_ffn_in, wb_ffn_out, i, fg, False, ctx_map, tm_ctx, TF_FFN)
    return xl.reshape(b, seq, d)
```

```python
import functools
import math

import ml_dtypes
import numpy as np
import jax
import jax.numpy as jnp
from jax import lax
from jax.experimental import pallas as pl
from jax.experimental.pallas import tpu as pltpu

F32 = jnp.float32
BF16 = jnp.bfloat16

N_MOD = 6
N_MIXERS = 3
RMS_EPS = 1e-6
GRID_W = 64
HYENA_ORDER = 2
FILTER_BANDS = 16
FILTER_EMB = 1 + 2 * FILTER_BANDS
HYENA_DECAY_MIN = math.log(1e-2) / 1.5
HYENA_DECAY_MAX = math.log(1e-2) / 0.3
HGRN_EXPAND = 128
GLA_HEADS = 4
GLA_GATE_RANK = 16
GLA_GATE_NORM = 16.0

LANES = 128
SUBLANES = 8
MXU_N = 256
V7X_VMEM_LIMIT = 56 * 1024 * 1024

FFT_P = 128
FFT_GROUP = 4 * SUBLANES
FFT_DT_OUTER = 1024
FFT_DT_INNER = 512
REC_STEP_ROWS = 256
HGRN_CHUNK = 64
GLA_CHUNK = 128
REC_GATE_TERMS = 2
REC_TOTAL_ROWS = 2 * SUBLANES
MOD_ROWS = 8

TM_ROWS = 512
TM_PROJ = 1024
TN_PROJ = 1024
TN_GLA_PROJ = 5 * MXU_N
TN_HYENA_IN = 1024
TF_FFN = 512
NORM_CHUNK = 2 * MXU_N
TN_MOD = 1024


def _cparams(*sem):
    return pltpu.CompilerParams(dimension_semantics=sem, vmem_limit_bytes=V7X_VMEM_LIMIT)


def _dot(a, b):
    return jnp.dot(a, b, preferred_element_type=F32)


def _dot_nt(a, b):
    return lax.dot_general(a, b, (((1,), (1,)), ((), ())), preferred_element_type=F32)


def _dot_tn(a, b):
    return lax.dot_general(a, b, (((0,), (0,)), ((), ())), preferred_element_type=F32)


def _split(x):
    hi = x.astype(BF16)
    lo = (x - hi.astype(F32)).astype(BF16)
    return hi, lo


def _stack3(x, pad_rows=0):
    hi, lo = _split(x)
    parts = [hi, lo, hi]
    if pad_rows:
        parts.append(jnp.zeros((pad_rows, x.shape[1]), BF16))
    return jnp.concatenate(parts, axis=0)


def _const3(c, pad_cols=0):
    hi = c.astype(ml_dtypes.bfloat16)
    lo = (c - hi.astype(np.float64)).astype(ml_dtypes.bfloat16)
    parts = [hi, hi, lo]
    if pad_cols:
        parts.append(np.zeros(c.shape[:-1] + (pad_cols,), ml_dtypes.bfloat16))
    return jnp.asarray(np.concatenate(parts, axis=-1))


def _dot3(a, bh, bl):
    ah, al = _split(a)
    return _dot(ah, bh) + _dot(ah, bl) + _dot(al, bh)


def _sigmoid(x):
    return jax.nn.sigmoid(x)


def _mod_kernel(c_ref, w_ref, b_ref, o_ref):
    c = c_ref[...]
    s = (c * _sigmoid(c)).astype(BF16)
    o_ref[...] = _dot(s, w_ref[...].astype(BF16)) + b_ref[...]


def _modulation(c8, w_mod, b_mod):
    depth, d, n = w_mod.shape
    tn = TN_MOD
    return pl.pallas_call(
        _mod_kernel,
        grid=(depth, n // tn),
        in_specs=[pl.BlockSpec((MOD_ROWS, d), lambda l, j: (0, 0)),
                  pl.BlockSpec((None, d, tn), lambda l, j: (l, 0, j)),
                  pl.BlockSpec((None, 1, tn), lambda l, j: (l, 0, j))],
        out_specs=pl.BlockSpec((None, MOD_ROWS, tn), lambda l, j: (l, 0, j)),
        out_shape=jax.ShapeDtypeStruct((depth, MOD_ROWS, n), F32),
        compiler_params=_cparams("parallel", "parallel"),
        name="modulation",
    )(c8, w_mod, b_mod.reshape(depth, 1, n))


def _mod_spec(d, bmap):
    return pl.BlockSpec((None, 1, d), lambda *idx: (bmap(idx[0]), 0, 0))


def _lane_tile(n, cap):
    best = n
    for t in range(LANES, min(n, cap) + 1, LANES):
        if n % t == 0:
            best = t
    return best


def _cast_kernel(x_ref, o_ref):
    o_ref[...] = x_ref[...].astype(o_ref.dtype)


def _to_bf16(w):
    shape = w.shape
    w2 = w.reshape(-1, shape[-1])
    r, c = w2.shape
    tr, tc = min(TM_ROWS, r), _lane_tile(c, 2 * TN_PROJ)
    out = pl.pallas_call(
        _cast_kernel,
        grid=(r // tr, c // tc),
        in_specs=[pl.BlockSpec((tr, tc), lambda i, j: (i, j))],
        out_specs=pl.BlockSpec((tr, tc), lambda i, j: (i, j)),
        out_shape=jax.ShapeDtypeStruct((r, c), BF16),
        compiler_params=_cparams("parallel", "parallel"),
        name="cast_bf16",
    )(w2)
    return out.reshape(shape)


def _proj_kernel(x_ref, g_ref, sh_ref, sc_ref, w_ref, o_ref, h_s):
    j = pl.program_id(1)

    @pl.when(j == 0)
    def _():
        x = x_ref[...]
        inv = lax.rsqrt(jnp.mean(x * x, axis=-1, keepdims=True) + RMS_EPS)
        acc = None
        for k0 in range(0, x.shape[1], NORM_CHUNK):
            ks = slice(k0, k0 + NORM_CHUNK)
            hk = ((x[:, ks] * inv * g_ref[:, ks]) * (1.0 + sc_ref[:, ks]) + sh_ref[:, ks]).astype(BF16)
            h_s[:, ks] = hk
            part = _dot(hk, w_ref[ks, :])
            acc = part if acc is None else acc + part
        o_ref[...] = acc

    @pl.when(j > 0)
    def _():
        o_ref[...] = _dot(h_s[...], w_ref[...])


def _projection(x, g, sh, sc, w, bmap, tm, tn):
    m, d = x.shape
    n = w.shape[1]
    return pl.pallas_call(
        _proj_kernel,
        grid=(m // tm, n // tn),
        in_specs=[pl.BlockSpec((tm, d), lambda i, j: (i, 0)),
                  pl.BlockSpec((1, d), lambda i, j: (0, 0)),
                  _mod_spec(d, bmap), _mod_spec(d, bmap),
                  pl.BlockSpec((d, tn), lambda i, j: (0, j))],
        out_specs=pl.BlockSpec((tm, tn), lambda i, j: (i, j)),
        out_shape=jax.ShapeDtypeStruct((m, n), F32),
        scratch_shapes=[pltpu.VMEM((tm, d), BF16)],
        compiler_params=_cparams("parallel", "arbitrary"),
        name="projection",
    )(x, g, sh, sc, w)


def _out_res_kernel(a_ref, w_ref, x_ref, mg_ref, o_ref):
    o_ref[...] = x_ref[...] + mg_ref[...] * _dot(a_ref[...], w_ref[...])


def _ffn_kernel(x_ref, g_ref, sh_ref, sc_ref, mg_ref, wg_ref, wu_ref, wo_ref, fg_ref, o_ref, h_s, acc_s,
                *, final_norm):
    j = pl.program_id(1)

    def swiglu(a, u):
        return _dot((a * _sigmoid(a) * u).astype(BF16), wo_ref[...])

    @pl.when(j == 0)
    def _():
        x = x_ref[...]
        inv = lax.rsqrt(jnp.mean(x * x, axis=-1, keepdims=True) + RMS_EPS)
        a = u = None
        for k0 in range(0, x.shape[1], NORM_CHUNK):
            ks = slice(k0, k0 + NORM_CHUNK)
            hk = ((x[:, ks] * inv * g_ref[:, ks]) * (1.0 + sc_ref[:, ks]) + sh_ref[:, ks]).astype(BF16)
            h_s[:, ks] = hk
            pa, pu = _dot(hk, wg_ref[ks, :]), _dot(hk, wu_ref[ks, :])
            a, u = (pa, pu) if a is None else (a + pa, u + pu)
        acc_s[...] = swiglu(a, u)

    @pl.when(j > 0)
    def _():
        h = h_s[...]
        acc_s[...] += swiglu(_dot(h, wg_ref[...]), _dot(h, wu_ref[...]))

    @pl.when(j == pl.num_programs(1) - 1)
    def _():
        y = x_ref[...] + mg_ref[...] * acc_s[...]
        if final_norm:
            y = y * lax.rsqrt(jnp.mean(y * y, axis=-1, keepdims=True) + RMS_EPS) * fg_ref[...]
        o_ref[...] = y


def _ffn(x, g, sh, sc, mg, w_in, w_out, layer, final_g, final_norm, bmap, tm, tf):
    m, d = x.shape
    f = w_out.shape[1]
    nf = f // tf
    return pl.pallas_call(
        functools.partial(_ffn_kernel, final_norm=final_norm),
        grid=(m // tm, nf),
        in_specs=[pl.BlockSpec((tm, d), lambda i, j: (i, 0)),
                  pl.BlockSpec((1, d), lambda i, j: (0, 0)),
                  _mod_spec(d, bmap), _mod_spec(d, bmap), _mod_spec(d, bmap),
                  pl.BlockSpec((None, d, tf), lambda i, j: (layer, 0, j)),
                  pl.BlockSpec((None, d, tf), lambda i, j: (layer, 0, nf + j)),
                  pl.BlockSpec((None, tf, d), lambda i, j: (layer, j, 0)),
                  pl.BlockSpec((1, d), lambda i, j: (0, 0))],
        out_specs=pl.BlockSpec((tm, d), lambda i, j: (i, 0)),
        out_shape=jax.ShapeDtypeStruct((m, d), F32),
        scratch_shapes=[pltpu.VMEM((tm, d), BF16), pltpu.VMEM((tm, d), F32)],
        compiler_params=_cparams("parallel", "arbitrary"),
        name="ffn",
    )(x, g, sh, sc, mg, w_in, w_in, w_out, final_g)


def _conv3_rows(acc, cw, row_len):
    tm = acc.shape[0]
    rid = lax.broadcasted_iota(jnp.int32, (tm, 1), 0) % row_len
    up = jnp.where(rid == 0, 0.0, pltpu.roll(acc, 1, 0))
    dn = jnp.where(rid == row_len - 1, 0.0, pltpu.roll(acc, tm - 1, 0))
    return cw[0:1] * up + cw[1:2] * acc + cw[2:3] * dn


def _hy_in_kernel(x_ref, g_ref, sh_ref, sc_ref, w_ref, cw_ref, o_ref, h_s, *, row_len):
    j = pl.program_id(1)
    tiles = [slice(n0, n0 + MXU_N) for n0 in range(0, o_ref.shape[1], MXU_N)]

    def epilogue(ns, acc):
        o_ref[:, ns] = _conv3_rows(acc, cw_ref[:, ns], row_len).astype(o_ref.dtype)

    @pl.when(j == 0)
    def _():
        x = x_ref[...]
        inv = lax.rsqrt(jnp.mean(x * x, axis=-1, keepdims=True) + RMS_EPS)
        accs = [None] * len(tiles)
        for k0 in range(0, x.shape[1], NORM_CHUNK):
            ks = slice(k0, k0 + NORM_CHUNK)
            hk = ((x[:, ks] * inv * g_ref[:, ks]) * (1.0 + sc_ref[:, ks]) + sh_ref[:, ks]).astype(BF16)
            h_s[:, ks] = hk
            for t, ns in enumerate(tiles):
                part = _dot(hk, w_ref[ks, ns])
                accs[t] = part if accs[t] is None else accs[t] + part
        for ns, acc in zip(tiles, accs):
            epilogue(ns, acc)

    @pl.when(j > 0)
    def _():
        h = h_s[...]
        for ns in tiles:
            epilogue(ns, _dot(h, w_ref[:, ns]))


def _hyena_in(x, g, sh, sc, w, cw, bmap, tm, tn, row_len, out_dtype):
    m, d = x.shape
    nd = d // tn
    return pl.pallas_call(
        functools.partial(_hy_in_kernel, row_len=row_len),
        grid=(m // tm, 3 * nd),
        in_specs=[pl.BlockSpec((tm, d), lambda i, j: (i, 0)),
                  pl.BlockSpec((1, d), lambda i, j: (0, 0)),
                  _mod_spec(d, bmap), _mod_spec(d, bmap),
                  pl.BlockSpec((d, tn), lambda i, j: (0, j)),
                  pl.BlockSpec((3, tn), lambda i, j: (0, j))],
        out_specs=pl.BlockSpec((None, tm, tn), lambda i, j: (j // nd, i, j % nd)),
        out_shape=jax.ShapeDtypeStruct((3, m, d), out_dtype),
        scratch_shapes=[pltpu.VMEM((tm, d), BF16)],
        compiler_params=_cparams("parallel", "arbitrary"),
        name="hyena_in",
    )(x, g, sh, sc, w, cw)


def _out_res(a, w, x, mg, bmap, tm):
    m, d = x.shape
    return pl.pallas_call(
        _out_res_kernel,
        grid=(m // tm,),
        in_specs=[pl.BlockSpec((tm, d), lambda i: (i, 0)),
                  pl.BlockSpec((d, d), lambda i: (0, 0)),
                  pl.BlockSpec((tm, d), lambda i: (i, 0)),
                  _mod_spec(d, bmap)],
        out_specs=pl.BlockSpec((tm, d), lambda i: (i, 0)),
        out_shape=jax.ShapeDtypeStruct((m, d), F32),
        compiler_params=_cparams("parallel"),
        name="out_residual",
    )(a, w, x, mg)


def _filter_kernel(z_ref, w1h_ref, w1l_ref, b1_ref, fq_ref, w2h_ref, w2l_ref, b2_ref,
                   wo_ref, dl_ref, h_ref, nrm_ref):
    z = z_ref[...]
    t = z[:, 0:1]
    valid = z[:, FILTER_EMB:FILTER_EMB + 1]
    a1 = _dot3(z, w1h_ref[...], w1l_ref[...]) + b1_ref[...]
    hid = jnp.sin(fq_ref[0:1, :] * a1)
    a2 = _dot3(hid, w2h_ref[...], w2l_ref[...]) + b2_ref[...]
    hid = jnp.sin(fq_ref[1:2, :] * a2)
    hh, hl = _split(hid)
    h = _dot(jnp.concatenate([hh, hl], axis=1), wo_ref[...]) * jnp.exp(-t * dl_ref[...]) * valid
    h_ref[...] = h.reshape(h_ref.shape)

    @pl.when((pl.program_id(1) == 0) & (pl.program_id(2) == 0))
    def _():
        nrm_ref[...] = jnp.zeros_like(nrm_ref)

    nrm_ref[...] += jnp.sum(jnp.abs(h), axis=0, keepdims=True)


def _pad2(a, rows, cols):
    return jnp.pad(a, ((0, rows - a.shape[0]), (0, cols - a.shape[1])))


def _filter_positions(seq, perm):
    if perm:
        r_hi = seq // FFT_P
        gt, ng = 2 * SUBLANES, FFT_P // (2 * SUBLANES)
        dirs, g, tl, th = np.meshgrid(np.arange(2), np.arange(ng), np.arange(gt), np.arange(r_hi), indexing="ij")
        n = FFT_P * (r_hi * dirs + th) + g * gt + tl
    else:
        r_hi, gt, ng = seq, 1, 1
        dirs, g, tl, th = np.meshgrid(np.arange(2), np.arange(1), np.arange(1), np.arange(seq), indexing="ij")
        n = seq * dirs + th
    n = n.reshape(-1)
    pos = np.where(n < seq, n, 2 * seq - 1 - n).astype(np.float32)
    z = np.zeros((n.size, LANES), np.float32)
    z[:, 0] = pos / np.float32(max(seq - 1, 1))
    bands = np.arange(1, FILTER_BANDS + 1, dtype=np.float32)
    ang = (np.float32(2.0 * math.pi / seq) * pos)[:, None] * bands[None, :]
    z[:, 1:1 + FILTER_BANDS] = np.cos(ang.astype(np.float64))
    z[:, 1 + FILTER_BANDS:FILTER_EMB] = -np.sin(ang.astype(np.float64))
    z[:, FILTER_EMB] = n != seq
    return jnp.asarray(z), gt, ng, r_hi


def _hyena_filter_time(fw1, fb1, ffreq, fw2, fb2, fwout, seq, perm):
    od = fwout.shape[1] // 2
    d = od // HYENA_ORDER
    z, gt, ng, r_hi = _filter_positions(seq, perm)
    rows = gt * r_hi
    ct = min(2 * FFT_DT_OUTER, od)
    hidden = LANES
    w1h, w1l = _split(_pad2(fw1, LANES, hidden))
    w2h, w2l = _split(_pad2(fw2, hidden, hidden))
    wo = _pad2(fwout, hidden, fwout.shape[1]).astype(BF16)
    wo2 = jnp.concatenate([wo, wo], axis=0)
    b1 = _pad2(fb1[None, :], 1, hidden)
    b2 = _pad2(fb2[None, :], 1, hidden)
    fq = _pad2(ffreq, 2, hidden)
    deltas = np.abs(np.linspace(HYENA_DECAY_MIN, HYENA_DECAY_MAX, d, dtype=np.float32))
    dl = jnp.asarray(np.tile(deltas, HYENA_ORDER)[None, :])
    nct = od // ct
    small = lambda shape: pl.BlockSpec(shape, lambda c, di, g: (0, 0))
    wspec = pl.BlockSpec((2 * hidden, ct), lambda c, di, g: (0, di * nct + c))
    return pl.pallas_call(
        _filter_kernel,
        grid=(nct, 2, ng),
        in_specs=[pl.BlockSpec((rows, LANES), lambda c, di, g: (di * ng + g, 0)),
                  small((LANES, hidden)), small((LANES, hidden)), small((1, hidden)), small((2, hidden)),
                  small((hidden, hidden)), small((hidden, hidden)), small((1, hidden)),
                  wspec,
                  pl.BlockSpec((1, ct), lambda c, di, g: (0, c))],
        out_specs=[pl.BlockSpec((None, gt, None, r_hi, ct), lambda c, di, g: (g, 0, di, 0, c)),
                   pl.BlockSpec((1, ct), lambda c, di, g: (0, c))],
        out_shape=[jax.ShapeDtypeStruct((ng, gt, 2, r_hi, od), F32), jax.ShapeDtypeStruct((1, od), F32)],
        compiler_params=_cparams("parallel", "arbitrary", "arbitrary"),
        name="hyena_filter",
    )(z, w1h, w1l, b1, fq, w2h, w2l, b2, wo2, dl)


def _fft_consts(seq, k_in):
    n = 2 * seq
    p = FFT_P
    q = n // p
    qh = q // 2
    ka_used = qh + 1
    ka_pad = -(-ka_used // SUBLANES) * SUBLANES
    ka = np.arange(ka_used)
    tl = np.arange(p)
    th = np.arange(k_in)
    theta = 2 * np.pi * (ka[None, :, None] * th[None, None, :] / q + ka[None, :, None] * tl[:, None, None] / n)
    f1 = np.zeros((p, 2 * ka_pad, k_in))
    f1[:, :ka_used] = np.cos(theta)
    f1[:, ka_pad:ka_pad + ka_used] = -np.sin(theta)
    k3 = 3 * k_in
    f1c = _const3(f1, pad_cols=-(-k3 // LANES) * LANES - k3)
    f1s = jnp.asarray(np.pad(f1, ((0, 0), (0, 0), (0, -(-k_in // LANES) * LANES - k_in))).astype(ml_dtypes.bfloat16))
    ang = 2 * np.pi * np.outer(np.arange(p), np.arange(p)) / p
    cc, sc = np.cos(ang), np.sin(ang)
    m2f_np, m2i_np = np.block([[cc, sc], [-sc, cc]]), np.block([[cc, -sc], [sc, cc]])
    m2f1, m2i1 = (jnp.asarray(m.astype(ml_dtypes.bfloat16)) for m in (m2f_np, m2i_np))
    tho = np.arange(qh)
    phi = 2 * np.pi * (tho[None, :, None] * ka[None, None, :] / q + ka[None, None, :] * tl[:, None, None] / n)
    wgt = np.where((ka == 0) | (ka == qh), 1.0, 2.0) / n
    kr = 2 * ka_pad
    g = np.zeros((p, qh, -(-kr // LANES) * LANES))
    col = 2 * SUBLANES * (ka // SUBLANES) + ka % SUBLANES
    g[:, :, col] = wgt * np.cos(phi)
    g[:, :, col + SUBLANES] = -wgt * np.sin(phi)
    gc = jnp.asarray(g.astype(ml_dtypes.bfloat16))
    return dict(q=q, qh=qh, ka_used=ka_used, ka_pad=ka_pad, f1=f1c, f1s=f1s, m2f1=m2f1, m2i1=m2i1, g=gc)


def _s1_kernel(x_ref, f_ref, ar_ref, ai_ref, a_s, *, group, ka_pad, pad_rows, natural, split):
    x = x_ref[...].astype(F32)
    if natural:
        x = jnp.swapaxes(x, 0, 1)
    zpad = None if split else jnp.zeros((pad_rows, x.shape[-1]), BF16)
    for j in range(group):
        xx = _stack3(x[j], pad_rows) if split else jnp.concatenate([x[j].astype(BF16), zpad], axis=0)
        a_s[j] = _dot(f_ref[j], xx)
    a = jnp.swapaxes(a_s[...], 0, 1)
    ar_ref[...] = a[:ka_pad].astype(ar_ref.dtype)
    ai_ref[...] = a[ka_pad:].astype(ai_ref.dtype)


def _fft_stage1(x5, part, f1c, ka_pad, group, dt, out_dtype, natural, split):
    if natural:
        _, b, k_in, p, d = x5.shape
        xspec = pl.BlockSpec((None, None, k_in, group, dt), lambda bi, g, c: (part, bi, 0, g, c))
    else:
        _, b, p, k_in, d = x5.shape
        xspec = pl.BlockSpec((None, None, group, k_in, dt), lambda bi, g, c: (part, bi, g, 0, c))
    kc = f1c.shape[-1]
    out = jax.ShapeDtypeStruct((b, ka_pad, p, d), out_dtype)
    ospec = pl.BlockSpec((None, ka_pad, group, dt), lambda bi, g, c: (bi, 0, g, c))
    pad_rows = kc - (3 if split else 1) * k_in
    return pl.pallas_call(
        functools.partial(_s1_kernel, group=group, ka_pad=ka_pad, pad_rows=pad_rows, natural=natural, split=split),
        grid=(b, p // group, d // dt),
        in_specs=[xspec,
                  pl.BlockSpec((group, 2 * ka_pad, kc), lambda bi, g, c: (g, 0, 0))],
        out_specs=[ospec, ospec],
        out_shape=[out, out],
        scratch_shapes=[pltpu.VMEM((group, 2 * ka_pad, dt), F32)],
        compiler_params=_cparams("parallel", "parallel", "parallel"),
        name="fft_stage1",
    )(x5, f1c)


def _s2_filter_kernel(ar_ref, ai_ref, m_ref, inv_ref, hr_ref, hi_ref, *, p):
    for i in range(ar_ref.shape[0]):
        x = jnp.concatenate([ar_ref[i], ai_ref[i]], axis=0)
        y = _dot(m_ref[...], x.astype(BF16)) * inv_ref[...]
        hr_ref[i] = y[:p]
        hi_ref[i] = y[p:]


def _fft_stage2_filter(ar, ai, m2f, inv_nrm, dt):
    ka, p, d = ar.shape
    spec = pl.BlockSpec((SUBLANES, p, dt), lambda k, c: (k, 0, c))
    out = jax.ShapeDtypeStruct((ka, p, d), F32)
    return pl.pallas_call(
        functools.partial(_s2_filter_kernel, p=p),
        grid=(ka // SUBLANES, d // dt),
        in_specs=[spec, spec,
                  pl.BlockSpec((2 * p, 2 * p), lambda k, c: (0, 0)),
                  pl.BlockSpec((1, dt), lambda k, c: (0, c))],
        out_specs=[spec, spec],
        out_shape=[out, out],
        compiler_params=_cparams("parallel", "parallel"),
        name="fft_stage2_filter",
    )(ar, ai, m2f, inv_nrm)


def _s2_kernel(ar_ref, ai_ref, hr_ref, hi_ref, mf_ref, mi_ref, c_ref, x_s, y_s, c_s, *, p, ka_used):
    kg = pl.program_id(0)
    assert ka_used % SUBLANES == 1

    def run(rows):
        for i in rows:
            x_s[i] = _dot(mf_ref[...], jnp.concatenate([ar_ref[i], ai_ref[i]], axis=0))
        for i in rows:
            xr, xi = x_s[i, :p], x_s[i, p:]
            hr, hi = hr_ref[i], hi_ref[i]
            y_s[i, :p] = (xr * hr - xi * hi).astype(BF16)
            y_s[i, p:] = (xr * hi + xi * hr).astype(BF16)
        for i in rows:
            c = _dot(mi_ref[...], y_s[i])
            c_s[i] = c[:p]
            c_s[SUBLANES + i] = c[p:]

    run(range(1))
    full = (kg + 1) * SUBLANES <= ka_used

    @pl.when(full)
    def _():
        run(range(1, SUBLANES))

    @pl.when(jnp.logical_not(full))
    def _():
        for i in range(1, SUBLANES):
            c_s[i] = jnp.zeros(c_s.shape[1:], F32)
            c_s[SUBLANES + i] = jnp.zeros(c_s.shape[1:], F32)

    c_ref[...] = jnp.swapaxes(c_s[...], 0, 1).astype(c_ref.dtype)


def _fft_stage2(ar, ai, hr, hi, m2f, m2i, ka_used, h_col0, dt):
    b, ka, p, d = ar.shape
    hc = h_col0 // dt
    aspec = pl.BlockSpec((None, SUBLANES, p, dt), lambda k, c, bi: (bi, k, 0, c))
    hspec = pl.BlockSpec((SUBLANES, p, dt), lambda k, c, bi: (k, 0, hc + c))
    mspec = pl.BlockSpec((2 * p, 2 * p), lambda k, c, bi: (0, 0))
    return pl.pallas_call(
        functools.partial(_s2_kernel, p=p, ka_used=ka_used),
        grid=(ka // SUBLANES, d // dt, b),
        in_specs=[aspec, aspec, hspec, hspec, mspec, mspec],
        out_specs=pl.BlockSpec((None, p, 2 * SUBLANES, dt), lambda k, c, bi: (bi, 0, k, c)),
        out_shape=jax.ShapeDtypeStruct((b, p, 2 * ka, d), BF16),
        scratch_shapes=[pltpu.VMEM((SUBLANES, 2 * p, dt), F32), pltpu.VMEM((SUBLANES, 2 * p, dt), BF16),
                        pltpu.VMEM((2 * SUBLANES, p, dt), F32)],
        compiler_params=_cparams("parallel", "parallel", "parallel"),
        name="fft_stage2",
    )(ar, ai, hr, hi, m2f, m2i)


def _s3_kernel(c_ref, g_ref, v_ref, x_ref, sk_ref, z_ref, y_s, *, group, pad_rows):
    zpad = jnp.zeros((pad_rows, c_ref.shape[-1]), BF16)
    for j in range(group):
        y_s[j] = _dot(g_ref[j], jnp.concatenate([c_ref[j], zpad], axis=0))
    y = jnp.swapaxes(y_s[...], 0, 1)
    v = v_ref[...].astype(F32)
    z_ref[...] = ((y + v * sk_ref[...]) * x_ref[...].astype(F32)).astype(z_ref.dtype)


def _fft_stage3(c, gc, v5, vpart, x5, xpart, skip, group, dt, out_dtype):
    b, p, kr, d = c.shape
    qh = v5.shape[2]
    kc = gc.shape[-1]
    pspec = lambda part: pl.BlockSpec((None, None, qh, group, dt), lambda bi, g, cc: (part, bi, 0, g, cc))
    return pl.pallas_call(
        functools.partial(_s3_kernel, group=group, pad_rows=kc - kr),
        grid=(b, p // group, d // dt),
        in_specs=[pl.BlockSpec((None, group, kr, dt), lambda bi, g, cc: (bi, g, 0, cc)),
                  pl.BlockSpec((group, qh, kc), lambda bi, g, cc: (g, 0, 0)),
                  pspec(vpart), pspec(xpart),
                  pl.BlockSpec((1, dt), lambda bi, g, cc: (0, cc))],
        out_specs=pspec(0),
        out_shape=jax.ShapeDtypeStruct((1, b, qh, p, d), out_dtype),
        scratch_shapes=[pltpu.VMEM((group, qh, dt), F32)],
        compiler_params=_cparams("parallel", "parallel", "parallel"),
        name="fft_stage3",
    )(c, gc, v5, x5, skip)


def _long_conv_lat(v5, vpart, x5, xpart, hr, hi, skip, plan, order, out_dtype):
    d = v5.shape[-1]
    dt1 = min(FFT_DT_OUTER, d)
    dt2 = min(FFT_DT_INNER, d)
    ar, ai = _fft_stage1(v5, vpart, plan["f1s"], plan["ka_pad"], FFT_GROUP, dt1, BF16, True, False)
    c = _fft_stage2(ar, ai, hr, hi, plan["m2f1"], plan["m2i1"], plan["ka_used"], order * d, dt2)
    return _fft_stage3(c, plan["g"], v5, vpart, x5, xpart, skip, FFT_GROUP, dt1, out_dtype)


def _filter_spectrum_lat(hy, seq, plan_f):
    fw1, fb1, ffreq, fw2, fb2, fwout = hy
    q = plan_f["q"]
    h_time, nrm = _hyena_filter_time(fw1, fb1, ffreq, fw2, fb2, fwout, seq, True)
    od = h_time.shape[-1]
    h5 = h_time.reshape(1, 1, FFT_P, q, od)
    dt = min(FFT_DT_OUTER, od)
    ar, ai = _fft_stage1(h5, 0, plan_f["f1"], plan_f["ka_pad"], SUBLANES, dt, F32, False, True)
    return _fft_stage2_filter(ar[0], ai[0], plan_f["m2f1"], 1.0 / nrm, dt)


def _dense_consts(seq):
    n = 2 * seq
    kf = seq + 1
    kf_pad = -(-kf // LANES) * LANES
    k = np.arange(kf)
    fwd_full = np.zeros((2 * kf_pad, n))
    ang = 2 * np.pi * np.outer(k, np.arange(n)) / n
    fwd_full[:kf] = np.cos(ang)
    fwd_full[kf_pad:kf_pad + kf] = -np.sin(ang)
    wgt = np.where((k == 0) | (k == seq), 1.0, 2.0) / n
    inv = np.zeros((seq, 2 * kf_pad))
    angi = 2 * np.pi * np.outer(np.arange(seq), k) / n
    inv[:, :kf] = wgt * np.cos(angi)
    inv[:, kf_pad:kf_pad + kf] = -wgt * np.sin(angi)
    return dict(kf_pad=kf_pad, fwd_full=_const3(fwd_full), fwd=_const3(fwd_full[:, :seq]), inv=_const3(inv))


def _dense_spec_kernel(h_ref, f_ref, inv_ref, hr_ref, hi_ref, *, kf_pad):
    y = _dot(f_ref[...], _stack3(h_ref[...])) * inv_ref[...]
    hr_ref[...] = y[:kf_pad]
    hi_ref[...] = y[kf_pad:]


def _filter_spectrum_ctx(hy, seq, cons):
    fw1, fb1, ffreq, fw2, fb2, fwout = hy
    h_time, nrm = _hyena_filter_time(fw1, fb1, ffreq, fw2, fb2, fwout, seq, False)
    od = h_time.shape[-1]
    n2 = 2 * seq
    h_time = h_time.reshape(n2, od)
    kf_pad = cons["kf_pad"]
    ct = min(FFT_DT_INNER, od)
    out = jax.ShapeDtypeStruct((kf_pad, od), F32)
    ospec = pl.BlockSpec((kf_pad, ct), lambda c: (0, c))
    return pl.pallas_call(
        functools.partial(_dense_spec_kernel, kf_pad=kf_pad),
        grid=(od // ct,),
        in_specs=[pl.BlockSpec((n2, ct), lambda c: (0, c)),
                  pl.BlockSpec((2 * kf_pad, 3 * n2), lambda c: (0, 0)),
                  pl.BlockSpec((1, ct), lambda c: (0, c))],
        out_specs=[ospec, ospec],
        out_shape=[out, out],
        compiler_params=_cparams("parallel"),
        name="dense_filter_spectrum",
    )(h_time, cons["fwd_full"], 1.0 / nrm)


def _dense_conv_kernel(v_ref, x1_ref, x2_ref, h1r_ref, h1i_ref, h2r_ref, h2i_ref, sk_ref,
                       f_ref, g_ref, z_ref, *, kf_pad):
    def conv(u, hr, hi):
        s = _dot(f_ref[...], _stack3(u))
        sr, si = s[:kf_pad], s[kf_pad:]
        y = jnp.concatenate([sr * hr - si * hi, sr * hi + si * hr], axis=0)
        return _dot(g_ref[...], _stack3(y))

    v = v_ref[...]
    z1 = x1_ref[...] * (conv(v, h1r_ref[...], h1i_ref[...]) + v * sk_ref[0:1, :])
    z2 = x2_ref[...] * (conv(z1, h2r_ref[...], h2i_ref[...]) + z1 * sk_ref[1:2, :])
    z_ref[...] = z2.astype(z_ref.dtype)


def _hyena_core_ctx(u3, hr, hi, fskip, cons, b, seq):
    d = u3.shape[-1]
    dt = min(MXU_N, d)
    nd = d // dt
    kf_pad = cons["kf_pad"]
    uspec = lambda part: pl.BlockSpec((None, seq, dt), lambda bi, c: (part, bi, c))
    hspec = lambda order: pl.BlockSpec((kf_pad, dt), lambda bi, c: (0, order * nd + c))
    return pl.pallas_call(
        functools.partial(_dense_conv_kernel, kf_pad=kf_pad),
        grid=(b, nd),
        in_specs=[uspec(0), uspec(1), uspec(2), hspec(0), hspec(0), hspec(1), hspec(1),
                  pl.BlockSpec((HYENA_ORDER, dt), lambda bi, c: (0, c)),
                  pl.BlockSpec((2 * kf_pad, 3 * seq), lambda bi, c: (0, 0)),
                  pl.BlockSpec((seq, 6 * kf_pad), lambda bi, c: (0, 0))],
        out_specs=pl.BlockSpec((seq, dt), lambda bi, c: (bi, c)),
        out_shape=jax.ShapeDtypeStruct((b * seq, d), BF16),
        compiler_params=_cparams("parallel", "parallel"),
        name="hyena_core_ctx",
    )(u3, u3, u3, hr, hi, hr, hi, fskip, cons["fwd"], cons["inv"])


def _rec_consts(chunk):
    t = np.arange(chunk)
    coefs, masks = [], []
    for direction in (0, 1):
        if direction == 0:
            rows = [t[None, :] <= t[:, None], t[None, :] > t[:, None]]
        else:
            rows = [t[None, :] >= t[:, None], t[None, :] < t[:, None]]
        mk = [np.eye(chunk)]
        m = chunk // 2
        while m >= 1:
            blk = t // (2 * m)
            half = (t // m) % 2
            mid = blk * 2 * m + m
            e = np.zeros((chunk, chunk))
            for r in range(chunk):
                if direction == 0:
                    if half[r] == 1:
                        e[r, mid[r]:r + 1] = 1
                    else:
                        e[r, r + 1:mid[r]] = 1
                else:
                    if half[r] == 0:
                        e[r, r:mid[r]] = 1
                    else:
                        e[r, mid[r]:r] = 1
            same = blk[:, None] == blk[None, :]
            if direction == 0:
                mk.append(same & (half[:, None] == 1) & (half[None, :] == 0))
            else:
                mk.append(same & (half[:, None] == 0) & (half[None, :] == 1))
            rows.append(e)
            m //= 2
        rows.append(np.ones((REC_TOTAL_ROWS, chunk)))
        a = np.concatenate([np.asarray(r, np.float64) for r in rows], axis=0)
        a3 = np.concatenate([a] * REC_GATE_TERMS, axis=1)
        pad = -(-a3.shape[1] // LANES) * LANES - a3.shape[1]
        a3 = np.pad(a3, ((0, 0), (0, pad)))
        coefs.append(a3)
        masks.append(np.stack([np.asarray(x, np.float32) for x in mk]))
    return (jnp.asarray(np.stack(coefs), dtype=BF16), jnp.asarray(np.stack(masks), dtype=F32))


def _rec_core(load_qkg, v_ref, coef_ref, mask_ref, s0_ref, o_ref, sfin_ref, q_s, k_s, st_s, ex_s, att_s,
              *, chunk, nsub, heads, dk, dv):
    di = pl.program_id(1)
    c = pl.program_id(2)
    levels = int(math.log2(chunk))
    tot = (2 + levels) * chunk

    @pl.when(c == 0)
    def _():
        st_s[...] = s0_ref[...]

    offs = [pl.multiple_of((s + di * (nsub - 1 - 2 * s)) * chunk, chunk) for s in range(nsub)]

    for s in range(nsub):
        q, k, g = load_qkg(pl.ds(offs[s], chunk))
        q_s[s] = q
        k_s[s] = k
        terms, rest = [], g
        for _ in range(REC_GATE_TERMS):
            terms.append(rest.astype(BF16))
            rest = rest - terms[-1].astype(F32)
        pad = coef_ref.shape[-1] - REC_GATE_TERMS * chunk
        if pad:
            terms.append(jnp.zeros((pad, g.shape[1]), BF16))
        ex_s[s] = jnp.exp(_dot(coef_ref[...], jnp.concatenate(terms, axis=0)))

    for s in range(nsub):
        for h in range(heads):
            ks = slice(h * dk, (h + 1) * dk)
            qh = q_s[s, :, ks]
            kh = k_s[s, :, ks]
            att = mask_ref[0] * _dot_nt(qh.astype(BF16), kh.astype(BF16))
            for lv in range(levels):
                e = ex_s[s, (2 + lv) * chunk:(3 + lv) * chunk, ks]
                att += mask_ref[1 + lv] * _dot_nt((qh * e).astype(BF16), (kh * e).astype(BF16))
            att_s[s, h] = att.astype(BF16)

    for s in range(nsub):
        rows = pl.ds(offs[s], chunk)
        for h in range(heads):
            ks = slice(h * dk, (h + 1) * dk)
            vs = slice(h * dv, (h + 1) * dv)
            vh = v_ref[rows, vs].astype(BF16)
            st = st_s[h]
            o = _dot(att_s[s, h], vh)
            o += _dot_nt((q_s[s, :, ks] * ex_s[s, 0:chunk, ks]).astype(BF16), st.astype(BF16))
            o_ref[rows, vs] = o.astype(o_ref.dtype)
            kd = (k_s[s, :, ks] * ex_s[s, chunk:2 * chunk, ks]).astype(BF16)
            st_s[h] = st * ex_s[s, tot:tot + 1, ks] + _dot_tn(vh, kd)

    @pl.when(c == pl.num_programs(2) - 1)
    def _():
        sfin_ref[...] = st_s[...]


def _hgrn_rec_kernel(q_ref, v_ref, f_ref, lb_ref, coef_ref, mask_ref, s0_ref, o_ref, sfin_ref,
                     q_s, k_s, st_s, ex_s, att_s, **kw):
    log_lb = lb_ref[0:1, :]
    log_1mlb = lb_ref[1:2, :]

    def load_qkg(rows):
        qr = q_ref[rows, :]
        fr = f_ref[rows, :]
        t2 = log_1mlb + jnp.minimum(fr, 0.0) - jnp.log(1.0 + jnp.exp(-jnp.abs(fr)))
        mx = jnp.maximum(log_lb, t2)
        g = mx + jnp.log(1.0 + jnp.exp(-jnp.abs(log_lb - t2)))
        return qr * _sigmoid(qr), 1.0 - jnp.exp(g), g

    _rec_core(load_qkg, v_ref, coef_ref, mask_ref, s0_ref, o_ref, sfin_ref, q_s, k_s, st_s, ex_s, att_s, **kw)


def _gla_rec_kernel(q_ref, k_ref, v_ref, a_ref, wup_ref, bup_ref, coef_ref, mask_ref, s0_ref,
                    o_ref, sfin_ref, q_s, k_s, st_s, ex_s, att_s, *, qscale, **kw):
    def load_qkg(rows):
        xg = _dot(a_ref[rows, :].astype(BF16), wup_ref[...]) + bup_ref[...]
        g = (jnp.minimum(xg, 0.0) - jnp.log(1.0 + jnp.exp(-jnp.abs(xg)))) * (1.0 / GLA_GATE_NORM)
        return q_ref[rows, :] * qscale, k_ref[rows, :], g

    _rec_core(load_qkg, v_ref, coef_ref, mask_ref, s0_ref, o_ref, sfin_ref, q_s, k_s, st_s, ex_s, att_s, **kw)


def _rec_call(kind, proj, extra, s0, consts, b, seq, d, heads, dk, dv):
    coef, masks = consts
    chunk = masks.shape[-1]
    rows = REC_STEP_ROWS
    nsub = rows // chunk
    nc = seq // rows
    hk = heads * dk
    hv = heads * dv
    levels = int(math.log2(chunk))
    n_rows = (2 + levels) * chunk + REC_TOTAL_ROWS

    def rmap(col):
        return lambda bi, di, c: (bi * nc + c + di * (nc - 1 - 2 * c), col)

    cspecs = [pl.BlockSpec((None, n_rows, coef.shape[-1]), lambda bi, di, c: (di, 0, 0)),
              pl.BlockSpec((None, levels + 1, chunk, chunk), lambda bi, di, c: (di, 0, 0, 0)),
              pl.BlockSpec((None, None, heads, dv, dk), lambda bi, di, c: (bi, di, 0, 0, 0))]
    if kind == "hgrn":
        lb = extra
        kernel = _hgrn_rec_kernel
        in_specs = [pl.BlockSpec((rows, d), rmap(0)),
                    pl.BlockSpec((rows, d), rmap(1)),
                    pl.BlockSpec((rows, d), lambda bi, di, c: (bi * nc + c + di * (nc - 1 - 2 * c), 3 + di)),
                    pl.BlockSpec((None, 2, d), lambda bi, di, c: (di, 0, 0))] + cspecs
        args = (proj, proj, proj, lb, coef, masks, s0)
        kw = {}
    else:
        wup, bup = extra
        kernel = _gla_rec_kernel
        in_specs = [pl.BlockSpec((rows, hk), rmap(0)),
                    pl.BlockSpec((rows, hk), rmap(1)),
                    pl.BlockSpec((rows, hv), rmap(1)),
                    pl.BlockSpec((rows, LANES), rmap((2 * hk + 2 * hv) // LANES)),
                    pl.BlockSpec((None, LANES, hk), lambda bi, di, c: (di, 0, 0)),
                    pl.BlockSpec((None, 1, hk), lambda bi, di, c: (di, 0, 0))] + cspecs
        args = (proj, proj, proj, proj, wup, bup, coef, masks, s0)
        kw = dict(qscale=float(dk) ** -0.5)
    return pl.pallas_call(
        functools.partial(kernel, chunk=chunk, nsub=nsub, heads=heads, dk=dk, dv=dv, **kw),
        grid=(b, 2, nc),
        in_specs=in_specs,
        out_specs=[pl.BlockSpec((None, rows, hv), lambda bi, di, c: (di, bi * nc + c + di * (nc - 1 - 2 * c), 0)),
                   pl.BlockSpec((None, None, heads, dv, dk), lambda bi, di, c: (bi, di, 0, 0, 0))],
        out_shape=[jax.ShapeDtypeStruct((2, b * seq, hv), BF16),
                   jax.ShapeDtypeStruct((b, 2, heads, dv, dk), F32)],
        scratch_shapes=[pltpu.VMEM((nsub, chunk, hk), F32), pltpu.VMEM((nsub, chunk, hk), F32),
                        pltpu.VMEM((heads, dv, dk), F32), pltpu.VMEM((nsub, n_rows, hk), F32),
                        pltpu.VMEM((nsub, heads, chunk, chunk), BF16)],
        compiler_params=_cparams("parallel", "parallel", "arbitrary"),
        name=kind + "_recurrence",
    )(*args)


def _gated_out_kernel(o_ref, gate_ref, gn_ref, w_ref, x_ref, mg_ref, out_ref, *, heads):
    d = x_ref.shape[1]
    dh = d // heads
    kc = max(dh, 2 * MXU_N)
    acc = None
    for k0 in range(0, d, kc):
        parts = []
        for h0 in range(k0, k0 + kc, dh):
            seg = o_ref[0, :, h0:h0 + dh].astype(F32) + o_ref[1, :, h0:h0 + dh].astype(F32)
            ms = jnp.mean(seg * seg, axis=-1, keepdims=True)
            parts.append(seg * lax.rsqrt(ms + RMS_EPS))
        gate = gate_ref[:, k0:k0 + kc]
        y = jnp.concatenate(parts, axis=1) * gn_ref[:, k0:k0 + kc] * (gate * _sigmoid(gate))
        part = _dot(y.astype(BF16), w_ref[k0:k0 + kc, :])
        acc = part if acc is None else acc + part
    out_ref[...] = x_ref[...] + mg_ref[...] * acc


def _gated_out(o2, proj, gate_col, gn, w, x, mg, bmap, heads, tm):
    m, d = x.shape
    return pl.pallas_call(
        functools.partial(_gated_out_kernel, heads=heads),
        grid=(m // tm,),
        in_specs=[pl.BlockSpec((2, tm, d), lambda i: (0, i, 0)),
                  pl.BlockSpec((tm, d), lambda i: (i, gate_col)),
                  pl.BlockSpec((1, d), lambda i: (0, 0)),
                  pl.BlockSpec((d, d), lambda i: (0, 0)),
                  pl.BlockSpec((tm, d), lambda i: (i, 0)),
                  _mod_spec(d, bmap)],
        out_specs=pl.BlockSpec((tm, d), lambda i: (i, 0)),
        out_shape=jax.ShapeDtypeStruct((m, d), F32),
        compiler_params=_cparams("parallel"),
        name="gated_out",
    )(o2, proj, gn, w, x, mg)


def _recurrent_mix(kind, x_ctx, x_lat, norm, ctx_map, w_in, extra, consts, b, lc, seq, d, heads, dk, dv, tn):
    g1, sh, sc = norm
    tm_c, tm_l = min(TM_PROJ, x_ctx.shape[0]), min(TM_PROJ, x_lat.shape[0])
    proj_ctx = _projection(x_ctx, g1, sh, sc, w_in, ctx_map, tm_c, tn)
    proj_lat = _projection(x_lat, g1, sh, sc, w_in, lambda i: (i * tm_l) // seq, tm_l, tn)
    s0 = jnp.zeros((b, 2, heads, dv, dk), F32)
    o_ctx, s_ctx = _rec_call(kind, proj_ctx, extra, s0, consts, b, lc, d, heads, dk, dv)
    o_lat, _ = _rec_call(kind, proj_lat, extra, s_ctx, consts, b, seq, d, heads, dk, dv)
    return (proj_ctx, o_ctx), (proj_lat, o_lat)


def kernel(x, c, ctx, c_ctx, w_mod, b_mod, norm1_g, norm2_g, w_ffn_in, w_ffn_out, final_g, hy_w_in, hy_conv_w, hy_fw1, hy_fb1, hy_ffreq, hy_fw2, hy_fb2, hy_fwout, hy_fskip, hy_w_out, hg_w_in, hg_lb_logits, hg_onorm_g, hg_w_out, gla_w_in, gla_w_up, gla_b_up, gla_onorm_g, gla_w_out):
    b, seq, d = x.shape
    lc = ctx.shape[1]
    depth = w_mod.shape[0]
    assert b + 1 <= MOD_ROWS and seq % (FFT_P * 2) == 0 and FFT_P % GRID_W == 0
    m_lat, m_ctx = b * seq, b * lc
    ctx_row = b
    tm_lat, tm_ctx = TM_ROWS, min(TM_ROWS, m_ctx)
    tm_in = TM_PROJ
    lat_map = lambda i: (i * tm_lat) // seq
    lat_map_in = lambda i: (i * tm_in) // seq
    ctx_map = lambda i: ctx_row

    xl = x.reshape(m_lat, d)
    xc = ctx.reshape(m_ctx, d)
    c8 = jnp.zeros((MOD_ROWS, d), F32).at[:b].set(c).at[ctx_row].set(c_ctx)
    mod = _modulation(c8, w_mod, b_mod).reshape(depth, MOD_ROWS, N_MOD, 1, d)

    plan = _fft_consts(seq, seq // FFT_P)
    plan_f = _fft_consts(seq, 2 * seq // FFT_P)
    dense = _dense_consts(lc)
    rec_consts = {1: _rec_consts(HGRN_CHUNK), 2: _rec_consts(GLA_CHUNK)}
    qh = plan["qh"]

    wb_ffn_in, wb_ffn_out = _to_bf16(w_ffn_in), _to_bf16(w_ffn_out)
    wb_hy_in, wb_hy_out = _to_bf16(hy_w_in), _to_bf16(hy_w_out)
    wb_hg_in, wb_hg_out = _to_bf16(hg_w_in), _to_bf16(hg_w_out)
    wb_gla_in, wb_gla_out = _to_bf16(gla_w_in), _to_bf16(gla_w_out)

    for i in range(depth):
        last = i == depth - 1
        kind, j = i % N_MIXERS, i // N_MIXERS
        mv = [mod[i, :, k] for k in range(N_MOD)]
        g1 = norm1_g[i][None, :]
        need_ctx = (not last) or kind != 0
        if kind == 0:
            w_in, w_out = wb_hy_in[j], wb_hy_out[j]
            hy = (hy_fw1[j], hy_fb1[j], hy_ffreq[j], hy_fw2[j], hy_fb2[j], hy_fwout[j])
            u = _hyena_in(xl, g1, mv[0], mv[1], w_in, hy_conv_w[j], lat_map_in,
                          tm_in, min(TN_HYENA_IN, d), GRID_W, BF16).reshape(3, b, qh, FFT_P, d)
            hr, hi = _filter_spectrum_lat(hy, seq, plan_f)
            z1 = _long_conv_lat(u, 0, u, 1, hr, hi, hy_fskip[j, 0][None, :], plan, 0, BF16)
            z2 = _long_conv_lat(z1, 0, u, 2, hr, hi, hy_fskip[j, 1][None, :], plan, 1, BF16)
            xl = _out_res(z2.reshape(m_lat, d), w_out, xl, mv[2], lat_map, tm_lat)
            if need_ctx:
                uc = _hyena_in(xc, g1, mv[0], mv[1], w_in, hy_conv_w[j], ctx_map, lc, min(TN_PROJ, d), lc, F32)
                hcr, hci = _filter_spectrum_ctx(hy, lc, dense)
                zc = _hyena_core_ctx(uc, hcr, hci, hy_fskip[j], dense, b, lc)
                xc = _out_res(zc, w_out, xc, mv[2], ctx_map, tm_ctx)
        else:
            if kind == 1:
                heads = d // HGRN_EXPAND
                dk = dv = HGRN_EXPAND
                w_in = wb_hg_in[j]
                lb_cum = jnp.cumsum(jax.nn.softmax(hg_lb_logits.astype(F32), axis=1), axis=1)
                lb = lb_cum[:, i] - lb_cum[:, 0]
                extra = jnp.stack([jnp.log(lb), jnp.log1p(-lb)], axis=1)
                gn, w_out, tn = hg_onorm_g[j], wb_hg_out[j], TN_PROJ
                rkind = "hgrn"
            else:
                heads = GLA_HEADS
                dk, dv = d // 2 // heads, d // heads
                n_in = gla_w_in.shape[-1]
                tn = TN_GLA_PROJ
                n_pad = -(-n_in // tn) * tn
                w_in = jnp.pad(wb_gla_in[j], ((0, 0), (0, n_pad - n_in)))
                r = GLA_GATE_RANK
                wup = jnp.zeros((2, LANES, heads * dk), F32)
                wup = wup.at[0, :r].set(gla_w_up[j, 0]).at[1, r:2 * r].set(gla_w_up[j, 1]).astype(BF16)
                extra = (wup, gla_b_up[j][:, None, :])
                gn, w_out = gla_onorm_g[j], wb_gla_out[j]
                rkind = "gla"
            (p_ctx, o_ctx), (p_lat, o_lat) = _recurrent_mix(
                rkind, xc, xl, (g1, mv[0], mv[1]), ctx_map, w_in, extra, rec_consts[kind],
                b, lc, seq, d, heads, dk, dv, tn)
            gate_col = 2 if kind == 1 else (2 * heads * dk + heads * dv) // d
            xl = _gated_out(o_lat, p_lat, gate_col, gn[None, :], w_out, xl, mv[2], lat_map, heads, tm_lat)
            xc = _gated_out(o_ctx, p_ctx, gate_col, gn[None, :], w_out, xc, mv[2], ctx_map, heads, tm_ctx)
        g2 = norm2_g[i][None, :]
        fg = final_g[None, :]
        xl = _ffn(xl, g2, mv[3], mv[4], mv[5], wb_ffn_in, wb_ffn_out, i, fg, last, lat_map, tm_lat, TF_FFN)
        if not last:
            xc = _ffn(xc, g2, mv[3], mv[4], mv[5], wb_ffn_in, wb_ffn_out, i, fg, False, ctx_map, tm_ctx, TF_FFN)
    return xl.reshape(b, seq, d)
```

```python
import functools
import math

import ml_dtypes
import numpy as np
import jax
import jax.numpy as jnp
from jax import lax
from jax.experimental import pallas as pl
from jax.experimental.pallas import tpu as pltpu

F32 = jnp.float32
BF16 = jnp.bfloat16

N_MOD = 6
N_MIXERS = 3
RMS_EPS = 1e-6
GRID_W = 64
HYENA_ORDER = 2
FILTER_BANDS = 16
FILTER_EMB = 1 + 2 * FILTER_BANDS
HYENA_DECAY_MIN = math.log(1e-2) / 1.5
HYENA_DECAY_MAX = math.log(1e-2) / 0.3
HGRN_EXPAND = 128
GLA_HEADS = 4
GLA_GATE_RANK = 16
GLA_GATE_NORM = 16.0

LANES = 128
SUBLANES = 8
MXU_N = 256
V7X_VMEM_LIMIT = 56 * 1024 * 1024

FFT_P = 128
FFT_GROUP = 4 * SUBLANES
FFT_DT_OUTER = 1024
FFT_DT_INNER = 512
REC_STEP_ROWS = 256
HGRN_CHUNK = 64
GLA_CHUNK = 128
REC_GATE_TERMS = 2
REC_TOTAL_ROWS = 2 * SUBLANES
MOD_ROWS = 8

TM_ROWS = 512
TM_PROJ = 1024
TN_PROJ = 1024
TN_GLA_PROJ = 5 * MXU_N
TN_HYENA_IN = 1024
TF_FFN = 512
NORM_CHUNK = 2 * MXU_N
TN_MOD = 1024


def _cparams(*sem):
    return pltpu.CompilerParams(dimension_semantics=sem, vmem_limit_bytes=V7X_VMEM_LIMIT)


def _dot(a, b):
    return jnp.dot(a, b, preferred_element_type=F32)


def _dot_nt(a, b):
    return lax.dot_general(a, b, (((1,), (1,)), ((), ())), preferred_element_type=F32)


def _dot_tn(a, b):
    return lax.dot_general(a, b, (((0,), (0,)), ((), ())), preferred_element_type=F32)


def _split(x):
    hi = x.astype(BF16)
    lo = (x - hi.astype(F32)).astype(BF16)
    return hi, lo


def _stack3(x, pad_rows=0):
    hi, lo = _split(x)
    parts = [hi, lo, hi]
    if pad_rows:
        parts.append(jnp.zeros((pad_rows, x.shape[1]), BF16))
    return jnp.concatenate(parts, axis=0)


def _const3(c, pad_cols=0):
    hi = c.astype(ml_dtypes.bfloat16)
    lo = (c - hi.astype(np.float64)).astype(ml_dtypes.bfloat16)
    parts = [hi, hi, lo]
    if pad_cols:
        parts.append(np.zeros(c.shape[:-1] + (pad_cols,), ml_dtypes.bfloat16))
    return jnp.asarray(np.concatenate(parts, axis=-1))


def _dot3(a, bh, bl):
    ah, al = _split(a)
    return _dot(ah, bh) + _dot(ah, bl) + _dot(al, bh)


def _sigmoid(x):
    return jax.nn.sigmoid(x)


def _mod_kernel(c_ref, w_ref, b_ref, o_ref):
    c = c_ref[...]
    s = (c * _sigmoid(c)).astype(BF16)
    o_ref[...] = _dot(s, w_ref[...].astype(BF16)) + b_ref[...]


def _modulation(c8, w_mod, b_mod):
    depth, d, n = w_mod.shape
    tn = TN_MOD
    return pl.pallas_call(
        _mod_kernel,
        grid=(depth, n // tn),
        in_specs=[pl.BlockSpec((MOD_ROWS, d), lambda l, j: (0, 0)),
                  pl.BlockSpec((None, d, tn), lambda l, j: (l, 0, j)),
                  pl.BlockSpec((None, 1, tn), lambda l, j: (l, 0, j))],
        out_specs=pl.BlockSpec((None, MOD_ROWS, tn), lambda l, j: (l, 0, j)),
        out_shape=jax.ShapeDtypeStruct((depth, MOD_ROWS, n), F32),
        compiler_params=_cparams("parallel", "parallel"),
        name="modulation",
    )(c8, w_mod, b_mod.reshape(depth, 1, n))


def _mod_spec(d, bmap):
    return pl.BlockSpec((None, 1, d), lambda *idx: (bmap(idx[0]), 0, 0))


def _lane_tile(n, cap):
    best = n
    for t in range(LANES, min(n, cap) + 1, LANES):
        if n % t == 0:
            best = t
    return best


def _cast_kernel(x_ref, o_ref):
    o_ref[...] = x_ref[...].astype(o_ref.dtype)


def _to_bf16(w):
    shape = w.shape
    w2 = w.reshape(-1, shape[-1])
    r, c = w2.shape
    tr, tc = min(TM_ROWS, r), _lane_tile(c, 2 * TN_PROJ)
    out = pl.pallas_call(
        _cast_kernel,
        grid=(r // tr, c // tc),
        in_specs=[pl.BlockSpec((tr, tc), lambda i, j: (i, j))],
        out_specs=pl.BlockSpec((tr, tc), lambda i, j: (i, j)),
        out_shape=jax.ShapeDtypeStruct((r, c), BF16),
        compiler_params=_cparams("parallel", "parallel"),
        name="cast_bf16",
    )(w2)
    return out.reshape(shape)


def _to_bf16_gate_up(w, tf):
    shape = w.shape
    w2 = w.reshape(-1, shape[-1])
    r, c = w2.shape
    nf = c // (2 * tf)
    tr = min(4 * TM_ROWS, r)
    out = pl.pallas_call(
        _cast_kernel,
        grid=(r // tr, 2 * nf),
        in_specs=[pl.BlockSpec((tr, tf), lambda i, j: (i, j))],
        out_specs=pl.BlockSpec((tr, tf), lambda i, j: (i, 2 * (j % nf) + j // nf)),
        out_shape=jax.ShapeDtypeStruct((r, c), BF16),
        compiler_params=_cparams("parallel", "parallel"),
        name="cast_bf16_gate_up",
    )(w2)
    return out.reshape(shape)


def _proj_kernel(x_ref, g_ref, sh_ref, sc_ref, w_ref, o_ref, h_s):
    j = pl.program_id(1)

    @pl.when(j == 0)
    def _():
        x = x_ref[...]
        inv = lax.rsqrt(jnp.mean(x * x, axis=-1, keepdims=True) + RMS_EPS)
        acc = None
        for k0 in range(0, x.shape[1], NORM_CHUNK):
            ks = slice(k0, k0 + NORM_CHUNK)
            hk = ((x[:, ks] * inv * g_ref[:, ks]) * (1.0 + sc_ref[:, ks]) + sh_ref[:, ks]).astype(BF16)
            h_s[:, ks] = hk
            part = _dot(hk, w_ref[ks, :])
            acc = part if acc is None else acc + part
        o_ref[...] = acc

    @pl.when(j > 0)
    def _():
        o_ref[...] = _dot(h_s[...], w_ref[...])


def _projection(x, g, sh, sc, w, bmap, tm, tn):
    m, d = x.shape
    n = w.shape[1]
    return pl.pallas_call(
        _proj_kernel,
        grid=(m // tm, n // tn),
        in_specs=[pl.BlockSpec((tm, d), lambda i, j: (i, 0)),
                  pl.BlockSpec((1, d), lambda i, j: (0, 0)),
                  _mod_spec(d, bmap), _mod_spec(d, bmap),
                  pl.BlockSpec((d, tn), lambda i, j: (0, j))],
        out_specs=pl.BlockSpec((tm, tn), lambda i, j: (i, j)),
        out_shape=jax.ShapeDtypeStruct((m, n), F32),
        scratch_shapes=[pltpu.VMEM((tm, d), BF16)],
        compiler_params=_cparams("parallel", "arbitrary"),
        name="projection",
    )(x, g, sh, sc, w)


def _out_res_kernel(a_ref, w_ref, x_ref, mg_ref, o_ref):
    o_ref[...] = x_ref[...] + mg_ref[...] * _dot(a_ref[...], w_ref[...])


def _ffn_kernel(x_ref, g_ref, sh_ref, sc_ref, mg_ref, wgu_ref, wo_ref, fg_ref, o_ref, h_s, acc_s,
                *, final_norm):
    j = pl.program_id(1)
    tf = wo_ref.shape[0]

    def swiglu(au):
        a, u = au[:, :tf], au[:, tf:]
        return _dot((a * _sigmoid(a) * u).astype(BF16), wo_ref[...])

    @pl.when(j == 0)
    def _():
        x = x_ref[...]
        inv = lax.rsqrt(jnp.mean(x * x, axis=-1, keepdims=True) + RMS_EPS)
        au = None
        for k0 in range(0, x.shape[1], NORM_CHUNK):
            ks = slice(k0, k0 + NORM_CHUNK)
            hk = ((x[:, ks] * inv * g_ref[:, ks]) * (1.0 + sc_ref[:, ks]) + sh_ref[:, ks]).astype(BF16)
            h_s[:, ks] = hk
            part = _dot(hk, wgu_ref[ks, :])
            au = part if au is None else au + part
        acc_s[...] = swiglu(au)

    @pl.when(j > 0)
    def _():
        acc_s[...] += swiglu(_dot(h_s[...], wgu_ref[...]))

    @pl.when(j == pl.num_programs(1) - 1)
    def _():
        y = x_ref[...] + mg_ref[...] * acc_s[...]
        if final_norm:
            y = y * lax.rsqrt(jnp.mean(y * y, axis=-1, keepdims=True) + RMS_EPS) * fg_ref[...]
        o_ref[...] = y


def _ffn(x, g, sh, sc, mg, w_in, w_out, layer, final_g, final_norm, bmap, tm, tf):
    m, d = x.shape
    f = w_out.shape[1]
    nf = f // tf
    return pl.pallas_call(
        functools.partial(_ffn_kernel, final_norm=final_norm),
        grid=(m // tm, nf),
        in_specs=[pl.BlockSpec((tm, d), lambda i, j: (i, 0)),
                  pl.BlockSpec((1, d), lambda i, j: (0, 0)),
                  _mod_spec(d, bmap), _mod_spec(d, bmap), _mod_spec(d, bmap),
                  pl.BlockSpec((None, d, 2 * tf), lambda i, j: (layer, 0, j)),
                  pl.BlockSpec((None, tf, d), lambda i, j: (layer, j, 0)),
                  pl.BlockSpec((1, d), lambda i, j: (0, 0))],
        out_specs=pl.BlockSpec((tm, d), lambda i, j: (i, 0)),
        out_shape=jax.ShapeDtypeStruct((m, d), F32),
        scratch_shapes=[pltpu.VMEM((tm, d), BF16), pltpu.VMEM((tm, d), F32)],
        compiler_params=_cparams("parallel", "arbitrary"),
        name="ffn",
    )(x, g, sh, sc, mg, w_in, w_out, final_g)


def _conv3_rows(acc, cw, row_len):
    tm = acc.shape[0]
    rid = lax.broadcasted_iota(jnp.int32, (tm, 1), 0) % row_len
    up = jnp.where(rid == 0, 0.0, pltpu.roll(acc, 1, 0))
    dn = jnp.where(rid == row_len - 1, 0.0, pltpu.roll(acc, tm - 1, 0))
    return cw[0:1] * up + cw[1:2] * acc + cw[2:3] * dn


def _hy_in_kernel(x_ref, g_ref, sh_ref, sc_ref, w_ref, cw_ref, o_ref, h_s, *, row_len):
    j = pl.program_id(1)
    tiles = [slice(n0, n0 + MXU_N) for n0 in range(0, o_ref.shape[1], MXU_N)]

    def epilogue(ns, acc):
        o_ref[:, ns] = _conv3_rows(acc, cw_ref[:, ns], row_len).astype(o_ref.dtype)

    @pl.when(j == 0)
    def _():
        x = x_ref[...]
        inv = lax.rsqrt(jnp.mean(x * x, axis=-1, keepdims=True) + RMS_EPS)
        accs = [None] * len(tiles)
        for k0 in range(0, x.shape[1], NORM_CHUNK):
            ks = slice(k0, k0 + NORM_CHUNK)
            hk = ((x[:, ks] * inv * g_ref[:, ks]) * (1.0 + sc_ref[:, ks]) + sh_ref[:, ks]).astype(BF16)
            h_s[:, ks] = hk
            for t, ns in enumerate(tiles):
                part = _dot(hk, w_ref[ks, ns])
                accs[t] = part if accs[t] is None else accs[t] + part
        for ns, acc in zip(tiles, accs):
            epilogue(ns, acc)

    @pl.when(j > 0)
    def _():
        h = h_s[...]
        for ns in tiles:
            epilogue(ns, _dot(h, w_ref[:, ns]))


def _hyena_in(x, g, sh, sc, w, cw, bmap, tm, tn, row_len, out_dtype):
    m, d = x.shape
    nd = d // tn
    return pl.pallas_call(
        functools.partial(_hy_in_kernel, row_len=row_len),
        grid=(m // tm, 3 * nd),
        in_specs=[pl.BlockSpec((tm, d), lambda i, j: (i, 0)),
                  pl.BlockSpec((1, d), lambda i, j: (0, 0)),
                  _mod_spec(d, bmap), _mod_spec(d, bmap),
                  pl.BlockSpec((d, tn), lambda i, j: (0, j)),
                  pl.BlockSpec((3, tn), lambda i, j: (0, j))],
        out_specs=pl.BlockSpec((None, tm, tn), lambda i, j: (j // nd, i, j % nd)),
        out_shape=jax.ShapeDtypeStruct((3, m, d), out_dtype),
        scratch_shapes=[pltpu.VMEM((tm, d), BF16)],
        compiler_params=_cparams("parallel", "arbitrary"),
        name="hyena_in",
    )(x, g, sh, sc, w, cw)


def _out_res(a, w, x, mg, bmap, tm):
    m, d = x.shape
    return pl.pallas_call(
        _out_res_kernel,
        grid=(m // tm,),
        in_specs=[pl.BlockSpec((tm, d), lambda i: (i, 0)),
                  pl.BlockSpec((d, d), lambda i: (0, 0)),
                  pl.BlockSpec((tm, d), lambda i: (i, 0)),
                  _mod_spec(d, bmap)],
        out_specs=pl.BlockSpec((tm, d), lambda i: (i, 0)),
        out_shape=jax.ShapeDtypeStruct((m, d), F32),
        compiler_params=_cparams("parallel"),
        name="out_residual",
    )(a, w, x, mg)


def _filter_kernel(z_ref, w1h_ref, w1l_ref, b1_ref, fq_ref, w2h_ref, w2l_ref, b2_ref,
                   wo_ref, dl_ref, h_ref, nrm_ref):
    z = z_ref[...]
    t = z[:, 0:1]
    valid = z[:, FILTER_EMB:FILTER_EMB + 1]
    a1 = _dot3(z, w1h_ref[...], w1l_ref[...]) + b1_ref[...]
    hid = jnp.sin(fq_ref[0:1, :] * a1)
    a2 = _dot3(hid, w2h_ref[...], w2l_ref[...]) + b2_ref[...]
    hid = jnp.sin(fq_ref[1:2, :] * a2)
    hh, hl = _split(hid)
    h = _dot(jnp.concatenate([hh, hl], axis=1), wo_ref[...]) * jnp.exp(-t * dl_ref[...]) * valid
    h_ref[...] = h.reshape(h_ref.shape)

    @pl.when((pl.program_id(1) == 0) & (pl.program_id(2) == 0))
    def _():
        nrm_ref[...] = jnp.zeros_like(nrm_ref)

    nrm_ref[...] += jnp.sum(jnp.abs(h), axis=0, keepdims=True)


def _pad2(a, rows, cols):
    return jnp.pad(a, ((0, rows - a.shape[0]), (0, cols - a.shape[1])))


def _filter_positions(seq, perm):
    if perm:
        r_hi = seq // FFT_P
        gt, ng = 2 * SUBLANES, FFT_P // (2 * SUBLANES)
        dirs, g, tl, th = np.meshgrid(np.arange(2), np.arange(ng), np.arange(gt), np.arange(r_hi), indexing="ij")
        n = FFT_P * (r_hi * dirs + th) + g * gt + tl
    else:
        r_hi, gt, ng = seq, 1, 1
        dirs, g, tl, th = np.meshgrid(np.arange(2), np.arange(1), np.arange(1), np.arange(seq), indexing="ij")
        n = seq * dirs + th
    n = n.reshape(-1)
    pos = np.where(n < seq, n, 2 * seq - 1 - n).astype(np.float32)
    z = np.zeros((n.size, LANES), np.float32)
    z[:, 0] = pos / np.float32(max(seq - 1, 1))
    bands = np.arange(1, FILTER_BANDS + 1, dtype=np.float32)
    ang = (np.float32(2.0 * math.pi / seq) * pos)[:, None] * bands[None, :]
    z[:, 1:1 + FILTER_BANDS] = np.cos(ang.astype(np.float64))
    z[:, 1 + FILTER_BANDS:FILTER_EMB] = -np.sin(ang.astype(np.float64))
    z[:, FILTER_EMB] = n != seq
    return jnp.asarray(z), gt, ng, r_hi


def _hyena_filter_time(fw1, fb1, ffreq, fw2, fb2, fwout, seq, perm):
    od = fwout.shape[1] // 2
    d = od // HYENA_ORDER
    z, gt, ng, r_hi = _filter_positions(seq, perm)
    rows = gt * r_hi
    ct = min(2 * FFT_DT_OUTER, od)
    hidden = LANES
    w1h, w1l = _split(_pad2(fw1, LANES, hidden))
    w2h, w2l = _split(_pad2(fw2, hidden, hidden))
    wo = _pad2(fwout, hidden, fwout.shape[1]).astype(BF16)
    wo2 = jnp.concatenate([wo, wo], axis=0)
    b1 = _pad2(fb1[None, :], 1, hidden)
    b2 = _pad2(fb2[None, :], 1, hidden)
    fq = _pad2(ffreq, 2, hidden)
    deltas = np.abs(np.linspace(HYENA_DECAY_MIN, HYENA_DECAY_MAX, d, dtype=np.float32))
    dl = jnp.asarray(np.tile(deltas, HYENA_ORDER)[None, :])
    nct = od // ct
    small = lambda shape: pl.BlockSpec(shape, lambda c, di, g: (0, 0))
    wspec = pl.BlockSpec((2 * hidden, ct), lambda c, di, g: (0, di * nct + c))
    return pl.pallas_call(
        _filter_kernel,
        grid=(nct, 2, ng),
        in_specs=[pl.BlockSpec((rows, LANES), lambda c, di, g: (di * ng + g, 0)),
                  small((LANES, hidden)), small((LANES, hidden)), small((1, hidden)), small((2, hidden)),
                  small((hidden, hidden)), small((hidden, hidden)), small((1, hidden)),
                  wspec,
                  pl.BlockSpec((1, ct), lambda c, di, g: (0, c))],
        out_specs=[pl.BlockSpec((None, gt, None, r_hi, ct), lambda c, di, g: (g, 0, di, 0, c)),
                   pl.BlockSpec((1, ct), lambda c, di, g: (0, c))],
        out_shape=[jax.ShapeDtypeStruct((ng, gt, 2, r_hi, od), F32), jax.ShapeDtypeStruct((1, od), F32)],
        compiler_params=_cparams("parallel", "arbitrary", "arbitrary"),
        name="hyena_filter",
    )(z, w1h, w1l, b1, fq, w2h, w2l, b2, wo2, dl)


def _fft_consts(seq, k_in):
    n = 2 * seq
    p = FFT_P
    q = n // p
    qh = q // 2
    ka_used = qh + 1
    ka_pad = -(-ka_used // SUBLANES) * SUBLANES
    ka = np.arange(ka_used)
    tl = np.arange(p)
    th = np.arange(k_in)
    theta = 2 * np.pi * (ka[None, :, None] * th[None, None, :] / q + ka[None, :, None] * tl[:, None, None] / n)
    f1 = np.zeros((p, 2 * ka_pad, k_in))
    f1[:, :ka_used] = np.cos(theta)
    f1[:, ka_pad:ka_pad + ka_used] = -np.sin(theta)
    k3 = 3 * k_in
    f1c = _const3(f1, pad_cols=-(-k3 // LANES) * LANES - k3)
    f1s = jnp.asarray(np.pad(f1, ((0, 0), (0, 0), (0, -(-k_in // LANES) * LANES - k_in))).astype(ml_dtypes.bfloat16))
    ang = 2 * np.pi * np.outer(np.arange(p), np.arange(p)) / p
    cc, sc = np.cos(ang), np.sin(ang)
    m2f_np, m2i_np = np.block([[cc, sc], [-sc, cc]]), np.block([[cc, -sc], [sc, cc]])
    m2f1, m2i1 = (jnp.asarray(m.astype(ml_dtypes.bfloat16)) for m in (m2f_np, m2i_np))
    tho = np.arange(qh)
    phi = 2 * np.pi * (tho[None, :, None] * ka[None, None, :] / q + ka[None, None, :] * tl[:, None, None] / n)
    wgt = np.where((ka == 0) | (ka == qh), 1.0, 2.0) / n
    kr = 2 * ka_pad
    g = np.zeros((p, qh, -(-kr // LANES) * LANES))
    col = 2 * SUBLANES * (ka // SUBLANES) + ka % SUBLANES
    g[:, :, col] = wgt * np.cos(phi)
    g[:, :, col + SUBLANES] = -wgt * np.sin(phi)
    gc = jnp.asarray(g.astype(ml_dtypes.bfloat16))
    return dict(q=q, qh=qh, ka_used=ka_used, ka_pad=ka_pad, f1=f1c, f1s=f1s, m2f1=m2f1, m2i1=m2i1, g=gc)


def _s1_kernel(x_ref, f_ref, ar_ref, ai_ref, a_s, *, group, ka_pad, pad_rows, natural, split):
    x = x_ref[...].astype(F32)
    if natural:
        x = jnp.swapaxes(x, 0, 1)
    zpad = None if split else jnp.zeros((pad_rows, x.shape[-1]), BF16)
    for j in range(group):
        xx = _stack3(x[j], pad_rows) if split else jnp.concatenate([x[j].astype(BF16), zpad], axis=0)
        a_s[j] = _dot(f_ref[j], xx)
    a = jnp.swapaxes(a_s[...], 0, 1)
    ar_ref[...] = a[:ka_pad].astype(ar_ref.dtype)
    ai_ref[...] = a[ka_pad:].astype(ai_ref.dtype)


def _fft_stage1(x5, part, f1c, ka_pad, group, dt, out_dtype, natural, split):
    if natural:
        _, b, k_in, p, d = x5.shape
        xspec = pl.BlockSpec((None, None, k_in, group, dt), lambda bi, g, c: (part, bi, 0, g, c))
    else:
        _, b, p, k_in, d = x5.shape
        xspec = pl.BlockSpec((None, None, group, k_in, dt), lambda bi, g, c: (part, bi, g, 0, c))
    kc = f1c.shape[-1]
    out = jax.ShapeDtypeStruct((b, ka_pad, p, d), out_dtype)
    ospec = pl.BlockSpec((None, ka_pad, group, dt), lambda bi, g, c: (bi, 0, g, c))
    pad_rows = kc - (3 if split else 1) * k_in
    return pl.pallas_call(
        functools.partial(_s1_kernel, group=group, ka_pad=ka_pad, pad_rows=pad_rows, natural=natural, split=split),
        grid=(b, p // group, d // dt),
        in_specs=[xspec,
                  pl.BlockSpec((group, 2 * ka_pad, kc), lambda bi, g, c: (g, 0, 0))],
        out_specs=[ospec, ospec],
        out_shape=[out, out],
        scratch_shapes=[pltpu.VMEM((group, 2 * ka_pad, dt), F32)],
        compiler_params=_cparams("parallel", "parallel", "parallel"),
        name="fft_stage1",
    )(x5, f1c)


def _s2_filter_kernel(ar_ref, ai_ref, m_ref, inv_ref, hr_ref, hi_ref, *, p):
    for i in range(ar_ref.shape[0]):
        x = jnp.concatenate([ar_ref[i], ai_ref[i]], axis=0)
        y = _dot(m_ref[...], x.astype(BF16)) * inv_ref[...]
        hr_ref[i] = y[:p]
        hi_ref[i] = y[p:]


def _fft_stage2_filter(ar, ai, m2f, inv_nrm, dt):
    ka, p, d = ar.shape
    spec = pl.BlockSpec((SUBLANES, p, dt), lambda k, c: (k, 0, c))
    out = jax.ShapeDtypeStruct((ka, p, d), F32)
    return pl.pallas_call(
        functools.partial(_s2_filter_kernel, p=p),
        grid=(ka // SUBLANES, d // dt),
        in_specs=[spec, spec,
                  pl.BlockSpec((2 * p, 2 * p), lambda k, c: (0, 0)),
                  pl.BlockSpec((1, dt), lambda k, c: (0, c))],
        out_specs=[spec, spec],
        out_shape=[out, out],
        compiler_params=_cparams("parallel", "parallel"),
        name="fft_stage2_filter",
    )(ar, ai, m2f, inv_nrm)


def _s2_kernel(ar_ref, ai_ref, hr_ref, hi_ref, mf_ref, mi_ref, c_ref, x_s, y_s, c_s, *, p, ka_used):
    kg = pl.program_id(0)
    assert ka_used % SUBLANES == 1

    def run(rows):
        for i in rows:
            x_s[i] = _dot(mf_ref[...], jnp.concatenate([ar_ref[i], ai_ref[i]], axis=0))
        for i in rows:
            xr, xi = x_s[i, :p], x_s[i, p:]
            hr, hi = hr_ref[i], hi_ref[i]
            y_s[i, :p] = (xr * hr - xi * hi).astype(BF16)
            y_s[i, p:] = (xr * hi + xi * hr).astype(BF16)
        for i in rows:
            c = _dot(mi_ref[...], y_s[i])
            c_s[i] = c[:p]
            c_s[SUBLANES + i] = c[p:]

    run(range(1))
    full = (kg + 1) * SUBLANES <= ka_used

    @pl.when(full)
    def _():
        run(range(1, SUBLANES))

    @pl.when(jnp.logical_not(full))
    def _():
        for i in range(1, SUBLANES):
            c_s[i] = jnp.zeros(c_s.shape[1:], F32)
            c_s[SUBLANES + i] = jnp.zeros(c_s.shape[1:], F32)

    c_ref[...] = jnp.swapaxes(c_s[...], 0, 1).astype(c_ref.dtype)


def _fft_stage2(ar, ai, hr, hi, m2f, m2i, ka_used, h_col0, dt):
    b, ka, p, d = ar.shape
    hc = h_col0 // dt
    aspec = pl.BlockSpec((None, SUBLANES, p, dt), lambda k, c, bi: (bi, k, 0, c))
    hspec = pl.BlockSpec((SUBLANES, p, dt), lambda k, c, bi: (k, 0, hc + c))
    mspec = pl.BlockSpec((2 * p, 2 * p), lambda k, c, bi: (0, 0))
    return pl.pallas_call(
        functools.partial(_s2_kernel, p=p, ka_used=ka_used),
        grid=(ka // SUBLANES, d // dt, b),
        in_specs=[aspec, aspec, hspec, hspec, mspec, mspec],
        out_specs=pl.BlockSpec((None, p, 2 * SUBLANES, dt), lambda k, c, bi: (bi, 0, k, c)),
        out_shape=jax.ShapeDtypeStruct((b, p, 2 * ka, d), BF16),
        scratch_shapes=[pltpu.VMEM((SUBLANES, 2 * p, dt), F32), pltpu.VMEM((SUBLANES, 2 * p, dt), BF16),
                        pltpu.VMEM((2 * SUBLANES, p, dt), F32)],
        compiler_params=_cparams("parallel", "parallel", "parallel"),
        name="fft_stage2",
    )(ar, ai, hr, hi, m2f, m2i)


def _s3_kernel(c_ref, g_ref, v_ref, x_ref, sk_ref, z_ref, y_s, *, group, pad_rows):
    zpad = jnp.zeros((pad_rows, c_ref.shape[-1]), BF16)
    for j in range(group):
        y_s[j] = _dot(g_ref[j], jnp.concatenate([c_ref[j], zpad], axis=0))
    y = jnp.swapaxes(y_s[...], 0, 1)
    v = v_ref[...].astype(F32)
    z_ref[...] = ((y + v * sk_ref[...]) * x_ref[...].astype(F32)).astype(z_ref.dtype)


def _fft_stage3(c, gc, v5, vpart, x5, xpart, skip, group, dt, out_dtype):
    b, p, kr, d = c.shape
    qh = v5.shape[2]
    kc = gc.shape[-1]
    pspec = lambda part: pl.BlockSpec((None, None, qh, group, dt), lambda bi, g, cc: (part, bi, 0, g, cc))
    return pl.pallas_call(
        functools.partial(_s3_kernel, group=group, pad_rows=kc - kr),
        grid=(b, p // group, d // dt),
        in_specs=[pl.BlockSpec((None, group, kr, dt), lambda bi, g, cc: (bi, g, 0, cc)),
                  pl.BlockSpec((group, qh, kc), lambda bi, g, cc: (g, 0, 0)),
                  pspec(vpart), pspec(xpart),
                  pl.BlockSpec((1, dt), lambda bi, g, cc: (0, cc))],
        out_specs=pspec(0),
        out_shape=jax.ShapeDtypeStruct((1, b, qh, p, d), out_dtype),
        scratch_shapes=[pltpu.VMEM((group, qh, dt), F32)],
        compiler_params=_cparams("parallel", "parallel", "parallel"),
        name="fft_stage3",
    )(c, gc, v5, x5, skip)


def _long_conv_lat(v5, vpart, x5, xpart, hr, hi, skip, plan, order, out_dtype):
    d = v5.shape[-1]
    dt1 = min(FFT_DT_OUTER, d)
    dt2 = min(FFT_DT_INNER, d)
    ar, ai = _fft_stage1(v5, vpart, plan["f1s"], plan["ka_pad"], FFT_GROUP, dt1, BF16, True, False)
    c = _fft_stage2(ar, ai, hr, hi, plan["m2f1"], plan["m2i1"], plan["ka_used"], order * d, dt2)
    return _fft_stage3(c, plan["g"], v5, vpart, x5, xpart, skip, FFT_GROUP, dt1, out_dtype)


def _filter_spectrum_lat(hy, seq, plan_f):
    fw1, fb1, ffreq, fw2, fb2, fwout = hy
    q = plan_f["q"]
    h_time, nrm = _hyena_filter_time(fw1, fb1, ffreq, fw2, fb2, fwout, seq, True)
    od = h_time.shape[-1]
    h5 = h_time.reshape(1, 1, FFT_P, q, od)
    dt = min(FFT_DT_OUTER, od)
    ar, ai = _fft_stage1(h5, 0, plan_f["f1"], plan_f["ka_pad"], SUBLANES, dt, F32, False, True)
    return _fft_stage2_filter(ar[0], ai[0], plan_f["m2f1"], 1.0 / nrm, dt)


def _dense_consts(seq):
    n = 2 * seq
    kf = seq + 1
    kf_pad = -(-kf // LANES) * LANES
    k = np.arange(kf)
    fwd_full = np.zeros((2 * kf_pad, n))
    ang = 2 * np.pi * np.outer(k, np.arange(n)) / n
    fwd_full[:kf] = np.cos(ang)
    fwd_full[kf_pad:kf_pad + kf] = -np.sin(ang)
    wgt = np.where((k == 0) | (k == seq), 1.0, 2.0) / n
    inv = np.zeros((seq, 2 * kf_pad))
    angi = 2 * np.pi * np.outer(np.arange(seq), k) / n
    inv[:, :kf] = wgt * np.cos(angi)
    inv[:, kf_pad:kf_pad + kf] = -wgt * np.sin(angi)
    return dict(kf_pad=kf_pad, fwd_full=_const3(fwd_full), fwd=_const3(fwd_full[:, :seq]), inv=_const3(inv))


def _dense_spec_kernel(h_ref, f_ref, inv_ref, hr_ref, hi_ref, *, kf_pad):
    y = _dot(f_ref[...], _stack3(h_ref[...])) * inv_ref[...]
    hr_ref[...] = y[:kf_pad]
    hi_ref[...] = y[kf_pad:]


def _filter_spectrum_ctx(hy, seq, cons):
    fw1, fb1, ffreq, fw2, fb2, fwout = hy
    h_time, nrm = _hyena_filter_time(fw1, fb1, ffreq, fw2, fb2, fwout, seq, False)
    od = h_time.shape[-1]
    n2 = 2 * seq
    h_time = h_time.reshape(n2, od)
    kf_pad = cons["kf_pad"]
    ct = min(FFT_DT_INNER, od)
    out = jax.ShapeDtypeStruct((kf_pad, od), F32)
    ospec = pl.BlockSpec((kf_pad, ct), lambda c: (0, c))
    return pl.pallas_call(
        functools.partial(_dense_spec_kernel, kf_pad=kf_pad),
        grid=(od // ct,),
        in_specs=[pl.BlockSpec((n2, ct), lambda c: (0, c)),
                  pl.BlockSpec((2 * kf_pad, 3 * n2), lambda c: (0, 0)),
                  pl.BlockSpec((1, ct), lambda c: (0, c))],
        out_specs=[ospec, ospec],
        out_shape=[out, out],
        compiler_params=_cparams("parallel"),
        name="dense_filter_spectrum",
    )(h_time, cons["fwd_full"], 1.0 / nrm)


def _dense_conv_kernel(v_ref, x1_ref, x2_ref, h1r_ref, h1i_ref, h2r_ref, h2i_ref, sk_ref,
                       f_ref, g_ref, z_ref, *, kf_pad):
    def conv(u, hr, hi):
        s = _dot(f_ref[...], _stack3(u))
        sr, si = s[:kf_pad], s[kf_pad:]
        y = jnp.concatenate([sr * hr - si * hi, sr * hi + si * hr], axis=0)
        return _dot(g_ref[...], _stack3(y))

    v = v_ref[...]
    z1 = x1_ref[...] * (conv(v, h1r_ref[...], h1i_ref[...]) + v * sk_ref[0:1, :])
    z2 = x2_ref[...] * (conv(z1, h2r_ref[...], h2i_ref[...]) + z1 * sk_ref[1:2, :])
    z_ref[...] = z2.astype(z_ref.dtype)


def _hyena_core_ctx(u3, hr, hi, fskip, cons, b, seq):
    d = u3.shape[-1]
    dt = min(MXU_N, d)
    nd = d // dt
    kf_pad = cons["kf_pad"]
    uspec = lambda part: pl.BlockSpec((None, seq, dt), lambda bi, c: (part, bi, c))
    hspec = lambda order: pl.BlockSpec((kf_pad, dt), lambda bi, c: (0, order * nd + c))
    return pl.pallas_call(
        functools.partial(_dense_conv_kernel, kf_pad=kf_pad),
        grid=(b, nd),
        in_specs=[uspec(0), uspec(1), uspec(2), hspec(0), hspec(0), hspec(1), hspec(1),
                  pl.BlockSpec((HYENA_ORDER, dt), lambda bi, c: (0, c)),
                  pl.BlockSpec((2 * kf_pad, 3 * seq), lambda bi, c: (0, 0)),
                  pl.BlockSpec((seq, 6 * kf_pad), lambda bi, c: (0, 0))],
        out_specs=pl.BlockSpec((seq, dt), lambda bi, c: (bi, c)),
        out_shape=jax.ShapeDtypeStruct((b * seq, d), BF16),
        compiler_params=_cparams("parallel", "parallel"),
        name="hyena_core_ctx",
    )(u3, u3, u3, hr, hi, hr, hi, fskip, cons["fwd"], cons["inv"])


def _rec_consts(chunk):
    t = np.arange(chunk)
    coefs, masks = [], []
    for direction in (0, 1):
        if direction == 0:
            rows = [t[None, :] <= t[:, None], t[None, :] > t[:, None]]
        else:
            rows = [t[None, :] >= t[:, None], t[None, :] < t[:, None]]
        mk = [np.eye(chunk)]
        m = chunk // 2
        while m >= 1:
            blk = t // (2 * m)
            half = (t // m) % 2
            mid = blk * 2 * m + m
            e = np.zeros((chunk, chunk))
            for r in range(chunk):
                if direction == 0:
                    if half[r] == 1:
                        e[r, mid[r]:r + 1] = 1
                    else:
                        e[r, r + 1:mid[r]] = 1
                else:
                    if half[r] == 0:
                        e[r, r:mid[r]] = 1
                    else:
                        e[r, mid[r]:r] = 1
            same = blk[:, None] == blk[None, :]
            if direction == 0:
                mk.append(same & (half[:, None] == 1) & (half[None, :] == 0))
            else:
                mk.append(same & (half[:, None] == 0) & (half[None, :] == 1))
            rows.append(e)
            m //= 2
        rows.append(np.ones((REC_TOTAL_ROWS, chunk)))
        a = np.concatenate([np.asarray(r, np.float64) for r in rows], axis=0)
        a3 = np.concatenate([a] * REC_GATE_TERMS, axis=1)
        pad = -(-a3.shape[1] // LANES) * LANES - a3.shape[1]
        a3 = np.pad(a3, ((0, 0), (0, pad)))
        coefs.append(a3)
        masks.append(np.stack([np.asarray(x, np.float32) for x in mk]))
    return (jnp.asarray(np.stack(coefs), dtype=BF16), jnp.asarray(np.stack(masks), dtype=F32))


def _rec_core(load_qkg, v_ref, coef_ref, mask_ref, s0_ref, o_ref, sfin_ref, q_s, k_s, st_s, ex_s, att_s,
              *, chunk, nsub, heads, dk, dv):
    di = pl.program_id(1)
    c = pl.program_id(2)
    levels = int(math.log2(chunk))
    tot = (2 + levels) * chunk

    @pl.when(c == 0)
    def _():
        st_s[...] = s0_ref[...]

    offs = [pl.multiple_of((s + di * (nsub - 1 - 2 * s)) * chunk, chunk) for s in range(nsub)]

    for s in range(nsub):
        q, k, g = load_qkg(pl.ds(offs[s], chunk))
        q_s[s] = q
        k_s[s] = k
        terms, rest = [], g
        for _ in range(REC_GATE_TERMS):
            terms.append(rest.astype(BF16))
            rest = rest - terms[-1].astype(F32)
        pad = coef_ref.shape[-1] - REC_GATE_TERMS * chunk
        if pad:
            terms.append(jnp.zeros((pad, g.shape[1]), BF16))
        ex_s[s] = jnp.exp(_dot(coef_ref[...], jnp.concatenate(terms, axis=0)))

    for s in range(nsub):
        for h in range(heads):
            ks = slice(h * dk, (h + 1) * dk)
            qh = q_s[s, :, ks]
            kh = k_s[s, :, ks]
            att = mask_ref[0] * _dot_nt(qh.astype(BF16), kh.astype(BF16))
            for lv in range(levels):
                e = ex_s[s, (2 + lv) * chunk:(3 + lv) * chunk, ks]
                att += mask_ref[1 + lv] * _dot_nt((qh * e).astype(BF16), (kh * e).astype(BF16))
            att_s[s, h] = att.astype(BF16)

    for s in range(nsub):
        rows = pl.ds(offs[s], chunk)
        for h in range(heads):
            ks = slice(h * dk, (h + 1) * dk)
            vs = slice(h * dv, (h + 1) * dv)
            vh = v_ref[rows, vs].astype(BF16)
            st = st_s[h]
            o = _dot(att_s[s, h], vh)
            o += _dot_nt((q_s[s, :, ks] * ex_s[s, 0:chunk, ks]).astype(BF16), st.astype(BF16))
            o_ref[rows, vs] = o.astype(o_ref.dtype)
            kd = (k_s[s, :, ks] * ex_s[s, chunk:2 * chunk, ks]).astype(BF16)
            st_s[h] = st * ex_s[s, tot:tot + 1, ks] + _dot_tn(vh, kd)

    @pl.when(c == pl.num_programs(2) - 1)
    def _():
        sfin_ref[...] = st_s[...]


def _hgrn_rec_kernel(q_ref, v_ref, f_ref, lb_ref, coef_ref, mask_ref, s0_ref, o_ref, sfin_ref,
                     q_s, k_s, st_s, ex_s, att_s, **kw):
    log_lb = lb_ref[0:1, :]
    log_1mlb = lb_ref[1:2, :]

    def load_qkg(rows):
        qr = q_ref[rows, :]
        fr = f_ref[rows, :]
        t2 = log_1mlb + jnp.minimum(fr, 0.0) - jnp.log(1.0 + jnp.exp(-jnp.abs(fr)))
        mx = jnp.maximum(log_lb, t2)
        g = mx + jnp.log(1.0 + jnp.exp(-jnp.abs(log_lb - t2)))
        return qr * _sigmoid(qr), 1.0 - jnp.exp(g), g

    _rec_core(load_qkg, v_ref, coef_ref, mask_ref, s0_ref, o_ref, sfin_ref, q_s, k_s, st_s, ex_s, att_s, **kw)


def _gla_rec_kernel(q_ref, k_ref, v_ref, a_ref, wup_ref, bup_ref, coef_ref, mask_ref, s0_ref,
                    o_ref, sfin_ref, q_s, k_s, st_s, ex_s, att_s, *, qscale, **kw):
    def load_qkg(rows):
        xg = _dot(a_ref[rows, :].astype(BF16), wup_ref[...]) + bup_ref[...]
        g = (jnp.minimum(xg, 0.0) - jnp.log(1.0 + jnp.exp(-jnp.abs(xg)))) * (1.0 / GLA_GATE_NORM)
        return q_ref[rows, :] * qscale, k_ref[rows, :], g

    _rec_core(load_qkg, v_ref, coef_ref, mask_ref, s0_ref, o_ref, sfin_ref, q_s, k_s, st_s, ex_s, att_s, **kw)


def _rec_call(kind, proj, extra, s0, consts, b, seq, d, heads, dk, dv):
    coef, masks = consts
    chunk = masks.shape[-1]
    rows = REC_STEP_ROWS
    nsub = rows // chunk
    nc = seq // rows
    hk = heads * dk
    hv = heads * dv
    levels = int(math.log2(chunk))
    n_rows = (2 + levels) * chunk + REC_TOTAL_ROWS

    def rmap(col):
        return lambda bi, di, c: (bi * nc + c + di * (nc - 1 - 2 * c), col)

    cspecs = [pl.BlockSpec((None, n_rows, coef.shape[-1]), lambda bi, di, c: (di, 0, 0)),
              pl.BlockSpec((None, levels + 1, chunk, chunk), lambda bi, di, c: (di, 0, 0, 0)),
              pl.BlockSpec((None, None, heads, dv, dk), lambda bi, di, c: (bi, di, 0, 0, 0))]
    if kind == "hgrn":
        lb = extra
        kernel = _hgrn_rec_kernel
        in_specs = [pl.BlockSpec((rows, d), rmap(0)),
                    pl.BlockSpec((rows, d), rmap(1)),
                    pl.BlockSpec((rows, d), lambda bi, di, c: (bi * nc + c + di * (nc - 1 - 2 * c), 3 + di)),
                    pl.BlockSpec((None, 2, d), lambda bi, di, c: (di, 0, 0))] + cspecs
        args = (proj, proj, proj, lb, coef, masks, s0)
        kw = {}
    else:
        wup, bup = extra
        kernel = _gla_rec_kernel
        in_specs = [pl.BlockSpec((rows, hk), rmap(0)),
                    pl.BlockSpec((rows, hk), rmap(1)),
                    pl.BlockSpec((rows, hv), rmap(1)),
                    pl.BlockSpec((rows, LANES), rmap((2 * hk + 2 * hv) // LANES)),
                    pl.BlockSpec((None, LANES, hk), lambda bi, di, c: (di, 0, 0)),
                    pl.BlockSpec((None, 1, hk), lambda bi, di, c: (di, 0, 0))] + cspecs
        args = (proj, proj, proj, proj, wup, bup, coef, masks, s0)
        kw = dict(qscale=float(dk) ** -0.5)
    return pl.pallas_call(
        functools.partial(kernel, chunk=chunk, nsub=nsub, heads=heads, dk=dk, dv=dv, **kw),
        grid=(b, 2, nc),
        in_specs=in_specs,
        out_specs=[pl.BlockSpec((None, rows, hv), lambda bi, di, c: (di, bi * nc + c + di * (nc - 1 - 2 * c), 0)),
                   pl.BlockSpec((None, None, heads, dv, dk), lambda bi, di, c: (bi, di, 0, 0, 0))],
        out_shape=[jax.ShapeDtypeStruct((2, b * seq, hv), BF16),
                   jax.ShapeDtypeStruct((b, 2, heads, dv, dk), F32)],
        scratch_shapes=[pltpu.VMEM((nsub, chunk, hk), F32), pltpu.VMEM((nsub, chunk, hk), F32),
                        pltpu.VMEM((heads, dv, dk), F32), pltpu.VMEM((nsub, n_rows, hk), F32),
                        pltpu.VMEM((nsub, heads, chunk, chunk), BF16)],
        compiler_params=_cparams("parallel", "parallel", "arbitrary"),
        name=kind + "_recurrence",
    )(*args)


def _gated_out_kernel(o_ref, gate_ref, gn_ref, w_ref, x_ref, mg_ref, out_ref, *, heads):
    d = x_ref.shape[1]
    dh = d // heads
    kc = max(dh, 2 * MXU_N)
    acc = None
    for k0 in range(0, d, kc):
        parts = []
        for h0 in range(k0, k0 + kc, dh):
            seg = o_ref[0, :, h0:h0 + dh].astype(F32) + o_ref[1, :, h0:h0 + dh].astype(F32)
            ms = jnp.mean(seg * seg, axis=-1, keepdims=True)
            parts.append(seg * lax.rsqrt(ms + RMS_EPS))
        gate = gate_ref[:, k0:k0 + kc]
        y = jnp.concatenate(parts, axis=1) * gn_ref[:, k0:k0 + kc] * (gate * _sigmoid(gate))
        part = _dot(y.astype(BF16), w_ref[k0:k0 + kc, :])
        acc = part if acc is None else acc + part
    out_ref[...] = x_ref[...] + mg_ref[...] * acc


def _gated_out(o2, proj, gate_col, gn, w, x, mg, bmap, heads, tm):
    m, d = x.shape
    return pl.pallas_call(
        functools.partial(_gated_out_kernel, heads=heads),
        grid=(m // tm,),
        in_specs=[pl.BlockSpec((2, tm, d), lambda i: (0, i, 0)),
                  pl.BlockSpec((tm, d), lambda i: (i, gate_col)),
                  pl.BlockSpec((1, d), lambda i: (0, 0)),
                  pl.BlockSpec((d, d), lambda i: (0, 0)),
                  pl.BlockSpec((tm, d), lambda i: (i, 0)),
                  _mod_spec(d, bmap)],
        out_specs=pl.BlockSpec((tm, d), lambda i: (i, 0)),
        out_shape=jax.ShapeDtypeStruct((m, d), F32),
        compiler_params=_cparams("parallel"),
        name="gated_out",
    )(o2, proj, gn, w, x, mg)


def _recurrent_mix(kind, x_ctx, x_lat, norm, ctx_map, w_in, extra, consts, b, lc, seq, d, heads, dk, dv, tn):
    g1, sh, sc = norm
    tm_c, tm_l = min(TM_PROJ, x_ctx.shape[0]), min(TM_PROJ, x_lat.shape[0])
    proj_ctx = _projection(x_ctx, g1, sh, sc, w_in, ctx_map, tm_c, tn)
    proj_lat = _projection(x_lat, g1, sh, sc, w_in, lambda i: (i * tm_l) // seq, tm_l, tn)
    s0 = jnp.zeros((b, 2, heads, dv, dk), F32)
    o_ctx, s_ctx = _rec_call(kind, proj_ctx, extra, s0, consts, b, lc, d, heads, dk, dv)
    o_lat, _ = _rec_call(kind, proj_lat, extra, s_ctx, consts, b, seq, d, heads, dk, dv)
    return (proj_ctx, o_ctx), (proj_lat, o_lat)


def kernel(x, c, ctx, c_ctx, w_mod, b_mod, norm1_g, norm2_g, w_ffn_in, w_ffn_out, final_g, hy_w_in, hy_conv_w, hy_fw1, hy_fb1, hy_ffreq, hy_fw2, hy_fb2, hy_fwout, hy_fskip, hy_w_out, hg_w_in, hg_lb_logits, hg_onorm_g, hg_w_out, gla_w_in, gla_w_up, gla_b_up, gla_onorm_g, gla_w_out):
    b, seq, d = x.shape
    lc = ctx.shape[1]
    depth = w_mod.shape[0]
    assert b + 1 <= MOD_ROWS and seq % (FFT_P * 2) == 0 and FFT_P % GRID_W == 0
    m_lat, m_ctx = b * seq, b * lc
    ctx_row = b
    tm_lat, tm_ctx = TM_ROWS, min(TM_ROWS, m_ctx)
    tm_in = TM_PROJ
    lat_map = lambda i: (i * tm_lat) // seq
    lat_map_in = lambda i: (i * tm_in) // seq
    ctx_map = lambda i: ctx_row

    xl = x.reshape(m_lat, d)
    xc = ctx.reshape(m_ctx, d)
    c8 = jnp.zeros((MOD_ROWS, d), F32).at[:b].set(c).at[ctx_row].set(c_ctx)
    mod = _modulation(c8, w_mod, b_mod).reshape(depth, MOD_ROWS, N_MOD, 1, d)

    plan = _fft_consts(seq, seq // FFT_P)
    plan_f = _fft_consts(seq, 2 * seq // FFT_P)
    dense = _dense_consts(lc)
    rec_consts = {1: _rec_consts(HGRN_CHUNK), 2: _rec_consts(GLA_CHUNK)}
    qh = plan["qh"]

    wb_ffn_in, wb_ffn_out = _to_bf16_gate_up(w_ffn_in, TF_FFN), _to_bf16(w_ffn_out)
    wb_hy_in, wb_hy_out = _to_bf16(hy_w_in), _to_bf16(hy_w_out)
    wb_hg_in, wb_hg_out = _to_bf16(hg_w_in), _to_bf16(hg_w_out)
    wb_gla_in, wb_gla_out = _to_bf16(gla_w_in), _to_bf16(gla_w_out)

    for i in range(depth):
        last = i == depth - 1
        kind, j = i % N_MIXERS, i // N_MIXERS
        mv = [mod[i, :, k] for k in range(N_MOD)]
        g1 = norm1_g[i][None, :]
        need_ctx = (not last) or kind != 0
        if kind == 0:
            w_in, w_out = wb_hy_in[j], wb_hy_out[j]
            hy = (hy_fw1[j], hy_fb1[j], hy_ffreq[j], hy_fw2[j], hy_fb2[j], hy_fwout[j])
            u = _hyena_in(xl, g1, mv[0], mv[1], w_in, hy_conv_w[j], lat_map_in,
                          tm_in, min(TN_HYENA_IN, d), GRID_W, BF16).reshape(3, b, qh, FFT_P, d)
            hr, hi = _filter_spectrum_lat(hy, seq, plan_f)
            z1 = _long_conv_lat(u, 0, u, 1, hr, hi, hy_fskip[j, 0][None, :], plan, 0, BF16)
            z2 = _long_conv_lat(z1, 0, u, 2, hr, hi, hy_fskip[j, 1][None, :], plan, 1, BF16)
            xl = _out_res(z2.reshape(m_lat, d), w_out, xl, mv[2], lat_map, tm_lat)
            if need_ctx:
                uc = _hyena_in(xc, g1, mv[0], mv[1], w_in, hy_conv_w[j], ctx_map, lc, min(TN_PROJ, d), lc, F32)
                hcr, hci = _filter_spectrum_ctx(hy, lc, dense)
                zc = _hyena_core_ctx(uc, hcr, hci, hy_fskip[j], dense, b, lc)
                xc = _out_res(zc, w_out, xc, mv[2], ctx_map, tm_ctx)
        else:
            if kind == 1:
                heads = d // HGRN_EXPAND
                dk = dv = HGRN_EXPAND
                w_in = wb_hg_in[j]
                lb_cum = jnp.cumsum(jax.nn.softmax(hg_lb_logits.astype(F32), axis=1), axis=1)
                lb = lb_cum[:, i] - lb_cum[:, 0]
                extra = jnp.stack([jnp.log(lb), jnp.log1p(-lb)], axis=1)
                gn, w_out, tn = hg_onorm_g[j], wb_hg_out[j], TN_PROJ
                rkind = "hgrn"
            else:
                heads = GLA_HEADS
                dk, dv = d // 2 // heads, d // heads
                n_in = gla_w_in.shape[-1]
                tn = TN_GLA_PROJ
                n_pad = -(-n_in // tn) * tn
                w_in = jnp.pad(wb_gla_in[j], ((0, 0), (0, n_pad - n_in)))
                r = GLA_GATE_RANK
                wup = jnp.zeros((2, LANES, heads * dk), F32)
                wup = wup.at[0, :r].set(gla_w_up[j, 0]).at[1, r:2 * r].set(gla_w_up[j, 1]).astype(BF16)
                extra = (wup, gla_b_up[j][:, None, :])
                gn, w_out = gla_onorm_g[j], wb_gla_out[j]
                rkind = "gla"
            (p_ctx, o_ctx), (p_lat, o_lat) = _recurrent_mix(
                rkind, xc, xl, (g1, mv[0], mv[1]), ctx_map, w_in, extra, rec_consts[kind],
                b, lc, seq, d, heads, dk, dv, tn)
            gate_col = 2 if kind == 1 else (2 * heads * dk + heads * dv) // d
            xl = _gated_out(o_lat, p_lat, gate_col, gn[None, :], w_out, xl, mv[2], lat_map, heads, tm_lat)
            xc = _gated_out(o_ctx, p_ctx, gate_col, gn[None, :], w_out, xc, mv[2], ctx_map, heads, tm_ctx)
        g2 = norm2_g[i][None, :]
        fg = final_g[None, :]
        xl = _ffn(xl, g2, mv[3], mv[4], mv[5], wb_ffn_in, wb_ffn_out, i, fg, last, lat_map, tm_lat, TF_FFN)
        if not last:
            xc = _ffn(xc, g2, mv[3], mv[4], mv[5], wb_ffn_in, wb_ffn_out, i, fg, False, ctx_map, tm_ctx, TF_FFN)
    return xl.reshape(b, seq, d)
```
